```python
import jax, jax.numpy as jnp
from jax import lax
import numpy as np

D_MODEL = 1024
BATCH = 8
SEQ = 4096
DEPTH = 2

GRID_W = 64
CTX_LEN = 256
EPS = 1e-6
A_WIDTH = 384
A_HEADS = 8
A_HD = A_WIDTH // A_HEADS
CONV_W = 4
RG_C = 8.0
B_HEADS = 4
B_DK = 96
B_DV = 96
B_WIDTH = B_HEADS * B_DK
B_CHUNK = 64
C_WIDTH = 256
C_HEADS = 4
C_HD = C_WIDTH // C_HEADS
C_CHUNK = 128
MIX_WIDTH = A_WIDTH + B_WIDTH + C_WIDTH
IN_SIZES = (A_WIDTH, A_WIDTH, B_WIDTH, B_WIDTH, B_WIDTH, B_WIDTH, B_WIDTH, C_WIDTH, C_WIDTH, C_WIDTH)
IN_COLS = sum(IN_SIZES)

kernel_name = "hybrid_rglru_hgrn2_chunkmlp_prefix_dit"


def _rmsnorm(x, w):
    xf = x.astype(jnp.float32)
    return xf * lax.rsqrt(jnp.mean(xf * xf, axis=-1, keepdims=True) + EPS) * w.astype(jnp.float32)


def _sincos_2d(rows, dim, dtype):
    r, col = jnp.meshgrid(jnp.arange(rows, dtype=jnp.float32), jnp.arange(GRID_W, dtype=jnp.float32), indexing='ij')
    r = r.reshape(-1)
    col = col.reshape(-1)
    q = dim // 4
    omega = 1.0 / (10000.0 ** (jnp.arange(q, dtype=jnp.float32) / q))
    ar = r[:, None] * omega
    ac = col[:, None] * omega
    return jnp.concatenate([jnp.sin(ar), jnp.cos(ar), jnp.sin(ac), jnp.cos(ac)], axis=-1).astype(dtype)


def _dwconv(x, w, b):
    out = lax.conv_general_dilated(
        x, w[:, None, :].astype(jnp.float32), window_strides=(1,),
        padding=[(CONV_W // 2, CONV_W - 1 - CONV_W // 2)],
        dimension_numbers=('NWC', 'WIO', 'NWC'), feature_group_count=x.shape[-1])
    return out + b.astype(jnp.float32)


def _rglru_coeffs(x, w_r, b_r, w_i, b_i, lam):
    bn, L, C = x.shape
    xh = x.reshape(bn, L, A_HEADS, A_HD)
    r = jax.nn.sigmoid(jnp.einsum('blhi,hij->blhj', xh, w_r.astype(jnp.float32)).reshape(bn, L, C) + b_r.astype(jnp.float32))
    i = jax.nn.sigmoid(jnp.einsum('blhi,hij->blhj', xh, w_i.astype(jnp.float32)).reshape(bn, L, C) + b_i.astype(jnp.float32))
    log_a = -RG_C * r * jax.nn.softplus(-lam.astype(jnp.float32))
    a = jnp.exp(log_a)
    u = jnp.sqrt(-jnp.expm1(2.0 * log_a)) * (i * x)
    return a, u


def _linear_scan(a, u, h0):
    u = u.at[:, 0].add(a[:, 0] * h0)

    def comb(lhs, rhs):
        a1, b1 = lhs
        a2, b2 = rhs
        return a1 * a2, a2 * b1 + b2

    _, h = lax.associative_scan(comb, (a, u), axis=1)
    return h


def _rglru_direction(xc_ctx, xc_lat, w_r, b_r, w_i, b_i, lam):
    a, u = _rglru_coeffs(xc_ctx, w_r, b_r, w_i, b_i, lam)
    h_ctx = _linear_scan(a, u, jnp.zeros_like(u[:, 0]))
    a, u = _rglru_coeffs(xc_lat, w_r, b_r, w_i, b_i, lam)
    h_lat = _linear_scan(a, u, h_ctx[:, -1])
    return h_ctx, h_lat


def _rglru_mixer(xa_ctx, xa_lat, conv_w, conv_b, w_r, b_r, w_i, b_i, lam, need_ctx_out):
    xc_c = _dwconv(xa_ctx.astype(jnp.float32), conv_w, conv_b)
    xc_l = _dwconv(xa_lat.astype(jnp.float32), conv_w, conv_b)
    hc_f, hl_f = _rglru_direction(xc_c, xc_l, w_r[0], b_r[0], w_i[0], b_i[0], lam[0])
    hc_b, hl_b = _rglru_direction(jnp.flip(xc_c, 1), jnp.flip(xc_l, 1), w_r[1], b_r[1], w_i[1], b_i[1], lam[1])
    y_lat = hl_f + jnp.flip(hl_b, 1)
    y_ctx = hc_f + jnp.flip(hc_b, 1) if need_ctx_out else None
    return y_ctx, y_lat


def _gla_chunked(q, k, v, logf, s0, want_out):
    bn, L, H, DK = q.shape
    DV = v.shape[-1]
    n = L // B_CHUNK
    rs = lambda t: t.reshape(bn, n, B_CHUNK, H, t.shape[-1])
    q, k, v, logf = rs(q), rs(k), rs(v), rs(logf)
    G = jnp.cumsum(logf, axis=2)
    G_last = G[:, :, -1]
    k_end = k * jnp.exp(G_last[:, :, None] - G)
    chunk_kv = jnp.einsum('bnchd,bnchv->bnhdv', k_end, v)
    decay = jnp.exp(G_last)

    def step(S, inp):
        dec, kv = inp
        return dec[..., None] * S + kv, S

    S_fin, S_starts = lax.scan(step, s0, (jnp.moveaxis(decay, 1, 0), jnp.moveaxis(chunk_kv, 1, 0)))
    if not want_out:
        return None, S_fin
    S_starts = jnp.moveaxis(S_starts, 0, 1)
    qg = q * jnp.exp(G)
    kg = k * jnp.exp(-G)
    scores = jnp.einsum('bnchd,bnshd->bnhcs', qg, kg)
    mask = jnp.tril(jnp.ones((B_CHUNK, B_CHUNK), dtype=bool))
    scores = jnp.where(mask, scores, 0.0)
    o = jnp.einsum('bnhcs,bnshv->bnchv', scores, v) + jnp.einsum('bnchd,bnhdv->bnchv', qg, S_starts)
    return o.reshape(bn, L, H, DV), S_fin


def _hgrn2_feats(q_raw, ff_raw, fb_raw, i_raw, lb):
    heads = lambda t: t.astype(jnp.float32).reshape(t.shape[0], t.shape[1], B_HEADS, -1)
    lbh = lb.reshape(B_HEADS, B_DK)
    q = jax.nn.silu(heads(q_raw))
    v = heads(i_raw)
    dirs = []
    for fr in (ff_raw, fb_raw):
        f = lbh + (1.0 - lbh) * jax.nn.sigmoid(heads(fr))
        dirs.append((1.0 - f, jnp.log(f)))
    return q, v, dirs


def _hgrn2_mixer(feats_ctx, feats_lat, norm_w, need_ctx_out):
    qc, vc, dc = feats_ctx
    ql, vl, dl = feats_lat
    bn = ql.shape[0]
    s0 = jnp.zeros((bn, B_HEADS, B_DK, B_DV), jnp.float32)
    fl = lambda t: jnp.flip(t, 1)
    oc_f, sc_f = _gla_chunked(qc, dc[0][0], vc, dc[0][1], s0, need_ctx_out)
    ol_f, _ = _gla_chunked(ql, dl[0][0], vl, dl[0][1], sc_f, True)
    oc_b, sc_b = _gla_chunked(fl(qc), fl(dc[1][0]), fl(vc), fl(dc[1][1]), s0, need_ctx_out)
    ol_b, _ = _gla_chunked(fl(ql), fl(dl[1][0]), fl(vl), fl(dl[1][1]), sc_b, True)
    g = norm_w.astype(jnp.float32).reshape(B_HEADS, B_DV)

    def head_norm(o):
        o = o * lax.rsqrt(jnp.mean(o * o, axis=-1, keepdims=True) + EPS) * g
        return o.reshape(o.shape[0], o.shape[1], B_WIDTH)

    y_lat = head_norm(ol_f + fl(ol_b))
    y_ctx = head_norm(oc_f + fl(oc_b)) if need_ctx_out else None
    return y_ctx, y_lat


def _chunk_mlp(u_raw, v_raw, norm_w, ws, bs):
    u = jax.nn.gelu(u_raw.astype(jnp.float32))
    v = jax.nn.gelu(v_raw.astype(jnp.float32))
    mu = jnp.mean(v, axis=-1, keepdims=True)
    var = jnp.mean(jnp.square(v - mu), axis=-1, keepdims=True)
    v = (v - mu) * lax.rsqrt(var + EPS) * norm_w.astype(jnp.float32)
    bn, L, _ = v.shape
    n = L // C_CHUNK
    vh = v.reshape(bn, n, C_CHUNK, C_HEADS, C_HD)
    mixed = jnp.einsum('hst,bnthe->bnshe', ws.astype(jnp.float32), vh) + bs.astype(jnp.float32).T[None, None, :, :, None]
    return u * mixed.reshape(bn, L, C_WIDTH)


def setup_inputs(seed: int = 0) -> dict:
    key = jax.random.key(seed)
    ks = jax.random.split(key, 24)
    f32 = jnp.float32
    nrm = lambda k, shape, s: jax.random.normal(k, shape, f32) * s
    u = jax.random.uniform(ks[14], (DEPTH, 2, A_WIDTH), f32, 0.9, 0.999)
    p = u ** (1.0 / RG_C)
    a_lambda = jnp.log(p) - jnp.log1p(-p)
    return {
        "x": nrm(ks[0], (BATCH, SEQ, D_MODEL), 1.0),
        "c": nrm(ks[1], (BATCH, D_MODEL), 1.0),
        "ctx": nrm(ks[2], (BATCH, CTX_LEN, D_MODEL), 1.0),
        "c_ctx": nrm(ks[3], (D_MODEL,), 1.0),
        "norm_w": 1.0 + nrm(ks[4], (DEPTH, D_MODEL), 0.02),
        "w_mod": nrm(ks[5], (DEPTH, D_MODEL, 3 * D_MODEL), D_MODEL ** -0.5),
        "b_mod": nrm(ks[6], (DEPTH, 3 * D_MODEL), 0.02),
        "w_in": nrm(ks[7], (DEPTH, D_MODEL, IN_COLS), D_MODEL ** -0.5),
        "a_conv_w": nrm(ks[8], (DEPTH, CONV_W, A_WIDTH), CONV_W ** -0.5),
        "a_conv_b": nrm(ks[9], (DEPTH, A_WIDTH), 0.02),
        "a_wr": nrm(ks[10], (DEPTH, 2, A_HEADS, A_HD, A_HD), A_HD ** -0.5),
        "a_br": nrm(ks[11], (DEPTH, 2, A_WIDTH), 0.02),
        "a_wi": nrm(ks[12], (DEPTH, 2, A_HEADS, A_HD, A_HD), A_HD ** -0.5),
        "a_bi": nrm(ks[13], (DEPTH, 2, A_WIDTH), 0.02),
        "a_lambda": a_lambda,
        "b_lb_logits": nrm(ks[15], (DEPTH + 1, B_WIDTH), 0.1),
        "b_norm_w": 1.0 + nrm(ks[16], (DEPTH, B_WIDTH), 0.02),
        "c_norm_w": 1.0 + nrm(ks[17], (DEPTH, C_WIDTH), 0.02),
        "c_ws": nrm(ks[18], (DEPTH, C_HEADS, C_CHUNK, C_CHUNK), C_CHUNK ** -0.5),
        "c_bs": nrm(ks[19], (DEPTH, C_HEADS, C_CHUNK), 0.02),
        "w_out": nrm(ks[20], (DEPTH, MIX_WIDTH, D_MODEL), MIX_WIDTH ** -0.5),
        "final_norm_w": 1.0 + nrm(ks[21], (D_MODEL,), 0.02),
    }


def reference(x, c, ctx, c_ctx, norm_w, w_mod, b_mod, w_in, a_conv_w, a_conv_b, a_wr, a_br, a_wi, a_bi,
              a_lambda, b_lb_logits, b_norm_w, c_norm_w, c_ws, c_bs, w_out, final_norm_w):
    n_lat = x.shape[1]
    ROWS = n_lat // GRID_W
    x = x + _sincos_2d(ROWS, D_MODEL, x.dtype)[None]
    silu_c = jax.nn.silu(c.astype(jnp.float32))
    silu_cc = jax.nn.silu(c_ctx.astype(jnp.float32))
    lbs = jnp.cumsum(jax.nn.softmax(b_lb_logits.astype(jnp.float32), axis=0), axis=0)
    cuts = np.cumsum(IN_SIZES)[:-1].tolist()
    for l in range(DEPTH):
        need_ctx_out = l < DEPTH - 1
        mod = silu_c @ w_mod[l] + b_mod[l]
        mod_c = silu_cc @ w_mod[l] + b_mod[l]
        sh, sc, g = jnp.split(mod, 3, axis=-1)
        sh_c, sc_c, g_c = jnp.split(mod_c, 3, axis=-1)
        h = _rmsnorm(x, norm_w[l]) * (1.0 + sc[:, None]) + sh[:, None]
        hc = _rmsnorm(ctx, norm_w[l]) * (1.0 + sc_c) + sh_c
        p = jnp.split(h @ w_in[l], cuts, axis=-1)
        pc = jnp.split(hc @ w_in[l], cuts, axis=-1)
        ya_c, ya_l = _rglru_mixer(pc[0], p[0], a_conv_w[l], a_conv_b[l], a_wr[l], a_br[l], a_wi[l], a_bi[l],
                                  a_lambda[l], need_ctx_out)
        yb_c, yb_l = _hgrn2_mixer(_hgrn2_feats(pc[2], pc[3], pc[4], pc[5], lbs[l]),
                                  _hgrn2_feats(p[2], p[3], p[4], p[5], lbs[l]), b_norm_w[l], need_ctx_out)
        yc_l = _chunk_mlp(p[7], p[8], c_norm_w[l], c_ws[l], c_bs[l])
        y = jnp.concatenate([ya_l * jax.nn.silu(p[1]), yb_l * jax.nn.silu(p[6]), yc_l * jax.nn.silu(p[9])], axis=-1)
        x = x + (g[:, None] * (y.astype(x.dtype) @ w_out[l])).astype(x.dtype)
        if need_ctx_out:
            yc_c = _chunk_mlp(pc[7], pc[8], c_norm_w[l], c_ws[l], c_bs[l])
            yctx = jnp.concatenate([ya_c * jax.nn.silu(pc[1]), yb_c * jax.nn.silu(pc[6]), yc_c * jax.nn.silu(pc[9])], axis=-1)
            ctx = ctx + (g_c * (yctx.astype(ctx.dtype) @ w_out[l])).astype(ctx.dtype)
    return _rmsnorm(x, final_norm_w).astype(x.dtype)
```

```python
import functools

import jax
import jax.numpy as jnp
import numpy as np
from jax import lax
from jax.experimental import pallas as pl
from jax.experimental.pallas import tpu as pltpu

D = 1024
GRID_W = 64
EPS = 1e-6
RG_C = 8.0
AW = 384
A_HEADS = 8
A_HD = 48
BH = 4
BDK = 96
CW = 256
CH = 4
CHD = 64
CCH = 128

LANES = 128
SUBLANES = 8
HP = LANES
BW = BH * HP
YW = AW + BW + CW
T = 512
SEG = 64
NSEG = T // SEG
PITCH = SEG + SUBLANES
KB_COLS = AW + 3 * BW
KF_COLS = AW + 2 * BW + 3 * CW
VMEM_LIMIT = 56 * 1024 * 1024

F32 = jnp.float32
BF16 = jnp.bfloat16


def _sigmoid(x):
    return 0.5 * jnp.tanh(0.5 * x) + 0.5


def _silu(x):
    return x * _sigmoid(x)


def _gelu(x):
    return 0.5 * x * (1.0 + jnp.tanh(np.sqrt(2.0 / np.pi).astype(np.float32) * (x + 0.044715 * (x * x * x))))


def _nt(a, b):
    return lax.dot_general(a, b, (((1,), (1,)), ((), ())), preferred_element_type=F32)


def _tn(a, b):
    return lax.dot_general(a, b, (((0,), (0,)), ((), ())), preferred_element_type=F32)


def _norm_mod(x, nw, sh, sc):
    ms = jnp.mean(x * x, axis=-1, keepdims=True)
    return (x * lax.rsqrt(ms + EPS) * nw) * (1.0 + sc) + sh


def _pos_block(rt_ref, ct_ref):
    left = jnp.concatenate([jnp.broadcast_to(rt_ref[s:s + 1, :], (SEG, D // 2)) for s in range(NSEG)], axis=0)
    right = jnp.concatenate([ct_ref[...]] * NSEG, axis=0)
    return jnp.concatenate([left, right], axis=1)


def _scan_steps(reverse, body, init):
    def step(jj, carry):
        j = (SEG - 1 - jj) if reverse else jj
        return body(j, carry)
    return lax.fori_loop(0, SEG, step, init, unroll=4)


def _a_scan(a_st, u_st, h_st, p_st, reverse):
    def body(j, carry):
        hs, ps = carry
        nh, npr = [], []
        for k in range(AW // LANES):
            a = a_st[k, pl.ds(j, NSEG, stride=PITCH), :]
            u = u_st[k, pl.ds(j, NSEG, stride=PITCH), :]
            h = a * hs[k] + u
            p = a * ps[k]
            h_st[k, pl.ds(j, NSEG, stride=PITCH), :] = h
            p_st[k, pl.ds(j, NSEG, stride=PITCH), :] = p
            nh.append(h)
            npr.append(p)
        return tuple(nh), tuple(npr)
    z = jnp.zeros((NSEG, LANES), F32)
    o = jnp.ones((NSEG, LANES), F32)
    n = AW // LANES
    return _scan_steps(reverse, body, ((z,) * n, (o,) * n))


def _cumsum_steps(lf_st, g_st, reverse):
    def body(j, gs):
        out = []
        for k in range(BH):
            g = gs[k] + lf_st[k, pl.ds(j, NSEG, stride=PITCH), :]
            g_st[k, pl.ds(j, NSEG, stride=PITCH), :] = g
            out.append(g)
        return tuple(out)
    z = jnp.zeros((NSEG, LANES), F32)
    return _scan_steps(reverse, body, (z,) * BH)


def _a_coeffs(g_scr, xc_ref, cl_ref, a_st, u_st):
    for s in range(NSEG):
        rows = pl.ds(s * SEG, SEG)
        for k in range(AW // LANES):
            lanes = slice(k * LANES, (k + 1) * LANES)
            r = _sigmoid(g_scr[rows, lanes])
            gi = _sigmoid(g_scr[rows, AW + k * LANES:AW + (k + 1) * LANES])
            la = r * cl_ref[:, lanes]
            th = jnp.tanh(la)
            om = (-2.0 * th) / (1.0 - th)
            a_st[k, pl.ds(s * PITCH, SEG), :] = jnp.exp(la)
            u_st[k, pl.ds(s * PITCH, SEG), :] = jnp.sqrt(om) * (gi * xc_ref[rows, lanes])


def _a_carries(hend, pend, hc_ref, nb, reverse):
    spq = NSEG // nb
    cin = [None] * NSEG
    n = AW // LANES
    for q in range(nb):
        c = [hc_ref[q:q + 1, k * LANES:(k + 1) * LANES] for k in range(n)]
        order = range((q + 1) * spq - 1, q * spq - 1, -1) if reverse else range(q * spq, (q + 1) * spq)
        for s in order:
            cin[s] = c
            c = [pend[k][s:s + 1, :] * c[k] + hend[k][s:s + 1, :] for k in range(n)]
        for k in range(n):
            hc_ref[q:q + 1, k * LANES:(k + 1) * LANES] = c[k]
    return cin


def _conv(xa_ext, cw_ref, cb_ref, xc_ref, nb):
    spq = NSEG // nb
    row = lax.broadcasted_iota(jnp.int32, (SEG, 1), 0)
    for s in range(NSEG):
        acc = jnp.broadcast_to(cb_ref[...], (SEG, AW))
        for j in range(4):
            tap = xa_ext[pl.ds(s * SEG + SUBLANES - 2 + j, SEG), :]
            if nb > 1 and s % spq == 0 and j < 2:
                tap = jnp.where(row + (j - 2) >= 0, tap, 0.0)
            if nb > 1 and s % spq == spq - 1 and j == 3:
                tap = jnp.where(row + 1 < SEG, tap, 0.0)
            acc = acc + cw_ref[j:j + 1, :] * tap
        xc_ref[pl.ds(s * SEG, SEG), :] = acc


def _kb_body(*refs, nb, nc, add_pos):
    it = iter(refs)
    x_ref = next(it)
    xh_ref = next(it) if nc > 1 else None
    if add_pos:
        rt_ref, ct_ref, hp_ref = next(it), next(it), next(it)
    (mod_ref, nw_ref, wb_ref, cw_ref, cb_ref, wg_ref, bg_ref, cl_ref, lb_ref, h0_ref, s0_ref,
     hbf_ref, xc_ref, hb_ref, ob_ref, q_ref, v_ref, hfin_ref, sfin_ref,
     p_scr, xa_ext, g_scr, k_scr, a_st, u_st, h_st, p_st, lf_st, g_st, hc_ref, s_scr, xaf) = it

    ic = pl.program_id(1)
    cc = nc - 1 - ic
    sh = mod_ref[0, :, 0:D]
    sc = mod_ref[0, :, D:2 * D]
    nw = nw_ref[...]

    @pl.when(ic == 0)
    def _():
        for q in range(nb):
            hc_ref[q:q + 1, :] = h0_ref[q]
        s_scr[...] = s0_ref[...]
        xaf[...] = jnp.zeros_like(xaf)

    xin = x_ref[...].reshape(T, D)
    if add_pos:
        xin = xin + _pos_block(rt_ref, ct_ref)
    hbf = _norm_mod(xin, nw, sh, sc).astype(BF16)
    hbf_ref[...] = hbf.reshape(hbf_ref.shape)
    p_scr[...] = jnp.dot(hbf, wb_ref[...], preferred_element_type=F32)

    if nc > 1:
        xh = xh_ref[0]
        if add_pos:
            xh = xh + hp_ref[0]
        hh = _norm_mod(xh, nw, sh, sc).astype(BF16)
        left = jnp.dot(hh, wb_ref[:, 0:AW], preferred_element_type=F32)
        xa_ext[0:SUBLANES, :] = jnp.where(cc > 0, left, 0.0)
    else:
        xa_ext[0:SUBLANES, :] = jnp.zeros((SUBLANES, AW), F32)
    xa_ext[pl.ds(SUBLANES, T), :] = p_scr[:, 0:AW]
    xa_ext[pl.ds(SUBLANES + T, SUBLANES), :] = xaf[...]
    xaf[...] = p_scr[0:SUBLANES, 0:AW]
    xc_flat = _Flat(xc_ref, AW)
    _conv(xa_ext, cw_ref, cb_ref, xc_flat, nb)
    g_scr[...] = jnp.dot(xc_flat[pl.ds(0, T), :].astype(BF16), wg_ref[...], preferred_element_type=F32) + bg_ref[...]
    _a_coeffs(g_scr, xc_flat, cl_ref, a_st, u_st)
    hend, pend = _a_scan(a_st, u_st, h_st, p_st, reverse=True)
    cin = _a_carries(hend, pend, hc_ref, nb, reverse=True)
    hb_flat = _Flat(hb_ref, AW)
    for s in range(NSEG):
        for k in range(AW // LANES):
            st_rows = pl.ds(s * PITCH, SEG)
            hb_flat[pl.ds(s * SEG, SEG), k * LANES:(k + 1) * LANES] = h_st[k, st_rows, :] + p_st[k, st_rows, :] * cin[s][k]
    for q in range(nb):
        hfin_ref[q] = hc_ref[q:q + 1, :]

    for s in range(NSEG):
        rows = pl.ds(s * SEG, SEG)
        for k in range(BH):
            lanes = slice(k * HP, (k + 1) * HP)
            lb = lb_ref[:, lanes]
            f = lb + (1.0 - lb) * _sigmoid(p_scr[rows, AW + BW + k * HP:AW + BW + (k + 1) * HP])
            k_scr[rows, lanes] = 1.0 - f
            lf_st[k, pl.ds(s * PITCH, SEG), :] = jnp.log(f)
    gtot = _cumsum_steps(lf_st, g_st, reverse=True)
    q_flat, v_flat, ob_flat = _Flat(q_ref, BW), _Flat(v_ref, BW), _Flat(ob_ref, BW)
    cps = NSEG // nb
    r_i = lax.broadcasted_iota(jnp.int32, (SEG, SEG), 0)
    c_i = lax.broadcasted_iota(jnp.int32, (SEG, SEG), 1)
    keep = c_i >= r_i
    for n in range(NSEG - 1, -1, -1):
        rows = pl.ds(n * SEG, SEG)
        sq = n // cps
        for k in range(BH):
            lanes = slice(k * HP, (k + 1) * HP)
            g = g_st[k, pl.ds(n * PITCH, SEG), :]
            tot = gtot[k][n:n + 1, :]
            qv = _silu(p_scr[rows, AW + k * HP:AW + (k + 1) * HP])
            q_flat[rows, lanes] = qv
            vb = p_scr[rows, AW + 2 * BW + k * HP:AW + 2 * BW + (k + 1) * HP].astype(BF16)
            v_flat[rows, lanes] = vb
            kk = k_scr[rows, lanes]
            qg = (qv * jnp.exp(g)).astype(BF16)
            kg = (kk * jnp.exp(-g)).astype(BF16)
            ke = (kk * jnp.exp(tot - g)).astype(BF16)
            att = jnp.where(keep, _nt(qg, kg), 0.0).astype(BF16)
            st = s_scr[sq, k]
            ob_flat[rows, lanes] = jnp.dot(att, vb, preferred_element_type=F32) + _nt(qg, st.astype(BF16))
            s_scr[sq, k] = jnp.exp(tot) * st + _tn(vb, ke)
    sfin_ref[...] = s_scr[...]


class _Flat:
    def __init__(self, ref, width):
        self.ref = ref
        self.rs = ref.shape[1]
        self.width = width

    def _split(self, idx):
        rows, lanes = idx
        start, size = rows.start, rows.size
        q, off = divmod(start, self.rs)
        assert off + size <= self.rs
        return q, pl.ds(off, size), lanes

    def __getitem__(self, idx):
        rows, lanes = idx
        if rows.size == self.ref.shape[0] * self.rs:
            return self.ref[:, :, lanes].reshape(rows.size, -1)
        q, r, lanes = self._split(idx)
        return self.ref[q, r, lanes]

    def __setitem__(self, idx, val):
        q, r, lanes = self._split(idx)
        self.ref[q, r, lanes] = val


def _kf_body(*refs, nb, nc, add_pos, final):
    it = iter(refs)
    x_ref = next(it)
    if add_pos:
        rt_ref, ct_ref = next(it), next(it)
    (hbf_ref, xc_ref, hb_ref, ob_ref, q_ref, v_ref, mod_ref, wf_ref, wg_ref, bg_ref, cl_ref, lb_ref, bnw_ref,
     cnw_ref, cws_ref, cbias_ref, wo_ref, fnw_ref, h0_ref, s0_ref,
     xo_ref, hfin_ref, sfin_ref,
     p_scr, g_scr, k_scr, y_scr, a_st, u_st, h_st, p_st, lf_st, g_st, hc_ref, s_scr) = it

    ic = pl.program_id(1)
    gm = mod_ref[0, :, 2 * D:3 * D]

    @pl.when(ic == 0)
    def _():
        for q in range(nb):
            hc_ref[q:q + 1, :] = h0_ref[q]
        s_scr[...] = s0_ref[...]

    hbf = hbf_ref[...].reshape(T, D)
    p_scr[...] = jnp.dot(hbf, wf_ref[...], preferred_element_type=F32)
    o_ag, o_ff, o_bg = 0, AW, AW + BW
    o_cu, o_cv, o_cg = AW + 2 * BW, AW + 2 * BW + CW, AW + 2 * BW + 2 * CW

    xc_flat, hb_flat = _Flat(xc_ref, AW), _Flat(hb_ref, AW)
    g_scr[...] = jnp.dot(xc_flat[pl.ds(0, T), :].astype(BF16), wg_ref[...], preferred_element_type=F32) + bg_ref[...]
    _a_coeffs(g_scr, xc_flat, cl_ref, a_st, u_st)
    hend, pend = _a_scan(a_st, u_st, h_st, p_st, reverse=False)
    cin = _a_carries(hend, pend, hc_ref, nb, reverse=False)
    for s in range(NSEG):
        rows = pl.ds(s * SEG, SEG)
        for k in range(AW // LANES):
            lanes = slice(k * LANES, (k + 1) * LANES)
            st_rows = pl.ds(s * PITCH, SEG)
            hf = h_st[k, st_rows, :] + p_st[k, st_rows, :] * cin[s][k]
            ya = (hf + hb_flat[rows, lanes]) * _silu(p_scr[rows, o_ag + k * LANES:o_ag + (k + 1) * LANES])
            y_scr[rows, lanes] = ya.astype(BF16)
    for q in range(nb):
        hfin_ref[q] = hc_ref[q:q + 1, :]

    for s in range(NSEG):
        rows = pl.ds(s * SEG, SEG)
        for k in range(BH):
            lanes = slice(k * HP, (k + 1) * HP)
            lb = lb_ref[:, lanes]
            f = lb + (1.0 - lb) * _sigmoid(p_scr[rows, o_ff + k * HP:o_ff + (k + 1) * HP])
            k_scr[rows, lanes] = 1.0 - f
            lf_st[k, pl.ds(s * PITCH, SEG), :] = jnp.log(f)
    gtot = _cumsum_steps(lf_st, g_st, reverse=False)
    q_flat, v_flat, ob_flat = _Flat(q_ref, BW), _Flat(v_ref, BW), _Flat(ob_ref, BW)
    cps = NSEG // nb
    r_i = lax.broadcasted_iota(jnp.int32, (SEG, SEG), 0)
    c_i = lax.broadcasted_iota(jnp.int32, (SEG, SEG), 1)
    keep = c_i <= r_i
    for n in range(NSEG):
        rows = pl.ds(n * SEG, SEG)
        sq = n // cps
        for k in range(BH):
            lanes = slice(k * HP, (k + 1) * HP)
            g = g_st[k, pl.ds(n * PITCH, SEG), :]
            tot = gtot[k][n:n + 1, :]
            qv = q_flat[rows, lanes]
            vb = v_flat[rows, lanes]
            kk = k_scr[rows, lanes]
            qg = (qv * jnp.exp(g)).astype(BF16)
            kg = (kk * jnp.exp(-g)).astype(BF16)
            ke = (kk * jnp.exp(tot - g)).astype(BF16)
            att = jnp.where(keep, _nt(qg, kg), 0.0).astype(BF16)
            st = s_scr[sq, k]
            o = jnp.dot(att, vb, preferred_element_type=F32) + _nt(qg, st.astype(BF16)) + ob_flat[rows, lanes]
            s_scr[sq, k] = jnp.exp(tot) * st + _tn(vb, ke)
            ms = jnp.sum(o * o, axis=-1, keepdims=True) * (1.0 / BDK)
            yb = (o * lax.rsqrt(ms + EPS) * bnw_ref[:, lanes]) * _silu(p_scr[rows, o_bg + k * HP:o_bg + (k + 1) * HP])
            y_scr[rows, AW + k * HP:AW + (k + 1) * HP] = yb.astype(BF16)
    sfin_ref[...] = s_scr[...]

    lane = lax.broadcasted_iota(jnp.int32, (CCH, CW), 1)
    for m in range(T // CCH):
        rows = pl.ds(m * CCH, CCH)
        u = _gelu(p_scr[rows, o_cu:o_cu + CW])
        v = _gelu(p_scr[rows, o_cv:o_cv + CW])
        mu = jnp.mean(v, axis=-1, keepdims=True)
        dv = v - mu
        var = jnp.mean(dv * dv, axis=-1, keepdims=True)
        vn = (dv * lax.rsqrt(var + EPS) * cnw_ref[...]).astype(BF16)
        full = jnp.dot(cws_ref[...], vn, preferred_element_type=F32)
        mixed = full[0:CCH]
        for h in range(1, CH):
            mixed = jnp.where(lane >= h * CHD, full[h * CCH:(h + 1) * CCH], mixed)
        yc = u * (mixed + cbias_ref[...]) * _silu(p_scr[rows, o_cg:o_cg + CW])
        y_scr[rows, AW + BW:YW] = yc.astype(BF16)

    xin = x_ref[...].reshape(T, D)
    if add_pos:
        xin = xin + _pos_block(rt_ref, ct_ref)
    xn = xin + gm * jnp.dot(y_scr[...], wo_ref[...], preferred_element_type=F32)
    if final:
        ms = jnp.mean(xn * xn, axis=-1, keepdims=True)
        xn = xn * lax.rsqrt(ms + EPS) * fnw_ref[...]
    xo_ref[...] = xn.reshape(xo_ref.shape)


def _mod_body(cs_ref, w_ref, b_ref, o_ref):
    s = _silu(cs_ref[...])
    o_ref[0] = jnp.dot(s, w_ref[0], preferred_element_type=F32, precision=lax.Precision.HIGHEST) + b_ref[0]


def _const_spec(shape):
    nd = len(shape)
    return pl.BlockSpec(shape, lambda ib, ic: (0,) * nd)


def _seq_spec(nb, rs, width, cidx):
    return pl.BlockSpec((nb, rs, width), lambda ib, ic: (ib, cidx(ic), 0))


def _scan_scratch():
    na = AW // LANES
    return ([pltpu.VMEM((na, NSEG * PITCH, LANES), F32)] * 4
            + [pltpu.VMEM((BH, NSEG * PITCH, LANES), F32)] * 2)


def _params():
    return pltpu.CompilerParams(dimension_semantics=("arbitrary", "arbitrary"), vmem_limit_bytes=VMEM_LIMIT)


def _run_kb(x, mod, mod_row, lw, h0, s0, pos, nb):
    bsz, seq, _ = x.shape
    rs = T // nb
    nc = seq // rs
    add_pos = pos is not None
    cidx = lambda ic: nc - 1 - ic
    mrow = (lambda ib: ib) if mod_row is None else (lambda ib: mod_row)
    ins, specs = [x], [_seq_spec(nb, rs, D, cidx)]
    if nc > 1:
        ins.append(x)
        specs.append(pl.BlockSpec((1, SUBLANES, D),
                                  lambda ib, ic: (ib, jnp.maximum(cidx(ic) * (rs // SUBLANES) - 1, 0), 0)))
    if add_pos:
        rowtab, coltab, hpos = pos
        ins += [rowtab, coltab, hpos]
        specs += [pl.BlockSpec((NSEG, D // 2), lambda ib, ic: (cidx(ic), 0)),
                  _const_spec(coltab.shape),
                  pl.BlockSpec((1, SUBLANES, D), lambda ib, ic: (cidx(ic), 0, 0))]
    ins += [mod, lw["nw"], lw["wb"], lw["cw"], lw["cb"], lw["wg_b"], lw["bg_b"], lw["cl_b"], lw["lb"], h0, s0]
    specs += [pl.BlockSpec((1, 1, 3 * D), lambda ib, ic: (mrow(ib), 0, 0)),
              _const_spec(lw["nw"].shape), _const_spec(lw["wb"].shape), _const_spec(lw["cw"].shape),
              _const_spec(lw["cb"].shape), _const_spec(lw["wg_b"].shape), _const_spec(lw["bg_b"].shape),
              _const_spec(lw["cl_b"].shape), _const_spec(lw["lb"].shape),
              pl.BlockSpec((nb, 1, AW), lambda ib, ic: (ib, 0, 0)),
              pl.BlockSpec((nb, BH, HP, HP), lambda ib, ic: (ib, 0, 0, 0))]
    out_shape = (jax.ShapeDtypeStruct((bsz, seq, D), BF16), jax.ShapeDtypeStruct((bsz, seq, AW), F32),
                 jax.ShapeDtypeStruct((bsz, seq, AW), F32), jax.ShapeDtypeStruct((bsz, seq, BW), F32),
                 jax.ShapeDtypeStruct((bsz, seq, BW), F32), jax.ShapeDtypeStruct((bsz, seq, BW), BF16),
                 jax.ShapeDtypeStruct((bsz, 1, AW), F32), jax.ShapeDtypeStruct((bsz, BH, HP, HP), F32))
    out_specs = (_seq_spec(nb, rs, D, cidx), _seq_spec(nb, rs, AW, cidx), _seq_spec(nb, rs, AW, cidx),
                 _seq_spec(nb, rs, BW, cidx), _seq_spec(nb, rs, BW, cidx), _seq_spec(nb, rs, BW, cidx),
                 pl.BlockSpec((nb, 1, AW), lambda ib, ic: (ib, 0, 0)),
                 pl.BlockSpec((nb, BH, HP, HP), lambda ib, ic: (ib, 0, 0, 0)))
    scratch = ([pltpu.VMEM((T, KB_COLS), F32), pltpu.VMEM((T + 2 * SUBLANES, AW), F32),
                pltpu.VMEM((T, 2 * AW), F32), pltpu.VMEM((T, BW), F32)]
               + _scan_scratch()
               + [pltpu.VMEM((SUBLANES, AW), F32), pltpu.VMEM((nb, BH, HP, HP), F32),
                  pltpu.VMEM((SUBLANES, AW), F32)])
    return pl.pallas_call(
        functools.partial(_kb_body, nb=nb, nc=nc, add_pos=add_pos),
        grid=(bsz // nb, nc), in_specs=specs, out_specs=out_specs, out_shape=out_shape,
        scratch_shapes=scratch, compiler_params=_params(), name="bwd_sweep")(*ins)


def _run_kf(x, kb_out, mod, mod_row, lw, fnw, h0, s0, pos, nb, final):
    bsz, seq, _ = x.shape
    rs = T // nb
    nc = seq // rs
    add_pos = pos is not None
    cidx = lambda ic: ic
    mrow = (lambda ib: ib) if mod_row is None else (lambda ib: mod_row)
    hbf, xc, hb, ob, q, v = kb_out
    ins, specs = [x], [_seq_spec(nb, rs, D, cidx)]
    if add_pos:
        rowtab, coltab, _ = pos
        ins += [rowtab, coltab]
        specs += [pl.BlockSpec((NSEG, D // 2), lambda ib, ic: (cidx(ic), 0)), _const_spec(coltab.shape)]
    ins += [hbf, xc, hb, ob, q, v, mod, lw["wf"], lw["wg_f"], lw["bg_f"], lw["cl_f"], lw["lb"], lw["bnw"],
            lw["cnw"], lw["cws"], lw["cbias"], lw["wo"], fnw, h0, s0]
    specs += [_seq_spec(nb, rs, D, cidx), _seq_spec(nb, rs, AW, cidx), _seq_spec(nb, rs, AW, cidx),
              _seq_spec(nb, rs, BW, cidx), _seq_spec(nb, rs, BW, cidx), _seq_spec(nb, rs, BW, cidx),
              pl.BlockSpec((1, 1, 3 * D), lambda ib, ic: (mrow(ib), 0, 0))]
    specs += [_const_spec(lw[n].shape) for n in ("wf", "wg_f", "bg_f", "cl_f", "lb", "bnw", "cnw", "cws", "cbias", "wo")]
    specs += [_const_spec(fnw.shape),
              pl.BlockSpec((nb, 1, AW), lambda ib, ic: (ib, 0, 0)),
              pl.BlockSpec((nb, BH, HP, HP), lambda ib, ic: (ib, 0, 0, 0))]
    out_shape = (jax.ShapeDtypeStruct((bsz, seq, D), F32),
                 jax.ShapeDtypeStruct((bsz, 1, AW), F32), jax.ShapeDtypeStruct((bsz, BH, HP, HP), F32))
    out_specs = (_seq_spec(nb, rs, D, cidx),
                 pl.BlockSpec((nb, 1, AW), lambda ib, ic: (ib, 0, 0)),
                 pl.BlockSpec((nb, BH, HP, HP), lambda ib, ic: (ib, 0, 0, 0)))
    scratch = ([pltpu.VMEM((T, KF_COLS), F32), pltpu.VMEM((T, 2 * AW), F32), pltpu.VMEM((T, BW), F32),
                pltpu.VMEM((T, YW), BF16)]
               + _scan_scratch()
               + [pltpu.VMEM((SUBLANES, AW), F32), pltpu.VMEM((nb, BH, HP, HP), F32)])
    return pl.pallas_call(
        functools.partial(_kf_body, nb=nb, nc=nc, add_pos=add_pos, final=final),
        grid=(bsz // nb, nc), in_specs=specs, out_specs=out_specs, out_shape=out_shape,
        scratch_shapes=scratch, compiler_params=_params(), name="fwd_sweep")(*ins)


def _pad_heads(w, fill=0.0):
    lead = w.shape[:-1]
    w = w.reshape(lead + (BH, BDK))
    w = jnp.pad(w, [(0, 0)] * len(lead) + [(0, 0), (0, HP - BDK)], constant_values=fill)
    return w.reshape(lead + (BW,))


def _block_diag(w):
    eye = jnp.eye(A_HEADS, dtype=w.dtype)
    return jnp.einsum("hij,hg->higj", w, eye).reshape(AW, AW)


def _layer_weights(l, norm_w, w_in, a_conv_w, a_conv_b, a_wr, a_br, a_wi, a_bi, a_lambda, lbs, b_norm_w,
                   c_norm_w, c_ws, c_bs, w_out):
    cuts = np.cumsum((AW, AW, 384, 384, 384, 384, 384, CW, CW, CW))[:-1].tolist()
    ax, ag, bq, bff, bfb, bi, bg, cu, cv, cg = jnp.split(w_in[l], cuts, axis=-1)
    wo_a, wo_b, wo_c = jnp.split(w_out[l], [AW, AW + 384], axis=0)
    wo_b = _pad_heads(wo_b.T).T
    lw = {
        "nw": norm_w[l][None, :],
        "wb": jnp.concatenate([ax, _pad_heads(bq), _pad_heads(bfb), _pad_heads(bi)], axis=1).astype(BF16),
        "wf": jnp.concatenate([ag, _pad_heads(bff), _pad_heads(bg), cu, cv, cg], axis=1).astype(BF16),
        "cw": a_conv_w[l], "cb": a_conv_b[l][None, :],
        "lb": _pad_heads(lbs[l], fill=1.0)[None, :],
        "bnw": _pad_heads(b_norm_w[l])[None, :],
        "cnw": c_norm_w[l][None, :],
        "cws": c_ws[l].reshape(CH * CCH, CCH).astype(BF16),
        "cbias": jnp.repeat(c_bs[l].T, CHD, axis=1),
        "wo": jnp.concatenate([wo_a, wo_b, wo_c], axis=0).astype(BF16),
    }
    for d, name in ((0, "f"), (1, "b")):
        lw["wg_" + name] = jnp.concatenate([_block_diag(a_wr[l, d]), _block_diag(a_wi[l, d])], axis=1).astype(BF16)
        lw["bg_" + name] = jnp.concatenate([a_br[l, d], a_bi[l, d]])[None, :]
        lw["cl_" + name] = (-RG_C * jax.nn.softplus(-a_lambda[l, d]))[None, :]
    return lw


def _pos_tables(rows):
    qd = D // 4
    omega = 1.0 / (10000.0 ** (jnp.arange(qd, dtype=F32) / qd))
    ar = jnp.arange(rows, dtype=F32)[:, None] * omega
    ac = jnp.arange(GRID_W, dtype=F32)[:, None] * omega
    rowtab = jnp.concatenate([jnp.sin(ar), jnp.cos(ar)], axis=-1)
    coltab = jnp.concatenate([jnp.sin(ac), jnp.cos(ac)], axis=-1)
    nc = rows * GRID_W // T
    hrow = jnp.broadcast_to(rowtab[jnp.maximum(jnp.arange(nc) * NSEG - 1, 0)][:, None, :], (nc, SUBLANES, D // 2))
    hcol = jnp.broadcast_to(coltab[GRID_W - SUBLANES:][None], (nc, SUBLANES, D // 2))
    return rowtab, coltab, jnp.concatenate([hrow, hcol], axis=-1)


def kernel(x, c, ctx, c_ctx, norm_w, w_mod, b_mod, w_in, a_conv_w, a_conv_b, a_wr, a_br, a_wi, a_bi, a_lambda,
           b_lb_logits, b_norm_w, c_norm_w, c_ws, c_bs, w_out, final_norm_w):
    bsz, n_lat, _ = x.shape
    depth = w_in.shape[0]
    assert n_lat % T == 0 and T % ctx.shape[1] == 0 and bsz % (T // ctx.shape[1]) == 0
    nb_ctx = T // ctx.shape[1]
    pos = _pos_tables(n_lat // GRID_W)

    mrows = 2 * SUBLANES
    cs = jnp.zeros((mrows, D), F32).at[:bsz].set(c).at[bsz].set(c_ctx)
    nblk = 3
    mod = pl.pallas_call(
        _mod_body, grid=(depth, nblk),
        in_specs=[pl.BlockSpec((mrows, D), lambda l, j: (0, 0)),
                  pl.BlockSpec((1, D, D), lambda l, j: (l, 0, j)),
                  pl.BlockSpec((1, 1, D), lambda l, j: (l, 0, j))],
        out_specs=pl.BlockSpec((1, mrows, D), lambda l, j: (l, 0, j)),
        out_shape=jax.ShapeDtypeStruct((depth, mrows, 3 * D), F32),
        name="modulation")(cs, w_mod, b_mod[:, None, :])
    mod = mod.reshape(depth, mrows, 1, 3 * D)

    lbs = jnp.cumsum(jax.nn.softmax(b_lb_logits, axis=0), axis=0)
    fnw = final_norm_w[None, :]
    h_zero = jnp.zeros((bsz, 1, AW), F32)
    s_zero = jnp.zeros((bsz, BH, HP, HP), F32)
    for l in range(depth):
        lw = _layer_weights(l, norm_w, w_in, a_conv_w, a_conv_b, a_wr, a_br, a_wi, a_bi, a_lambda, lbs, b_norm_w,
                            c_norm_w, c_ws, c_bs, w_out)
        kb = _run_kb(ctx, mod[l], bsz, lw, h_zero, s_zero, None, nb_ctx)
        ctx_new, hf_c, sf_c = _run_kf(ctx, kb[:6], mod[l], bsz, lw, fnw, h_zero, s_zero, None, nb_ctx, False)
        hb_c, sb_c = kb[6], kb[7]
        kb = _run_kb(x, mod[l], None, lw, hb_c, sb_c, pos if l == 0 else None, 1)
        x, _, _ = _run_kf(x, kb[:6], mod[l], None, lw, fnw, hf_c, sf_c, pos if l == 0 else None, 1, l == depth - 1)
        ctx = ctx_new
    return x
```

```python
import functools

import jax
import jax.numpy as jnp
import numpy as np
from jax import lax
from jax.experimental import pallas as pl
from jax.experimental.pallas import tpu as pltpu

D = 1024
GRID_W = 64
EPS = 1e-6
RG_C = 8.0
AW = 384
A_HEADS = 8
A_HD = 48
BH = 4
BDK = 96
CW = 256
CH = 4
CHD = 64
CCH = 128

LANES = 128
SUBLANES = 8
HP = LANES
BW = BH * HP
YW = AW + BW + CW
T = 512
SEG = 64
NSEG = T // SEG
PITCH = SEG + SUBLANES
KB_COLS = AW + 3 * BW
KF_COLS = AW + 2 * BW + 3 * CW
FILL_COLS = 256
VMEM_LIMIT = 58 * 1024 * 1024

F32 = jnp.float32
BF16 = jnp.bfloat16


def _sigmoid(x):
    return 0.5 * jnp.tanh(0.5 * x) + 0.5


def _silu(x):
    return x * _sigmoid(x)


def _gelu(x):
    return 0.5 * x * (1.0 + jnp.tanh(np.sqrt(2.0 / np.pi).astype(np.float32) * (x + 0.044715 * (x * x * x))))


def _nt(a, b):
    return lax.dot_general(a, b, (((1,), (1,)), ((), ())), preferred_element_type=F32)


def _tn(a, b):
    return lax.dot_general(a, b, (((0,), (0,)), ((), ())), preferred_element_type=F32)


def _norm_mod(x, nw, sh, sc):
    ms = jnp.mean(x * x, axis=-1, keepdims=True)
    return (x * lax.rsqrt(ms + EPS) * nw) * (1.0 + sc) + sh


def _pos_block(rt_ref, ct_ref):
    left = jnp.concatenate([jnp.broadcast_to(rt_ref[s:s + 1, :], (SEG, D // 2)) for s in range(NSEG)], axis=0)
    right = jnp.concatenate([ct_ref[...]] * NSEG, axis=0)
    return jnp.concatenate([left, right], axis=1)


class _Fill:
    def __init__(self, thunks):
        self.thunks = list(thunks)

    def __call__(self, n=1):
        for _ in range(min(n, len(self.thunks))):
            self.thunks.pop(0)()

    def flush(self):
        self(len(self.thunks))


def _scan_steps(reverse, body, carry, fill=None, fill_at=()):
    for jj in range(SEG):
        if fill is not None and jj in fill_at:
            fill()
        carry = body((SEG - 1 - jj) if reverse else jj, carry)
    return carry


def _a_scan(a_st, u_st, h_st, p_st, reverse, fill=None, fill_at=()):
    def body(j, carry):
        hs, ps = carry
        nh, npr = [], []
        for k in range(AW // LANES):
            a = a_st[k, pl.ds(j, NSEG, stride=PITCH), :]
            u = u_st[k, pl.ds(j, NSEG, stride=PITCH), :]
            h = a * hs[k] + u
            p = a * ps[k]
            h_st[k, pl.ds(j, NSEG, stride=PITCH), :] = h
            p_st[k, pl.ds(j, NSEG, stride=PITCH), :] = p
            nh.append(h)
            npr.append(p)
        return tuple(nh), tuple(npr)
    z = jnp.zeros((NSEG, LANES), F32)
    o = jnp.ones((NSEG, LANES), F32)
    n = AW // LANES
    return _scan_steps(reverse, body, ((z,) * n, (o,) * n), fill, fill_at)


def _cumsum_steps(lf_st, g_st, reverse, fill=None, fill_at=()):
    def body(j, gs):
        out = []
        for k in range(BH):
            g = gs[k] + lf_st[k, pl.ds(j, NSEG, stride=PITCH), :]
            g_st[k, pl.ds(j, NSEG, stride=PITCH), :] = g
            out.append(g)
        return tuple(out)
    z = jnp.zeros((NSEG, LANES), F32)
    return _scan_steps(reverse, body, (z,) * BH, fill, fill_at)


def _a_coeffs(g_scr, xc_ref, cl_ref, a_st, u_st, fill=None, fill_at=()):
    for s in range(NSEG):
        if fill is not None and s in fill_at:
            fill()
        rows = pl.ds(s * SEG, SEG)
        for k in range(AW // LANES):
            lanes = slice(k * LANES, (k + 1) * LANES)
            r = _sigmoid(g_scr[rows, lanes])
            gi = _sigmoid(g_scr[rows, AW + k * LANES:AW + (k + 1) * LANES])
            la = r * cl_ref[:, lanes]
            th = jnp.tanh(la)
            om = (-2.0 * th) / (1.0 - th)
            a_st[k, pl.ds(s * PITCH, SEG), :] = jnp.exp(la)
            u_st[k, pl.ds(s * PITCH, SEG), :] = jnp.sqrt(om) * (gi * xc_ref[rows, lanes])


def _a_carries(hend, pend, hc_ref, nb, reverse):
    spq = NSEG // nb
    cin = [None] * NSEG
    n = AW // LANES
    for q in range(nb):
        c = [hc_ref[q:q + 1, k * LANES:(k + 1) * LANES] for k in range(n)]
        order = range((q + 1) * spq - 1, q * spq - 1, -1) if reverse else range(q * spq, (q + 1) * spq)
        for s in order:
            cin[s] = c
            c = [pend[k][s:s + 1, :] * c[k] + hend[k][s:s + 1, :] for k in range(n)]
        for k in range(n):
            hc_ref[q:q + 1, k * LANES:(k + 1) * LANES] = c[k]
    return cin


def _conv(xa_ext, cw_ref, cb_ref, xc_ref, nb):
    spq = NSEG // nb
    row = lax.broadcasted_iota(jnp.int32, (SEG, 1), 0)
    for s in range(NSEG):
        acc = jnp.broadcast_to(cb_ref[...], (SEG, AW))
        for j in range(4):
            tap = xa_ext[pl.ds(s * SEG + SUBLANES - 2 + j, SEG), :]
            if nb > 1 and s % spq == 0 and j < 2:
                tap = jnp.where(row + (j - 2) >= 0, tap, 0.0)
            if nb > 1 and s % spq == spq - 1 and j == 3:
                tap = jnp.where(row + 1 < SEG, tap, 0.0)
            acc = acc + cw_ref[j:j + 1, :] * tap
        xc_ref[pl.ds(s * SEG, SEG), :] = acc


def _gla(reverse, nb, load_qv, k_scr, g_st, gtot, s_scr, stage, emit_out, fill, fill_plan):
    qg_s, kg_s, ke_s, att_s, kv_s, sin_s = stage
    cps = NSEG // nb
    order = range(NSEG - 1, -1, -1) if reverse else range(NSEG)
    r_i = lax.broadcasted_iota(jnp.int32, (SEG, SEG), 0)
    c_i = lax.broadcasted_iota(jnp.int32, (SEG, SEG), 1)
    keep = (c_i >= r_i) if reverse else (c_i <= r_i)
    tiles = [(n, k) for n in order for k in range(BH)]

    def spots(name):
        at = fill_plan.get(name, ())
        return lambda i: fill() if i in at else None

    hit = spots("a")
    for i, (n, k) in enumerate(tiles):
        hit(i)
        rows, lanes = pl.ds(n * SEG, SEG), slice(k * HP, (k + 1) * HP)
        g = g_st[k, pl.ds(n * PITCH, SEG), :]
        tot = gtot[k][n:n + 1, :]
        qv, _ = load_qv(n, k)
        kk = k_scr[rows, lanes]
        qg_s[rows, lanes] = (qv * jnp.exp(g)).astype(BF16)
        kg_s[rows, lanes] = (kk * jnp.exp(-g)).astype(BF16)
        ke_s[rows, lanes] = (kk * jnp.exp(tot - g)).astype(BF16)

    hit = spots("b")
    for i, (n, k) in enumerate(tiles):
        hit(i)
        rows, lanes = pl.ds(n * SEG, SEG), slice(k * HP, (k + 1) * HP)
        vb = load_qv(n, k, only_v=True)[1]
        att_s[k, rows, :] = jnp.where(keep, _nt(qg_s[rows, lanes], kg_s[rows, lanes]), 0.0).astype(BF16)
        kv_s[n, k] = _tn(vb, ke_s[rows, lanes])

    hit = spots("c")
    for k in range(BH):
        hit(k)
        st = None
        for n in order:
            sq = n // cps
            if st is None:
                st = s_scr[sq, k]
            sin_s[n, k] = st.astype(BF16)
            st = jnp.exp(gtot[k][n:n + 1, :]) * st + kv_s[n, k]
            last = (n % cps == 0) if reverse else (n % cps == cps - 1)
            if last:
                s_scr[sq, k] = st
                st = None

    hit = spots("d")
    for i, (n, k) in enumerate(tiles):
        hit(i)
        rows, lanes = pl.ds(n * SEG, SEG), slice(k * HP, (k + 1) * HP)
        o = (jnp.dot(att_s[k, rows, :], load_qv(n, k, only_v=True)[1], preferred_element_type=F32)
             + _nt(qg_s[rows, lanes], sin_s[n, k]))
        emit_out(n, k, o)


def _gla_scratch():
    return ([pltpu.VMEM((T, BW), BF16)] * 3
            + [pltpu.VMEM((BH, T, SEG), BF16), pltpu.VMEM((NSEG, BH, HP, HP), F32),
               pltpu.VMEM((NSEG, BH, HP, HP), BF16)])


class _Flat:
    def __init__(self, ref, width):
        self.ref = ref
        self.rs = ref.shape[1]
        self.width = width

    def _split(self, idx):
        rows, lanes = idx
        start, size = rows.start, rows.size
        q, off = divmod(start, self.rs)
        assert off + size <= self.rs
        return q, pl.ds(off, size), lanes

    def __getitem__(self, idx):
        rows, lanes = idx
        if rows.size == self.ref.shape[0] * self.rs:
            return self.ref[:, :, lanes].reshape(rows.size, -1)
        q, r, lanes = self._split(idx)
        return self.ref[q, r, lanes]

    def __setitem__(self, idx, val):
        q, r, lanes = self._split(idx)
        self.ref[q, r, lanes] = val


def _pipelined(flat, front, rest, bufs_a, bufs_b, first_front):
    @pl.when(flat == 0)
    def _():
        _Fill(first_front(*bufs_a)).flush()

    @pl.when(flat % 2 == 0)
    def _():
        fill = _Fill(front(*bufs_b))
        rest(*bufs_a, fill)
        fill.flush()

    @pl.when(flat % 2 == 1)
    def _():
        fill = _Fill(front(*bufs_a))
        rest(*bufs_b, fill)
        fill.flush()


def _kb_body(*refs, nb, nc, add_pos):
    it = iter(refs)
    x_ref, xn_ref = next(it), next(it)
    xh_ref = next(it) if nc > 1 else None
    if add_pos:
        rt_ref, rtn_ref, ct_ref, hp_ref = next(it), next(it), next(it), next(it)
    else:
        rt_ref = rtn_ref = ct_ref = hp_ref = None
    (mod_ref, modn_ref, nw_ref, wb_ref, cw_ref, cb_ref, wg_ref, bg_ref, cl_ref, lb_ref, h0_ref, s0_ref,
     hbf_ref, xc_ref, hb_ref, ob_ref, q_ref, v_ref, hfin_ref, sfin_ref,
     p_a, p_b, h_a, h_b, xa_ext, g_scr, k_scr, a_st, u_st, h_st, p_st, lf_st, g_st, hc_ref, s_scr, xaf,
     *gla_stage) = it

    ic = pl.program_id(1)
    flat = pl.program_id(0) * nc + ic
    cc = nc - 1 - ic

    @pl.when(ic == 0)
    def _():
        for q in range(nb):
            hc_ref[q:q + 1, :] = h0_ref[q]
        s_scr[...] = s0_ref[...]
        xaf[...] = jnp.zeros_like(xaf)

    def project(x_r, rt_r, m_r, p_dst, h_dst):
        x_flat = _Flat(x_r, D)

        def norm_piece(s):
            def run():
                rows = pl.ds(s * SEG, SEG)
                xin = x_flat[rows, slice(None)]
                if add_pos:
                    xin = xin + jnp.concatenate(
                        [jnp.broadcast_to(rt_r[s:s + 1, :], (SEG, D // 2)), ct_ref[...]], axis=1)
                h_dst[rows, :] = _norm_mod(xin, nw_ref[...], m_r[0, :, 0:D], m_r[0, :, D:2 * D]).astype(BF16)
            return run

        def dot_piece(c0, c1):
            def run():
                p_dst[:, c0:c1] = jnp.dot(h_dst[...], wb_ref[:, c0:c1], preferred_element_type=F32)
            return run

        return ([norm_piece(s) for s in range(NSEG)]
                + [dot_piece(c, min(c + FILL_COLS, KB_COLS)) for c in range(0, KB_COLS, FILL_COLS)])

    def rest(p_scr, h_cur, fill):
        hbf_ref[...] = h_cur[...].reshape(hbf_ref.shape)
        if nc > 1:
            xh = xh_ref[0]
            if add_pos:
                xh = xh + hp_ref[0]
            hh = _norm_mod(xh, nw_ref[...], mod_ref[0, :, 0:D], mod_ref[0, :, D:2 * D]).astype(BF16)
            left = jnp.dot(hh, wb_ref[:, 0:AW], preferred_element_type=F32)
            xa_ext[0:SUBLANES, :] = jnp.where(cc > 0, left, 0.0)
        else:
            xa_ext[0:SUBLANES, :] = jnp.zeros((SUBLANES, AW), F32)
        xa_ext[pl.ds(SUBLANES, T), :] = p_scr[:, 0:AW]
        xa_ext[pl.ds(SUBLANES + T, SUBLANES), :] = xaf[...]
        xaf[...] = p_scr[0:SUBLANES, 0:AW]
        xc_flat = _Flat(xc_ref, AW)
        _conv(xa_ext, cw_ref, cb_ref, xc_flat, nb)
        fill(NSEG // 2)
        g_scr[...] = jnp.dot(xc_flat[pl.ds(0, T), :].astype(BF16), wg_ref[...],
                             preferred_element_type=F32) + bg_ref[...]
        _a_coeffs(g_scr, xc_flat, cl_ref, a_st, u_st)
        quarter = tuple(range(0, SEG, SEG // 4))
        hend, pend = _a_scan(a_st, u_st, h_st, p_st, reverse=True, fill=fill, fill_at=quarter)
        cin = _a_carries(hend, pend, hc_ref, nb, reverse=True)
        hb_flat = _Flat(hb_ref, AW)
        for s in range(NSEG):
            for k in range(AW // LANES):
                st_rows = pl.ds(s * PITCH, SEG)
                hb_flat[pl.ds(s * SEG, SEG), k * LANES:(k + 1) * LANES] = (
                    h_st[k, st_rows, :] + p_st[k, st_rows, :] * cin[s][k])
        for q in range(nb):
            hfin_ref[q] = hc_ref[q:q + 1, :]

        for s in range(NSEG):
            if s % (NSEG // 2) == 0:
                fill()
            rows = pl.ds(s * SEG, SEG)
            for k in range(BH):
                lanes = slice(k * HP, (k + 1) * HP)
                lb = lb_ref[:, lanes]
                f = lb + (1.0 - lb) * _sigmoid(p_scr[rows, AW + BW + k * HP:AW + BW + (k + 1) * HP])
                k_scr[rows, lanes] = 1.0 - f
                lf_st[k, pl.ds(s * PITCH, SEG), :] = jnp.log(f)
        gtot = _cumsum_steps(lf_st, g_st, reverse=True, fill=fill, fill_at=(0,))
        q_flat, v_flat, ob_flat = _Flat(q_ref, BW), _Flat(v_ref, BW), _Flat(ob_ref, BW)

        def load_qv(n, k, only_v=False):
            rows, lanes = pl.ds(n * SEG, SEG), slice(k * HP, (k + 1) * HP)
            if only_v:
                return None, v_flat[rows, lanes]
            qv = _silu(p_scr[rows, AW + k * HP:AW + (k + 1) * HP])
            q_flat[rows, lanes] = qv
            vb = p_scr[rows, AW + 2 * BW + k * HP:AW + 2 * BW + (k + 1) * HP].astype(BF16)
            v_flat[rows, lanes] = vb
            return qv, vb

        def emit_out(n, k, o):
            ob_flat[pl.ds(n * SEG, SEG), slice(k * HP, (k + 1) * HP)] = o

        _gla(True, nb, load_qv, k_scr, g_st, gtot, s_scr, gla_stage, emit_out, fill,
             {"a": (0, 10, 20), "c": (0, 2)})
        sfin_ref[...] = s_scr[...]

    _pipelined(flat,
               functools.partial(project, xn_ref, rtn_ref, modn_ref),
               rest, (p_a, h_a), (p_b, h_b),
               functools.partial(project, x_ref, rt_ref, mod_ref))


def _kf_body(*refs, nb, nc, add_pos, final):
    it = iter(refs)
    x_ref = next(it)
    if add_pos:
        rt_ref, ct_ref = next(it), next(it)
    (hbf_ref, hbfn_ref, xc_ref, hb_ref, ob_ref, q_ref, v_ref, mod_ref, wf_ref, wg_ref, bg_ref, cl_ref, lb_ref,
     bnw_ref, cnw_ref, cws_ref, cbias_ref, wo_ref, fnw_ref, h0_ref, s0_ref,
     xo_ref, hfin_ref, sfin_ref,
     p_a, p_b, g_scr, k_scr, y_scr, part_scr, a_st, u_st, h_st, p_st, lf_st, g_st, hc_ref, s_scr,
     *gla_stage) = it

    ic = pl.program_id(1)
    flat = pl.program_id(0) * nc + ic
    gm = mod_ref[0, :, 2 * D:3 * D]

    @pl.when(ic == 0)
    def _():
        for q in range(nb):
            hc_ref[q:q + 1, :] = h0_ref[q]
        s_scr[...] = s0_ref[...]

    o_ag, o_ff, o_bg = 0, AW, AW + BW
    o_cu, o_cv, o_cg = AW + 2 * BW, AW + 2 * BW + CW, AW + 2 * BW + 2 * CW

    def project(h_r, p_dst):
        def dot_piece(c0, c1):
            def run():
                p_dst[:, c0:c1] = jnp.dot(h_r[...].reshape(T, D), wf_ref[:, c0:c1], preferred_element_type=F32)
            return run
        return [dot_piece(c, min(c + FILL_COLS, KF_COLS)) for c in range(0, KF_COLS, FILL_COLS)]

    def rest(p_scr, fill):
        xc_flat, hb_flat = _Flat(xc_ref, AW), _Flat(hb_ref, AW)
        g_scr[...] = jnp.dot(xc_flat[pl.ds(0, T), :].astype(BF16), wg_ref[...],
                             preferred_element_type=F32) + bg_ref[...]
        _a_coeffs(g_scr, xc_flat, cl_ref, a_st, u_st, fill=fill, fill_at=(0, NSEG // 2))
        hend, pend = _a_scan(a_st, u_st, h_st, p_st, reverse=False, fill=fill, fill_at=(0, SEG // 2))
        cin = _a_carries(hend, pend, hc_ref, nb, reverse=False)
        fill()
        for s in range(NSEG):
            rows = pl.ds(s * SEG, SEG)
            for k in range(AW // LANES):
                lanes = slice(k * LANES, (k + 1) * LANES)
                st_rows = pl.ds(s * PITCH, SEG)
                hf = h_st[k, st_rows, :] + p_st[k, st_rows, :] * cin[s][k]
                ya = (hf + hb_flat[rows, lanes]) * _silu(p_scr[rows, o_ag + k * LANES:o_ag + (k + 1) * LANES])
                y_scr[rows, lanes] = ya.astype(BF16)
        for q in range(nb):
            hfin_ref[q] = hc_ref[q:q + 1, :]

        for s in range(NSEG):
            rows = pl.ds(s * SEG, SEG)
            for k in range(BH):
                lanes = slice(k * HP, (k + 1) * HP)
                lb = lb_ref[:, lanes]
                f = lb + (1.0 - lb) * _sigmoid(p_scr[rows, o_ff + k * HP:o_ff + (k + 1) * HP])
                k_scr[rows, lanes] = 1.0 - f
                lf_st[k, pl.ds(s * PITCH, SEG), :] = jnp.log(f)
        gtot = _cumsum_steps(lf_st, g_st, reverse=False, fill=fill, fill_at=(0,))

        lane = lax.broadcasted_iota(jnp.int32, (CCH, CW), 1)
        for m in range(T // CCH):
            if m % 2 == 0:
                fill()
            rows = pl.ds(m * CCH, CCH)
            u = _gelu(p_scr[rows, o_cu:o_cu + CW])
            v = _gelu(p_scr[rows, o_cv:o_cv + CW])
            mu = jnp.mean(v, axis=-1, keepdims=True)
            dv = v - mu
            var = jnp.mean(dv * dv, axis=-1, keepdims=True)
            vn = (dv * lax.rsqrt(var + EPS) * cnw_ref[...]).astype(BF16)
            full = jnp.dot(cws_ref[...], vn, preferred_element_type=F32)
            mixed = full[0:CCH]
            for h in range(1, CH):
                mixed = jnp.where(lane >= h * CHD, full[h * CCH:(h + 1) * CCH], mixed)
            yc = u * (mixed + cbias_ref[...]) * _silu(p_scr[rows, o_cg:o_cg + CW])
            y_scr[rows, AW:AW + CW] = yc.astype(BF16)

        def out_piece(c0, c1):
            def run():
                part_scr[:, c0:c1] = jnp.dot(y_scr[:, 0:AW + CW], wo_ref[0:AW + CW, c0:c1],
                                             preferred_element_type=F32)
            return run
        fill.thunks.extend(out_piece(c, c + FILL_COLS) for c in range(0, D, FILL_COLS))

        q_flat, v_flat, ob_flat = _Flat(q_ref, BW), _Flat(v_ref, BW), _Flat(ob_ref, BW)

        def load_qv(n, k, only_v=False):
            rows, lanes = pl.ds(n * SEG, SEG), slice(k * HP, (k + 1) * HP)
            return (None if only_v else q_flat[rows, lanes]), v_flat[rows, lanes]

        def emit_out(n, k, o):
            rows, lanes = pl.ds(n * SEG, SEG), slice(k * HP, (k + 1) * HP)
            o = o + ob_flat[rows, lanes]
            ms = jnp.sum(o * o, axis=-1, keepdims=True) * (1.0 / BDK)
            yb = ((o * lax.rsqrt(ms + EPS) * bnw_ref[:, lanes])
                  * _silu(p_scr[rows, o_bg + k * HP:o_bg + (k + 1) * HP]))
            y_scr[rows, AW + CW + k * HP:AW + CW + (k + 1) * HP] = yb.astype(BF16)

        _gla(False, nb, load_qv, k_scr, g_st, gtot, s_scr, gla_stage, emit_out, fill,
             {"a": (0, 10, 20), "c": (0, 2)})
        sfin_ref[...] = s_scr[...]
        fill.flush()

        xin = x_ref[...].reshape(T, D)
        if add_pos:
            xin = xin + _pos_block(rt_ref, ct_ref)
        out = part_scr[...] + jnp.dot(y_scr[:, AW + CW:YW], wo_ref[AW + CW:YW, :], preferred_element_type=F32)
        xn = xin + gm * out
        if final:
            ms = jnp.mean(xn * xn, axis=-1, keepdims=True)
            xn = xn * lax.rsqrt(ms + EPS) * fnw_ref[...]
        xo_ref[...] = xn.reshape(xo_ref.shape)

    _pipelined(flat, functools.partial(project, hbfn_ref), rest, (p_a,), (p_b,),
               functools.partial(project, hbf_ref))


def _mod_body(cs_ref, w_ref, b_ref, o_ref):
    s = _silu(cs_ref[...])
    o_ref[0] = jnp.dot(s, w_ref[0], preferred_element_type=F32, precision=lax.Precision.HIGHEST) + b_ref[0]


def _const_spec(shape):
    nd = len(shape)
    return pl.BlockSpec(shape, lambda ib, ic: (0,) * nd, pipeline_mode=pl.Buffered(1))


def _seq_spec(nb, rs, width, bc):
    return pl.BlockSpec((nb, rs, width), lambda ib, ic: bc(ib, ic) + (0,))


def _step_maps(ngroups, nc, reverse):
    def cur(ib, ic):
        return ib, (nc - 1 - ic) if reverse else ic

    def nxt(ib, ic):
        f = jnp.minimum(ib * nc + ic + 1, ngroups * nc - 1)
        return cur(f // nc, f % nc)
    return cur, nxt


def _scan_scratch():
    na = AW // LANES
    return ([pltpu.VMEM((na, NSEG * PITCH, LANES), F32)] * 4
            + [pltpu.VMEM((BH, NSEG * PITCH, LANES), F32)] * 2)


def _params():
    return pltpu.CompilerParams(dimension_semantics=("arbitrary", "arbitrary"), vmem_limit_bytes=VMEM_LIMIT)


def _state_specs(nb):
    return [pl.BlockSpec((nb, 1, AW), lambda ib, ic: (ib, 0, 0)),
            pl.BlockSpec((nb, BH, HP, HP), lambda ib, ic: (ib, 0, 0, 0))]


def _run_kb(x, mod, mod_row, lw, h0, s0, pos, nb):
    bsz, seq, _ = x.shape
    rs = T // nb
    nc = seq // rs
    add_pos = pos is not None
    cur, nxt = _step_maps(bsz // nb, nc, reverse=True)
    mrow = (lambda g: g) if mod_row is None else (lambda g: mod_row)
    mod_spec = lambda bc: pl.BlockSpec((1, 1, 3 * D), lambda ib, ic: (mrow(bc(ib, ic)[0]), 0, 0))
    ins, specs = [x, x], [_seq_spec(nb, rs, D, cur), _seq_spec(nb, rs, D, nxt)]
    if nc > 1:
        ins.append(x)
        specs.append(pl.BlockSpec(
            (1, SUBLANES, D), lambda ib, ic: (ib, jnp.maximum(cur(ib, ic)[1] * (rs // SUBLANES) - 1, 0), 0)))
    if add_pos:
        rowtab, coltab, hpos = pos
        ins += [rowtab, rowtab, coltab, hpos]
        specs += [pl.BlockSpec((NSEG, D // 2), lambda ib, ic: (cur(ib, ic)[1], 0)),
                  pl.BlockSpec((NSEG, D // 2), lambda ib, ic: (nxt(ib, ic)[1], 0)),
                  _const_spec(coltab.shape),
                  pl.BlockSpec((1, SUBLANES, D), lambda ib, ic: (cur(ib, ic)[1], 0, 0))]
    names = ("nw", "wb", "cw", "cb", "wg_b", "bg_b", "cl_b", "lb")
    ins += [mod, mod] + [lw[n] for n in names] + [h0, s0]
    specs += [mod_spec(cur), mod_spec(nxt)] + [_const_spec(lw[n].shape) for n in names] + _state_specs(nb)
    out_shape = (jax.ShapeDtypeStruct((bsz, seq, D), BF16), jax.ShapeDtypeStruct((bsz, seq, AW), F32),
                 jax.ShapeDtypeStruct((bsz, seq, AW), F32), jax.ShapeDtypeStruct((bsz, seq, BW), F32),
                 jax.ShapeDtypeStruct((bsz, seq, BW), F32), jax.ShapeDtypeStruct((bsz, seq, BW), BF16),
                 jax.ShapeDtypeStruct((bsz, 1, AW), F32), jax.ShapeDtypeStruct((bsz, BH, HP, HP), F32))
    out_specs = tuple([_seq_spec(nb, rs, w, cur) for w in (D, AW, AW, BW, BW, BW)] + _state_specs(nb))
    scratch = ([pltpu.VMEM((T, KB_COLS), F32)] * 2 + [pltpu.VMEM((T, D), BF16)] * 2
               + [pltpu.VMEM((T + 2 * SUBLANES, AW), F32), pltpu.VMEM((T, 2 * AW), F32), pltpu.VMEM((T, BW), F32)]
               + _scan_scratch()
               + [pltpu.VMEM((SUBLANES, AW), F32), pltpu.VMEM((nb, BH, HP, HP), F32),
                  pltpu.VMEM((SUBLANES, AW), F32)]
               + _gla_scratch())
    return pl.pallas_call(
        functools.partial(_kb_body, nb=nb, nc=nc, add_pos=add_pos),
        grid=(bsz // nb, nc), in_specs=specs, out_specs=out_specs, out_shape=out_shape,
        scratch_shapes=scratch, compiler_params=_params(), name="bwd_sweep")(*ins)


def _run_kf(x, kb_out, mod, mod_row, lw, fnw, h0, s0, pos, nb, final):
    bsz, seq, _ = x.shape
    rs = T // nb
    nc = seq // rs
    add_pos = pos is not None
    cur, nxt = _step_maps(bsz // nb, nc, reverse=False)
    mrow = (lambda g: g) if mod_row is None else (lambda g: mod_row)
    hbf, xc, hb, ob, q, v = kb_out
    ins, specs = [x], [_seq_spec(nb, rs, D, cur)]
    if add_pos:
        rowtab, coltab, _ = pos
        ins += [rowtab, coltab]
        specs += [pl.BlockSpec((NSEG, D // 2), lambda ib, ic: (cur(ib, ic)[1], 0)), _const_spec(coltab.shape)]
    names = ("wf", "wg_f", "bg_f", "cl_f", "lb", "bnw", "cnw", "cws", "cbias", "wo")
    ins += [hbf, hbf, xc, hb, ob, q, v, mod] + [lw[n] for n in names] + [fnw, h0, s0]
    specs += [_seq_spec(nb, rs, D, cur), _seq_spec(nb, rs, D, nxt)]
    specs += [_seq_spec(nb, rs, w, cur) for w in (AW, AW, BW, BW, BW)]
    specs += [pl.BlockSpec((1, 1, 3 * D), lambda ib, ic: (mrow(ib), 0, 0))]
    specs += [_const_spec(lw[n].shape) for n in names] + [_const_spec(fnw.shape)] + _state_specs(nb)
    out_shape = (jax.ShapeDtypeStruct((bsz, seq, D), F32),
                 jax.ShapeDtypeStruct((bsz, 1, AW), F32), jax.ShapeDtypeStruct((bsz, BH, HP, HP), F32))
    out_specs = tuple([_seq_spec(nb, rs, D, cur)] + _state_specs(nb))
    scratch = ([pltpu.VMEM((T, KF_COLS), F32)] * 2
               + [pltpu.VMEM((T, 2 * AW), F32), pltpu.VMEM((T, BW), F32), pltpu.VMEM((T, YW), BF16),
                  pltpu.VMEM((T, D), F32)]
               + _scan_scratch()
               + [pltpu.VMEM((SUBLANES, AW), F32), pltpu.VMEM((nb, BH, HP, HP), F32)]
               + _gla_scratch())
    return pl.pallas_call(
        functools.partial(_kf_body, nb=nb, nc=nc, add_pos=add_pos, final=final),
        grid=(bsz // nb, nc), in_specs=specs, out_specs=out_specs, out_shape=out_shape,
        scratch_shapes=scratch, compiler_params=_params(), name="fwd_sweep")(*ins)


def _pad_heads(w, fill=0.0):
    lead = w.shape[:-1]
    w = w.reshape(lead + (BH, BDK))
    w = jnp.pad(w, [(0, 0)] * len(lead) + [(0, 0), (0, HP - BDK)], constant_values=fill)
    return w.reshape(lead + (BW,))


def _block_diag(w):
    eye = jnp.eye(A_HEADS, dtype=w.dtype)
    return jnp.einsum("hij,hg->higj", w, eye).reshape(AW, AW)


def _layer_weights(l, norm_w, w_in, a_conv_w, a_conv_b, a_wr, a_br, a_wi, a_bi, a_lambda, lbs, b_norm_w,
                   c_norm_w, c_ws, c_bs, w_out):
    cuts = np.cumsum((AW, AW, 384, 384, 384, 384, 384, CW, CW, CW))[:-1].tolist()
    ax, ag, bq, bff, bfb, bi, bg, cu, cv, cg = jnp.split(w_in[l], cuts, axis=-1)
    wo_a, wo_b, wo_c = jnp.split(w_out[l], [AW, AW + 384], axis=0)
    wo_b = _pad_heads(wo_b.T).T
    lw = {
        "nw": norm_w[l][None, :],
        "wb": jnp.concatenate([ax, _pad_heads(bq), _pad_heads(bfb), _pad_heads(bi)], axis=1).astype(BF16),
        "wf": jnp.concatenate([ag, _pad_heads(bff), _pad_heads(bg), cu, cv, cg], axis=1).astype(BF16),
        "cw": a_conv_w[l], "cb": a_conv_b[l][None, :],
        "lb": _pad_heads(lbs[l], fill=1.0)[None, :],
        "bnw": _pad_heads(b_norm_w[l])[None, :],
        "cnw": c_norm_w[l][None, :],
        "cws": c_ws[l].reshape(CH * CCH, CCH).astype(BF16),
        "cbias": jnp.repeat(c_bs[l].T, CHD, axis=1),
        "wo": jnp.concatenate([wo_a, wo_c, wo_b], axis=0).astype(BF16),
    }
    for d, name in ((0, "f"), (1, "b")):
        lw["wg_" + name] = jnp.concatenate([_block_diag(a_wr[l, d]), _block_diag(a_wi[l, d])], axis=1).astype(BF16)
        lw["bg_" + name] = jnp.concatenate([a_br[l, d], a_bi[l, d]])[None, :]
        lw["cl_" + name] = (-RG_C * jax.nn.softplus(-a_lambda[l, d]))[None, :]
    return lw


def _pos_tables(rows):
    qd = D // 4
    omega = 1.0 / (10000.0 ** (jnp.arange(qd, dtype=F32) / qd))
    ar = jnp.arange(rows, dtype=F32)[:, None] * omega
    ac = jnp.arange(GRID_W, dtype=F32)[:, None] * omega
    rowtab = jnp.concatenate([jnp.sin(ar), jnp.cos(ar)], axis=-1)
    coltab = jnp.concatenate([jnp.sin(ac), jnp.cos(ac)], axis=-1)
    nc = rows * GRID_W // T
    hrow = jnp.broadcast_to(rowtab[jnp.maximum(jnp.arange(nc) * NSEG - 1, 0)][:, None, :], (nc, SUBLANES, D // 2))
    hcol = jnp.broadcast_to(coltab[GRID_W - SUBLANES:][None], (nc, SUBLANES, D // 2))
    return rowtab, coltab, jnp.concatenate([hrow, hcol], axis=-1)


def kernel(x, c, ctx, c_ctx, norm_w, w_mod, b_mod, w_in, a_conv_w, a_conv_b, a_wr, a_br, a_wi, a_bi, a_lambda,
           b_lb_logits, b_norm_w, c_norm_w, c_ws, c_bs, w_out, final_norm_w):
    bsz, n_lat, _ = x.shape
    depth = w_in.shape[0]
    assert n_lat % T == 0 and T % ctx.shape[1] == 0 and bsz % (T // ctx.shape[1]) == 0
    nb_ctx = T // ctx.shape[1]
    pos = _pos_tables(n_lat // GRID_W)

    mrows = 2 * SUBLANES
    cs = jnp.zeros((mrows, D), F32).at[:bsz].set(c).at[bsz].set(c_ctx)
    nblk = 3
    mod = pl.pallas_call(
        _mod_body, grid=(depth, nblk),
        in_specs=[pl.BlockSpec((mrows, D), lambda l, j: (0, 0)),
                  pl.BlockSpec((1, D, D), lambda l, j: (l, 0, j)),
                  pl.BlockSpec((1, 1, D), lambda l, j: (l, 0, j))],
        out_specs=pl.BlockSpec((1, mrows, D), lambda l, j: (l, 0, j)),
        out_shape=jax.ShapeDtypeStruct((depth, mrows, 3 * D), F32),
        name="modulation")(cs, w_mod, b_mod[:, None, :])
    mod = mod.reshape(depth, mrows, 1, 3 * D)

    lbs = jnp.cumsum(jax.nn.softmax(b_lb_logits, axis=0), axis=0)
    fnw = final_norm_w[None, :]
    h_zero = jnp.zeros((bsz, 1, AW), F32)
    s_zero = jnp.zeros((bsz, BH, HP, HP), F32)
    for l in range(depth):
        lw = _layer_weights(l, norm_w, w_in, a_conv_w, a_conv_b, a_wr, a_br, a_wi, a_bi, a_lambda, lbs, b_norm_w,
                            c_norm_w, c_ws, c_bs, w_out)
        kb = _run_kb(ctx, mod[l], bsz, lw, h_zero, s_zero, None, nb_ctx)
        ctx_new, hf_c, sf_c = _run_kf(ctx, kb[:6], mod[l], bsz, lw, fnw, h_zero, s_zero, None, nb_ctx, False)
        hb_c, sb_c = kb[6], kb[7]
        kb = _run_kb(x, mod[l], None, lw, hb_c, sb_c, pos if l == 0 else None, 1)
        x, _, _ = _run_kf(x, kb[:6], mod[l], None, lw, fnw, hf_c, sf_c, pos if l == 0 else None, 1, l == depth - 1)
        ctx = ctx_new
    return x
```

```python
import functools

import jax
import jax.numpy as jnp
import numpy as np
from jax import lax
from jax.experimental import pallas as pl
from jax.experimental.pallas import tpu as pltpu

D = 1024
GRID_W = 64
EPS = 1e-6
RG_C = 8.0
AW = 384
A_HEADS = 8
A_HD = 48
BH = 4
BDK = 96
CW = 256
CH = 4
CHD = 64
CCH = 128

LANES = 128
SUBLANES = 8
HP = LANES
BW = BH * HP
YW = AW + BW + CW
T = 512
SEG = 64
NSEG = T // SEG
PITCH = SEG + SUBLANES
KB_COLS = AW + 3 * BW
KF_COLS = AW + 2 * BW + 3 * CW
FILL_COLS = 256
FILL_ROWS = T // 2
VMEM_LIMIT = 58 * 1024 * 1024

F32 = jnp.float32
BF16 = jnp.bfloat16


def _sigmoid(x):
    return 0.5 * jnp.tanh(0.5 * x) + 0.5


def _silu(x):
    return x * _sigmoid(x)


def _gelu(x):
    return 0.5 * x * (1.0 + jnp.tanh(np.sqrt(2.0 / np.pi).astype(np.float32) * (x + 0.044715 * (x * x * x))))


def _nt(a, b):
    return lax.dot_general(a, b, (((1,), (1,)), ((), ())), preferred_element_type=F32)


def _tn(a, b):
    return lax.dot_general(a, b, (((0,), (0,)), ((), ())), preferred_element_type=F32)


def _norm_mod(x, nw, sh, sc):
    ms = jnp.mean(x * x, axis=-1, keepdims=True)
    return (x * lax.rsqrt(ms + EPS) * nw) * (1.0 + sc) + sh


def _pos_rows(rt_ref, ct_ref, s):
    return jnp.concatenate([jnp.broadcast_to(rt_ref[s:s + 1, :], (SEG, D // 2)), ct_ref[...]], axis=1)


class _Fill:
    def __init__(self, pieces, rate):
        self.pieces = list(pieces)
        self.rate = rate
        self.credit = 0.0
        self.spent = 0.0

    def __call__(self, credit):
        self.credit += credit
        while self.pieces and self.spent + 0.5 * self.pieces[0][0] <= self.credit * self.rate:
            cost, thunk = self.pieces.pop(0)
            self.spent += cost
            thunk()

    def extend(self, pieces):
        self.pieces.extend(pieces)

    def flush(self):
        while self.pieces:
            self.pieces.pop(0)[1]()


class _Fills:
    def __init__(self, mxu_pieces, mxu_rate, vpu_pieces=(), vpu_rate=0.0):
        self.mxu = _Fill(mxu_pieces, mxu_rate)
        self.vpu = _Fill(vpu_pieces, vpu_rate)

    def after_vpu(self, cycles):
        self.mxu(cycles)

    def after_mxu(self, cycles):
        self.vpu(cycles)

    def flush(self):
        self.vpu.flush()
        self.mxu.flush()


C_CONV_SEG, C_COEFF_TILE, C_SCAN_STEP, C_FIX_TILE, C_YA_TILE = 100, 50, 12, 8, 25
C_PASS1_TILE, C_CUMSUM_STEP, C_GLA_A_TILE, C_GLA_C_TILE, C_SILU_TILE, C_GLA_EPI_TILE = 20, 8, 40, 15, 12, 30
C_GATE_DOT, C_GLA_B_TILE, C_GLA_D_TILE = 384, 40, 40
C_PROJ_PIECE, C_OUT_PIECE, C_NORM_PIECE = 256, 192, 130


def _a_scan(a_st, u_st, h_st, p_st, reverse, fills):
    n = AW // LANES
    hs = [jnp.zeros((NSEG, LANES), F32)] * n
    ps = [jnp.ones((NSEG, LANES), F32)] * n
    for jj in range(SEG):
        j = (SEG - 1 - jj) if reverse else jj
        for k in range(n):
            a = a_st[k, pl.ds(j, NSEG, stride=PITCH), :]
            u = u_st[k, pl.ds(j, NSEG, stride=PITCH), :]
            hs[k] = a * hs[k] + u
            ps[k] = a * ps[k]
            h_st[k, pl.ds(j, NSEG, stride=PITCH), :] = hs[k]
            p_st[k, pl.ds(j, NSEG, stride=PITCH), :] = ps[k]
        fills.after_vpu(C_SCAN_STEP)
    return hs, ps


def _cumsum_steps(lf_st, g_st, reverse, fills):
    gs = [jnp.zeros((NSEG, LANES), F32)] * BH
    for jj in range(SEG):
        j = (SEG - 1 - jj) if reverse else jj
        for k in range(BH):
            gs[k] = gs[k] + lf_st[k, pl.ds(j, NSEG, stride=PITCH), :]
            g_st[k, pl.ds(j, NSEG, stride=PITCH), :] = gs[k]
        fills.after_vpu(C_CUMSUM_STEP)
    return gs


def _a_coeffs(g_scr, xc_ref, cl_ref, a_st, u_st, fills):
    for s in range(NSEG):
        rows = pl.ds(s * SEG, SEG)
        for k in range(AW // LANES):
            lanes = slice(k * LANES, (k + 1) * LANES)
            r = _sigmoid(g_scr[rows, lanes])
            gi = _sigmoid(g_scr[rows, AW + k * LANES:AW + (k + 1) * LANES])
            la = r * cl_ref[:, lanes]
            th = jnp.tanh(la)
            om = (-2.0 * th) / (1.0 - th)
            a_st[k, pl.ds(s * PITCH, SEG), :] = jnp.exp(la)
            u_st[k, pl.ds(s * PITCH, SEG), :] = jnp.sqrt(om) * (gi * xc_ref[rows, lanes])
            fills.after_vpu(C_COEFF_TILE)


def _a_carries(hend, pend, hc_ref, nb, reverse):
    spq = NSEG // nb
    cin = [None] * NSEG
    n = AW // LANES
    for q in range(nb):
        c = [hc_ref[q:q + 1, k * LANES:(k + 1) * LANES] for k in range(n)]
        order = range((q + 1) * spq - 1, q * spq - 1, -1) if reverse else range(q * spq, (q + 1) * spq)
        for s in order:
            cin[s] = c
            c = [pend[k][s:s + 1, :] * c[k] + hend[k][s:s + 1, :] for k in range(n)]
        for k in range(n):
            hc_ref[q:q + 1, k * LANES:(k + 1) * LANES] = c[k]
    return cin


def _conv(xa_ext, cw_ref, cb_ref, xc_ref, nb, fills):
    spq = NSEG // nb
    row = lax.broadcasted_iota(jnp.int32, (SEG, 1), 0)
    for s in range(NSEG):
        acc = jnp.broadcast_to(cb_ref[...], (SEG, AW))
        for j in range(4):
            tap = xa_ext[pl.ds(s * SEG + SUBLANES - 2 + j, SEG), :]
            if nb > 1 and s % spq == 0 and j < 2:
                tap = jnp.where(row + (j - 2) >= 0, tap, 0.0)
            if nb > 1 and s % spq == spq - 1 and j == 3:
                tap = jnp.where(row + 1 < SEG, tap, 0.0)
            acc = acc + cw_ref[j:j + 1, :] * tap
        xc_ref[pl.ds(s * SEG, SEG), :] = acc
        fills.after_vpu(C_CONV_SEG)


def _forget_gates(p_scr, col0, lb_ref, k_scr, lf_st, fills):
    for s in range(NSEG):
        rows = pl.ds(s * SEG, SEG)
        for k in range(BH):
            lanes = slice(k * HP, (k + 1) * HP)
            lb = lb_ref[:, lanes]
            f = lb + (1.0 - lb) * _sigmoid(p_scr[rows, col0 + k * HP:col0 + (k + 1) * HP])
            k_scr[rows, lanes] = 1.0 - f
            lf_st[k, pl.ds(s * PITCH, SEG), :] = jnp.log(f)
            fills.after_vpu(C_PASS1_TILE)


def _gla(reverse, nb, load_q, load_v, k_scr, g_st, gtot, s_scr, stage, emit_out, fills):
    qg_s, kg_s, ke_s, att_s, kv_s, sin_s = stage
    cps = NSEG // nb
    order = range(NSEG - 1, -1, -1) if reverse else range(NSEG)
    r_i = lax.broadcasted_iota(jnp.int32, (SEG, SEG), 0)
    c_i = lax.broadcasted_iota(jnp.int32, (SEG, SEG), 1)
    keep = (c_i >= r_i) if reverse else (c_i <= r_i)
    tiles = [(n, k) for n in order for k in range(BH)]

    for n, k in tiles:
        rows, lanes = pl.ds(n * SEG, SEG), slice(k * HP, (k + 1) * HP)
        g = g_st[k, pl.ds(n * PITCH, SEG), :]
        tot = gtot[k][n:n + 1, :]
        qv = load_q(n, k)
        kk = k_scr[rows, lanes]
        qg_s[rows, lanes] = (qv * jnp.exp(g)).astype(BF16)
        kg_s[rows, lanes] = (kk * jnp.exp(-g)).astype(BF16)
        ke_s[rows, lanes] = (kk * jnp.exp(tot - g)).astype(BF16)
        fills.after_vpu(C_GLA_A_TILE)

    for n, k in tiles:
        rows, lanes = pl.ds(n * SEG, SEG), slice(k * HP, (k + 1) * HP)
        att_s[k, rows, :] = jnp.where(keep, _nt(qg_s[rows, lanes], kg_s[rows, lanes]), 0.0).astype(BF16)
        kv_s[n, k] = _tn(load_v(n, k), ke_s[rows, lanes])
        fills.after_mxu(C_GLA_B_TILE)

    for k in range(BH):
        st = None
        for n in order:
            sq = n // cps
            if st is None:
                st = s_scr[sq, k]
            sin_s[n, k] = st.astype(BF16)
            st = jnp.exp(gtot[k][n:n + 1, :]) * st + kv_s[n, k]
            last = (n % cps == 0) if reverse else (n % cps == cps - 1)
            if last:
                s_scr[sq, k] = st
                st = None
            fills.after_vpu(C_GLA_C_TILE)

    for n, k in tiles:
        rows, lanes = pl.ds(n * SEG, SEG), slice(k * HP, (k + 1) * HP)
        o = (jnp.dot(att_s[k, rows, :], load_v(n, k), preferred_element_type=F32)
             + _nt(qg_s[rows, lanes], sin_s[n, k]))
        emit_out(n, k, o)
        fills.after_mxu(C_GLA_D_TILE)


def _gla_scratch():
    return ([pltpu.VMEM((T, BW), BF16)] * 3
            + [pltpu.VMEM((BH, T, SEG), BF16), pltpu.VMEM((NSEG, BH, HP, HP), F32),
               pltpu.VMEM((NSEG, BH, HP, HP), BF16)])


class _Flat:
    def __init__(self, ref, width):
        self.ref = ref
        self.rs = ref.shape[1]
        self.width = width

    def _split(self, idx):
        rows, lanes = idx
        start, size = rows.start, rows.size
        q, off = divmod(start, self.rs)
        assert off + size <= self.rs
        return q, pl.ds(off, size), lanes

    def __getitem__(self, idx):
        rows, lanes = idx
        if rows.size == self.ref.shape[0] * self.rs:
            return self.ref[:, :, lanes].reshape(rows.size, -1)
        q, r, lanes = self._split(idx)
        return self.ref[q, r, lanes]

    def __setitem__(self, idx, val):
        q, r, lanes = self._split(idx)
        self.ref[q, r, lanes] = val


def _dot_pieces(lhs_rows, w_ref, dst, col_lo, col_hi, cost):
    def piece(r0, c0, c1):
        def run():
            dst[r0:r0 + FILL_ROWS, c0:c1] = jnp.dot(lhs_rows(r0, r0 + FILL_ROWS), w_ref[:, c0:c1],
                                                    preferred_element_type=F32)
        return cost, run
    return [piece(r0, c0, min(c0 + FILL_COLS, col_hi))
            for c0 in range(col_lo, col_hi, FILL_COLS) for r0 in range(0, T, FILL_ROWS)]


def _kb_body(*refs, nb, nc, add_pos):
    it = iter(refs)
    x0_ref, xn_ref = next(it), next(it)
    xh_ref = next(it) if nc > 1 else None
    if add_pos:
        rt0_ref, rtn_ref, ct_ref, hp_ref = next(it), next(it), next(it), next(it)
    else:
        rt0_ref = rtn_ref = ct_ref = hp_ref = None
    (mod_ref, modn_ref, nw_ref, wb_ref, cw_ref, cb_ref, wg_ref, bg_ref, cl_ref, lb_ref, h0_ref, s0_ref,
     hbf_ref, xc_ref, hb_ref, ob_ref, q_ref, v_ref, hfin_ref, sfin_ref,
     p_scr, h_scr, xa_ext, g_scr, k_scr, a_st, u_st, h_st, p_st, lf_st, g_st, hc_ref, s_scr, xaf,
     *gla_stage) = it

    ic = pl.program_id(1)
    flat = pl.program_id(0) * nc + ic
    cc = nc - 1 - ic

    @pl.when(ic == 0)
    def _():
        for q in range(nb):
            hc_ref[q:q + 1, :] = h0_ref[q]
        s_scr[...] = s0_ref[...]
        xaf[...] = jnp.zeros_like(xaf)

    def norm_pieces(x_r, rt_r, m_r, h_dst):
        x_flat = _Flat(x_r, D)

        def piece(s):
            def run():
                rows = pl.ds(s * SEG, SEG)
                xin = x_flat[rows, slice(None)]
                if add_pos:
                    xin = xin + _pos_rows(rt_r, ct_ref, s)
                h_dst[rows, :] = _norm_mod(xin, nw_ref[...], m_r[0, :, 0:D], m_r[0, :, D:2 * D]).astype(BF16)
            return C_NORM_PIECE, run
        return [piece(s) for s in range(NSEG)]

    def proj_pieces(h_src, p_dst):
        return _dot_pieces(lambda r0, r1: h_src[r0:r1, :], wb_ref, p_dst, 0, KB_COLS, C_PROJ_PIECE)

    @pl.when(flat == 0)
    def _():
        for _, run in norm_pieces(x0_ref, rt0_ref, mod_ref, h_scr) + proj_pieces(h_scr, p_scr):
            run()

    def step():
        hbf_ref[...] = h_scr[...].reshape(hbf_ref.shape)
        q_flat, v_flat, ob_flat = _Flat(q_ref, BW), _Flat(v_ref, BW), _Flat(ob_ref, BW)
        nofill = _Fills((), 0.0)

        if nc > 1:
            xh = xh_ref[0]
            if add_pos:
                xh = xh + hp_ref[0]
            hh = _norm_mod(xh, nw_ref[...], mod_ref[0, :, 0:D], mod_ref[0, :, D:2 * D]).astype(BF16)
            left = jnp.dot(hh, wb_ref[:, 0:AW], preferred_element_type=F32)
            xa_ext[0:SUBLANES, :] = jnp.where(cc > 0, left, 0.0)
        else:
            xa_ext[0:SUBLANES, :] = jnp.zeros((SUBLANES, AW), F32)
        xa_ext[pl.ds(SUBLANES, T), :] = p_scr[:, 0:AW]
        xa_ext[pl.ds(SUBLANES + T, SUBLANES), :] = xaf[...]
        xaf[...] = p_scr[0:SUBLANES, 0:AW]

        for s in range(NSEG):
            rows = pl.ds(s * SEG, SEG)
            for k in range(BH):
                lanes = slice(k * HP, (k + 1) * HP)
                q_flat[rows, lanes] = _silu(p_scr[rows, AW + k * HP:AW + (k + 1) * HP])
                v_flat[rows, lanes] = p_scr[rows, AW + 2 * BW + k * HP:AW + 2 * BW + (k + 1) * HP].astype(BF16)
        _forget_gates(p_scr, AW + BW, lb_ref, k_scr, lf_st, nofill)

        for _, run in norm_pieces(xn_ref, rtn_ref, modn_ref, h_scr):
            run()
        mxu_pieces = proj_pieces(h_scr, p_scr)
        vpu_credit = NSEG * C_CONV_SEG + (AW // LANES) * NSEG * (C_COEFF_TILE + C_FIX_TILE) \
            + SEG * (C_SCAN_STEP + C_CUMSUM_STEP) + BH * NSEG * (C_GLA_A_TILE + C_GLA_C_TILE)
        fills = _Fills(mxu_pieces, len(mxu_pieces) * C_PROJ_PIECE / vpu_credit)

        xc_flat = _Flat(xc_ref, AW)
        _conv(xa_ext, cw_ref, cb_ref, xc_flat, nb, fills)
        g_scr[...] = jnp.dot(xc_flat[pl.ds(0, T), :].astype(BF16), wg_ref[...],
                             preferred_element_type=F32) + bg_ref[...]
        fills.after_mxu(C_GATE_DOT)
        _a_coeffs(g_scr, xc_flat, cl_ref, a_st, u_st, fills)
        hend, pend = _a_scan(a_st, u_st, h_st, p_st, True, fills)
        cin = _a_carries(hend, pend, hc_ref, nb, reverse=True)
        hb_flat = _Flat(hb_ref, AW)
        for s in range(NSEG):
            for k in range(AW // LANES):
                st_rows = pl.ds(s * PITCH, SEG)
                hb_flat[pl.ds(s * SEG, SEG), k * LANES:(k + 1) * LANES] = (
                    h_st[k, st_rows, :] + p_st[k, st_rows, :] * cin[s][k])
                fills.after_vpu(C_FIX_TILE)
        for q in range(nb):
            hfin_ref[q] = hc_ref[q:q + 1, :]

        gtot = _cumsum_steps(lf_st, g_st, True, fills)

        def load_q(n, k):
            return q_flat[pl.ds(n * SEG, SEG), slice(k * HP, (k + 1) * HP)]

        def load_v(n, k):
            return v_flat[pl.ds(n * SEG, SEG), slice(k * HP, (k + 1) * HP)]

        def emit_out(n, k, o):
            ob_flat[pl.ds(n * SEG, SEG), slice(k * HP, (k + 1) * HP)] = o

        _gla(True, nb, load_q, load_v, k_scr, g_st, gtot, s_scr, gla_stage, emit_out, fills)
        sfin_ref[...] = s_scr[...]
        fills.flush()

    step()


def _kf_body(*refs, nb, nc, add_pos, final):
    it = iter(refs)
    x_ref = next(it)
    if add_pos:
        rt_ref, ct_ref = next(it), next(it)
    (hbf0_ref, hbfn_ref, xc_ref, hb_ref, ob_ref, q_ref, v_ref, mod_ref, wf_ref, wg_ref, bg_ref, cl_ref, lb_ref,
     bnw_ref, cnw_ref, cws_ref, cbias_ref, wo_ref, fnw_ref, h0_ref, s0_ref,
     xo_ref, hfin_ref, sfin_ref,
     p_scr, sg_scr, g_scr, k_scr, y_scr, part_scr, a_st, u_st, h_st, p_st, lf_st, g_st, hc_ref, s_scr,
     *gla_stage) = it

    ic = pl.program_id(1)
    flat = pl.program_id(0) * nc + ic

    @pl.when(ic == 0)
    def _():
        for q in range(nb):
            hc_ref[q:q + 1, :] = h0_ref[q]
        s_scr[...] = s0_ref[...]

    o_ag, o_ff, o_bg = 0, AW, AW + BW
    o_cu, o_cv, o_cg = AW + 2 * BW, AW + 2 * BW + CW, AW + 2 * BW + 2 * CW

    def proj_pieces(h_r, col_lo, col_hi):
        h_flat = _Flat(h_r, D)
        return _dot_pieces(lambda r0, r1: h_flat[pl.ds(r0, r1 - r0), slice(None)], wf_ref, p_scr, col_lo, col_hi,
                           C_PROJ_PIECE)

    @pl.when(flat == 0)
    def _():
        for _, run in proj_pieces(hbf0_ref, 0, KF_COLS):
            run()

    def step():
        n_out = (D // FILL_COLS) * (T // FILL_ROWS)
        n_proj = -(-KF_COLS // FILL_COLS) * (T // FILL_ROWS)
        vpu_credit = ((AW // LANES + BH) * NSEG * C_SILU_TILE
                      + (AW // LANES) * NSEG * (C_COEFF_TILE + C_YA_TILE) + SEG * (C_SCAN_STEP + C_CUMSUM_STEP)
                      + BH * NSEG * (C_PASS1_TILE + C_GLA_A_TILE + C_GLA_C_TILE + C_GLA_EPI_TILE))
        fills = _Fills(proj_pieces(hbfn_ref, o_cu, KF_COLS), (n_proj * C_PROJ_PIECE + n_out * C_OUT_PIECE) / vpu_credit)
        nofill = _Fills((), 0.0)

        xc_flat, hb_flat = _Flat(xc_ref, AW), _Flat(hb_ref, AW)
        g_scr[...] = jnp.dot(xc_flat[pl.ds(0, T), :].astype(BF16), wg_ref[...],
                             preferred_element_type=F32) + bg_ref[...]

        lane = lax.broadcasted_iota(jnp.int32, (CCH, CW), 1)
        for m in range(T // CCH):
            rows = pl.ds(m * CCH, CCH)
            u = _gelu(p_scr[rows, o_cu:o_cu + CW])
            v = _gelu(p_scr[rows, o_cv:o_cv + CW])
            mu = jnp.mean(v, axis=-1, keepdims=True)
            dv = v - mu
            var = jnp.mean(dv * dv, axis=-1, keepdims=True)
            vn = (dv * lax.rsqrt(var + EPS) * cnw_ref[...]).astype(BF16)
            full = jnp.dot(cws_ref[...], vn, preferred_element_type=F32)
            mixed = full[0:CCH]
            for h in range(1, CH):
                mixed = jnp.where(lane >= h * CHD, full[h * CCH:(h + 1) * CCH], mixed)
            yc = u * (mixed + cbias_ref[...]) * _silu(p_scr[rows, o_cg:o_cg + CW])
            y_scr[rows, AW:AW + CW] = yc.astype(BF16)

        for s in range(NSEG):
            rows = pl.ds(s * SEG, SEG)
            for c0 in list(range(o_ag, o_ag + AW, LANES)) + list(range(o_bg, o_bg + BW, LANES)):
                dst = c0 - o_ag if c0 < o_ff else AW + c0 - o_bg
                sg_scr[rows, dst:dst + LANES] = _silu(p_scr[rows, c0:c0 + LANES])
                fills.after_vpu(C_SILU_TILE)
        _forget_gates(p_scr, o_ff, lb_ref, k_scr, lf_st, fills)
        fills.mxu.extend(proj_pieces(hbfn_ref, 0, o_cu))

        _a_coeffs(g_scr, xc_flat, cl_ref, a_st, u_st, fills)
        hend, pend = _a_scan(a_st, u_st, h_st, p_st, False, fills)
        cin = _a_carries(hend, pend, hc_ref, nb, reverse=False)
        for s in range(NSEG):
            rows = pl.ds(s * SEG, SEG)
            for k in range(AW // LANES):
                lanes = slice(k * LANES, (k + 1) * LANES)
                st_rows = pl.ds(s * PITCH, SEG)
                hf = h_st[k, st_rows, :] + p_st[k, st_rows, :] * cin[s][k]
                ya = (hf + hb_flat[rows, lanes]) * sg_scr[rows, lanes]
                y_scr[rows, lanes] = ya.astype(BF16)
                fills.after_vpu(C_YA_TILE)
        for q in range(nb):
            hfin_ref[q] = hc_ref[q:q + 1, :]

        fills.mxu.extend(_dot_pieces(lambda r0, r1: y_scr[r0:r1, 0:AW + CW], wo_ref.at[0:AW + CW, :],
                                     part_scr, 0, D, C_OUT_PIECE))

        gtot = _cumsum_steps(lf_st, g_st, False, fills)

        q_flat, v_flat, ob_flat = _Flat(q_ref, BW), _Flat(v_ref, BW), _Flat(ob_ref, BW)

        def load_q(n, k):
            return q_flat[pl.ds(n * SEG, SEG), slice(k * HP, (k + 1) * HP)]

        def load_v(n, k):
            return v_flat[pl.ds(n * SEG, SEG), slice(k * HP, (k + 1) * HP)]

        def emit_out(n, k, o):
            rows, lanes = pl.ds(n * SEG, SEG), slice(k * HP, (k + 1) * HP)
            o = o + ob_flat[rows, lanes]
            ms = jnp.sum(o * o, axis=-1, keepdims=True) * (1.0 / BDK)
            yb = (o * lax.rsqrt(ms + EPS) * bnw_ref[:, lanes]) * sg_scr[rows, AW + k * HP:AW + (k + 1) * HP]
            y_scr[rows, AW + CW + k * HP:AW + CW + (k + 1) * HP] = yb.astype(BF16)
            fills.after_vpu(C_GLA_EPI_TILE)

        _gla(False, nb, load_q, load_v, k_scr, g_st, gtot, s_scr, gla_stage, emit_out, fills)
        sfin_ref[...] = s_scr[...]
        fills.flush()

        gm = mod_ref[0, :, 2 * D:3 * D]
        x_flat, xo_flat = _Flat(x_ref, D), _Flat(xo_ref, D)
        for r0 in range(0, T, FILL_ROWS):
            rows = pl.ds(r0, FILL_ROWS)
            xin = x_flat[rows, slice(None)]
            if add_pos:
                xin = xin + jnp.concatenate(
                    [_pos_rows(rt_ref, ct_ref, s) for s in range(r0 // SEG, (r0 + FILL_ROWS) // SEG)], axis=0)
            out = part_scr[rows, :] + jnp.dot(y_scr[rows, AW + CW:YW], wo_ref[AW + CW:YW, :],
                                              preferred_element_type=F32)
            xn = xin + gm * out
            if final:
                ms = jnp.mean(xn * xn, axis=-1, keepdims=True)
                xn = xn * lax.rsqrt(ms + EPS) * fnw_ref[...]
            xo_flat[rows, slice(None)] = xn

    step()


def _mod_body(cs_ref, w_ref, b_ref, o_ref):
    s = _silu(cs_ref[...])
    o_ref[0] = jnp.dot(s, w_ref[0], preferred_element_type=F32, precision=lax.Precision.HIGHEST) + b_ref[0]


def _const_spec(shape):
    nd = len(shape)
    return pl.BlockSpec(shape, lambda ib, ic: (0,) * nd, pipeline_mode=pl.Buffered(1))


def _seq_spec(nb, rs, width, bc):
    return pl.BlockSpec((nb, rs, width), lambda ib, ic: bc(ib, ic) + (0,))


def _step_maps(ngroups, nc, reverse):
    last = ngroups * nc - 1

    def at(f):
        return f // nc, (nc - 1 - f % nc) if reverse else f % nc

    def cur(ib, ic):
        return at(ib * nc + ic)

    def ahead(d):
        return lambda ib, ic: at(jnp.minimum(ib * nc + ic + d, last))

    def fixed(f):
        return lambda ib, ic: at(min(f, last))
    return cur, ahead, fixed


def _scan_scratch():
    na = AW // LANES
    return ([pltpu.VMEM((na, NSEG * PITCH, LANES), F32)] * 4
            + [pltpu.VMEM((BH, NSEG * PITCH, LANES), F32)] * 2)


def _params():
    return pltpu.CompilerParams(dimension_semantics=("arbitrary", "arbitrary"), vmem_limit_bytes=VMEM_LIMIT)


def _state_specs(nb):
    return [pl.BlockSpec((nb, 1, AW), lambda ib, ic: (ib, 0, 0)),
            pl.BlockSpec((nb, BH, HP, HP), lambda ib, ic: (ib, 0, 0, 0))]


def _run_kb(x, mod, mod_row, lw, h0, s0, pos, nb):
    bsz, seq, _ = x.shape
    rs = T // nb
    nc = seq // rs
    add_pos = pos is not None
    cur, ahead, fixed = _step_maps(bsz // nb, nc, reverse=True)
    steps2 = (fixed(0), ahead(1))
    mrow = (lambda g: g) if mod_row is None else (lambda g: mod_row)
    mod_spec = lambda bc: pl.BlockSpec((1, 1, 3 * D), lambda ib, ic: (mrow(bc(ib, ic)[0]), 0, 0))
    ins, specs = [x] * 2, [_seq_spec(nb, rs, D, bc) for bc in steps2]
    if nc > 1:
        ins.append(x)
        specs.append(pl.BlockSpec(
            (1, SUBLANES, D), lambda ib, ic: (ib, jnp.maximum(cur(ib, ic)[1] * (rs // SUBLANES) - 1, 0), 0)))
    if add_pos:
        rowtab, coltab, hpos = pos
        row_spec = lambda bc: pl.BlockSpec((NSEG, D // 2), lambda ib, ic: (bc(ib, ic)[1], 0))
        ins += [rowtab] * 2 + [coltab, hpos]
        specs += [row_spec(bc) for bc in steps2] + [
            _const_spec(coltab.shape), pl.BlockSpec((1, SUBLANES, D), lambda ib, ic: (cur(ib, ic)[1], 0, 0))]
    names = ("nw", "wb", "cw", "cb", "wg_b", "bg_b", "cl_b", "lb")
    ins += [mod] * 2 + [lw[n] for n in names] + [h0, s0]
    specs += ([mod_spec(cur), mod_spec(ahead(1))]
              + [_const_spec(lw[n].shape) for n in names] + _state_specs(nb))
    out_shape = (jax.ShapeDtypeStruct((bsz, seq, D), BF16), jax.ShapeDtypeStruct((bsz, seq, AW), F32),
                 jax.ShapeDtypeStruct((bsz, seq, AW), F32), jax.ShapeDtypeStruct((bsz, seq, BW), F32),
                 jax.ShapeDtypeStruct((bsz, seq, BW), F32), jax.ShapeDtypeStruct((bsz, seq, BW), BF16),
                 jax.ShapeDtypeStruct((bsz, 1, AW), F32), jax.ShapeDtypeStruct((bsz, BH, HP, HP), F32))
    out_specs = tuple([_seq_spec(nb, rs, w, cur) for w in (D, AW, AW, BW, BW, BW)] + _state_specs(nb))
    scratch = ([pltpu.VMEM((T, KB_COLS), F32), pltpu.VMEM((T, D), BF16)]
               + [pltpu.VMEM((T + 2 * SUBLANES, AW), F32), pltpu.VMEM((T, 2 * AW), F32), pltpu.VMEM((T, BW), F32)]
               + _scan_scratch()
               + [pltpu.VMEM((SUBLANES, AW), F32), pltpu.VMEM((nb, BH, HP, HP), F32),
                  pltpu.VMEM((SUBLANES, AW), F32)]
               + _gla_scratch())
    return pl.pallas_call(
        functools.partial(_kb_body, nb=nb, nc=nc, add_pos=add_pos),
        grid=(bsz // nb, nc), in_specs=specs, out_specs=out_specs, out_shape=out_shape,
        scratch_shapes=scratch, compiler_params=_params(), name="bwd_sweep")(*ins)


def _run_kf(x, kb_out, mod, mod_row, lw, fnw, h0, s0, pos, nb, final):
    bsz, seq, _ = x.shape
    rs = T // nb
    nc = seq // rs
    add_pos = pos is not None
    cur, ahead, fixed = _step_maps(bsz // nb, nc, reverse=False)
    mrow = (lambda g: g) if mod_row is None else (lambda g: mod_row)
    hbf, xc, hb, ob, q, v = kb_out
    ins, specs = [x], [_seq_spec(nb, rs, D, cur)]
    if add_pos:
        rowtab, coltab, _ = pos
        ins += [rowtab, coltab]
        specs += [pl.BlockSpec((NSEG, D // 2), lambda ib, ic: (cur(ib, ic)[1], 0)), _const_spec(coltab.shape)]
    names = ("wf", "wg_f", "bg_f", "cl_f", "lb", "bnw", "cnw", "cws", "cbias", "wo")
    ins += [hbf, hbf, xc, hb, ob, q, v, mod] + [lw[n] for n in names] + [fnw, h0, s0]
    specs += [_seq_spec(nb, rs, D, fixed(0)), _seq_spec(nb, rs, D, ahead(1))]
    specs += [_seq_spec(nb, rs, w, cur) for w in (AW, AW, BW, BW, BW)]
    specs += [pl.BlockSpec((1, 1, 3 * D), lambda ib, ic: (mrow(ib), 0, 0))]
    specs += [_const_spec(lw[n].shape) for n in names] + [_const_spec(fnw.shape)] + _state_specs(nb)
    out_shape = (jax.ShapeDtypeStruct((bsz, seq, D), F32),
                 jax.ShapeDtypeStruct((bsz, 1, AW), F32), jax.ShapeDtypeStruct((bsz, BH, HP, HP), F32))
    out_specs = tuple([_seq_spec(nb, rs, D, cur)] + _state_specs(nb))
    scratch = ([pltpu.VMEM((T, KF_COLS), F32), pltpu.VMEM((T, AW + BW), F32)]
               + [pltpu.VMEM((T, 2 * AW), F32), pltpu.VMEM((T, BW), F32), pltpu.VMEM((T, YW), BF16),
                  pltpu.VMEM((T, D), F32)]
               + _scan_scratch()
               + [pltpu.VMEM((SUBLANES, AW), F32), pltpu.VMEM((nb, BH, HP, HP), F32)]
               + _gla_scratch())
    return pl.pallas_call(
        functools.partial(_kf_body, nb=nb, nc=nc, add_pos=add_pos, final=final),
        grid=(bsz // nb, nc), in_specs=specs, out_specs=out_specs, out_shape=out_shape,
        scratch_shapes=scratch, compiler_params=_params(), name="fwd_sweep")(*ins)


def _pad_heads(w, fill=0.0):
    lead = w.shape[:-1]
    w = w.reshape(lead + (BH, BDK))
    w = jnp.pad(w, [(0, 0)] * len(lead) + [(0, 0), (0, HP - BDK)], constant_values=fill)
    return w.reshape(lead + (BW,))


def _block_diag(w):
    eye = jnp.eye(A_HEADS, dtype=w.dtype)
    return jnp.einsum("hij,hg->higj", w, eye).reshape(AW, AW)


def _layer_weights(l, norm_w, w_in, a_conv_w, a_conv_b, a_wr, a_br, a_wi, a_bi, a_lambda, lbs, b_norm_w,
                   c_norm_w, c_ws, c_bs, w_out):
    cuts = np.cumsum((AW, AW, 384, 384, 384, 384, 384, CW, CW, CW))[:-1].tolist()
    ax, ag, bq, bff, bfb, bi, bg, cu, cv, cg = jnp.split(w_in[l], cuts, axis=-1)
    wo_a, wo_b, wo_c = jnp.split(w_out[l], [AW, AW + 384], axis=0)
    wo_b = _pad_heads(wo_b.T).T
    lw = {
        "nw": norm_w[l][None, :],
        "wb": jnp.concatenate([ax, _pad_heads(bq), _pad_heads(bfb), _pad_heads(bi)], axis=1).astype(BF16),
        "wf": jnp.concatenate([ag, _pad_heads(bff), _pad_heads(bg), cu, cv, cg], axis=1).astype(BF16),
        "cw": a_conv_w[l], "cb": a_conv_b[l][None, :],
        "lb": _pad_heads(lbs[l], fill=1.0)[None, :],
        "bnw": _pad_heads(b_norm_w[l])[None, :],
        "cnw": c_norm_w[l][None, :],
        "cws": c_ws[l].reshape(CH * CCH, CCH).astype(BF16),
        "cbias": jnp.repeat(c_bs[l].T, CHD, axis=1),
        "wo": jnp.concatenate([wo_a, wo_c, wo_b], axis=0).astype(BF16),
    }
    for d, name in ((0, "f"), (1, "b")):
        lw["wg_" + name] = jnp.concatenate([_block_diag(a_wr[l, d]), _block_diag(a_wi[l, d])], axis=1).astype(BF16)
        lw["bg_" + name] = jnp.concatenate([a_br[l, d], a_bi[l, d]])[None, :]
        lw["cl_" + name] = (-RG_C * jax.nn.softplus(-a_lambda[l, d]))[None, :]
    return lw


def _pos_tables(rows):
    qd = D // 4
    omega = 1.0 / (10000.0 ** (jnp.arange(qd, dtype=F32) / qd))
    ar = jnp.arange(rows, dtype=F32)[:, None] * omega
    ac = jnp.arange(GRID_W, dtype=F32)[:, None] * omega
    rowtab = jnp.concatenate([jnp.sin(ar), jnp.cos(ar)], axis=-1)
    coltab = jnp.concatenate([jnp.sin(ac), jnp.cos(ac)], axis=-1)
    nc = rows * GRID_W // T
    hrow = jnp.broadcast_to(rowtab[jnp.maximum(jnp.arange(nc) * NSEG - 1, 0)][:, None, :], (nc, SUBLANES, D // 2))
    hcol = jnp.broadcast_to(coltab[GRID_W - SUBLANES:][None], (nc, SUBLANES, D // 2))
    return rowtab, coltab, jnp.concatenate([hrow, hcol], axis=-1)


def kernel(x, c, ctx, c_ctx, norm_w, w_mod, b_mod, w_in, a_conv_w, a_conv_b, a_wr, a_br, a_wi, a_bi, a_lambda,
           b_lb_logits, b_norm_w, c_norm_w, c_ws, c_bs, w_out, final_norm_w):
    bsz, n_lat, _ = x.shape
    depth = w_in.shape[0]
    assert n_lat % T == 0 and T % ctx.shape[1] == 0 and bsz % (T // ctx.shape[1]) == 0
    nb_ctx = T // ctx.shape[1]
    pos = _pos_tables(n_lat // GRID_W)

    mrows = 2 * SUBLANES
    cs = jnp.zeros((mrows, D), F32).at[:bsz].set(c).at[bsz].set(c_ctx)
    nblk = 3
    mod = pl.pallas_call(
        _mod_body, grid=(depth, nblk),
        in_specs=[pl.BlockSpec((mrows, D), lambda l, j: (0, 0)),
                  pl.BlockSpec((1, D, D), lambda l, j: (l, 0, j)),
                  pl.BlockSpec((1, 1, D), lambda l, j: (l, 0, j))],
        out_specs=pl.BlockSpec((1, mrows, D), lambda l, j: (l, 0, j)),
        out_shape=jax.ShapeDtypeStruct((depth, mrows, 3 * D), F32),
        name="modulation")(cs, w_mod, b_mod[:, None, :])
    mod = mod.reshape(depth, mrows, 1, 3 * D)

    lbs = jnp.cumsum(jax.nn.softmax(b_lb_logits, axis=0), axis=0)
    fnw = final_norm_w[None, :]
    h_zero = jnp.zeros((bsz, 1, AW), F32)
    s_zero = jnp.zeros((bsz, BH, HP, HP), F32)
    for l in range(depth):
        lw = _layer_weights(l, norm_w, w_in, a_conv_w, a_conv_b, a_wr, a_br, a_wi, a_bi, a_lambda, lbs, b_norm_w,
                            c_norm_w, c_ws, c_bs, w_out)
        kb = _run_kb(ctx, mod[l], bsz, lw, h_zero, s_zero, None, nb_ctx)
        ctx_new, hf_c, sf_c = _run_kf(ctx, kb[:6], mod[l], bsz, lw, fnw, h_zero, s_zero, None, nb_ctx, False)
        hb_c, sb_c = kb[6], kb[7]
        kb = _run_kb(x, mod[l], None, lw, hb_c, sb_c, pos if l == 0 else None, 1)
        x, _, _ = _run_kf(x, kb[:6], mod[l], None, lw, fnw, hf_c, sf_c, pos if l == 0 else None, 1, l == depth - 1)
        ctx = ctx_new
    return x
```

```python
import functools

import jax
import jax.numpy as jnp
import numpy as np
from jax import lax
from jax.experimental import pallas as pl
from jax.experimental.pallas import tpu as pltpu

D = 1024
GRID_W = 64
EPS = 1e-6
RG_C = 8.0
AW = 384
A_HEADS = 8
A_HD = 48
BH = 4
BDK = 96
CW = 256
CH = 4
CHD = 64
CCH = 128

LANES = 128
SUBLANES = 8
HP = LANES
BW = BH * HP
YW = AW + BW + CW
NA = AW // LANES
T = 512
SEG = 64
NSEG = T // SEG
PITCH = SEG + SUBLANES
KB_COLS = AW + 3 * BW
KF_COLS = AW + 2 * BW + 3 * CW
FILL_COLS = 256
FILL_ROWS = T // 2
ROW_TILE = 16
MLP_TILE = 32
VMEM_LIMIT = 58 * 1024 * 1024

F32 = jnp.float32
BF16 = jnp.bfloat16

C_CONV_SEG, C_COEFF_TILE, C_SCAN_STEP, C_FIX_TILE, C_YA_TILE = 100, 50, 12, 8, 15
C_GATES_TILE, C_CUMSUM_STEP, C_GLA_A_TILE, C_GLA_C_TILE, C_SILU_TILE = 20, 8, 40, 15, 12
C_PRE_TILE, C_MLP_TILE, C_GLA_B_TILE, C_GLA_D_TILE = 25, 90, 40, 40
C_PROJ_PIECE, C_OUT_PIECE, C_NORM_PIECE = FILL_ROWS, 3 * FILL_ROWS // 4, 130


def _sigmoid(x):
    return 0.5 * jnp.tanh(0.5 * x) + 0.5


def _silu(x):
    return x * _sigmoid(x)


def _gelu(x):
    return 0.5 * x * (1.0 + jnp.tanh(np.sqrt(2.0 / np.pi).astype(np.float32) * (x + 0.044715 * (x * x * x))))


def _nt(a, b):
    return lax.dot_general(a, b, (((1,), (1,)), ((), ())), preferred_element_type=F32)


def _tn(a, b):
    return lax.dot_general(a, b, (((0,), (0,)), ((), ())), preferred_element_type=F32)


def _norm_mod(x, nw, sh, sc):
    ms = jnp.mean(x * x, axis=-1, keepdims=True)
    return (x * lax.rsqrt(ms + EPS) * nw) * (1.0 + sc) + sh


def _pos_rows(rt_ref, ct_ref, r0):
    s, c0 = divmod(r0, SEG)
    return jnp.concatenate([jnp.broadcast_to(rt_ref[s:s + 1, :], (ROW_TILE, D // 2)),
                            ct_ref[c0:c0 + ROW_TILE, :]], axis=1)


class _Fill:
    def __init__(self, items=(), rate=1.0):
        self.items = list(items)
        self.rate = rate
        self.credit = 0.0
        self.spent = 0.0

    def __call__(self, credit):
        self.credit += credit
        while self.items and self.spent + 0.5 * self.items[0][0] <= self.credit * self.rate:
            cost, thunk = self.items.pop(0)
            self.spent += cost
            thunk()

    def extend(self, items):
        self.items.extend(items)

    def flush(self):
        while self.items:
            self.items.pop(0)[1]()


def _cost(items):
    return sum(c for c, _ in items)


def _emit(items, fill=None):
    for cost, thunk in items:
        thunk()
        if fill is not None:
            fill(cost)


def _paired(items, partner_items):
    partner = _Fill(partner_items, _cost(partner_items) / max(_cost(items), 1))
    _emit(items, partner)
    partner.flush()


class _Flat:
    def __init__(self, ref, width):
        self.ref = ref
        self.rs = ref.shape[1]
        self.width = width

    def _split(self, idx):
        rows, lanes = idx
        start, size = rows.start, rows.size
        q, off = divmod(start, self.rs)
        assert off + size <= self.rs
        return q, pl.ds(off, size), lanes

    def __getitem__(self, idx):
        q, r, lanes = self._split(idx)
        return self.ref[q, r, lanes]

    def __setitem__(self, idx, val):
        q, r, lanes = self._split(idx)
        self.ref[q, r, lanes] = val


def _tiles():
    return [(s, k) for s in range(NSEG) for k in range(BH)]


class _AScan:
    def __init__(self, a_st, u_st, h_st, p_st, hc_ref, nb, reverse):
        self.refs = (a_st, u_st, h_st, p_st)
        self.hc_ref, self.nb, self.reverse = hc_ref, nb, reverse
        self.hs = [jnp.zeros((NSEG, LANES), F32)] * NA
        self.ps = [jnp.ones((NSEG, LANES), F32)] * NA

    def coeff_items(self, g_scr, xc_ref, cl_ref):
        a_st, u_st = self.refs[:2]

        def item(s, k):
            def run():
                rows, lanes = pl.ds(s * SEG, SEG), slice(k * LANES, (k + 1) * LANES)
                r = _sigmoid(g_scr[rows, lanes])
                gi = _sigmoid(g_scr[rows, AW + k * LANES:AW + (k + 1) * LANES])
                la = r * cl_ref[:, lanes]
                th = jnp.tanh(la)
                om = (-2.0 * th) / (1.0 - th)
                a_st[k, pl.ds(s * PITCH, SEG), :] = jnp.exp(la)
                u_st[k, pl.ds(s * PITCH, SEG), :] = jnp.sqrt(om) * (gi * xc_ref[rows, lanes])
            return C_COEFF_TILE, run
        return [item(s, k) for s in range(NSEG) for k in range(NA)]

    def step_items(self):
        a_st, u_st, h_st, p_st = self.refs

        def item(j):
            def run():
                for k in range(NA):
                    a = a_st[k, pl.ds(j, NSEG, stride=PITCH), :]
                    u = u_st[k, pl.ds(j, NSEG, stride=PITCH), :]
                    self.hs[k] = a * self.hs[k] + u
                    self.ps[k] = a * self.ps[k]
                    h_st[k, pl.ds(j, NSEG, stride=PITCH), :] = self.hs[k]
                    p_st[k, pl.ds(j, NSEG, stride=PITCH), :] = self.ps[k]
            return C_SCAN_STEP, run
        order = range(SEG - 1, -1, -1) if self.reverse else range(SEG)
        return [item(j) for j in order]

    def chain(self):
        spq = NSEG // self.nb
        cin = [None] * NSEG
        for q in range(self.nb):
            c = [self.hc_ref[q:q + 1, k * LANES:(k + 1) * LANES] for k in range(NA)]
            order = range((q + 1) * spq - 1, q * spq - 1, -1) if self.reverse else range(q * spq, (q + 1) * spq)
            for s in order:
                cin[s] = c
                c = [self.ps[k][s:s + 1, :] * c[k] + self.hs[k][s:s + 1, :] for k in range(NA)]
            for k in range(NA):
                self.hc_ref[q:q + 1, k * LANES:(k + 1) * LANES] = c[k]
        return cin

    def full(self, cin, s, k):
        _, _, h_st, p_st = self.refs
        st_rows = pl.ds(s * PITCH, SEG)
        return h_st[k, st_rows, :] + p_st[k, st_rows, :] * cin[s][k]


class _Gla:
    def __init__(self, reverse, nb, k_scr, lf_st, g_st, s_scr, stage):
        self.reverse, self.nb = reverse, nb
        self.k_scr, self.lf_st, self.g_st, self.s_scr = k_scr, lf_st, g_st, s_scr
        self.qg_s, self.kg_s, self.ke_s, self.att_s, self.kv_s, self.sin_s = stage
        self.gs = [jnp.zeros((NSEG, LANES), F32)] * BH
        order = range(NSEG - 1, -1, -1) if reverse else range(NSEG)
        self.order = list(order)
        self.tiles = [(n, k) for n in order for k in range(BH)]

    def gate_items(self, p_scr, col0, lb_ref):
        def item(s, k):
            def run():
                rows, lanes = pl.ds(s * SEG, SEG), slice(k * HP, (k + 1) * HP)
                lb = lb_ref[:, lanes]
                f = lb + (1.0 - lb) * _sigmoid(p_scr[rows, col0 + k * HP:col0 + (k + 1) * HP])
                self.k_scr[rows, lanes] = 1.0 - f
                self.lf_st[k, pl.ds(s * PITCH, SEG), :] = jnp.log(f)
            return C_GATES_TILE, run
        return [item(s, k) for s, k in _tiles()]

    def cumsum_items(self):
        def item(j):
            def run():
                for k in range(BH):
                    self.gs[k] = self.gs[k] + self.lf_st[k, pl.ds(j, NSEG, stride=PITCH), :]
                    self.g_st[k, pl.ds(j, NSEG, stride=PITCH), :] = self.gs[k]
            return C_CUMSUM_STEP, run
        order = range(SEG - 1, -1, -1) if self.reverse else range(SEG)
        return [item(j) for j in order]

    def a_items(self, load_q):
        def item(n, k):
            def run():
                rows, lanes = pl.ds(n * SEG, SEG), slice(k * HP, (k + 1) * HP)
                g = self.g_st[k, pl.ds(n * PITCH, SEG), :]
                tot = self.gs[k][n:n + 1, :]
                qv = load_q(n, k)
                kk = self.k_scr[rows, lanes]
                self.qg_s[rows, lanes] = (qv * jnp.exp(g)).astype(BF16)
                self.kg_s[rows, lanes] = (kk * jnp.exp(-g)).astype(BF16)
                self.ke_s[rows, lanes] = (kk * jnp.exp(tot - g)).astype(BF16)
            return C_GLA_A_TILE, run
        return [item(n, k) for n, k in self.tiles]

    def b_items(self, load_v):
        r_i = lax.broadcasted_iota(jnp.int32, (SEG, SEG), 0)
        c_i = lax.broadcasted_iota(jnp.int32, (SEG, SEG), 1)
        keep = (c_i >= r_i) if self.reverse else (c_i <= r_i)

        def item(n, k):
            def run():
                rows, lanes = pl.ds(n * SEG, SEG), slice(k * HP, (k + 1) * HP)
                self.att_s[k, rows, :] = jnp.where(
                    keep, _nt(self.qg_s[rows, lanes], self.kg_s[rows, lanes]), 0.0).astype(BF16)
                self.kv_s[n, k] = _tn(load_v(n, k), self.ke_s[rows, lanes])
            return C_GLA_B_TILE, run
        return [item(n, k) for n, k in self.tiles]

    def c_items(self):
        cps = NSEG // self.nb
        state = {}

        def item(n, k):
            def run():
                sq = n // cps
                st = state.get(k)
                if st is None:
                    st = self.s_scr[sq, k]
                self.sin_s[n, k] = st.astype(BF16)
                st = jnp.exp(self.gs[k][n:n + 1, :]) * st + self.kv_s[n, k]
                last = (n % cps == 0) if self.reverse else (n % cps == cps - 1)
                if last:
                    self.s_scr[sq, k] = st
                    st = None
                state[k] = st
            return C_GLA_C_TILE, run
        return [item(n, k) for n in self.order for k in range(BH)]

    def d_items(self, load_v, emit_out):
        def item(n, k):
            def run():
                rows, lanes = pl.ds(n * SEG, SEG), slice(k * HP, (k + 1) * HP)
                o = (jnp.dot(self.att_s[k, rows, :], load_v(n, k), preferred_element_type=F32)
                     + _nt(self.qg_s[rows, lanes], self.sin_s[n, k]))
                emit_out(n, k, o)
            return C_GLA_D_TILE, run
        return [item(n, k) for n, k in self.tiles]


def _gla_scratch():
    return ([pltpu.VMEM((T, BW), BF16)] * 3
            + [pltpu.VMEM((BH, T, SEG), BF16), pltpu.VMEM((NSEG, BH, HP, HP), F32),
               pltpu.VMEM((NSEG, BH, HP, HP), BF16)])


def _dot_items(lhs_rows, w_ref, dst, col_lo, col_hi, cost):
    def piece(r0, c0, c1):
        def run():
            dst[r0:r0 + FILL_ROWS, c0:c1] = jnp.dot(lhs_rows(r0, r0 + FILL_ROWS), w_ref[:, c0:c1],
                                                    preferred_element_type=F32)
        return cost, run
    return [piece(r0, c0, min(c0 + FILL_COLS, col_hi))
            for c0 in range(col_lo, col_hi, FILL_COLS) for r0 in range(0, T, FILL_ROWS)]


def _kb_body(*refs, nb, nc, add_pos):
    it = iter(refs)
    x0_ref, xn_ref = next(it), next(it)
    xh_ref = next(it) if nc > 1 else None
    if add_pos:
        rt0_ref, rtn_ref, ct_ref, hp_ref = next(it), next(it), next(it), next(it)
    else:
        rt0_ref = rtn_ref = ct_ref = hp_ref = None
    (mod_ref, modn_ref, nw_ref, wb_ref, cw_ref, cb_ref, wg_ref, bg_ref, cl_ref, lb_ref, h0_ref, s0_ref,
     hbf_ref, xc_ref, hb_ref, ob_ref, q_ref, v_ref, hfin_ref, sfin_ref,
     p_scr, h_scr, xa_ext, g_scr, k_scr, a_st, u_st, h_st, p_st, lf_st, g_st, hc_ref, s_scr, xaf,
     xcb_scr, *gla_stage) = it

    ic = pl.program_id(1)
    flat = pl.program_id(0) * nc + ic
    cc = nc - 1 - ic

    @pl.when(ic == 0)
    def _():
        for q in range(nb):
            hc_ref[q:q + 1, :] = h0_ref[q]
        s_scr[...] = s0_ref[...]
        xaf[...] = jnp.zeros_like(xaf)

    def norm_items(x_r, rt_r, m_r):
        x_flat = _Flat(x_r, D)

        def item(s):
            def run():
                for r0 in range(s * SEG, (s + 1) * SEG, ROW_TILE):
                    rows = pl.ds(r0, ROW_TILE)
                    xin = x_flat[rows, slice(None)]
                    if add_pos:
                        xin = xin + _pos_rows(rt_r, ct_ref, r0)
                    h_scr[rows, :] = _norm_mod(xin, nw_ref[...], m_r[0, :, 0:D], m_r[0, :, D:2 * D]).astype(BF16)
            return C_NORM_PIECE, run
        return [item(s) for s in range(NSEG)]

    def proj_items():
        return _dot_items(lambda r0, r1: h_scr[r0:r1, :], wb_ref, p_scr, 0, KB_COLS, C_PROJ_PIECE)

    @pl.when(flat == 0)
    def _():
        _emit(norm_items(x0_ref, rt0_ref, mod_ref) + proj_items())

    def step():
        xc_flat, hb_flat = _Flat(xc_ref, AW), _Flat(hb_ref, AW)
        q_flat, v_flat, ob_flat = _Flat(q_ref, BW), _Flat(v_ref, BW), _Flat(ob_ref, BW)
        scan = _AScan(a_st, u_st, h_st, p_st, hc_ref, nb, reverse=True)
        gla = _Gla(True, nb, k_scr, lf_st, g_st, s_scr, gla_stage)

        hbf_ref[...] = h_scr[...].reshape(hbf_ref.shape)
        if nc > 1:
            xh = xh_ref[0]
            if add_pos:
                xh = xh + hp_ref[0]
            hh = _norm_mod(xh, nw_ref[...], mod_ref[0, :, 0:D], mod_ref[0, :, D:2 * D]).astype(BF16)
            left = jnp.dot(hh, wb_ref[:, 0:AW], preferred_element_type=F32)
            xa_ext[0:SUBLANES, :] = jnp.where(cc > 0, left, 0.0)
        else:
            xa_ext[0:SUBLANES, :] = jnp.zeros((SUBLANES, AW), F32)
        xa_ext[pl.ds(SUBLANES, T), :] = p_scr[:, 0:AW]
        xa_ext[pl.ds(SUBLANES + T, SUBLANES), :] = xaf[...]
        xaf[...] = p_scr[0:SUBLANES, 0:AW]

        def qv_item(s, k):
            def run():
                rows, lanes = pl.ds(s * SEG, SEG), slice(k * HP, (k + 1) * HP)
                q_flat[rows, lanes] = _silu(p_scr[rows, AW + k * HP:AW + (k + 1) * HP])
                v_flat[rows, lanes] = p_scr[rows, AW + 2 * BW + k * HP:AW + 2 * BW + (k + 1) * HP].astype(BF16)
            return C_PRE_TILE, run

        def conv_item(s):
            def run():
                spq = NSEG // nb
                row = lax.broadcasted_iota(jnp.int32, (SEG, 1), 0)
                rows = pl.ds(s * SEG, SEG)
                for k in range(NA):
                    lanes = slice(k * LANES, (k + 1) * LANES)
                    acc = jnp.broadcast_to(cb_ref[:, lanes], (SEG, LANES))
                    for j in range(4):
                        tap = xa_ext[pl.ds(s * SEG + SUBLANES - 2 + j, SEG), lanes]
                        if nb > 1 and s % spq == 0 and j < 2:
                            tap = jnp.where(row + (j - 2) >= 0, tap, 0.0)
                        if nb > 1 and s % spq == spq - 1 and j == 3:
                            tap = jnp.where(row + 1 < SEG, tap, 0.0)
                        acc = acc + cw_ref[j:j + 1, lanes] * tap
                    xc_flat[rows, lanes] = acc
                    xcb_scr[rows, lanes] = acc.astype(BF16)
            return C_CONV_SEG, run

        def load_q(n, k):
            return q_flat[pl.ds(n * SEG, SEG), slice(k * HP, (k + 1) * HP)]

        def load_v(n, k):
            return v_flat[pl.ds(n * SEG, SEG), slice(k * HP, (k + 1) * HP)]

        def emit_out(n, k, o):
            ob_flat[pl.ds(n * SEG, SEG), slice(k * HP, (k + 1) * HP)] = o

        _emit([qv_item(s, k) for s, k in _tiles()] + gla.gate_items(p_scr, AW + BW, lb_ref))
        _emit(norm_items(xn_ref, rtn_ref, modn_ref))

        conv = [conv_item(s) for s in range(NSEG)]
        coeff, steps, cums = scan.coeff_items(g_scr, xc_flat, cl_ref), scan.step_items(), gla.cumsum_items()
        proj = proj_items()
        credit = (_cost(conv) + _cost(coeff) + _cost(steps) + _cost(cums)
                  + BH * NSEG * (C_GLA_A_TILE + C_GLA_C_TILE) + NA * NSEG * C_FIX_TILE)
        mx = _Fill(proj, _cost(proj) / credit)
        _emit(conv, mx)
        g_scr[...] = jnp.dot(xcb_scr[...], wg_ref[...], preferred_element_type=F32) + bg_ref[...]
        _emit(coeff, mx)
        _emit(steps, mx)
        cin = scan.chain()
        for q in range(nb):
            hfin_ref[q] = hc_ref[q:q + 1, :]

        def fix_item(s, k):
            def run():
                hb_flat[pl.ds(s * SEG, SEG), k * LANES:(k + 1) * LANES] = scan.full(cin, s, k)
            return C_FIX_TILE, run
        _emit([fix_item(s, k) for s in range(NSEG) for k in range(NA)], mx)

        _emit(cums, mx)
        _emit(gla.a_items(load_q), mx)
        _emit(gla.b_items(load_v))
        _emit(gla.c_items(), mx)
        _emit(gla.d_items(load_v, emit_out))
        sfin_ref[...] = s_scr[...]
        mx.flush()

    step()


def _kf_body(*refs, nb, nc, add_pos, final):
    it = iter(refs)
    x_ref = next(it)
    if add_pos:
        rt_ref, ct_ref = next(it), next(it)
    (hbf0_ref, hbfn_ref, xc_ref, hb_ref, ob_ref, q_ref, v_ref, mod_ref, wf_ref, wg_ref, bg_ref, cl_ref, lb_ref,
     bnw_ref, cnw_ref, cws_ref, cbias_ref, wo_ref, fnw_ref, h0_ref, s0_ref,
     xo_ref, hfin_ref, sfin_ref,
     p_scr, sg_scr, hn_scr, g_scr, k_scr, y_scr, part_scr, a_st, u_st, h_st, p_st, lf_st, g_st, hc_ref, s_scr,
     xcb_scr, vn_scr, mix_scr, *gla_stage) = it

    ic = pl.program_id(1)
    flat = pl.program_id(0) * nc + ic

    @pl.when(ic == 0)
    def _():
        for q in range(nb):
            hc_ref[q:q + 1, :] = h0_ref[q]
        s_scr[...] = s0_ref[...]

    o_ag, o_ff, o_bg = 0, AW, AW + BW
    o_cu, o_cv, o_cg = AW + 2 * BW, AW + 2 * BW + CW, AW + 2 * BW + 2 * CW

    def proj_items(col_lo, col_hi):
        return _dot_items(lambda r0, r1: hn_scr[r0:r1, :], wf_ref, p_scr, col_lo, col_hi, C_PROJ_PIECE)

    @pl.when(flat == 0)
    def _():
        hn_scr[...] = hbf0_ref[...].reshape(T, D)
        _emit(proj_items(0, KF_COLS))

    def step():
        xc_flat, hb_flat = _Flat(xc_ref, AW), _Flat(hb_ref, AW)
        q_flat, v_flat, ob_flat = _Flat(q_ref, BW), _Flat(v_ref, BW), _Flat(ob_ref, BW)
        scan = _AScan(a_st, u_st, h_st, p_st, hc_ref, nb, reverse=False)
        gla = _Gla(False, nb, k_scr, lf_st, g_st, s_scr, gla_stage)

        hn_scr[...] = hbfn_ref[...].reshape(T, D)
        for s in range(NSEG):
            rows = pl.ds(s * SEG, SEG)
            xcb_scr[rows, :] = xc_flat[rows, slice(None)].astype(BF16)

        def silu_item(s, c0):
            def run():
                rows = pl.ds(s * SEG, SEG)
                dst = c0 - o_ag if c0 < o_ff else AW + c0 - o_bg
                sg_scr[rows, dst:dst + LANES] = _silu(p_scr[rows, c0:c0 + LANES])
            return C_SILU_TILE, run
        gate_cols = list(range(o_ag, o_ag + AW, LANES)) + list(range(o_bg, o_bg + BW, LANES))

        def mlp_items(m):
            lane = lax.broadcasted_iota(jnp.int32, (MLP_TILE, CW), 1)

            def norm_item(r0):
                def run():
                    v = _gelu(p_scr[pl.ds(m * CCH + r0, MLP_TILE), o_cv:o_cv + CW])
                    mu = jnp.mean(v, axis=-1, keepdims=True)
                    dv = v - mu
                    var = jnp.mean(dv * dv, axis=-1, keepdims=True)
                    vn_scr[r0:r0 + MLP_TILE, :] = (dv * lax.rsqrt(var + EPS) * cnw_ref[...]).astype(BF16)
                return C_MLP_TILE, run

            def mix():
                mix_scr[...] = jnp.dot(cws_ref[...], vn_scr[...], preferred_element_type=F32)

            def out_item(r0):
                def run():
                    rows = pl.ds(m * CCH + r0, MLP_TILE)
                    mixed = mix_scr[r0:r0 + MLP_TILE, :]
                    for h in range(1, CH):
                        mixed = jnp.where(lane >= h * CHD, mix_scr[h * CCH + r0:h * CCH + r0 + MLP_TILE, :], mixed)
                    u = _gelu(p_scr[rows, o_cu:o_cu + CW])
                    yc = u * (mixed + cbias_ref[r0:r0 + MLP_TILE, :]) * _silu(p_scr[rows, o_cg:o_cg + CW])
                    y_scr[rows, BW + AW:YW] = yc.astype(BF16)
                return C_MLP_TILE, run
            tiles = range(0, CCH, MLP_TILE)
            return [norm_item(r0) for r0 in tiles] + [(0, mix)] + [out_item(r0) for r0 in tiles]

        def load_q(n, k):
            return q_flat[pl.ds(n * SEG, SEG), slice(k * HP, (k + 1) * HP)]

        def load_v(n, k):
            return v_flat[pl.ds(n * SEG, SEG), slice(k * HP, (k + 1) * HP)]

        def emit_out(n, k, o):
            rows, lanes = pl.ds(n * SEG, SEG), slice(k * HP, (k + 1) * HP)
            o = o + ob_flat[rows, lanes]
            ms = jnp.sum(o * o, axis=-1, keepdims=True) * (1.0 / BDK)
            yb = (o * lax.rsqrt(ms + EPS) * bnw_ref[:, lanes]) * sg_scr[rows, AW + k * HP:AW + (k + 1) * HP]
            y_scr[rows, lanes] = yb.astype(BF16)

        g_scr[...] = jnp.dot(xcb_scr[...], wg_ref[...], preferred_element_type=F32) + bg_ref[...]
        _emit([it for m in range(T // CCH) for it in mlp_items(m)])

        pre = [silu_item(s, c0) for s in range(NSEG) for c0 in gate_cols] + gla.gate_items(p_scr, o_ff, lb_ref)
        coeff, steps, cums = scan.coeff_items(g_scr, xc_flat, cl_ref), scan.step_items(), gla.cumsum_items()
        n_proj = -(-KF_COLS // FILL_COLS) * (T // FILL_ROWS)
        n_out = (D // FILL_COLS) * (T // FILL_ROWS)
        credit = (_cost(pre) + _cost(coeff) + _cost(steps) + _cost(cums)
                  + BH * NSEG * (C_GLA_A_TILE + C_GLA_C_TILE + C_GLA_D_TILE) + NA * NSEG * C_YA_TILE)
        mx = _Fill(proj_items(o_cu, KF_COLS), (n_proj * C_PROJ_PIECE + n_out * C_OUT_PIECE) / credit)
        _emit(pre, mx)
        mx.extend(proj_items(0, o_cu))

        _emit(coeff, mx)
        _emit(steps, mx)
        cin = scan.chain()
        for q in range(nb):
            hfin_ref[q] = hc_ref[q:q + 1, :]

        def ya_item(s, k):
            def run():
                rows, lanes = pl.ds(s * SEG, SEG), slice(k * LANES, (k + 1) * LANES)
                ya = (scan.full(cin, s, k) + hb_flat[rows, lanes]) * sg_scr[rows, lanes]
                y_scr[rows, BW + k * LANES:BW + (k + 1) * LANES] = ya.astype(BF16)
            return C_YA_TILE, run
        _emit([ya_item(s, k) for s in range(NSEG) for k in range(NA)], mx)
        mx.extend(_dot_items(lambda r0, r1: y_scr[r0:r1, BW:YW], wo_ref.at[BW:YW, :], part_scr, 0, D, C_OUT_PIECE))

        _emit(cums, mx)
        _emit(gla.a_items(load_q), mx)
        _emit(gla.b_items(load_v))
        _emit(gla.c_items(), mx)
        _emit(gla.d_items(load_v, emit_out), mx)
        sfin_ref[...] = s_scr[...]
        mx.flush()

        gm = mod_ref[0, :, 2 * D:3 * D]
        x_flat, xo_flat = _Flat(x_ref, D), _Flat(xo_ref, D)
        for r0 in range(0, T, FILL_ROWS):
            rows = pl.ds(r0, FILL_ROWS)
            part_scr[rows, :] = part_scr[rows, :] + jnp.dot(y_scr[rows, 0:BW], wo_ref[0:BW, :],
                                                            preferred_element_type=F32)
        for r0 in range(0, T, ROW_TILE):
            rows = pl.ds(r0, ROW_TILE)
            xin = x_flat[rows, slice(None)]
            if add_pos:
                xin = xin + _pos_rows(rt_ref, ct_ref, r0)
            xn = xin + gm * part_scr[rows, :]
            if final:
                ms = jnp.mean(xn * xn, axis=-1, keepdims=True)
                xn = xn * lax.rsqrt(ms + EPS) * fnw_ref[...]
            xo_flat[rows, slice(None)] = xn

    step()


def _mod_body(cs_ref, w_ref, b_ref, o_ref):
    s = _silu(cs_ref[...])
    o_ref[0] = jnp.dot(s, w_ref[0], preferred_element_type=F32, precision=lax.Precision.HIGHEST) + b_ref[0]


def _const_spec(shape):
    nd = len(shape)
    return pl.BlockSpec(shape, lambda ib, ic: (0,) * nd, pipeline_mode=pl.Buffered(1))


def _seq_spec(nb, rs, width, bc):
    return pl.BlockSpec((nb, rs, width), lambda ib, ic: bc(ib, ic) + (0,))


def _step_maps(ngroups, nc, reverse):
    last = ngroups * nc - 1

    def at(f):
        return f // nc, (nc - 1 - f % nc) if reverse else f % nc

    def cur(ib, ic):
        return at(ib * nc + ic)

    def ahead(d):
        return lambda ib, ic: at(jnp.minimum(ib * nc + ic + d, last))

    def fixed(f):
        return lambda ib, ic: at(min(f, last))
    return cur, ahead, fixed


def _scan_scratch():
    return ([pltpu.VMEM((NA, NSEG * PITCH, LANES), F32)] * 4
            + [pltpu.VMEM((BH, NSEG * PITCH, LANES), F32)] * 2)


def _params():
    return pltpu.CompilerParams(dimension_semantics=("arbitrary", "arbitrary"), vmem_limit_bytes=VMEM_LIMIT)


def _state_specs(nb):
    return [pl.BlockSpec((nb, 1, AW), lambda ib, ic: (ib, 0, 0)),
            pl.BlockSpec((nb, BH, HP, HP), lambda ib, ic: (ib, 0, 0, 0))]


def _run_kb(x, mod, mod_row, weights, layer, h0, s0, pos, nb):
    bsz, seq, _ = x.shape
    rs = T // nb
    nc = seq // rs
    add_pos = pos is not None
    cur, ahead, fixed = _step_maps(bsz // nb, nc, reverse=True)
    steps2 = (fixed(0), ahead(1))
    mrow = (lambda g: g) if mod_row is None else (lambda g: mod_row)
    mod_spec = lambda bc: pl.BlockSpec((1, 1, 3 * D), lambda ib, ic: (mrow(bc(ib, ic)[0]), 0, 0))
    ins, specs = [x] * 2, [_seq_spec(nb, rs, D, bc) for bc in steps2]
    if nc > 1:
        ins.append(x)
        specs.append(pl.BlockSpec(
            (1, SUBLANES, D), lambda ib, ic: (ib, jnp.maximum(cur(ib, ic)[1] * (rs // SUBLANES) - 1, 0), 0)))
    if add_pos:
        rowtab, coltab, hpos = pos
        row_spec = lambda bc: pl.BlockSpec((NSEG, D // 2), lambda ib, ic: (bc(ib, ic)[1], 0))
        ins += [rowtab] * 2 + [coltab, hpos]
        specs += [row_spec(bc) for bc in steps2] + [
            _const_spec(coltab.shape), pl.BlockSpec((1, SUBLANES, D), lambda ib, ic: (cur(ib, ic)[1], 0, 0))]
    w_ins, w_specs = _weight_operands(weights, ("nw", "wb", "cw", "cb", "wg", "bg", "cl", "lb"), layer, BWD)
    ins += [mod] * 2 + w_ins + [h0, s0]
    specs += [mod_spec(cur), mod_spec(ahead(1))] + w_specs + _state_specs(nb)
    out_shape = (jax.ShapeDtypeStruct((bsz, seq, D), BF16), jax.ShapeDtypeStruct((bsz, seq, AW), F32),
                 jax.ShapeDtypeStruct((bsz, seq, AW), F32), jax.ShapeDtypeStruct((bsz, seq, BW), F32),
                 jax.ShapeDtypeStruct((bsz, seq, BW), F32), jax.ShapeDtypeStruct((bsz, seq, BW), BF16),
                 jax.ShapeDtypeStruct((bsz, 1, AW), F32), jax.ShapeDtypeStruct((bsz, BH, HP, HP), F32))
    out_specs = tuple([_seq_spec(nb, rs, w, cur) for w in (D, AW, AW, BW, BW, BW)] + _state_specs(nb))
    scratch = ([pltpu.VMEM((T, KB_COLS), F32), pltpu.VMEM((T, D), BF16)]
               + [pltpu.VMEM((T + 2 * SUBLANES, AW), F32), pltpu.VMEM((T, 2 * AW), F32), pltpu.VMEM((T, BW), F32)]
               + _scan_scratch()
               + [pltpu.VMEM((SUBLANES, AW), F32), pltpu.VMEM((nb, BH, HP, HP), F32),
                  pltpu.VMEM((SUBLANES, AW), F32), pltpu.VMEM((T, AW), BF16)]
               + _gla_scratch())
    return pl.pallas_call(
        functools.partial(_kb_body, nb=nb, nc=nc, add_pos=add_pos),
        grid=(bsz // nb, nc), in_specs=specs, out_specs=out_specs, out_shape=out_shape,
        scratch_shapes=scratch, compiler_params=_params(), name="bwd_sweep")(*ins)


def _run_kf(x, kb_out, mod, mod_row, weights, layer, fnw, h0, s0, pos, nb, final):
    bsz, seq, _ = x.shape
    rs = T // nb
    nc = seq // rs
    add_pos = pos is not None
    cur, ahead, fixed = _step_maps(bsz // nb, nc, reverse=False)
    mrow = (lambda g: g) if mod_row is None else (lambda g: mod_row)
    hbf, xc, hb, ob, q, v = kb_out
    ins, specs = [x], [_seq_spec(nb, rs, D, cur)]
    if add_pos:
        rowtab, coltab, _ = pos
        ins += [rowtab, coltab]
        specs += [pl.BlockSpec((NSEG, D // 2), lambda ib, ic: (cur(ib, ic)[1], 0)), _const_spec(coltab.shape)]
    w_ins, w_specs = _weight_operands(
        weights, ("wf", "wg", "bg", "cl", "lb", "bnw", "cnw", "cws", "cbias", "wo"), layer, FWD)
    ins += [hbf, hbf, xc, hb, ob, q, v, mod] + w_ins + [fnw, h0, s0]
    specs += [_seq_spec(nb, rs, D, fixed(0)), _seq_spec(nb, rs, D, ahead(1))]
    specs += [_seq_spec(nb, rs, w, cur) for w in (AW, AW, BW, BW, BW)]
    specs += [pl.BlockSpec((1, 1, 3 * D), lambda ib, ic: (mrow(ib), 0, 0))]
    specs += w_specs + [_const_spec(fnw.shape)] + _state_specs(nb)
    out_shape = (jax.ShapeDtypeStruct((bsz, seq, D), F32),
                 jax.ShapeDtypeStruct((bsz, 1, AW), F32), jax.ShapeDtypeStruct((bsz, BH, HP, HP), F32))
    out_specs = tuple([_seq_spec(nb, rs, D, cur)] + _state_specs(nb))
    scratch = ([pltpu.VMEM((T, KF_COLS), F32), pltpu.VMEM((T, AW + BW), F32), pltpu.VMEM((T, D), BF16)]
               + [pltpu.VMEM((T, 2 * AW), F32), pltpu.VMEM((T, BW), F32), pltpu.VMEM((T, YW), BF16),
                  pltpu.VMEM((T, D), F32)]
               + _scan_scratch()
               + [pltpu.VMEM((SUBLANES, AW), F32), pltpu.VMEM((nb, BH, HP, HP), F32),
                  pltpu.VMEM((T, AW), BF16), pltpu.VMEM((CCH, CW), BF16), pltpu.VMEM((CH * CCH, CW), F32)]
               + _gla_scratch())
    return pl.pallas_call(
        functools.partial(_kf_body, nb=nb, nc=nc, add_pos=add_pos, final=final),
        grid=(bsz // nb, nc), in_specs=specs, out_specs=out_specs, out_shape=out_shape,
        scratch_shapes=scratch, compiler_params=_params(), name="fwd_sweep")(*ins)


def _pad_heads(w, fill=0.0):
    lead = w.shape[:-1]
    w = w.reshape(lead + (BH, BDK))
    w = jnp.pad(w, [(0, 0)] * len(lead) + [(0, 0), (0, HP - BDK)], constant_values=fill)
    return w.reshape(lead + (BW,))


def _prepare_weights(norm_w, w_in, a_conv_w, a_conv_b, a_wr, a_br, a_wi, a_bi, a_lambda, lbs, b_norm_w,
                     c_norm_w, c_ws, c_bs, w_out):
    depth = w_in.shape[0]
    nbw = BH * BDK
    a_cols, c_cols = w_in[:, :, :2 * AW], w_in[:, :, 2 * AW + 5 * nbw:]
    b_cols = _pad_heads(w_in[:, :, 2 * AW:2 * AW + 5 * nbw].reshape(depth, D, 5, nbw))
    wo_b = jnp.swapaxes(_pad_heads(jnp.swapaxes(w_out[:, AW:AW + nbw], 1, 2)), 1, 2)
    eye = jnp.eye(A_HEADS, dtype=F32)

    def block_diag(w):
        return jnp.einsum("ldhij,hg->ldhigj", w, eye).reshape(depth, 2, AW, AW)
    return {
        "nw": norm_w[:, None, :],
        "wb": jnp.concatenate([a_cols[:, :, :AW], b_cols[:, :, 0], b_cols[:, :, 2], b_cols[:, :, 3]],
                              axis=-1).astype(BF16),
        "wf": jnp.concatenate([a_cols[:, :, AW:], b_cols[:, :, 1], b_cols[:, :, 4], c_cols], axis=-1).astype(BF16),
        "cw": a_conv_w, "cb": a_conv_b[:, None, :],
        "lb": _pad_heads(lbs[:depth], fill=1.0)[:, None, :],
        "bnw": _pad_heads(b_norm_w)[:, None, :],
        "cnw": c_norm_w[:, None, :],
        "cws": c_ws.reshape(depth, CH * CCH, CCH).astype(BF16),
        "cbias": jnp.repeat(jnp.swapaxes(c_bs, 1, 2), CHD, axis=2),
        "wo": jnp.concatenate([wo_b, w_out[:, :AW], w_out[:, AW + nbw:]], axis=1).astype(BF16),
        "wg": jnp.concatenate([block_diag(a_wr), block_diag(a_wi)], axis=-1).astype(BF16),
        "bg": jnp.concatenate([a_br, a_bi], axis=-1)[:, :, None, :],
        "cl": (-RG_C * jax.nn.softplus(-a_lambda))[:, :, None, :],
    }


FWD, BWD = 0, 1
PER_DIRECTION = ("wg", "bg", "cl")


def _weight_operands(weights, names, layer, direction):
    arrays, specs = [], []
    for n in names:
        w = weights[n]
        lead = (layer, direction) if n in PER_DIRECTION else (layer,)
        rest = w.shape[len(lead):]
        arrays.append(w)
        specs.append(pl.BlockSpec((None,) * len(lead) + rest,
                                  lambda ib, ic, lead=lead, rest=rest: lead + (0,) * len(rest),
                                  pipeline_mode=pl.Buffered(1)))
    return arrays, specs


def _pos_tables(rows):
    qd = D // 4
    omega = 1.0 / (10000.0 ** (jnp.arange(qd, dtype=F32) / qd))
    ar = jnp.arange(rows, dtype=F32)[:, None] * omega
    ac = jnp.arange(GRID_W, dtype=F32)[:, None] * omega
    rowtab = jnp.concatenate([jnp.sin(ar), jnp.cos(ar)], axis=-1)
    coltab = jnp.concatenate([jnp.sin(ac), jnp.cos(ac)], axis=-1)
    nc = rows * GRID_W // T
    hrow = jnp.broadcast_to(rowtab[jnp.maximum(jnp.arange(nc) * NSEG - 1, 0)][:, None, :], (nc, SUBLANES, D // 2))
    hcol = jnp.broadcast_to(coltab[GRID_W - SUBLANES:][None], (nc, SUBLANES, D // 2))
    return rowtab, coltab, jnp.concatenate([hrow, hcol], axis=-1)


def kernel(x, c, ctx, c_ctx, norm_w, w_mod, b_mod, w_in, a_conv_w, a_conv_b, a_wr, a_br, a_wi, a_bi, a_lambda,
           b_lb_logits, b_norm_w, c_norm_w, c_ws, c_bs, w_out, final_norm_w):
    bsz, n_lat, _ = x.shape
    depth = w_in.shape[0]
    assert n_lat % T == 0 and T % ctx.shape[1] == 0 and bsz % (T // ctx.shape[1]) == 0
    nb_ctx = T // ctx.shape[1]
    pos = _pos_tables(n_lat // GRID_W)

    mrows = 2 * SUBLANES
    cs = jnp.zeros((mrows, D), F32).at[:bsz].set(c).at[bsz].set(c_ctx)
    nblk = 3
    mod = pl.pallas_call(
        _mod_body, grid=(depth, nblk),
        in_specs=[pl.BlockSpec((mrows, D), lambda l, j: (0, 0)),
                  pl.BlockSpec((1, D, D), lambda l, j: (l, 0, j)),
                  pl.BlockSpec((1, 1, D), lambda l, j: (l, 0, j))],
        out_specs=pl.BlockSpec((1, mrows, D), lambda l, j: (l, 0, j)),
        out_shape=jax.ShapeDtypeStruct((depth, mrows, 3 * D), F32),
        name="modulation")(cs, w_mod, b_mod[:, None, :])
    mod = mod.reshape(depth, mrows, 1, 3 * D)

    lbs = jnp.cumsum(jax.nn.softmax(b_lb_logits, axis=0), axis=0)
    fnw = final_norm_w[None, :]
    h_zero = jnp.zeros((bsz, 1, AW), F32)
    s_zero = jnp.zeros((bsz, BH, HP, HP), F32)
    weights = _prepare_weights(norm_w, w_in, a_conv_w, a_conv_b, a_wr, a_br, a_wi, a_bi, a_lambda, lbs, b_norm_w,
                               c_norm_w, c_ws, c_bs, w_out)
    for l in range(depth):
        lpos = pos if l == 0 else None
        kb = _run_kb(ctx, mod[l], bsz, weights, l, h_zero, s_zero, None, nb_ctx)
        ctx_new, hf_c, sf_c = _run_kf(ctx, kb[:6], mod[l], bsz, weights, l, fnw, h_zero, s_zero, None, nb_ctx, False)
        hb_c, sb_c = kb[6], kb[7]
        kb = _run_kb(x, mod[l], None, weights, l, hb_c, sb_c, lpos, 1)
        x, _, _ = _run_kf(x, kb[:6], mod[l], None, weights, l, fnw, hf_c, sf_c, lpos, 1, l == depth - 1)
        ctx = ctx_new
    return x
```

```python
import functools

import jax
import jax.numpy as jnp
import numpy as np
from jax import lax
from jax.experimental import pallas as pl
from jax.experimental.pallas import tpu as pltpu

D = 1024
GRID_W = 64
EPS = 1e-6
RG_C = 8.0
AW = 384
A_HEADS = 8
A_HD = 48
BH = 4
BDK = 96
CW = 256
CH = 4
CHD = 64
CCH = 128

LANES = 128
SUBLANES = 8
HP = LANES
BW = BH * HP
YW = AW + BW + CW
NA = AW // LANES
T = 512
SEG = 64
NSEG = T // SEG
PITCH = SEG + SUBLANES
KB_COLS = AW + 3 * BW
KF_COLS = AW + 2 * BW + 3 * CW
FILL_COLS = 256
FILL_ROWS = T // 2
ROW_TILE = 16
MLP_TILE = 32
VMEM_LIMIT = 58 * 1024 * 1024

F32 = jnp.float32
BF16 = jnp.bfloat16

C_CONV_SEG, C_COEFF_TILE, C_SCAN_STEP, C_FIX_TILE, C_YA_TILE = 100, 50, 12, 8, 15
C_GATES_TILE, C_CUMSUM_STEP, C_GLA_A_TILE, C_GLA_C_TILE, C_SILU_TILE = 20, 8, 40, 15, 12
C_PRE_TILE, C_MLP_TILE, C_GLA_B_TILE, C_GLA_D_TILE = 25, 90, 40, 40
C_PROJ_PIECE, C_OUT_PIECE, C_NORM_PIECE = FILL_ROWS, 3 * FILL_ROWS // 4, 130


def _sigmoid(x):
    return 0.5 * jnp.tanh(0.5 * x) + 0.5


def _silu(x):
    return x * _sigmoid(x)


def _gelu(x):
    return 0.5 * x * (1.0 + jnp.tanh(np.sqrt(2.0 / np.pi).astype(np.float32) * (x + 0.044715 * (x * x * x))))


def _nt(a, b):
    return lax.dot_general(a, b, (((1,), (1,)), ((), ())), preferred_element_type=F32)


def _tn(a, b):
    return lax.dot_general(a, b, (((0,), (0,)), ((), ())), preferred_element_type=F32)


def _norm_mod(x, nw, sh, sc):
    ms = jnp.mean(x * x, axis=-1, keepdims=True)
    return (x * lax.rsqrt(ms + EPS) * nw) * (1.0 + sc) + sh


def _pos_rows(rt_ref, ct_ref, r0):
    s, c0 = divmod(r0, SEG)
    return jnp.concatenate([jnp.broadcast_to(rt_ref[s:s + 1, :], (ROW_TILE, D // 2)),
                            ct_ref[c0:c0 + ROW_TILE, :]], axis=1)


class _Fill:
    def __init__(self, items=(), rate=1.0):
        self.items = list(items)
        self.rate = rate
        self.credit = 0.0
        self.spent = 0.0

    def __call__(self, credit):
        self.credit += credit
        while self.items and self.spent + 0.5 * self.items[0][0] <= self.credit * self.rate:
            cost, thunk = self.items.pop(0)
            self.spent += cost
            thunk()

    def extend(self, items):
        self.items.extend(items)

    def flush(self):
        while self.items:
            self.items.pop(0)[1]()


def _cost(items):
    return sum(c for c, _ in items)


def _emit(items, fill=None):
    for cost, thunk in items:
        thunk()
        if fill is not None:
            fill(cost)


def _paired(items, partner_items):
    partner = _Fill(partner_items, _cost(partner_items) / max(_cost(items), 1))
    _emit(items, partner)
    partner.flush()


class _Flat:
    def __init__(self, ref, width):
        self.ref = ref
        self.rs = ref.shape[1]
        self.width = width

    def _split(self, idx):
        rows, lanes = idx
        start, size = rows.start, rows.size
        q, off = divmod(start, self.rs)
        assert off + size <= self.rs
        return q, pl.ds(off, size), lanes

    def __getitem__(self, idx):
        q, r, lanes = self._split(idx)
        return self.ref[q, r, lanes]

    def __setitem__(self, idx, val):
        q, r, lanes = self._split(idx)
        self.ref[q, r, lanes] = val


def _tiles():
    return [(s, k) for s in range(NSEG) for k in range(BH)]


class _AScan:
    def __init__(self, a_st, u_st, h_st, p_st, hc_ref, nb, reverse):
        self.refs = (a_st, u_st, h_st, p_st)
        self.hc_ref, self.nb, self.reverse = hc_ref, nb, reverse
        self.hs = [jnp.zeros((NSEG, LANES), F32)] * NA
        self.ps = [jnp.ones((NSEG, LANES), F32)] * NA

    def coeff_items(self, g_scr, xc_ref, cl_ref):
        a_st, u_st = self.refs[:2]

        def item(s, k):
            def run():
                rows, lanes = pl.ds(s * SEG, SEG), slice(k * LANES, (k + 1) * LANES)
                r = _sigmoid(g_scr[rows, lanes])
                gi = _sigmoid(g_scr[rows, AW + k * LANES:AW + (k + 1) * LANES])
                la = r * cl_ref[:, lanes]
                th = jnp.tanh(la)
                om = (-2.0 * th) / (1.0 - th)
                a_st[k, pl.ds(s * PITCH, SEG), :] = jnp.exp(la)
                u_st[k, pl.ds(s * PITCH, SEG), :] = jnp.sqrt(om) * (gi * xc_ref[rows, lanes])
            return C_COEFF_TILE, run
        return [item(s, k) for s in range(NSEG) for k in range(NA)]

    def step_items(self):
        a_st, u_st, h_st, p_st = self.refs

        def item(j):
            def run():
                for k in range(NA):
                    a = a_st[k, pl.ds(j, NSEG, stride=PITCH), :]
                    u = u_st[k, pl.ds(j, NSEG, stride=PITCH), :]
                    self.hs[k] = a * self.hs[k] + u
                    self.ps[k] = a * self.ps[k]
                    h_st[k, pl.ds(j, NSEG, stride=PITCH), :] = self.hs[k]
                    p_st[k, pl.ds(j, NSEG, stride=PITCH), :] = self.ps[k]
            return C_SCAN_STEP, run
        order = range(SEG - 1, -1, -1) if self.reverse else range(SEG)
        return [item(j) for j in order]

    def chain(self):
        spq = NSEG // self.nb
        cin = [None] * NSEG
        for q in range(self.nb):
            c = [self.hc_ref[q:q + 1, k * LANES:(k + 1) * LANES] for k in range(NA)]
            order = range((q + 1) * spq - 1, q * spq - 1, -1) if self.reverse else range(q * spq, (q + 1) * spq)
            for s in order:
                cin[s] = c
                c = [self.ps[k][s:s + 1, :] * c[k] + self.hs[k][s:s + 1, :] for k in range(NA)]
            for k in range(NA):
                self.hc_ref[q:q + 1, k * LANES:(k + 1) * LANES] = c[k]
        return cin

    def full(self, cin, s, k):
        _, _, h_st, p_st = self.refs
        st_rows = pl.ds(s * PITCH, SEG)
        return h_st[k, st_rows, :] + p_st[k, st_rows, :] * cin[s][k]


class _Gla:
    def __init__(self, reverse, nb, k_scr, lf_st, g_st, s_scr, stage):
        self.reverse, self.nb = reverse, nb
        self.k_scr, self.lf_st, self.g_st, self.s_scr = k_scr, lf_st, g_st, s_scr
        self.qg_s, self.kg_s, self.ke_s, self.att_s, self.kv_s, self.sin_s = stage
        self.gs = [jnp.zeros((NSEG, LANES), F32)] * BH
        order = range(NSEG - 1, -1, -1) if reverse else range(NSEG)
        self.order = list(order)
        self.tiles = [(n, k) for n in order for k in range(BH)]

    def gate_items(self, p_scr, col0, lb_ref):
        def item(s, k):
            def run():
                rows, lanes = pl.ds(s * SEG, SEG), slice(k * HP, (k + 1) * HP)
                lb = lb_ref[:, lanes]
                f = lb + (1.0 - lb) * _sigmoid(p_scr[rows, col0 + k * HP:col0 + (k + 1) * HP])
                self.k_scr[rows, lanes] = 1.0 - f
                self.lf_st[k, pl.ds(s * PITCH, SEG), :] = jnp.log(f)
            return C_GATES_TILE, run
        return [item(s, k) for s, k in _tiles()]

    def cumsum_items(self):
        def item(j):
            def run():
                for k in range(BH):
                    self.gs[k] = self.gs[k] + self.lf_st[k, pl.ds(j, NSEG, stride=PITCH), :]
                    self.g_st[k, pl.ds(j, NSEG, stride=PITCH), :] = self.gs[k]
            return C_CUMSUM_STEP, run
        order = range(SEG - 1, -1, -1) if self.reverse else range(SEG)
        return [item(j) for j in order]

    def a_items(self, load_q):
        def item(n, k):
            def run():
                rows, lanes = pl.ds(n * SEG, SEG), slice(k * HP, (k + 1) * HP)
                g = self.g_st[k, pl.ds(n * PITCH, SEG), :]
                tot = self.gs[k][n:n + 1, :]
                qv = load_q(n, k)
                kk = self.k_scr[rows, lanes]
                self.qg_s[rows, lanes] = (qv * jnp.exp(g)).astype(BF16)
                self.kg_s[rows, lanes] = (kk * jnp.exp(-g)).astype(BF16)
                self.ke_s[rows, lanes] = (kk * jnp.exp(tot - g)).astype(BF16)
            return C_GLA_A_TILE, run
        return [item(n, k) for n, k in self.tiles]

    def b_items(self, load_v):
        r_i = lax.broadcasted_iota(jnp.int32, (SEG, SEG), 0)
        c_i = lax.broadcasted_iota(jnp.int32, (SEG, SEG), 1)
        keep = (c_i >= r_i) if self.reverse else (c_i <= r_i)

        def item(n, k):
            def run():
                rows, lanes = pl.ds(n * SEG, SEG), slice(k * HP, (k + 1) * HP)
                self.att_s[k, rows, :] = jnp.where(
                    keep, _nt(self.qg_s[rows, lanes], self.kg_s[rows, lanes]), 0.0).astype(BF16)
                self.kv_s[n, k] = _tn(load_v(n, k), self.ke_s[rows, lanes])
            return C_GLA_B_TILE, run
        return [item(n, k) for n, k in self.tiles]

    def c_items(self):
        cps = NSEG // self.nb
        state = {}

        def item(n, k):
            def run():
                sq = n // cps
                st = state.get(k)
                if st is None:
                    st = self.s_scr[sq, k]
                self.sin_s[n, k] = st.astype(BF16)
                st = jnp.exp(self.gs[k][n:n + 1, :]) * st + self.kv_s[n, k]
                last = (n % cps == 0) if self.reverse else (n % cps == cps - 1)
                if last:
                    self.s_scr[sq, k] = st
                    st = None
                state[k] = st
            return C_GLA_C_TILE, run
        return [item(n, k) for n in self.order for k in range(BH)]

    def d_items(self, load_v, emit_out):
        def item(n, k):
            def run():
                rows, lanes = pl.ds(n * SEG, SEG), slice(k * HP, (k + 1) * HP)
                o = (jnp.dot(self.att_s[k, rows, :], load_v(n, k), preferred_element_type=F32)
                     + _nt(self.qg_s[rows, lanes], self.sin_s[n, k]))
                emit_out(n, k, o)
            return C_GLA_D_TILE, run
        return [item(n, k) for n, k in self.tiles]


def _gla_scratch():
    return ([pltpu.VMEM((T, BW), BF16)] * 3
            + [pltpu.VMEM((BH, T, SEG), BF16), pltpu.VMEM((NSEG, BH, HP, HP), F32),
               pltpu.VMEM((NSEG, BH, HP, HP), BF16)])


def _dot_items(lhs_rows, w_ref, dst, col_lo, col_hi, cost):
    def piece(r0, c0, c1):
        def run():
            dst[r0:r0 + FILL_ROWS, c0:c1] = jnp.dot(lhs_rows(r0, r0 + FILL_ROWS), w_ref[:, c0:c1],
                                                    preferred_element_type=F32)
        return cost, run
    return [piece(r0, c0, min(c0 + FILL_COLS, col_hi))
            for c0 in range(col_lo, col_hi, FILL_COLS) for r0 in range(0, T, FILL_ROWS)]


def _kb_body(*refs, nb, nc, add_pos):
    it = iter(refs)
    x0_ref, xn_ref = next(it), next(it)
    xh_ref = next(it) if nc > 1 else None
    if add_pos:
        rt0_ref, rtn_ref, ct_ref, hp_ref = next(it), next(it), next(it), next(it)
    else:
        rt0_ref = rtn_ref = ct_ref = hp_ref = None
    (mod_ref, modn_ref, nw_ref, wb_ref, cw_ref, cb_ref, wg_ref, bg_ref, cl_ref, lb_ref, h0_ref, s0_ref,
     hbf_ref, xc_ref, hb_ref, ob_ref, q_ref, v_ref, hfin_ref, sfin_ref,
     p_scr, h_scr, xa_ext, g_scr, k_scr, a_st, u_st, h_st, p_st, lf_st, g_st, hc_ref, s_scr, xaf,
     *gla_stage) = it

    ic = pl.program_id(1)
    flat = pl.program_id(0) * nc + ic
    cc = nc - 1 - ic

    @pl.when(ic == 0)
    def _():
        for q in range(nb):
            hc_ref[q:q + 1, :] = h0_ref[q]
        s_scr[...] = s0_ref[...]
        xaf[...] = jnp.zeros_like(xaf)

    def norm_items(x_r, rt_r, m_r):
        x_flat = _Flat(x_r, D)

        def item(s):
            def run():
                rows = pl.ds(s * SEG, SEG)
                xin = x_flat[rows, slice(None)]
                if add_pos:
                    xin = xin + jnp.concatenate(
                        [_pos_rows(rt_r, ct_ref, r0) for r0 in range(s * SEG, (s + 1) * SEG, ROW_TILE)], axis=0)
                h_scr[rows, :] = _norm_mod(xin, nw_ref[...], m_r[0, :, 0:D], m_r[0, :, D:2 * D]).astype(BF16)
            return C_NORM_PIECE, run
        return [item(s) for s in range(NSEG)]

    def proj_items():
        return _dot_items(lambda r0, r1: h_scr[r0:r1, :], wb_ref, p_scr, 0, KB_COLS, C_PROJ_PIECE)

    @pl.when(flat == 0)
    def _():
        _emit(norm_items(x0_ref, rt0_ref, mod_ref) + proj_items())

    def step():
        xc_flat, hb_flat = _Flat(xc_ref, AW), _Flat(hb_ref, AW)
        q_flat, v_flat, ob_flat = _Flat(q_ref, BW), _Flat(v_ref, BW), _Flat(ob_ref, BW)
        scan = _AScan(a_st, u_st, h_st, p_st, hc_ref, nb, reverse=True)
        gla = _Gla(True, nb, k_scr, lf_st, g_st, s_scr, gla_stage)

        hbf_ref[...] = h_scr[...].reshape(hbf_ref.shape)
        if nc > 1:
            xh = xh_ref[0]
            if add_pos:
                xh = xh + hp_ref[0]
            hh = _norm_mod(xh, nw_ref[...], mod_ref[0, :, 0:D], mod_ref[0, :, D:2 * D]).astype(BF16)
            left = jnp.dot(hh, wb_ref[:, 0:AW], preferred_element_type=F32)
            xa_ext[0:SUBLANES, :] = jnp.where(cc > 0, left, 0.0)
        else:
            xa_ext[0:SUBLANES, :] = jnp.zeros((SUBLANES, AW), F32)
        xa_ext[pl.ds(SUBLANES, T), :] = p_scr[:, 0:AW]
        xa_ext[pl.ds(SUBLANES + T, SUBLANES), :] = xaf[...]
        xaf[...] = p_scr[0:SUBLANES, 0:AW]

        def qv_item(s, k):
            def run():
                rows, lanes = pl.ds(s * SEG, SEG), slice(k * HP, (k + 1) * HP)
                q_flat[rows, lanes] = _silu(p_scr[rows, AW + k * HP:AW + (k + 1) * HP])
                v_flat[rows, lanes] = p_scr[rows, AW + 2 * BW + k * HP:AW + 2 * BW + (k + 1) * HP].astype(BF16)
            return C_PRE_TILE, run

        def conv_item(s):
            def run():
                spq = NSEG // nb
                row = lax.broadcasted_iota(jnp.int32, (SEG, 1), 0)
                acc = jnp.broadcast_to(cb_ref[...], (SEG, AW))
                for j in range(4):
                    tap = xa_ext[pl.ds(s * SEG + SUBLANES - 2 + j, SEG), :]
                    if nb > 1 and s % spq == 0 and j < 2:
                        tap = jnp.where(row + (j - 2) >= 0, tap, 0.0)
                    if nb > 1 and s % spq == spq - 1 and j == 3:
                        tap = jnp.where(row + 1 < SEG, tap, 0.0)
                    acc = acc + cw_ref[j:j + 1, :] * tap
                xc_flat[pl.ds(s * SEG, SEG), slice(None)] = acc
            return C_CONV_SEG, run

        def load_q(n, k):
            return q_flat[pl.ds(n * SEG, SEG), slice(k * HP, (k + 1) * HP)]

        def load_v(n, k):
            return v_flat[pl.ds(n * SEG, SEG), slice(k * HP, (k + 1) * HP)]

        def emit_out(n, k, o):
            ob_flat[pl.ds(n * SEG, SEG), slice(k * HP, (k + 1) * HP)] = o

        _emit([qv_item(s, k) for s, k in _tiles()] + gla.gate_items(p_scr, AW + BW, lb_ref))
        _emit(norm_items(xn_ref, rtn_ref, modn_ref))

        conv = [conv_item(s) for s in range(NSEG)]
        coeff, steps, cums = scan.coeff_items(g_scr, xc_flat, cl_ref), scan.step_items(), gla.cumsum_items()
        proj = proj_items()
        credit = (_cost(conv) + _cost(coeff) + _cost(steps) + _cost(cums)
                  + BH * NSEG * (C_GLA_A_TILE + C_GLA_C_TILE) + NA * NSEG * C_FIX_TILE)
        mx = _Fill(proj, _cost(proj) / credit)
        _emit(conv, mx)
        g_scr[...] = jnp.dot(xc_ref[...].reshape(T, AW).astype(BF16), wg_ref[...],
                             preferred_element_type=F32) + bg_ref[...]
        _emit(coeff, mx)
        _emit(steps, mx)
        cin = scan.chain()
        for q in range(nb):
            hfin_ref[q] = hc_ref[q:q + 1, :]

        def fix_item(s, k):
            def run():
                hb_flat[pl.ds(s * SEG, SEG), k * LANES:(k + 1) * LANES] = scan.full(cin, s, k)
            return C_FIX_TILE, run
        _emit([fix_item(s, k) for s in range(NSEG) for k in range(NA)], mx)

        _emit(cums, mx)
        _emit(gla.a_items(load_q), mx)
        _emit(gla.b_items(load_v))
        _emit(gla.c_items(), mx)
        _emit(gla.d_items(load_v, emit_out))
        sfin_ref[...] = s_scr[...]
        mx.flush()

    step()


def _kf_body(*refs, nb, nc, add_pos, final):
    it = iter(refs)
    x_ref = next(it)
    if add_pos:
        rt_ref, ct_ref = next(it), next(it)
    (hbf0_ref, hbfn_ref, xc_ref, hb_ref, ob_ref, q_ref, v_ref, mod_ref, wf_ref, wg_ref, bg_ref, cl_ref, lb_ref,
     bnw_ref, cnw_ref, cws_ref, cbias_ref, wo_ref, fnw_ref, h0_ref, s0_ref,
     xo_ref, hfin_ref, sfin_ref,
     p_scr, sg_scr, hn_scr, g_scr, k_scr, y_scr, part_scr, a_st, u_st, h_st, p_st, lf_st, g_st, hc_ref, s_scr,
     xcb_scr, vn_scr, mix_scr, *gla_stage) = it

    ic = pl.program_id(1)
    flat = pl.program_id(0) * nc + ic

    @pl.when(ic == 0)
    def _():
        for q in range(nb):
            hc_ref[q:q + 1, :] = h0_ref[q]
        s_scr[...] = s0_ref[...]

    o_ag, o_ff, o_bg = 0, AW, AW + BW
    o_cu, o_cv, o_cg = AW + 2 * BW, AW + 2 * BW + CW, AW + 2 * BW + 2 * CW

    def proj_items(col_lo, col_hi):
        return _dot_items(lambda r0, r1: hn_scr[r0:r1, :], wf_ref, p_scr, col_lo, col_hi, C_PROJ_PIECE)

    @pl.when(flat == 0)
    def _():
        hn_scr[...] = hbf0_ref[...].reshape(T, D)
        _emit(proj_items(0, KF_COLS))

    def step():
        xc_flat, hb_flat = _Flat(xc_ref, AW), _Flat(hb_ref, AW)
        q_flat, v_flat, ob_flat = _Flat(q_ref, BW), _Flat(v_ref, BW), _Flat(ob_ref, BW)
        scan = _AScan(a_st, u_st, h_st, p_st, hc_ref, nb, reverse=False)
        gla = _Gla(False, nb, k_scr, lf_st, g_st, s_scr, gla_stage)

        hn_scr[...] = hbfn_ref[...].reshape(T, D)
        for s in range(NSEG):
            rows = pl.ds(s * SEG, SEG)
            xcb_scr[rows, :] = xc_flat[rows, slice(None)].astype(BF16)

        def silu_item(s, c0):
            def run():
                rows = pl.ds(s * SEG, SEG)
                dst = c0 - o_ag if c0 < o_ff else AW + c0 - o_bg
                sg_scr[rows, dst:dst + LANES] = _silu(p_scr[rows, c0:c0 + LANES])
            return C_SILU_TILE, run
        gate_cols = list(range(o_ag, o_ag + AW, LANES)) + list(range(o_bg, o_bg + BW, LANES))

        def mlp_items(m):
            lane = lax.broadcasted_iota(jnp.int32, (MLP_TILE, CW), 1)

            def norm_item(r0):
                def run():
                    v = _gelu(p_scr[pl.ds(m * CCH + r0, MLP_TILE), o_cv:o_cv + CW])
                    mu = jnp.mean(v, axis=-1, keepdims=True)
                    dv = v - mu
                    var = jnp.mean(dv * dv, axis=-1, keepdims=True)
                    vn_scr[r0:r0 + MLP_TILE, :] = (dv * lax.rsqrt(var + EPS) * cnw_ref[...]).astype(BF16)
                return C_MLP_TILE, run

            def mix():
                mix_scr[...] = jnp.dot(cws_ref[...], vn_scr[...], preferred_element_type=F32)

            def out_item(r0):
                def run():
                    rows = pl.ds(m * CCH + r0, MLP_TILE)
                    mixed = mix_scr[r0:r0 + MLP_TILE, :]
                    for h in range(1, CH):
                        mixed = jnp.where(lane >= h * CHD, mix_scr[h * CCH + r0:h * CCH + r0 + MLP_TILE, :], mixed)
                    u = _gelu(p_scr[rows, o_cu:o_cu + CW])
                    yc = u * (mixed + cbias_ref[r0:r0 + MLP_TILE, :]) * _silu(p_scr[rows, o_cg:o_cg + CW])
                    y_scr[rows, BW + AW:YW] = yc.astype(BF16)
                return C_MLP_TILE, run
            tiles = range(0, CCH, MLP_TILE)
            return [norm_item(r0) for r0 in tiles] + [(0, mix)] + [out_item(r0) for r0 in tiles]

        def load_q(n, k):
            return q_flat[pl.ds(n * SEG, SEG), slice(k * HP, (k + 1) * HP)]

        def load_v(n, k):
            return v_flat[pl.ds(n * SEG, SEG), slice(k * HP, (k + 1) * HP)]

        def emit_out(n, k, o):
            rows, lanes = pl.ds(n * SEG, SEG), slice(k * HP, (k + 1) * HP)
            o = o + ob_flat[rows, lanes]
            ms = jnp.sum(o * o, axis=-1, keepdims=True) * (1.0 / BDK)
            yb = (o * lax.rsqrt(ms + EPS) * bnw_ref[:, lanes]) * sg_scr[rows, AW + k * HP:AW + (k + 1) * HP]
            y_scr[rows, lanes] = yb.astype(BF16)

        g_scr[...] = jnp.dot(xcb_scr[...], wg_ref[...], preferred_element_type=F32) + bg_ref[...]
        _emit([it for m in range(T // CCH) for it in mlp_items(m)])

        pre = [silu_item(s, c0) for s in range(NSEG) for c0 in gate_cols] + gla.gate_items(p_scr, o_ff, lb_ref)
        coeff, steps, cums = scan.coeff_items(g_scr, xc_flat, cl_ref), scan.step_items(), gla.cumsum_items()
        n_proj = -(-KF_COLS // FILL_COLS) * (T // FILL_ROWS)
        n_out = (D // FILL_COLS) * (T // FILL_ROWS)
        credit = (_cost(pre) + _cost(coeff) + _cost(steps) + _cost(cums)
                  + BH * NSEG * (C_GLA_A_TILE + C_GLA_C_TILE + C_GLA_D_TILE) + NA * NSEG * C_YA_TILE)
        mx = _Fill(proj_items(o_cu, KF_COLS), (n_proj * C_PROJ_PIECE + n_out * C_OUT_PIECE) / credit)
        _emit(pre, mx)
        mx.extend(proj_items(0, o_cu))

        _emit(coeff, mx)
        _emit(steps, mx)
        cin = scan.chain()
        for q in range(nb):
            hfin_ref[q] = hc_ref[q:q + 1, :]

        def ya_item(s, k):
            def run():
                rows, lanes = pl.ds(s * SEG, SEG), slice(k * LANES, (k + 1) * LANES)
                ya = (scan.full(cin, s, k) + hb_flat[rows, lanes]) * sg_scr[rows, lanes]
                y_scr[rows, BW + k * LANES:BW + (k + 1) * LANES] = ya.astype(BF16)
            return C_YA_TILE, run
        _emit([ya_item(s, k) for s in range(NSEG) for k in range(NA)], mx)
        mx.extend(_dot_items(lambda r0, r1: y_scr[r0:r1, BW:YW], wo_ref.at[BW:YW, :], part_scr, 0, D, C_OUT_PIECE))

        _emit(cums, mx)
        _emit(gla.a_items(load_q), mx)
        _emit(gla.b_items(load_v))
        _emit(gla.c_items(), mx)
        _emit(gla.d_items(load_v, emit_out), mx)
        sfin_ref[...] = s_scr[...]
        mx.flush()

        gm = mod_ref[0, :, 2 * D:3 * D]
        x_flat, xo_flat = _Flat(x_ref, D), _Flat(xo_ref, D)
        for r0 in range(0, T, FILL_ROWS):
            rows = pl.ds(r0, FILL_ROWS)
            part_scr[rows, :] = part_scr[rows, :] + jnp.dot(y_scr[rows, 0:BW], wo_ref[0:BW, :],
                                                            preferred_element_type=F32)
        for r0 in range(0, T, ROW_TILE):
            rows = pl.ds(r0, ROW_TILE)
            xin = x_flat[rows, slice(None)]
            if add_pos:
                xin = xin + _pos_rows(rt_ref, ct_ref, r0)
            xn = xin + gm * part_scr[rows, :]
            if final:
                ms = jnp.mean(xn * xn, axis=-1, keepdims=True)
                xn = xn * lax.rsqrt(ms + EPS) * fnw_ref[...]
            xo_flat[rows, slice(None)] = xn

    step()


def _mod_body(cs_ref, w_ref, b_ref, o_ref):
    s = _silu(cs_ref[...])
    o_ref[0] = jnp.dot(s, w_ref[0], preferred_element_type=F32, precision=lax.Precision.HIGHEST) + b_ref[0]


def _const_spec(shape):
    nd = len(shape)
    return pl.BlockSpec(shape, lambda ib, ic: (0,) * nd, pipeline_mode=pl.Buffered(1))


def _seq_spec(nb, rs, width, bc):
    return pl.BlockSpec((nb, rs, width), lambda ib, ic: bc(ib, ic) + (0,))


def _step_maps(ngroups, nc, reverse):
    last = ngroups * nc - 1

    def at(f):
        return f // nc, (nc - 1 - f % nc) if reverse else f % nc

    def cur(ib, ic):
        return at(ib * nc + ic)

    def ahead(d):
        return lambda ib, ic: at(jnp.minimum(ib * nc + ic + d, last))

    def fixed(f):
        return lambda ib, ic: at(min(f, last))
    return cur, ahead, fixed


def _scan_scratch():
    return ([pltpu.VMEM((NA, NSEG * PITCH, LANES), F32)] * 4
            + [pltpu.VMEM((BH, NSEG * PITCH, LANES), F32)] * 2)


def _params():
    return pltpu.CompilerParams(dimension_semantics=("arbitrary", "arbitrary"), vmem_limit_bytes=VMEM_LIMIT)


def _state_specs(nb):
    return [pl.BlockSpec((nb, 1, AW), lambda ib, ic: (ib, 0, 0)),
            pl.BlockSpec((nb, BH, HP, HP), lambda ib, ic: (ib, 0, 0, 0))]


def _run_kb(x, mod, mod_row, weights, layer, h0, s0, pos, nb):
    bsz, seq, _ = x.shape
    rs = T // nb
    nc = seq // rs
    add_pos = pos is not None
    cur, ahead, fixed = _step_maps(bsz // nb, nc, reverse=True)
    steps2 = (fixed(0), ahead(1))
    mrow = (lambda g: g) if mod_row is None else (lambda g: mod_row)
    mod_spec = lambda bc: pl.BlockSpec((1, 1, 3 * D), lambda ib, ic: (mrow(bc(ib, ic)[0]), 0, 0))
    ins, specs = [x] * 2, [_seq_spec(nb, rs, D, bc) for bc in steps2]
    if nc > 1:
        ins.append(x)
        specs.append(pl.BlockSpec(
            (1, SUBLANES, D), lambda ib, ic: (ib, jnp.maximum(cur(ib, ic)[1] * (rs // SUBLANES) - 1, 0), 0)))
    if add_pos:
        rowtab, coltab, hpos = pos
        row_spec = lambda bc: pl.BlockSpec((NSEG, D // 2), lambda ib, ic: (bc(ib, ic)[1], 0))
        ins += [rowtab] * 2 + [coltab, hpos]
        specs += [row_spec(bc) for bc in steps2] + [
            _const_spec(coltab.shape), pl.BlockSpec((1, SUBLANES, D), lambda ib, ic: (cur(ib, ic)[1], 0, 0))]
    w_ins, w_specs = _weight_operands(weights, ("nw", "wb", "cw", "cb", "wg", "bg", "cl", "lb"), layer, BWD)
    ins += [mod] * 2 + w_ins + [h0, s0]
    specs += [mod_spec(cur), mod_spec(ahead(1))] + w_specs + _state_specs(nb)
    out_shape = (jax.ShapeDtypeStruct((bsz, seq, D), BF16), jax.ShapeDtypeStruct((bsz, seq, AW), F32),
                 jax.ShapeDtypeStruct((bsz, seq, AW), F32), jax.ShapeDtypeStruct((bsz, seq, BW), F32),
                 jax.ShapeDtypeStruct((bsz, seq, BW), F32), jax.ShapeDtypeStruct((bsz, seq, BW), BF16),
                 jax.ShapeDtypeStruct((bsz, 1, AW), F32), jax.ShapeDtypeStruct((bsz, BH, HP, HP), F32))
    out_specs = tuple([_seq_spec(nb, rs, w, cur) for w in (D, AW, AW, BW, BW, BW)] + _state_specs(nb))
    scratch = ([pltpu.VMEM((T, KB_COLS), F32), pltpu.VMEM((T, D), BF16)]
               + [pltpu.VMEM((T + 2 * SUBLANES, AW), F32), pltpu.VMEM((T, 2 * AW), F32), pltpu.VMEM((T, BW), F32)]
               + _scan_scratch()
               + [pltpu.VMEM((SUBLANES, AW), F32), pltpu.VMEM((nb, BH, HP, HP), F32),
                  pltpu.VMEM((SUBLANES, AW), F32)]
               + _gla_scratch())
    return pl.pallas_call(
        functools.partial(_kb_body, nb=nb, nc=nc, add_pos=add_pos),
        grid=(bsz // nb, nc), in_specs=specs, out_specs=out_specs, out_shape=out_shape,
        scratch_shapes=scratch, compiler_params=_params(), name="bwd_sweep")(*ins)


def _run_kf(x, kb_out, mod, mod_row, weights, layer, fnw, h0, s0, pos, nb, final):
    bsz, seq, _ = x.shape
    rs = T // nb
    nc = seq // rs
    add_pos = pos is not None
    cur, ahead, fixed = _step_maps(bsz // nb, nc, reverse=False)
    mrow = (lambda g: g) if mod_row is None else (lambda g: mod_row)
    hbf, xc, hb, ob, q, v = kb_out
    ins, specs = [x], [_seq_spec(nb, rs, D, cur)]
    if add_pos:
        rowtab, coltab, _ = pos
        ins += [rowtab, coltab]
        specs += [pl.BlockSpec((NSEG, D // 2), lambda ib, ic: (cur(ib, ic)[1], 0)), _const_spec(coltab.shape)]
    w_ins, w_specs = _weight_operands(
        weights, ("wf", "wg", "bg", "cl", "lb", "bnw", "cnw", "cws", "cbias", "wo"), layer, FWD)
    ins += [hbf, hbf, xc, hb, ob, q, v, mod] + w_ins + [fnw, h0, s0]
    specs += [_seq_spec(nb, rs, D, fixed(0)), _seq_spec(nb, rs, D, ahead(1))]
    specs += [_seq_spec(nb, rs, w, cur) for w in (AW, AW, BW, BW, BW)]
    specs += [pl.BlockSpec((1, 1, 3 * D), lambda ib, ic: (mrow(ib), 0, 0))]
    specs += w_specs + [_const_spec(fnw.shape)] + _state_specs(nb)
    out_shape = (jax.ShapeDtypeStruct((bsz, seq, D), F32),
                 jax.ShapeDtypeStruct((bsz, 1, AW), F32), jax.ShapeDtypeStruct((bsz, BH, HP, HP), F32))
    out_specs = tuple([_seq_spec(nb, rs, D, cur)] + _state_specs(nb))
    scratch = ([pltpu.VMEM((T, KF_COLS), F32), pltpu.VMEM((T, AW + BW), F32), pltpu.VMEM((T, D), BF16)]
               + [pltpu.VMEM((T, 2 * AW), F32), pltpu.VMEM((T, BW), F32), pltpu.VMEM((T, YW), BF16),
                  pltpu.VMEM((T, D), F32)]
               + _scan_scratch()
               + [pltpu.VMEM((SUBLANES, AW), F32), pltpu.VMEM((nb, BH, HP, HP), F32),
                  pltpu.VMEM((T, AW), BF16), pltpu.VMEM((CCH, CW), BF16), pltpu.VMEM((CH * CCH, CW), F32)]
               + _gla_scratch())
    return pl.pallas_call(
        functools.partial(_kf_body, nb=nb, nc=nc, add_pos=add_pos, final=final),
        grid=(bsz // nb, nc), in_specs=specs, out_specs=out_specs, out_shape=out_shape,
        scratch_shapes=scratch, compiler_params=_params(), name="fwd_sweep")(*ins)


def _pad_heads(w, fill=0.0):
    lead = w.shape[:-1]
    w = w.reshape(lead + (BH, BDK))
    w = jnp.pad(w, [(0, 0)] * len(lead) + [(0, 0), (0, HP - BDK)], constant_values=fill)
    return w.reshape(lead + (BW,))


def _prepare_weights(norm_w, w_in, a_conv_w, a_conv_b, a_wr, a_br, a_wi, a_bi, a_lambda, lbs, b_norm_w,
                     c_norm_w, c_ws, c_bs, w_out):
    depth = w_in.shape[0]
    nbw = BH * BDK
    zero_cols = jnp.zeros((depth, D, HP - BDK), w_in.dtype)
    zero_rows = jnp.zeros((depth, HP - BDK, D), w_out.dtype)

    def group(g):
        base = 2 * AW + g * nbw
        return [p for h in range(BH) for p in (w_in[:, :, base + h * BDK:base + (h + 1) * BDK], zero_cols)]
    wo_b = [p for h in range(BH) for p in (w_out[:, AW + h * BDK:AW + (h + 1) * BDK], zero_rows)]
    eye = jnp.eye(A_HEADS, dtype=F32)

    def block_diag(w):
        return jnp.einsum("ldhij,hg->ldhigj", w, eye).reshape(depth, 2, AW, AW)
    return {
        "nw": norm_w[:, None, :],
        "wb": jnp.concatenate([w_in[:, :, :AW]] + group(0) + group(2) + group(3), axis=-1).astype(BF16),
        "wf": jnp.concatenate([w_in[:, :, AW:2 * AW]] + group(1) + group(4) + [w_in[:, :, 2 * AW + 5 * nbw:]],
                              axis=-1).astype(BF16),
        "cw": a_conv_w, "cb": a_conv_b[:, None, :],
        "lb": _pad_heads(lbs[:depth], fill=1.0)[:, None, :],
        "bnw": _pad_heads(b_norm_w)[:, None, :],
        "cnw": c_norm_w[:, None, :],
        "cws": c_ws.reshape(depth, CH * CCH, CCH).astype(BF16),
        "cbias": jnp.repeat(jnp.swapaxes(c_bs, 1, 2), CHD, axis=2),
        "wo": jnp.concatenate(wo_b + [w_out[:, :AW], w_out[:, AW + nbw:]], axis=1).astype(BF16),
        "wg": jnp.concatenate([block_diag(a_wr), block_diag(a_wi)], axis=-1).astype(BF16),
        "bg": jnp.concatenate([a_br, a_bi], axis=-1)[:, :, None, :],
        "cl": (-RG_C * jax.nn.softplus(-a_lambda))[:, :, None, :],
    }


FWD, BWD = 0, 1
PER_DIRECTION = ("wg", "bg", "cl")


def _weight_operands(weights, names, layer, direction):
    arrays, specs = [], []
    for n in names:
        w = weights[n]
        lead = (layer, direction) if n in PER_DIRECTION else (layer,)
        rest = w.shape[len(lead):]
        arrays.append(w)
        specs.append(pl.BlockSpec((None,) * len(lead) + rest,
                                  lambda ib, ic, lead=lead, rest=rest: lead + (0,) * len(rest),
                                  pipeline_mode=pl.Buffered(1)))
    return arrays, specs


def _pos_tables(rows):
    qd = D // 4
    omega = 1.0 / (10000.0 ** (jnp.arange(qd, dtype=F32) / qd))
    ar = jnp.arange(rows, dtype=F32)[:, None] * omega
    ac = jnp.arange(GRID_W, dtype=F32)[:, None] * omega
    rowtab = jnp.concatenate([jnp.sin(ar), jnp.cos(ar)], axis=-1)
    coltab = jnp.concatenate([jnp.sin(ac), jnp.cos(ac)], axis=-1)
    nc = rows * GRID_W // T
    hrow = jnp.broadcast_to(rowtab[jnp.maximum(jnp.arange(nc) * NSEG - 1, 0)][:, None, :], (nc, SUBLANES, D // 2))
    hcol = jnp.broadcast_to(coltab[GRID_W - SUBLANES:][None], (nc, SUBLANES, D // 2))
    return rowtab, coltab, jnp.concatenate([hrow, hcol], axis=-1)


def kernel(x, c, ctx, c_ctx, norm_w, w_mod, b_mod, w_in, a_conv_w, a_conv_b, a_wr, a_br, a_wi, a_bi, a_lambda,
           b_lb_logits, b_norm_w, c_norm_w, c_ws, c_bs, w_out, final_norm_w):
    bsz, n_lat, _ = x.shape
    depth = w_in.shape[0]
    assert n_lat % T == 0 and T % ctx.shape[1] == 0 and bsz % (T // ctx.shape[1]) == 0
    nb_ctx = T // ctx.shape[1]
    pos = _pos_tables(n_lat // GRID_W)

    mrows = 2 * SUBLANES
    cs = jnp.zeros((mrows, D), F32).at[:bsz].set(c).at[bsz].set(c_ctx)
    nblk = 3
    mod = pl.pallas_call(
        _mod_body, grid=(depth, nblk),
        in_specs=[pl.BlockSpec((mrows, D), lambda l, j: (0, 0)),
                  pl.BlockSpec((1, D, D), lambda l, j: (l, 0, j)),
                  pl.BlockSpec((1, 1, D), lambda l, j: (l, 0, j))],
        out_specs=pl.BlockSpec((1, mrows, D), lambda l, j: (l, 0, j)),
        out_shape=jax.ShapeDtypeStruct((depth, mrows, 3 * D), F32),
        name="modulation")(cs, w_mod, b_mod[:, None, :])
    mod = mod.reshape(depth, mrows, 1, 3 * D)

    lbs = jnp.cumsum(jax.nn.softmax(b_lb_logits, axis=0), axis=0)
    fnw = final_norm_w[None, :]
    h_zero = jnp.zeros((bsz, 1, AW), F32)
    s_zero = jnp.zeros((bsz, BH, HP, HP), F32)
    weights = _prepare_weights(norm_w, w_in, a_conv_w, a_conv_b, a_wr, a_br, a_wi, a_bi, a_lambda, lbs, b_norm_w,
                               c_norm_w, c_ws, c_bs, w_out)
    for l in range(depth):
        lpos = pos if l == 0 else None
        kb = _run_kb(ctx, mod[l], bsz, weights, l, h_zero, s_zero, None, nb_ctx)
        ctx_new, hf_c, sf_c = _run_kf(ctx, kb[:6], mod[l], bsz, weights, l, fnw, h_zero, s_zero, None, nb_ctx, False)
        hb_c, sb_c = kb[6], kb[7]
        kb = _run_kb(x, mod[l], None, weights, l, hb_c, sb_c, lpos, 1)
        x, _, _ = _run_kf(x, kb[:6], mod[l], None, weights, l, fnw, hf_c, sf_c, lpos, 1, l == depth - 1)
        ctx = ctx_new
    return x
```

```python
import functools

import jax
import jax.numpy as jnp
import numpy as np
from jax import lax
from jax.experimental import pallas as pl
from jax.experimental.pallas import tpu as pltpu

D = 1024
GRID_W = 64
EPS = 1e-6
RG_C = 8.0
AW = 384
A_HEADS = 8
A_HD = 48
BH = 4
BDK = 96
CW = 256
CH = 4
CHD = 64
CCH = 128

LANES = 128
SUBLANES = 8
HP = LANES
BW = BH * HP
YW = AW + BW + CW
NA = AW // LANES
T = 512
SEG = 64
NSEG = T // SEG
PITCH = SEG + SUBLANES
KB_COLS = AW + 3 * BW
KF_COLS = AW + 2 * BW + 3 * CW
FILL_COLS = 256
FILL_ROWS = T // 2
ROW_TILE = 16
MLP_TILE = 32
VMEM_LIMIT = 58 * 1024 * 1024

F32 = jnp.float32
BF16 = jnp.bfloat16

C_CONV_SEG, C_COEFF_TILE, C_SCAN_STEP, C_FIX_TILE, C_YA_TILE = 100, 50, 12, 8, 15
C_GATES_TILE, C_CUMSUM_STEP, C_GLA_A_TILE, C_GLA_C_TILE, C_SILU_TILE = 20, 8, 40, 15, 12
C_PRE_TILE, C_MLP_TILE, C_GLA_B_TILE, C_GLA_D_TILE = 25, 90, 40, 40
C_PROJ_PIECE, C_OUT_PIECE, C_NORM_PIECE = FILL_ROWS, 3 * FILL_ROWS // 4, 130
PACE = 1.4


def _sigmoid(x):
    return 0.5 * jnp.tanh(0.5 * x) + 0.5


def _silu(x):
    return x * _sigmoid(x)


def _gelu(x):
    return 0.5 * x * (1.0 + jnp.tanh(np.sqrt(2.0 / np.pi).astype(np.float32) * (x + 0.044715 * (x * x * x))))


def _nt(a, b):
    return lax.dot_general(a, b, (((1,), (1,)), ((), ())), preferred_element_type=F32)


def _tn(a, b):
    return lax.dot_general(a, b, (((0,), (0,)), ((), ())), preferred_element_type=F32)


def _norm_mod(x, nw, sh, sc):
    ms = jnp.mean(x * x, axis=-1, keepdims=True)
    return (x * lax.rsqrt(ms + EPS) * nw) * (1.0 + sc) + sh


def _pos_rows(rt_ref, ct_ref, r0):
    s, c0 = divmod(r0, SEG)
    return jnp.concatenate([jnp.broadcast_to(rt_ref[s:s + 1, :], (ROW_TILE, D // 2)),
                            ct_ref[c0:c0 + ROW_TILE, :]], axis=1)


class _Fill:
    def __init__(self, items=(), rate=1.0):
        self.items = list(items)
        self.rate = rate
        self.credit = 0.0
        self.spent = 0.0

    def __call__(self, credit):
        self.credit += credit
        while self.items and self.spent + 0.5 * self.items[0][0] <= self.credit * self.rate:
            cost, thunk = self.items.pop(0)
            self.spent += cost
            thunk()

    def extend(self, items):
        self.items.extend(items)

    def flush(self):
        while self.items:
            self.items.pop(0)[1]()


def _cost(items):
    return sum(c for c, _ in items)


def _emit(items, fill=None):
    for cost, thunk in items:
        thunk()
        if fill is not None:
            fill(cost)


def _paired(items, partner_items):
    partner = _Fill(partner_items, _cost(partner_items) / max(_cost(items), 1))
    _emit(items, partner)
    partner.flush()


class _Flat:
    def __init__(self, ref, width):
        self.ref = ref
        self.rs = ref.shape[1]
        self.width = width

    def _split(self, idx):
        rows, lanes = idx
        start, size = rows.start, rows.size
        q, off = divmod(start, self.rs)
        assert off + size <= self.rs
        return q, pl.ds(off, size), lanes

    def __getitem__(self, idx):
        q, r, lanes = self._split(idx)
        return self.ref[q, r, lanes]

    def __setitem__(self, idx, val):
        q, r, lanes = self._split(idx)
        self.ref[q, r, lanes] = val


def _tiles():
    return [(s, k) for s in range(NSEG) for k in range(BH)]


class _AScan:
    def __init__(self, a_st, u_st, h_st, p_st, hc_ref, nb, reverse):
        self.refs = (a_st, u_st, h_st, p_st)
        self.hc_ref, self.nb, self.reverse = hc_ref, nb, reverse
        self.hs = [jnp.zeros((NSEG, LANES), F32)] * NA
        self.ps = [jnp.ones((NSEG, LANES), F32)] * NA

    def coeff_items(self, g_scr, xc_ref, cl_ref):
        a_st, u_st = self.refs[:2]

        def item(s, k):
            def run():
                rows, lanes = pl.ds(s * SEG, SEG), slice(k * LANES, (k + 1) * LANES)
                r = _sigmoid(g_scr[rows, lanes])
                gi = _sigmoid(g_scr[rows, AW + k * LANES:AW + (k + 1) * LANES])
                la = r * cl_ref[:, lanes]
                th = jnp.tanh(la)
                om = (-2.0 * th) / (1.0 - th)
                a_st[k, pl.ds(s * PITCH, SEG), :] = jnp.exp(la)
                u_st[k, pl.ds(s * PITCH, SEG), :] = jnp.sqrt(om) * (gi * xc_ref[rows, lanes])
            return C_COEFF_TILE, run
        return [item(s, k) for s in range(NSEG) for k in range(NA)]

    def step_items(self):
        a_st, u_st, h_st, p_st = self.refs

        def item(j):
            def run():
                for k in range(NA):
                    a = a_st[k, pl.ds(j, NSEG, stride=PITCH), :]
                    u = u_st[k, pl.ds(j, NSEG, stride=PITCH), :]
                    self.hs[k] = a * self.hs[k] + u
                    self.ps[k] = a * self.ps[k]
                    h_st[k, pl.ds(j, NSEG, stride=PITCH), :] = self.hs[k]
                    p_st[k, pl.ds(j, NSEG, stride=PITCH), :] = self.ps[k]
            return C_SCAN_STEP, run
        order = range(SEG - 1, -1, -1) if self.reverse else range(SEG)
        return [item(j) for j in order]

    def chain(self):
        spq = NSEG // self.nb
        cin = [None] * NSEG
        for q in range(self.nb):
            c = [self.hc_ref[q:q + 1, k * LANES:(k + 1) * LANES] for k in range(NA)]
            order = range((q + 1) * spq - 1, q * spq - 1, -1) if self.reverse else range(q * spq, (q + 1) * spq)
            for s in order:
                cin[s] = c
                c = [self.ps[k][s:s + 1, :] * c[k] + self.hs[k][s:s + 1, :] for k in range(NA)]
            for k in range(NA):
                self.hc_ref[q:q + 1, k * LANES:(k + 1) * LANES] = c[k]
        return cin

    def full(self, cin, s, k):
        _, _, h_st, p_st = self.refs
        st_rows = pl.ds(s * PITCH, SEG)
        return h_st[k, st_rows, :] + p_st[k, st_rows, :] * cin[s][k]


class _Gla:
    def __init__(self, reverse, nb, k_scr, lf_st, g_st, s_scr, stage):
        self.reverse, self.nb = reverse, nb
        self.k_scr, self.lf_st, self.g_st, self.s_scr = k_scr, lf_st, g_st, s_scr
        self.kg_s, self.ke_s, self.lhs_s, self.kv_s, self.rhs_s = stage
        self.gs = [jnp.zeros((NSEG, LANES), F32)] * BH
        order = range(NSEG - 1, -1, -1) if reverse else range(NSEG)
        self.order = list(order)
        self.tiles = [(n, k) for n in order for k in range(BH)]

    def gate_items(self, p_scr, col0, lb_ref):
        def item(s, k):
            def run():
                rows, lanes = pl.ds(s * SEG, SEG), slice(k * HP, (k + 1) * HP)
                lb = lb_ref[:, lanes]
                f = lb + (1.0 - lb) * _sigmoid(p_scr[rows, col0 + k * HP:col0 + (k + 1) * HP])
                self.k_scr[rows, lanes] = 1.0 - f
                self.lf_st[k, pl.ds(s * PITCH, SEG), :] = jnp.log(f)
            return C_GATES_TILE, run
        return [item(s, k) for s, k in _tiles()]

    def cumsum_items(self):
        def item(j):
            def run():
                for k in range(BH):
                    self.gs[k] = self.gs[k] + self.lf_st[k, pl.ds(j, NSEG, stride=PITCH), :]
                    self.g_st[k, pl.ds(j, NSEG, stride=PITCH), :] = self.gs[k]
            return C_CUMSUM_STEP, run
        order = range(SEG - 1, -1, -1) if self.reverse else range(SEG)
        return [item(j) for j in order]

    def a_items(self, load_q, load_v):
        def item(n, k):
            def run():
                rows, lanes = pl.ds(n * SEG, SEG), slice(k * HP, (k + 1) * HP)
                g = self.g_st[k, pl.ds(n * PITCH, SEG), :]
                tot = self.gs[k][n:n + 1, :]
                qv = load_q(n, k)
                kk = self.k_scr[rows, lanes]
                self.lhs_s[k, rows, 0:HP] = (qv * jnp.exp(g)).astype(BF16)
                self.rhs_s[n, k, HP:HP + SEG, :] = load_v(n, k)
                self.kg_s[rows, lanes] = (kk * jnp.exp(-g)).astype(BF16)
                self.ke_s[rows, lanes] = (kk * jnp.exp(tot - g)).astype(BF16)
            return C_GLA_A_TILE, run
        return [item(n, k) for n, k in self.tiles]

    def b_items(self, load_v):
        r_i = lax.broadcasted_iota(jnp.int32, (SEG, SEG), 0)
        c_i = lax.broadcasted_iota(jnp.int32, (SEG, SEG), 1)
        keep = (c_i >= r_i) if self.reverse else (c_i <= r_i)

        def item(n, k):
            def run():
                rows, lanes = pl.ds(n * SEG, SEG), slice(k * HP, (k + 1) * HP)
                self.lhs_s[k, rows, HP:HP + SEG] = jnp.where(
                    keep, _nt(self.lhs_s[k, rows, 0:HP], self.kg_s[rows, lanes]), 0.0).astype(BF16)
                self.kv_s[n, k] = _tn(load_v(n, k), self.ke_s[rows, lanes])
            return C_GLA_B_TILE, run
        return [item(n, k) for n, k in self.tiles]

    def c_items(self):
        cps = NSEG // self.nb
        state = {}

        def item(n, k):
            def run():
                sq = n // cps
                st = state.get(k)
                if st is None:
                    st = self.s_scr[sq, k]
                self.rhs_s[n, k, 0:HP, :] = st.T.astype(BF16)
                st = jnp.exp(self.gs[k][n:n + 1, :]) * st + self.kv_s[n, k]
                last = (n % cps == 0) if self.reverse else (n % cps == cps - 1)
                if last:
                    self.s_scr[sq, k] = st
                    st = None
                state[k] = st
            return C_GLA_C_TILE, run
        return [item(n, k) for n in self.order for k in range(BH)]

    def d_items(self, emit_out):
        def item(n, k):
            def run():
                rows, lanes = pl.ds(n * SEG, SEG), slice(k * HP, (k + 1) * HP)
                o = jnp.dot(self.lhs_s[k, rows, :], self.rhs_s[n, k], preferred_element_type=F32)
                emit_out(n, k, o)
            return C_GLA_D_TILE, run
        return [item(n, k) for n, k in self.tiles]


def _gla_scratch():
    return ([pltpu.VMEM((T, BW), BF16)] * 2
            + [pltpu.VMEM((BH, T, HP + SEG), BF16), pltpu.VMEM((NSEG, BH, HP, HP), F32),
               pltpu.VMEM((NSEG, BH, HP + SEG, HP), BF16)])


def _dot_items(lhs_rows, w_ref, dst, col_lo, col_hi, cost):
    def piece(r0, c0, c1):
        def run():
            dst[r0:r0 + FILL_ROWS, c0:c1] = jnp.dot(lhs_rows(r0, r0 + FILL_ROWS), w_ref[:, c0:c1],
                                                    preferred_element_type=F32)
        return cost, run
    return [piece(r0, c0, min(c0 + FILL_COLS, col_hi))
            for c0 in range(col_lo, col_hi, FILL_COLS) for r0 in range(0, T, FILL_ROWS)]


def _kb_body(*refs, nb, nc, add_pos):
    it = iter(refs)
    x0_ref, xn_ref = next(it), next(it)
    xh_ref = next(it) if nc > 1 else None
    if add_pos:
        rt0_ref, rtn_ref, ct_ref, hp_ref = next(it), next(it), next(it), next(it)
    else:
        rt0_ref = rtn_ref = ct_ref = hp_ref = None
    (mod_ref, modn_ref, nw_ref, wb_ref, cw_ref, cb_ref, wg_ref, bg_ref, cl_ref, lb_ref, h0_ref, s0_ref,
     hbf_ref, xc_ref, hb_ref, ob_ref, q_ref, v_ref, hfin_ref, sfin_ref,
     p_scr, h_scr, xa_ext, g_scr, k_scr, a_st, u_st, h_st, p_st, lf_st, g_st, hc_ref, s_scr, xaf,
     *gla_stage) = it

    ic = pl.program_id(1)
    flat = pl.program_id(0) * nc + ic
    cc = nc - 1 - ic

    @pl.when(ic == 0)
    def _():
        for q in range(nb):
            hc_ref[q:q + 1, :] = h0_ref[q]
        s_scr[...] = s0_ref[...]
        xaf[...] = jnp.zeros_like(xaf)

    def norm_items(x_r, rt_r, m_r):
        x_flat = _Flat(x_r, D)

        def item(s):
            def run():
                rows = pl.ds(s * SEG, SEG)
                xin = x_flat[rows, slice(None)]
                if add_pos:
                    xin = xin + jnp.concatenate(
                        [_pos_rows(rt_r, ct_ref, r0) for r0 in range(s * SEG, (s + 1) * SEG, ROW_TILE)], axis=0)
                h_scr[rows, :] = _norm_mod(xin, nw_ref[...], m_r[0, :, 0:D], m_r[0, :, D:2 * D]).astype(BF16)
            return C_NORM_PIECE, run
        return [item(s) for s in range(NSEG)]

    def proj_items():
        return _dot_items(lambda r0, r1: h_scr[r0:r1, :], wb_ref, p_scr, 0, KB_COLS, C_PROJ_PIECE)

    gla = _Gla(True, nb, k_scr, lf_st, g_st, s_scr, gla_stage)
    q_flat, v_flat, ob_flat = _Flat(q_ref, BW), _Flat(v_ref, BW), _Flat(ob_ref, BW)

    def hgrn_input_items():
        def qv_item(s, k):
            def run():
                rows, lanes = pl.ds(s * SEG, SEG), slice(k * HP, (k + 1) * HP)
                q_flat[rows, lanes] = _silu(p_scr[rows, AW + k * HP:AW + (k + 1) * HP])
                v_flat[rows, lanes] = p_scr[rows, AW + 2 * BW + k * HP:AW + 2 * BW + (k + 1) * HP].astype(BF16)
            return C_PRE_TILE, run
        return [qv_item(s, k) for s, k in _tiles()] + gla.gate_items(p_scr, AW + BW, lb_ref)

    @pl.when(flat == 0)
    def _():
        _emit(norm_items(x0_ref, rt0_ref, mod_ref) + proj_items())

    def step():
        xc_flat, hb_flat = _Flat(xc_ref, AW), _Flat(hb_ref, AW)
        scan = _AScan(a_st, u_st, h_st, p_st, hc_ref, nb, reverse=True)

        hbf_ref[...] = h_scr[...].reshape(hbf_ref.shape)
        if nc > 1:
            xh = xh_ref[0]
            if add_pos:
                xh = xh + hp_ref[0]
            hh = _norm_mod(xh, nw_ref[...], mod_ref[0, :, 0:D], mod_ref[0, :, D:2 * D]).astype(BF16)
            left = jnp.dot(hh, wb_ref[:, 0:AW], preferred_element_type=F32)
            xa_ext[0:SUBLANES, :] = jnp.where(cc > 0, left, 0.0)
        else:
            xa_ext[0:SUBLANES, :] = jnp.zeros((SUBLANES, AW), F32)
        xa_ext[pl.ds(SUBLANES, T), :] = p_scr[:, 0:AW]
        xa_ext[pl.ds(SUBLANES + T, SUBLANES), :] = xaf[...]
        xaf[...] = p_scr[0:SUBLANES, 0:AW]

        def conv_item(s):
            def run():
                spq = NSEG // nb
                row = lax.broadcasted_iota(jnp.int32, (SEG, 1), 0)
                acc = jnp.broadcast_to(cb_ref[...], (SEG, AW))
                for j in range(4):
                    tap = xa_ext[pl.ds(s * SEG + SUBLANES - 2 + j, SEG), :]
                    if nb > 1 and s % spq == 0 and j < 2:
                        tap = jnp.where(row + (j - 2) >= 0, tap, 0.0)
                    if nb > 1 and s % spq == spq - 1 and j == 3:
                        tap = jnp.where(row + 1 < SEG, tap, 0.0)
                    acc = acc + cw_ref[j:j + 1, :] * tap
                xc_flat[pl.ds(s * SEG, SEG), slice(None)] = acc
            return C_CONV_SEG, run

        def load_q(n, k):
            return q_flat[pl.ds(n * SEG, SEG), slice(k * HP, (k + 1) * HP)]

        def load_v(n, k):
            return v_flat[pl.ds(n * SEG, SEG), slice(k * HP, (k + 1) * HP)]

        def emit_out(n, k, o):
            ob_flat[pl.ds(n * SEG, SEG), slice(k * HP, (k + 1) * HP)] = o

        _emit(hgrn_input_items())
        _emit(norm_items(xn_ref, rtn_ref, modn_ref))

        conv = [conv_item(s) for s in range(NSEG)]
        coeff, steps, cums = scan.coeff_items(g_scr, xc_flat, cl_ref), scan.step_items(), gla.cumsum_items()
        proj = proj_items()
        credit = (_cost(conv) + _cost(coeff) + _cost(steps) + _cost(cums)
                  + BH * NSEG * (C_GLA_A_TILE + C_GLA_C_TILE) + NA * NSEG * C_FIX_TILE)
        mx = _Fill(proj, PACE * _cost(proj) / credit)
        _emit(conv, mx)
        g_scr[...] = jnp.dot(xc_ref[...].reshape(T, AW).astype(BF16), wg_ref[...],
                             preferred_element_type=F32) + bg_ref[...]
        _emit(coeff, mx)
        _emit(steps, mx)
        cin = scan.chain()
        for q in range(nb):
            hfin_ref[q] = hc_ref[q:q + 1, :]

        def fix_item(s, k):
            def run():
                hb_flat[pl.ds(s * SEG, SEG), k * LANES:(k + 1) * LANES] = scan.full(cin, s, k)
            return C_FIX_TILE, run
        _emit([fix_item(s, k) for s in range(NSEG) for k in range(NA)], mx)

        _emit(cums, mx)
        _emit(gla.a_items(load_q, load_v), mx)
        _emit(gla.b_items(load_v))
        _emit(gla.c_items(), mx)
        _emit(gla.d_items(emit_out))
        sfin_ref[...] = s_scr[...]
        mx.flush()

    step()


def _kf_body(*refs, nb, nc, add_pos, final):
    it = iter(refs)
    x_ref = next(it)
    if add_pos:
        rt_ref, ct_ref = next(it), next(it)
    (hbf0_ref, hbfn_ref, xc_ref, hb_ref, ob_ref, q_ref, v_ref, mod_ref, wf_ref, wg_ref, bg_ref, cl_ref, lb_ref,
     bnw_ref, cnw_ref, cws_ref, cbias_ref, wo_ref, fnw_ref, h0_ref, s0_ref,
     xo_ref, hfin_ref, sfin_ref,
     p_scr, sg_scr, hn_scr, g_scr, k_scr, y_scr, part_scr, a_st, u_st, h_st, p_st, lf_st, g_st, hc_ref, s_scr,
     xcb_scr, vn_scr, mix_scr, *gla_stage) = it

    ic = pl.program_id(1)
    flat = pl.program_id(0) * nc + ic

    @pl.when(ic == 0)
    def _():
        for q in range(nb):
            hc_ref[q:q + 1, :] = h0_ref[q]
        s_scr[...] = s0_ref[...]

    o_ag, o_ff, o_bg = 0, AW, AW + BW
    o_cu, o_cv, o_cg = AW + 2 * BW, AW + 2 * BW + CW, AW + 2 * BW + 2 * CW

    def proj_items(col_lo, col_hi):
        return _dot_items(lambda r0, r1: hn_scr[r0:r1, :], wf_ref, p_scr, col_lo, col_hi, C_PROJ_PIECE)

    @pl.when(flat == 0)
    def _():
        hn_scr[...] = hbf0_ref[...].reshape(T, D)
        _emit(proj_items(0, KF_COLS))

    def step():
        xc_flat, hb_flat = _Flat(xc_ref, AW), _Flat(hb_ref, AW)
        q_flat, v_flat, ob_flat = _Flat(q_ref, BW), _Flat(v_ref, BW), _Flat(ob_ref, BW)
        scan = _AScan(a_st, u_st, h_st, p_st, hc_ref, nb, reverse=False)
        gla = _Gla(False, nb, k_scr, lf_st, g_st, s_scr, gla_stage)

        hn_scr[...] = hbfn_ref[...].reshape(T, D)
        for s in range(NSEG):
            rows = pl.ds(s * SEG, SEG)
            xcb_scr[rows, :] = xc_flat[rows, slice(None)].astype(BF16)

        def silu_item(s, c0):
            def run():
                rows = pl.ds(s * SEG, SEG)
                dst = c0 - o_ag if c0 < o_ff else AW + c0 - o_bg
                sg_scr[rows, dst:dst + LANES] = _silu(p_scr[rows, c0:c0 + LANES])
            return C_SILU_TILE, run
        gate_cols = list(range(o_ag, o_ag + AW, LANES)) + list(range(o_bg, o_bg + BW, LANES))

        def mlp_items(m):
            lane = lax.broadcasted_iota(jnp.int32, (MLP_TILE, CW), 1)

            def norm_item(r0):
                def run():
                    v = _gelu(p_scr[pl.ds(m * CCH + r0, MLP_TILE), o_cv:o_cv + CW])
                    mu = jnp.mean(v, axis=-1, keepdims=True)
                    dv = v - mu
                    var = jnp.mean(dv * dv, axis=-1, keepdims=True)
                    vn_scr[r0:r0 + MLP_TILE, :] = (dv * lax.rsqrt(var + EPS) * cnw_ref[...]).astype(BF16)
                return C_MLP_TILE, run

            def mix():
                mix_scr[...] = jnp.dot(cws_ref[...], vn_scr[...], preferred_element_type=F32)

            def out_item(r0):
                def run():
                    rows = pl.ds(m * CCH + r0, MLP_TILE)
                    mixed = mix_scr[r0:r0 + MLP_TILE, :]
                    for h in range(1, CH):
                        mixed = jnp.where(lane >= h * CHD, mix_scr[h * CCH + r0:h * CCH + r0 + MLP_TILE, :], mixed)
                    u = _gelu(p_scr[rows, o_cu:o_cu + CW])
                    yc = u * (mixed + cbias_ref[r0:r0 + MLP_TILE, :]) * _silu(p_scr[rows, o_cg:o_cg + CW])
                    y_scr[rows, BW + AW:YW] = yc.astype(BF16)
                return C_MLP_TILE, run
            tiles = range(0, CCH, MLP_TILE)
            return [norm_item(r0) for r0 in tiles] + [(0, mix)] + [out_item(r0) for r0 in tiles]

        def load_q(n, k):
            return q_flat[pl.ds(n * SEG, SEG), slice(k * HP, (k + 1) * HP)]

        def load_v(n, k):
            return v_flat[pl.ds(n * SEG, SEG), slice(k * HP, (k + 1) * HP)]

        def emit_out(n, k, o):
            rows, lanes = pl.ds(n * SEG, SEG), slice(k * HP, (k + 1) * HP)
            o = o + ob_flat[rows, lanes]
            ms = jnp.sum(o * o, axis=-1, keepdims=True) * (1.0 / BDK)
            yb = (o * lax.rsqrt(ms + EPS) * bnw_ref[:, lanes]) * sg_scr[rows, AW + k * HP:AW + (k + 1) * HP]
            y_scr[rows, lanes] = yb.astype(BF16)

        g_scr[...] = jnp.dot(xcb_scr[...], wg_ref[...], preferred_element_type=F32) + bg_ref[...]
        _emit([it for m in range(T // CCH) for it in mlp_items(m)])

        pre = [silu_item(s, c0) for s in range(NSEG) for c0 in gate_cols] + gla.gate_items(p_scr, o_ff, lb_ref)
        coeff, steps, cums = scan.coeff_items(g_scr, xc_flat, cl_ref), scan.step_items(), gla.cumsum_items()
        n_proj = -(-KF_COLS // FILL_COLS) * (T // FILL_ROWS)
        n_out = (D // FILL_COLS) * (T // FILL_ROWS)
        credit = (_cost(pre) + _cost(coeff) + _cost(steps) + _cost(cums)
                  + BH * NSEG * (C_GLA_A_TILE + C_GLA_C_TILE + C_GLA_D_TILE) + NA * NSEG * C_YA_TILE)
        mx = _Fill(proj_items(o_cu, KF_COLS), PACE * (n_proj * C_PROJ_PIECE + n_out * C_OUT_PIECE) / credit)
        _emit(pre, mx)
        mx.extend(proj_items(0, o_cu))

        _emit(coeff, mx)
        _emit(steps, mx)
        cin = scan.chain()
        for q in range(nb):
            hfin_ref[q] = hc_ref[q:q + 1, :]

        def ya_item(s, k):
            def run():
                rows, lanes = pl.ds(s * SEG, SEG), slice(k * LANES, (k + 1) * LANES)
                ya = (scan.full(cin, s, k) + hb_flat[rows, lanes]) * sg_scr[rows, lanes]
                y_scr[rows, BW + k * LANES:BW + (k + 1) * LANES] = ya.astype(BF16)
            return C_YA_TILE, run
        _emit([ya_item(s, k) for s in range(NSEG) for k in range(NA)], mx)
        mx.extend(_dot_items(lambda r0, r1: y_scr[r0:r1, BW:YW], wo_ref.at[BW:YW, :], part_scr, 0, D, C_OUT_PIECE))

        _emit(cums, mx)
        _emit(gla.a_items(load_q, load_v), mx)
        _emit(gla.b_items(load_v))
        _emit(gla.c_items(), mx)
        _emit(gla.d_items(emit_out), mx)
        sfin_ref[...] = s_scr[...]
        mx.flush()

        gm = mod_ref[0, :, 2 * D:3 * D]
        x_flat, xo_flat = _Flat(x_ref, D), _Flat(xo_ref, D)
        def out_item(r0, c0):
            def run():
                rows, cols = pl.ds(r0, FILL_ROWS), slice(c0, c0 + FILL_COLS)
                part_scr[rows, cols] = part_scr[rows, cols] + jnp.dot(y_scr[rows, 0:BW], wo_ref[0:BW, cols],
                                                                      preferred_element_type=F32)
            return C_OUT_PIECE, run

        def residual_item(r0):
            def run():
                rows = pl.ds(r0, ROW_TILE)
                xin = x_flat[rows, slice(None)]
                if add_pos:
                    xin = xin + _pos_rows(rt_ref, ct_ref, r0)
                xn = xin + gm * part_scr[rows, :]
                if final:
                    ms = jnp.mean(xn * xn, axis=-1, keepdims=True)
                    xn = xn * lax.rsqrt(ms + EPS) * fnw_ref[...]
                xo_flat[rows, slice(None)] = xn
            return C_OUT_PIECE * (D // FILL_COLS) * ROW_TILE // FILL_ROWS, run

        pending = []
        for r0 in range(0, T, FILL_ROWS):
            _paired([out_item(r0, c0) for c0 in range(0, D, FILL_COLS)], pending)
            pending = [residual_item(r) for r in range(r0, r0 + FILL_ROWS, ROW_TILE)]
        _emit(pending)

    step()


def _mod_body(cs_ref, w_ref, b_ref, o_ref):
    s = _silu(cs_ref[...])
    o_ref[0] = jnp.dot(s, w_ref[0], preferred_element_type=F32, precision=lax.Precision.HIGHEST) + b_ref[0]


def _const_spec(shape):
    nd = len(shape)
    return pl.BlockSpec(shape, lambda ib, ic: (0,) * nd, pipeline_mode=pl.Buffered(1))


def _seq_spec(nb, rs, width, bc):
    return pl.BlockSpec((nb, rs, width), lambda ib, ic: bc(ib, ic) + (0,))


def _step_maps(ngroups, nc, reverse):
    last = ngroups * nc - 1

    def at(f):
        return f // nc, (nc - 1 - f % nc) if reverse else f % nc

    def cur(ib, ic):
        return at(ib * nc + ic)

    def ahead(d):
        return lambda ib, ic: at(jnp.minimum(ib * nc + ic + d, last))

    def fixed(f):
        return lambda ib, ic: at(min(f, last))
    return cur, ahead, fixed


def _scan_scratch():
    return ([pltpu.VMEM((NA, NSEG * PITCH, LANES), F32)] * 4
            + [pltpu.VMEM((BH, NSEG * PITCH, LANES), F32)] * 2)


def _params():
    return pltpu.CompilerParams(dimension_semantics=("arbitrary", "arbitrary"), vmem_limit_bytes=VMEM_LIMIT)


def _state_specs(nb):
    return [pl.BlockSpec((nb, 1, AW), lambda ib, ic: (ib, 0, 0)),
            pl.BlockSpec((nb, BH, HP, HP), lambda ib, ic: (ib, 0, 0, 0))]


def _run_kb(x, mod, mod_row, weights, layer, h0, s0, pos, nb):
    bsz, seq, _ = x.shape
    rs = T // nb
    nc = seq // rs
    add_pos = pos is not None
    cur, ahead, fixed = _step_maps(bsz // nb, nc, reverse=True)
    steps2 = (fixed(0), ahead(1))
    mrow = (lambda g: g) if mod_row is None else (lambda g: mod_row)
    mod_spec = lambda bc: pl.BlockSpec((1, 1, 3 * D), lambda ib, ic: (mrow(bc(ib, ic)[0]), 0, 0))
    ins, specs = [x] * 2, [_seq_spec(nb, rs, D, bc) for bc in steps2]
    if nc > 1:
        ins.append(x)
        specs.append(pl.BlockSpec(
            (1, SUBLANES, D), lambda ib, ic: (ib, jnp.maximum(cur(ib, ic)[1] * (rs // SUBLANES) - 1, 0), 0)))
    if add_pos:
        rowtab, coltab, hpos = pos
        row_spec = lambda bc: pl.BlockSpec((NSEG, D // 2), lambda ib, ic: (bc(ib, ic)[1], 0))
        ins += [rowtab] * 2 + [coltab, hpos]
        specs += [row_spec(bc) for bc in steps2] + [
            _const_spec(coltab.shape), pl.BlockSpec((1, SUBLANES, D), lambda ib, ic: (cur(ib, ic)[1], 0, 0))]
    w_ins, w_specs = _weight_operands(weights, ("nw", "wb", "cw", "cb", "wg", "bg", "cl", "lb"), layer, BWD)
    ins += [mod] * 2 + w_ins + [h0, s0]
    specs += [mod_spec(cur), mod_spec(ahead(1))] + w_specs + _state_specs(nb)
    out_shape = (jax.ShapeDtypeStruct((bsz, seq, D), BF16), jax.ShapeDtypeStruct((bsz, seq, AW), F32),
                 jax.ShapeDtypeStruct((bsz, seq, AW), F32), jax.ShapeDtypeStruct((bsz, seq, BW), F32),
                 jax.ShapeDtypeStruct((bsz, seq, BW), F32), jax.ShapeDtypeStruct((bsz, seq, BW), BF16),
                 jax.ShapeDtypeStruct((bsz, 1, AW), F32), jax.ShapeDtypeStruct((bsz, BH, HP, HP), F32))
    out_specs = tuple([_seq_spec(nb, rs, w, cur) for w in (D, AW, AW, BW, BW, BW)] + _state_specs(nb))
    scratch = ([pltpu.VMEM((T, KB_COLS), F32), pltpu.VMEM((T, D), BF16)]
               + [pltpu.VMEM((T + 2 * SUBLANES, AW), F32), pltpu.VMEM((T, 2 * AW), F32), pltpu.VMEM((T, BW), F32)]
               + _scan_scratch()
               + [pltpu.VMEM((SUBLANES, AW), F32), pltpu.VMEM((nb, BH, HP, HP), F32),
                  pltpu.VMEM((SUBLANES, AW), F32)]
               + _gla_scratch())
    return pl.pallas_call(
        functools.partial(_kb_body, nb=nb, nc=nc, add_pos=add_pos),
        grid=(bsz // nb, nc), in_specs=specs, out_specs=out_specs, out_shape=out_shape,
        scratch_shapes=scratch, compiler_params=_params(), name="bwd_sweep")(*ins)


def _run_kf(x, kb_out, mod, mod_row, weights, layer, fnw, h0, s0, pos, nb, final):
    bsz, seq, _ = x.shape
    rs = T // nb
    nc = seq // rs
    add_pos = pos is not None
    cur, ahead, fixed = _step_maps(bsz // nb, nc, reverse=False)
    mrow = (lambda g: g) if mod_row is None else (lambda g: mod_row)
    hbf, xc, hb, ob, q, v = kb_out
    ins, specs = [x], [_seq_spec(nb, rs, D, cur)]
    if add_pos:
        rowtab, coltab, _ = pos
        ins += [rowtab, coltab]
        specs += [pl.BlockSpec((NSEG, D // 2), lambda ib, ic: (cur(ib, ic)[1], 0)), _const_spec(coltab.shape)]
    w_ins, w_specs = _weight_operands(
        weights, ("wf", "wg", "bg", "cl", "lb", "bnw", "cnw", "cws", "cbias", "wo"), layer, FWD)
    ins += [hbf, hbf, xc, hb, ob, q, v, mod] + w_ins + [fnw, h0, s0]
    specs += [_seq_spec(nb, rs, D, fixed(0)), _seq_spec(nb, rs, D, ahead(1))]
    specs += [_seq_spec(nb, rs, w, cur) for w in (AW, AW, BW, BW, BW)]
    specs += [pl.BlockSpec((1, 1, 3 * D), lambda ib, ic: (mrow(ib), 0, 0))]
    specs += w_specs + [_const_spec(fnw.shape)] + _state_specs(nb)
    out_shape = (jax.ShapeDtypeStruct((bsz, seq, D), F32),
                 jax.ShapeDtypeStruct((bsz, 1, AW), F32), jax.ShapeDtypeStruct((bsz, BH, HP, HP), F32))
    out_specs = tuple([_seq_spec(nb, rs, D, cur)] + _state_specs(nb))
    scratch = ([pltpu.VMEM((T, KF_COLS), F32), pltpu.VMEM((T, AW + BW), F32), pltpu.VMEM((T, D), BF16)]
               + [pltpu.VMEM((T, 2 * AW), F32), pltpu.VMEM((T, BW), F32), pltpu.VMEM((T, YW), BF16),
                  pltpu.VMEM((T, D), F32)]
               + _scan_scratch()
               + [pltpu.VMEM((SUBLANES, AW), F32), pltpu.VMEM((nb, BH, HP, HP), F32),
                  pltpu.VMEM((T, AW), BF16), pltpu.VMEM((CCH, CW), BF16), pltpu.VMEM((CH * CCH, CW), F32)]
               + _gla_scratch())
    return pl.pallas_call(
        functools.partial(_kf_body, nb=nb, nc=nc, add_pos=add_pos, final=final),
        grid=(bsz // nb, nc), in_specs=specs, out_specs=out_specs, out_shape=out_shape,
        scratch_shapes=scratch, compiler_params=_params(), name="fwd_sweep")(*ins)


def _pad_heads(w, fill=0.0):
    lead = w.shape[:-1]
    w = w.reshape(lead + (BH, BDK))
    w = jnp.pad(w, [(0, 0)] * len(lead) + [(0, 0), (0, HP - BDK)], constant_values=fill)
    return w.reshape(lead + (BW,))


def _prepare_weights(norm_w, w_in, a_conv_w, a_conv_b, a_wr, a_br, a_wi, a_bi, a_lambda, lbs, b_norm_w,
                     c_norm_w, c_ws, c_bs, w_out):
    depth = w_in.shape[0]
    nbw = BH * BDK
    w_in, w_out = w_in.astype(BF16), w_out.astype(BF16)
    zero_cols = jnp.zeros((depth, D, HP - BDK), BF16)
    zero_rows = jnp.zeros((depth, HP - BDK, D), BF16)

    def group(g):
        base = 2 * AW + g * nbw
        return [p for h in range(BH) for p in (w_in[:, :, base + h * BDK:base + (h + 1) * BDK], zero_cols)]
    wo_b = [p for h in range(BH) for p in (w_out[:, AW + h * BDK:AW + (h + 1) * BDK], zero_rows)]
    eye = jnp.eye(A_HEADS, dtype=F32)

    def block_diag(w):
        return jnp.einsum("ldhij,hg->ldhigj", w, eye).reshape(depth, 2, AW, AW)
    return {
        "nw": norm_w[:, None, :],
        "wb": jnp.concatenate([w_in[:, :, :AW]] + group(0) + group(2) + group(3), axis=-1),
        "wf": jnp.concatenate([w_in[:, :, AW:2 * AW]] + group(1) + group(4) + [w_in[:, :, 2 * AW + 5 * nbw:]],
                              axis=-1),
        "cw": a_conv_w, "cb": a_conv_b[:, None, :],
        "lb": _pad_heads(lbs[:depth], fill=1.0)[:, None, :],
        "bnw": _pad_heads(b_norm_w)[:, None, :],
        "cnw": c_norm_w[:, None, :],
        "cws": c_ws.reshape(depth, CH * CCH, CCH).astype(BF16),
        "cbias": jnp.repeat(jnp.swapaxes(c_bs, 1, 2), CHD, axis=2),
        "wo": jnp.concatenate(wo_b + [w_out[:, :AW], w_out[:, AW + nbw:]], axis=1),
        "wg": jnp.concatenate([block_diag(a_wr), block_diag(a_wi)], axis=-1).astype(BF16),
        "bg": jnp.concatenate([a_br, a_bi], axis=-1)[:, :, None, :],
        "cl": (-RG_C * jax.nn.softplus(-a_lambda))[:, :, None, :],
    }


FWD, BWD = 0, 1
PER_DIRECTION = ("wg", "bg", "cl")


def _weight_operands(weights, names, layer, direction):
    arrays, specs = [], []
    for n in names:
        w = weights[n]
        lead = (layer, direction) if n in PER_DIRECTION else (layer,)
        rest = w.shape[len(lead):]
        arrays.append(w)
        specs.append(pl.BlockSpec((None,) * len(lead) + rest,
                                  lambda ib, ic, lead=lead, rest=rest: lead + (0,) * len(rest),
                                  pipeline_mode=pl.Buffered(1)))
    return arrays, specs


def _pos_tables(rows):
    qd = D // 4
    omega = 1.0 / (10000.0 ** (jnp.arange(qd, dtype=F32) / qd))
    ar = jnp.arange(rows, dtype=F32)[:, None] * omega
    ac = jnp.arange(GRID_W, dtype=F32)[:, None] * omega
    rowtab = jnp.concatenate([jnp.sin(ar), jnp.cos(ar)], axis=-1)
    coltab = jnp.concatenate([jnp.sin(ac), jnp.cos(ac)], axis=-1)
    nc = rows * GRID_W // T
    hrow = jnp.broadcast_to(rowtab[jnp.maximum(jnp.arange(nc) * NSEG - 1, 0)][:, None, :], (nc, SUBLANES, D // 2))
    hcol = jnp.broadcast_to(coltab[GRID_W - SUBLANES:][None], (nc, SUBLANES, D // 2))
    return rowtab, coltab, jnp.concatenate([hrow, hcol], axis=-1)


def kernel(x, c, ctx, c_ctx, norm_w, w_mod, b_mod, w_in, a_conv_w, a_conv_b, a_wr, a_br, a_wi, a_bi, a_lambda,
           b_lb_logits, b_norm_w, c_norm_w, c_ws, c_bs, w_out, final_norm_w):
    bsz, n_lat, _ = x.shape
    depth = w_in.shape[0]
    assert n_lat % T == 0 and T % ctx.shape[1] == 0 and bsz % (T // ctx.shape[1]) == 0
    nb_ctx = T // ctx.shape[1]
    pos = _pos_tables(n_lat // GRID_W)

    mrows = 2 * SUBLANES
    cs = jnp.zeros((mrows, D), F32).at[:bsz].set(c).at[bsz].set(c_ctx)
    nblk = 3
    mod = pl.pallas_call(
        _mod_body, grid=(depth, nblk),
        in_specs=[pl.BlockSpec((mrows, D), lambda l, j: (0, 0)),
                  pl.BlockSpec((1, D, D), lambda l, j: (l, 0, j)),
                  pl.BlockSpec((1, 1, D), lambda l, j: (l, 0, j))],
        out_specs=pl.BlockSpec((1, mrows, D), lambda l, j: (l, 0, j)),
        out_shape=jax.ShapeDtypeStruct((depth, mrows, 3 * D), F32),
        name="modulation")(cs, w_mod, b_mod[:, None, :])
    mod = mod.reshape(depth, mrows, 1, 3 * D)

    lbs = jnp.cumsum(jax.nn.softmax(b_lb_logits, axis=0), axis=0)
    fnw = final_norm_w[None, :]
    h_zero = jnp.zeros((bsz, 1, AW), F32)
    s_zero = jnp.zeros((bsz, BH, HP, HP), F32)
    weights = _prepare_weights(norm_w, w_in, a_conv_w, a_conv_b, a_wr, a_br, a_wi, a_bi, a_lambda, lbs, b_norm_w,
                               c_norm_w, c_ws, c_bs, w_out)
    for l in range(depth):
        lpos = pos if l == 0 else None
        kb = _run_kb(ctx, mod[l], bsz, weights, l, h_zero, s_zero, None, nb_ctx)
        ctx_new, hf_c, sf_c = _run_kf(ctx, kb[:6], mod[l], bsz, weights, l, fnw, h_zero, s_zero, None, nb_ctx, False)
        hb_c, sb_c = kb[6], kb[7]
        kb = _run_kb(x, mod[l], None, weights, l, hb_c, sb_c, lpos, 1)
        x, _, _ = _run_kf(x, kb[:6], mod[l], None, weights, l, fnw, hf_c, sf_c, lpos, 1, l == depth - 1)
        ctx = ctx_new
    return x
```

```python
import functools

import jax
import jax.numpy as jnp
import numpy as np
from jax import lax
from jax.experimental import pallas as pl
from jax.experimental.pallas import tpu as pltpu

D = 1024
GRID_W = 64
EPS = 1e-6
RG_C = 8.0
AW = 384
A_HEADS = 8
A_HD = 48
BH = 4
BDK = 96
CW = 256
CH = 4
CHD = 64
CCH = 128

LANES = 128
SUBLANES = 8
HP = LANES
BW = BH * HP
YW = AW + BW + CW
NA = AW // LANES
T = 512
SEG = 64
NSEG = T // SEG
PITCH = SEG + SUBLANES
KB_COLS = AW + 3 * BW
KF_COLS = AW + 2 * BW + 3 * CW
FILL_COLS = 256
FILL_ROWS = T // 2
ROW_TILE = 16
MLP_TILE = 32
VMEM_LIMIT = 58 * 1024 * 1024

F32 = jnp.float32
BF16 = jnp.bfloat16

C_CONV_SEG, C_COEFF_TILE, C_SCAN_STEP, C_FIX_TILE, C_YA_TILE = 100, 50, 12, 8, 15
C_GATES_TILE, C_CUMSUM_STEP, C_GLA_A_TILE, C_GLA_C_TILE, C_SILU_TILE = 20, 8, 40, 15, 12
C_PRE_TILE, C_MLP_TILE, C_GLA_B_TILE, C_GLA_D_TILE = 25, 90, 40, 40
C_PROJ_PIECE, C_OUT_PIECE, C_NORM_PIECE = FILL_ROWS, 3 * FILL_ROWS // 4, 130
PACE = 1.4


def _silu(x):
    hx = 0.5 * x
    return hx * (jnp.tanh(hx) + 1.0)


def _gelu(x):
    c = np.float32(np.sqrt(2.0 / np.pi))
    hx = 0.5 * x
    return hx * (jnp.tanh(x * (c + (0.044715 * c) * (x * x))) + 1.0)


def _nt(a, b):
    return lax.dot_general(a, b, (((1,), (1,)), ((), ())), preferred_element_type=F32)


def _tn(a, b):
    return lax.dot_general(a, b, (((0,), (0,)), ((), ())), preferred_element_type=F32)


def _norm_mod(x, nw, sh, sc):
    ms = jnp.mean(x * x, axis=-1, keepdims=True)
    return (x * lax.rsqrt(ms + EPS) * nw) * (1.0 + sc) + sh


def _pos_rows(rt_ref, ct_ref, r0):
    s, c0 = divmod(r0, SEG)
    return jnp.concatenate([jnp.broadcast_to(rt_ref[s:s + 1, :], (ROW_TILE, D // 2)),
                            ct_ref[c0:c0 + ROW_TILE, :]], axis=1)


class _Fill:
    def __init__(self, items=(), rate=1.0):
        self.items = list(items)
        self.rate = rate
        self.credit = 0.0
        self.spent = 0.0

    def __call__(self, credit):
        self.credit += credit
        while self.items and self.spent + 0.5 * self.items[0][0] <= self.credit * self.rate:
            cost, thunk = self.items.pop(0)
            self.spent += cost
            thunk()

    def extend(self, items):
        self.items.extend(items)

    def flush(self):
        while self.items:
            self.items.pop(0)[1]()


def _cost(items):
    return sum(c for c, _ in items)


def _emit(items, fill=None):
    for cost, thunk in items:
        thunk()
        if fill is not None:
            fill(cost)


def _paired(items, partner_items):
    partner = _Fill(partner_items, _cost(partner_items) / max(_cost(items), 1))
    _emit(items, partner)
    partner.flush()


class _Flat:
    def __init__(self, ref, width):
        self.ref = ref
        self.rs = ref.shape[1]
        self.width = width

    def _split(self, idx):
        rows, lanes = idx
        start, size = rows.start, rows.size
        q, off = divmod(start, self.rs)
        assert off + size <= self.rs
        return q, pl.ds(off, size), lanes

    def __getitem__(self, idx):
        q, r, lanes = self._split(idx)
        return self.ref[q, r, lanes]

    def __setitem__(self, idx, val):
        q, r, lanes = self._split(idx)
        self.ref[q, r, lanes] = val


def _tiles():
    return [(s, k) for s in range(NSEG) for k in range(BH)]


class _AScan:
    def __init__(self, a_st, u_st, h_st, p_st, hc_ref, nb, reverse):
        self.refs = (a_st, u_st, h_st, p_st)
        self.hc_ref, self.nb, self.reverse = hc_ref, nb, reverse
        self.hs = [jnp.zeros((NSEG, LANES), F32)] * NA
        self.ps = [jnp.ones((NSEG, LANES), F32)] * NA

    def coeff_items(self, g_scr, xc_ref, cl_ref):
        a_st, u_st = self.refs[:2]

        def item(s, k):
            def run():
                rows, lanes = pl.ds(s * SEG, SEG), slice(k * LANES, (k + 1) * LANES)
                tr = jnp.tanh(0.5 * g_scr[rows, lanes])
                ti = jnp.tanh(0.5 * g_scr[rows, AW + k * LANES:AW + (k + 1) * LANES])
                hcl = 0.5 * cl_ref[:, lanes]
                la = hcl * tr + hcl
                th = jnp.tanh(la)
                om4 = (-0.5 * th) / (1.0 - th)
                a_st[k, pl.ds(s * PITCH, SEG), :] = jnp.exp(la)
                u_st[k, pl.ds(s * PITCH, SEG), :] = jnp.sqrt(om4) * ((ti + 1.0) * xc_ref[rows, lanes])
            return C_COEFF_TILE, run
        return [item(s, k) for s in range(NSEG) for k in range(NA)]

    def step_items(self):
        a_st, u_st, h_st, p_st = self.refs

        def item(j):
            def run():
                for k in range(NA):
                    a = a_st[k, pl.ds(j, NSEG, stride=PITCH), :]
                    u = u_st[k, pl.ds(j, NSEG, stride=PITCH), :]
                    self.hs[k] = a * self.hs[k] + u
                    self.ps[k] = a * self.ps[k]
                    h_st[k, pl.ds(j, NSEG, stride=PITCH), :] = self.hs[k]
                    p_st[k, pl.ds(j, NSEG, stride=PITCH), :] = self.ps[k]
            return C_SCAN_STEP, run
        order = range(SEG - 1, -1, -1) if self.reverse else range(SEG)
        return [item(j) for j in order]

    def chain(self):
        spq = NSEG // self.nb
        cin = [None] * NSEG
        for q in range(self.nb):
            c = [self.hc_ref[q:q + 1, k * LANES:(k + 1) * LANES] for k in range(NA)]
            order = range((q + 1) * spq - 1, q * spq - 1, -1) if self.reverse else range(q * spq, (q + 1) * spq)
            for s in order:
                cin[s] = c
                c = [self.ps[k][s:s + 1, :] * c[k] + self.hs[k][s:s + 1, :] for k in range(NA)]
            for k in range(NA):
                self.hc_ref[q:q + 1, k * LANES:(k + 1) * LANES] = c[k]
        return cin

    def full(self, cin, s, k):
        _, _, h_st, p_st = self.refs
        st_rows = pl.ds(s * PITCH, SEG)
        return h_st[k, st_rows, :] + p_st[k, st_rows, :] * cin[s][k]


class _Gla:
    def __init__(self, reverse, nb, k_scr, lf_st, g_st, s_scr, stage, states_only=False):
        self.reverse, self.nb, self.states_only = reverse, nb, states_only
        self.k_scr, self.lf_st, self.g_st, self.s_scr = k_scr, lf_st, g_st, s_scr
        self.kg_s, self.ke_s, self.lhs_s, self.kv_s, self.rhs_s = stage
        self.gs = [jnp.zeros((NSEG, LANES), F32)] * BH
        order = range(NSEG - 1, -1, -1) if reverse else range(NSEG)
        self.order = list(order)
        self.tiles = [(n, k) for n in order for k in range(BH)]

    def gate_items(self, p_scr, col0, lb_ref):
        def item(s, k):
            def run():
                rows, lanes = pl.ds(s * SEG, SEG), slice(k * HP, (k + 1) * HP)
                lb = lb_ref[:, lanes]
                c0, c1 = 0.5 + 0.5 * lb, 0.5 - 0.5 * lb
                ct = c1 * jnp.tanh(0.5 * p_scr[rows, col0 + k * HP:col0 + (k + 1) * HP])
                self.k_scr[rows, lanes] = c1 - ct
                self.lf_st[k, pl.ds(s * PITCH, SEG), :] = jnp.log(c0 + ct)
            return C_GATES_TILE, run
        return [item(s, k) for s, k in _tiles()]

    def cumsum_items(self):
        def item(j):
            def run():
                for k in range(BH):
                    self.gs[k] = self.gs[k] + self.lf_st[k, pl.ds(j, NSEG, stride=PITCH), :]
                    self.g_st[k, pl.ds(j, NSEG, stride=PITCH), :] = self.gs[k]
            return C_CUMSUM_STEP, run
        order = range(SEG - 1, -1, -1) if self.reverse else range(SEG)
        return [item(j) for j in order]

    def a_items(self, load_q, load_v):
        def item(n, k):
            def run():
                rows, lanes = pl.ds(n * SEG, SEG), slice(k * HP, (k + 1) * HP)
                g = self.g_st[k, pl.ds(n * PITCH, SEG), :]
                tot = self.gs[k][n:n + 1, :]
                kk = self.k_scr[rows, lanes]
                if not self.states_only:
                    self.lhs_s[k, rows, 0:HP] = (load_q(n, k) * jnp.exp(g)).astype(BF16)
                    self.rhs_s[n, k, HP:HP + SEG, :] = load_v(n, k)
                    self.kg_s[rows, lanes] = (kk * jnp.exp(-g)).astype(BF16)
                self.ke_s[rows, lanes] = (kk * jnp.exp(tot - g)).astype(BF16)
            return C_GLA_A_TILE, run
        return [item(n, k) for n, k in self.tiles]

    def b_items(self, load_v):
        r_i = lax.broadcasted_iota(jnp.int32, (SEG, SEG), 0)
        c_i = lax.broadcasted_iota(jnp.int32, (SEG, SEG), 1)
        keep = (c_i >= r_i) if self.reverse else (c_i <= r_i)

        def item(n, k):
            def run():
                rows, lanes = pl.ds(n * SEG, SEG), slice(k * HP, (k + 1) * HP)
                if not self.states_only:
                    self.lhs_s[k, rows, HP:HP + SEG] = jnp.where(
                        keep, _nt(self.lhs_s[k, rows, 0:HP], self.kg_s[rows, lanes]), 0.0).astype(BF16)
                self.kv_s[n, k] = _tn(load_v(n, k), self.ke_s[rows, lanes])
            return C_GLA_B_TILE, run
        return [item(n, k) for n, k in self.tiles]

    def c_items(self):
        cps = NSEG // self.nb
        state = {}

        def item(n, k):
            def run():
                sq = n // cps
                st = state.get(k)
                if st is None:
                    st = self.s_scr[sq, k]
                if not self.states_only:
                    self.rhs_s[n, k, 0:HP, :] = st.T.astype(BF16)
                st = jnp.exp(self.gs[k][n:n + 1, :]) * st + self.kv_s[n, k]
                last = (n % cps == 0) if self.reverse else (n % cps == cps - 1)
                if last:
                    self.s_scr[sq, k] = st
                    st = None
                state[k] = st
            return C_GLA_C_TILE, run
        return [item(n, k) for n in self.order for k in range(BH)]

    def d_items(self, emit_out):
        def item(n, k):
            def run():
                rows, lanes = pl.ds(n * SEG, SEG), slice(k * HP, (k + 1) * HP)
                o = jnp.dot(self.lhs_s[k, rows, :], self.rhs_s[n, k], preferred_element_type=F32)
                emit_out(n, k, o)
            return C_GLA_D_TILE, run
        return [item(n, k) for n, k in self.tiles]


def _gla_scratch():
    return ([pltpu.VMEM((T, BW), BF16)] * 2
            + [pltpu.VMEM((BH, T, HP + SEG), BF16), pltpu.VMEM((NSEG, BH, HP, HP), F32),
               pltpu.VMEM((NSEG, BH, HP + SEG, HP), BF16)])


def _dot_items(lhs_rows, w_ref, dst, col_lo, col_hi, cost):
    def piece(r0, c0, c1):
        def run():
            dst[r0:r0 + FILL_ROWS, c0:c1] = jnp.dot(lhs_rows(r0, r0 + FILL_ROWS), w_ref[:, c0:c1],
                                                    preferred_element_type=F32)
        return cost, run
    return [piece(r0, c0, min(c0 + FILL_COLS, col_hi))
            for c0 in range(col_lo, col_hi, FILL_COLS) for r0 in range(0, T, FILL_ROWS)]


def _kb_body(*refs, nb, nc, add_pos, want_out):
    it = iter(refs)
    x0_ref, xn_ref = next(it), next(it)
    xh_ref = next(it) if nc > 1 else None
    if add_pos:
        rt0_ref, rtn_ref, ct_ref, hp_ref = next(it), next(it), next(it), next(it)
    else:
        rt0_ref = rtn_ref = ct_ref = hp_ref = None
    (mod_ref, modn_ref, nw_ref, wb_ref, cw_ref, cb_ref, wg_ref, bg_ref, cl_ref, lb_ref, h0_ref, s0_ref,
     hbf_ref, xc_ref) = (next(it) for _ in range(14))
    hb_ref, ob_ref, q_ref = (next(it), next(it), next(it)) if want_out else (None, None, None)
    (v_ref, hfin_ref, sfin_ref,
     p_scr, h_scr, xa_ext, g_scr, k_scr, a_st, u_st, h_st, p_st, lf_st, g_st, hc_ref, s_scr, xaf,
     *gla_stage) = it

    ic = pl.program_id(1)
    flat = pl.program_id(0) * nc + ic
    cc = nc - 1 - ic

    @pl.when(ic == 0)
    def _():
        for q in range(nb):
            hc_ref[q:q + 1, :] = h0_ref[q]
        s_scr[...] = s0_ref[...]
        xaf[...] = jnp.zeros_like(xaf)

    def norm_items(x_r, rt_r, m_r):
        x_flat = _Flat(x_r, D)

        def item(s):
            def run():
                rows = pl.ds(s * SEG, SEG)
                xin = x_flat[rows, slice(None)]
                if add_pos:
                    xin = xin + jnp.concatenate(
                        [_pos_rows(rt_r, ct_ref, r0) for r0 in range(s * SEG, (s + 1) * SEG, ROW_TILE)], axis=0)
                h_scr[rows, :] = _norm_mod(xin, nw_ref[...], m_r[0, :, 0:D], m_r[0, :, D:2 * D]).astype(BF16)
            return C_NORM_PIECE, run
        return [item(s) for s in range(NSEG)]

    def proj_items():
        return _dot_items(lambda r0, r1: h_scr[r0:r1, :], wb_ref, p_scr, 0, KB_COLS, C_PROJ_PIECE)

    gla = _Gla(True, nb, k_scr, lf_st, g_st, s_scr, gla_stage, states_only=not want_out)
    v_flat = _Flat(v_ref, BW)
    if want_out:
        q_flat, ob_flat = _Flat(q_ref, BW), _Flat(ob_ref, BW)

    def hgrn_input_items():
        def qv_item(s, k):
            def run():
                rows, lanes = pl.ds(s * SEG, SEG), slice(k * HP, (k + 1) * HP)
                if want_out:
                    q_flat[rows, lanes] = _silu(p_scr[rows, AW + k * HP:AW + (k + 1) * HP])
                v_flat[rows, lanes] = p_scr[rows, AW + 2 * BW + k * HP:AW + 2 * BW + (k + 1) * HP].astype(BF16)
            return C_PRE_TILE, run
        return [qv_item(s, k) for s, k in _tiles()] + gla.gate_items(p_scr, AW + BW, lb_ref)

    @pl.when(flat == 0)
    def _():
        _emit(norm_items(x0_ref, rt0_ref, mod_ref) + proj_items())

    def step():
        xc_flat = _Flat(xc_ref, AW)
        hb_flat = _Flat(hb_ref, AW) if want_out else None
        scan = _AScan(a_st, u_st, h_st, p_st, hc_ref, nb, reverse=True)

        hbf_ref[...] = h_scr[...].reshape(hbf_ref.shape)
        if nc > 1:
            xh = xh_ref[0]
            if add_pos:
                xh = xh + hp_ref[0]
            hh = _norm_mod(xh, nw_ref[...], mod_ref[0, :, 0:D], mod_ref[0, :, D:2 * D]).astype(BF16)
            left = jnp.dot(hh, wb_ref[:, 0:AW], preferred_element_type=F32)
            xa_ext[0:SUBLANES, :] = jnp.where(cc > 0, left, 0.0)
        else:
            xa_ext[0:SUBLANES, :] = jnp.zeros((SUBLANES, AW), F32)
        xa_ext[pl.ds(SUBLANES, T), :] = p_scr[:, 0:AW]
        xa_ext[pl.ds(SUBLANES + T, SUBLANES), :] = xaf[...]
        xaf[...] = p_scr[0:SUBLANES, 0:AW]

        def conv_item(s):
            def run():
                spq = NSEG // nb
                row = lax.broadcasted_iota(jnp.int32, (SEG, 1), 0)
                acc = jnp.broadcast_to(cb_ref[...], (SEG, AW))
                for j in range(4):
                    tap = xa_ext[pl.ds(s * SEG + SUBLANES - 2 + j, SEG), :]
                    if nb > 1 and s % spq == 0 and j < 2:
                        tap = jnp.where(row + (j - 2) >= 0, tap, 0.0)
                    if nb > 1 and s % spq == spq - 1 and j == 3:
                        tap = jnp.where(row + 1 < SEG, tap, 0.0)
                    acc = acc + cw_ref[j:j + 1, :] * tap
                xc_flat[pl.ds(s * SEG, SEG), slice(None)] = acc
            return C_CONV_SEG, run

        def load_q(n, k):
            return q_flat[pl.ds(n * SEG, SEG), slice(k * HP, (k + 1) * HP)]

        def load_v(n, k):
            return v_flat[pl.ds(n * SEG, SEG), slice(k * HP, (k + 1) * HP)]

        def emit_out(n, k, o):
            ob_flat[pl.ds(n * SEG, SEG), slice(k * HP, (k + 1) * HP)] = o

        _emit(hgrn_input_items())
        _emit(norm_items(xn_ref, rtn_ref, modn_ref))

        conv = [conv_item(s) for s in range(NSEG)]
        coeff, steps, cums = scan.coeff_items(g_scr, xc_flat, cl_ref), scan.step_items(), gla.cumsum_items()
        proj = proj_items()
        credit = (_cost(conv) + _cost(coeff) + _cost(steps) + _cost(cums)
                  + BH * NSEG * (C_GLA_A_TILE + C_GLA_C_TILE) + NA * NSEG * C_FIX_TILE)
        mx = _Fill(proj, PACE * _cost(proj) / credit)
        _emit(conv, mx)
        g_scr[...] = jnp.dot(xc_ref[...].reshape(T, AW).astype(BF16), wg_ref[...],
                             preferred_element_type=F32) + bg_ref[...]
        _emit(coeff, mx)
        _emit(steps, mx)
        cin = scan.chain()
        for q in range(nb):
            hfin_ref[q] = hc_ref[q:q + 1, :]

        def fix_item(s, k):
            def run():
                hb_flat[pl.ds(s * SEG, SEG), k * LANES:(k + 1) * LANES] = scan.full(cin, s, k)
            return C_FIX_TILE, run
        if want_out:
            _emit([fix_item(s, k) for s in range(NSEG) for k in range(NA)], mx)

        _emit(cums, mx)
        _emit(gla.a_items(load_q, load_v), mx)
        _emit(gla.b_items(load_v))
        _emit(gla.c_items(), mx)
        if want_out:
            _emit(gla.d_items(emit_out))
        sfin_ref[...] = s_scr[...]
        mx.flush()

    step()


def _kf_body(*refs, nb, nc, add_pos, final):
    it = iter(refs)
    x_ref = next(it)
    if add_pos:
        rt_ref, ct_ref = next(it), next(it)
    (hbf0_ref, hbfn_ref, xc_ref, hb_ref, ob_ref, q_ref, v_ref, mod_ref, wf_ref, wg_ref, bg_ref, cl_ref, lb_ref,
     bnw_ref, cnw_ref, cws_ref, cbias_ref, wo_ref, fnw_ref, h0_ref, s0_ref,
     xo_ref, hfin_ref, sfin_ref,
     p_scr, sg_scr, hn_scr, g_scr, k_scr, y_scr, part_scr, a_st, u_st, h_st, p_st, lf_st, g_st, hc_ref, s_scr,
     xcb_scr, vn_scr, mix_scr, *gla_stage) = it

    ic = pl.program_id(1)
    flat = pl.program_id(0) * nc + ic

    @pl.when(ic == 0)
    def _():
        for q in range(nb):
            hc_ref[q:q + 1, :] = h0_ref[q]
        s_scr[...] = s0_ref[...]

    o_ag, o_ff, o_bg = 0, AW, AW + BW
    o_cu, o_cv, o_cg = AW + 2 * BW, AW + 2 * BW + CW, AW + 2 * BW + 2 * CW

    def proj_items(col_lo, col_hi):
        return _dot_items(lambda r0, r1: hn_scr[r0:r1, :], wf_ref, p_scr, col_lo, col_hi, C_PROJ_PIECE)

    @pl.when(flat == 0)
    def _():
        hn_scr[...] = hbf0_ref[...].reshape(T, D)
        _emit(proj_items(0, KF_COLS))

    def step():
        xc_flat, hb_flat = _Flat(xc_ref, AW), _Flat(hb_ref, AW)
        q_flat, v_flat, ob_flat = _Flat(q_ref, BW), _Flat(v_ref, BW), _Flat(ob_ref, BW)
        scan = _AScan(a_st, u_st, h_st, p_st, hc_ref, nb, reverse=False)
        gla = _Gla(False, nb, k_scr, lf_st, g_st, s_scr, gla_stage)

        hn_scr[...] = hbfn_ref[...].reshape(T, D)
        for s in range(NSEG):
            rows = pl.ds(s * SEG, SEG)
            xcb_scr[rows, :] = xc_flat[rows, slice(None)].astype(BF16)

        def silu_item(s, c0):
            def run():
                rows = pl.ds(s * SEG, SEG)
                dst = c0 - o_ag if c0 < o_ff else AW + c0 - o_bg
                sg_scr[rows, dst:dst + LANES] = _silu(p_scr[rows, c0:c0 + LANES])
            return C_SILU_TILE, run
        gate_cols = list(range(o_ag, o_ag + AW, LANES)) + list(range(o_bg, o_bg + BW, LANES))

        def mlp_items(m):
            lane = lax.broadcasted_iota(jnp.int32, (MLP_TILE, CW), 1)

            def norm_item(r0):
                def run():
                    v = _gelu(p_scr[pl.ds(m * CCH + r0, MLP_TILE), o_cv:o_cv + CW])
                    mu = jnp.mean(v, axis=-1, keepdims=True)
                    dv = v - mu
                    var = jnp.mean(dv * dv, axis=-1, keepdims=True)
                    vn_scr[r0:r0 + MLP_TILE, :] = (dv * lax.rsqrt(var + EPS) * cnw_ref[...]).astype(BF16)
                return C_MLP_TILE, run

            def mix():
                mix_scr[...] = jnp.dot(cws_ref[...], vn_scr[...], preferred_element_type=F32)

            def out_item(r0):
                def run():
                    rows = pl.ds(m * CCH + r0, MLP_TILE)
                    mixed = mix_scr[r0:r0 + MLP_TILE, :]
                    for h in range(1, CH):
                        mixed = jnp.where(lane >= h * CHD, mix_scr[h * CCH + r0:h * CCH + r0 + MLP_TILE, :], mixed)
                    u = _gelu(p_scr[rows, o_cu:o_cu + CW])
                    yc = u * (mixed + cbias_ref[r0:r0 + MLP_TILE, :]) * _silu(p_scr[rows, o_cg:o_cg + CW])
                    y_scr[rows, BW + AW:YW] = yc.astype(BF16)
                return C_MLP_TILE, run
            tiles = range(0, CCH, MLP_TILE)
            return [norm_item(r0) for r0 in tiles] + [(0, mix)] + [out_item(r0) for r0 in tiles]

        def load_q(n, k):
            return q_flat[pl.ds(n * SEG, SEG), slice(k * HP, (k + 1) * HP)]

        def load_v(n, k):
            return v_flat[pl.ds(n * SEG, SEG), slice(k * HP, (k + 1) * HP)]

        def emit_out(n, k, o):
            rows, lanes = pl.ds(n * SEG, SEG), slice(k * HP, (k + 1) * HP)
            o = o + ob_flat[rows, lanes]
            ms = jnp.sum(o * o, axis=-1, keepdims=True) * (1.0 / BDK)
            yb = (o * lax.rsqrt(ms + EPS) * bnw_ref[:, lanes]) * sg_scr[rows, AW + k * HP:AW + (k + 1) * HP]
            y_scr[rows, lanes] = yb.astype(BF16)

        g_scr[...] = jnp.dot(xcb_scr[...], wg_ref[...], preferred_element_type=F32) + bg_ref[...]
        _emit([it for m in range(T // CCH) for it in mlp_items(m)])

        pre = [silu_item(s, c0) for s in range(NSEG) for c0 in gate_cols] + gla.gate_items(p_scr, o_ff, lb_ref)
        coeff, steps, cums = scan.coeff_items(g_scr, xc_flat, cl_ref), scan.step_items(), gla.cumsum_items()
        n_proj = -(-KF_COLS // FILL_COLS) * (T // FILL_ROWS)
        n_out = (D // FILL_COLS) * (T // FILL_ROWS)
        credit = (_cost(pre) + _cost(coeff) + _cost(steps) + _cost(cums)
                  + BH * NSEG * (C_GLA_A_TILE + C_GLA_C_TILE + C_GLA_D_TILE) + NA * NSEG * C_YA_TILE)
        mx = _Fill(proj_items(o_cu, KF_COLS), PACE * (n_proj * C_PROJ_PIECE + n_out * C_OUT_PIECE) / credit)
        _emit(pre, mx)
        mx.extend(proj_items(0, o_cu))

        _emit(coeff, mx)
        _emit(steps, mx)
        cin = scan.chain()
        for q in range(nb):
            hfin_ref[q] = hc_ref[q:q + 1, :]

        def ya_item(s, k):
            def run():
                rows, lanes = pl.ds(s * SEG, SEG), slice(k * LANES, (k + 1) * LANES)
                ya = (scan.full(cin, s, k) + hb_flat[rows, lanes]) * sg_scr[rows, lanes]
                y_scr[rows, BW + k * LANES:BW + (k + 1) * LANES] = ya.astype(BF16)
            return C_YA_TILE, run
        _emit([ya_item(s, k) for s in range(NSEG) for k in range(NA)], mx)
        mx.extend(_dot_items(lambda r0, r1: y_scr[r0:r1, BW:YW], wo_ref.at[BW:YW, :], part_scr, 0, D, C_OUT_PIECE))

        _emit(cums, mx)
        _emit(gla.a_items(load_q, load_v), mx)
        _emit(gla.b_items(load_v))
        _emit(gla.c_items(), mx)
        _emit(gla.d_items(emit_out), mx)
        sfin_ref[...] = s_scr[...]
        mx.flush()

        gm = mod_ref[0, :, 2 * D:3 * D]
        x_flat, xo_flat = _Flat(x_ref, D), _Flat(xo_ref, D)
        def out_item(r0, c0):
            def run():
                rows, cols = pl.ds(r0, FILL_ROWS), slice(c0, c0 + FILL_COLS)
                part_scr[rows, cols] = part_scr[rows, cols] + jnp.dot(y_scr[rows, 0:BW], wo_ref[0:BW, cols],
                                                                      preferred_element_type=F32)
            return C_OUT_PIECE, run

        def residual_item(r0):
            def run():
                rows = pl.ds(r0, ROW_TILE)
                xin = x_flat[rows, slice(None)]
                if add_pos:
                    xin = xin + _pos_rows(rt_ref, ct_ref, r0)
                xn = xin + gm * part_scr[rows, :]
                if final:
                    ms = jnp.mean(xn * xn, axis=-1, keepdims=True)
                    xn = xn * lax.rsqrt(ms + EPS) * fnw_ref[...]
                xo_flat[rows, slice(None)] = xn
            return C_OUT_PIECE * (D // FILL_COLS) * ROW_TILE // FILL_ROWS, run

        pending = []
        for r0 in range(0, T, FILL_ROWS):
            _paired([out_item(r0, c0) for c0 in range(0, D, FILL_COLS)], pending)
            pending = [residual_item(r) for r in range(r0, r0 + FILL_ROWS, ROW_TILE)]
        _emit(pending)

    step()


def _kf_states_body(hbf0_ref, hbfn_ref, xc_ref, v_ref, wf_ref, wg_ref, bg_ref, cl_ref, lb_ref, h0_ref, s0_ref,
                    hfin_ref, sfin_ref,
                    p_scr, hn_scr, g_scr, k_scr, a_st, u_st, h_st, p_st, lf_st, g_st, hc_ref, s_scr, xcb_scr,
                    *gla_stage, nb, nc):
    ic = pl.program_id(1)
    flat = pl.program_id(0) * nc + ic

    @pl.when(ic == 0)
    def _():
        for q in range(nb):
            hc_ref[q:q + 1, :] = h0_ref[q]
        s_scr[...] = s0_ref[...]

    def proj_items():
        return _dot_items(lambda r0, r1: hn_scr[r0:r1, :], wf_ref.at[:, AW:AW + BW], p_scr, 0, BW, C_PROJ_PIECE)

    @pl.when(flat == 0)
    def _():
        hn_scr[...] = hbf0_ref[...].reshape(T, D)
        _emit(proj_items())

    xc_flat, v_flat = _Flat(xc_ref, AW), _Flat(v_ref, BW)
    scan = _AScan(a_st, u_st, h_st, p_st, hc_ref, nb, reverse=False)
    gla = _Gla(False, nb, k_scr, lf_st, g_st, s_scr, gla_stage, states_only=True)

    hn_scr[...] = hbfn_ref[...].reshape(T, D)
    for s in range(NSEG):
        rows = pl.ds(s * SEG, SEG)
        xcb_scr[rows, :] = xc_flat[rows, slice(None)].astype(BF16)
    g_scr[...] = jnp.dot(xcb_scr[...], wg_ref[...], preferred_element_type=F32) + bg_ref[...]
    _emit(gla.gate_items(p_scr, 0, lb_ref))

    coeff, steps, cums = scan.coeff_items(g_scr, xc_flat, cl_ref), scan.step_items(), gla.cumsum_items()
    proj = proj_items()
    mx = _Fill(proj, _cost(proj) / (_cost(coeff) + _cost(steps) + _cost(cums)))
    _emit(coeff, mx)
    _emit(steps, mx)
    scan.chain()
    for q in range(nb):
        hfin_ref[q] = hc_ref[q:q + 1, :]
    _emit(cums, mx)

    def load_v(n, k):
        return v_flat[pl.ds(n * SEG, SEG), slice(k * HP, (k + 1) * HP)]
    _emit(gla.a_items(None, load_v))
    _emit(gla.b_items(load_v))
    _emit(gla.c_items())
    sfin_ref[...] = s_scr[...]
    mx.flush()


def _mod_body(cs_ref, w_ref, b_ref, o_ref):
    s = _silu(cs_ref[...])
    o_ref[0] = jnp.dot(s, w_ref[0], preferred_element_type=F32, precision=lax.Precision.HIGHEST) + b_ref[0]


def _const_spec(shape):
    nd = len(shape)
    return pl.BlockSpec(shape, lambda ib, ic: (0,) * nd, pipeline_mode=pl.Buffered(1))


def _seq_spec(nb, rs, width, bc):
    return pl.BlockSpec((nb, rs, width), lambda ib, ic: bc(ib, ic) + (0,))


def _step_maps(ngroups, nc, reverse):
    last = ngroups * nc - 1

    def at(f):
        return f // nc, (nc - 1 - f % nc) if reverse else f % nc

    def cur(ib, ic):
        return at(ib * nc + ic)

    def ahead(d):
        return lambda ib, ic: at(jnp.minimum(ib * nc + ic + d, last))

    def fixed(f):
        return lambda ib, ic: at(min(f, last))
    return cur, ahead, fixed


def _scan_scratch():
    return ([pltpu.VMEM((NA, NSEG * PITCH, LANES), F32)] * 4
            + [pltpu.VMEM((BH, NSEG * PITCH, LANES), F32)] * 2)


def _params():
    return pltpu.CompilerParams(dimension_semantics=("arbitrary", "arbitrary"), vmem_limit_bytes=VMEM_LIMIT)


def _state_specs(nb):
    return [pl.BlockSpec((nb, 1, AW), lambda ib, ic: (ib, 0, 0)),
            pl.BlockSpec((nb, BH, HP, HP), lambda ib, ic: (ib, 0, 0, 0))]


def _run_kb(x, mod, mod_row, weights, layer, h0, s0, pos, nb, want_out=True):
    bsz, seq, _ = x.shape
    rs = T // nb
    nc = seq // rs
    add_pos = pos is not None
    cur, ahead, fixed = _step_maps(bsz // nb, nc, reverse=True)
    steps2 = (fixed(0), ahead(1))
    mrow = (lambda g: g) if mod_row is None else (lambda g: mod_row)
    mod_spec = lambda bc: pl.BlockSpec((1, 1, 3 * D), lambda ib, ic: (mrow(bc(ib, ic)[0]), 0, 0))
    ins, specs = [x] * 2, [_seq_spec(nb, rs, D, bc) for bc in steps2]
    if nc > 1:
        ins.append(x)
        specs.append(pl.BlockSpec(
            (1, SUBLANES, D), lambda ib, ic: (ib, jnp.maximum(cur(ib, ic)[1] * (rs // SUBLANES) - 1, 0), 0)))
    if add_pos:
        rowtab, coltab, hpos = pos
        row_spec = lambda bc: pl.BlockSpec((NSEG, D // 2), lambda ib, ic: (bc(ib, ic)[1], 0))
        ins += [rowtab] * 2 + [coltab, hpos]
        specs += [row_spec(bc) for bc in steps2] + [
            _const_spec(coltab.shape), pl.BlockSpec((1, SUBLANES, D), lambda ib, ic: (cur(ib, ic)[1], 0, 0))]
    w_ins, w_specs = _weight_operands(weights, ("nw", "wb", "cw", "cb", "wg", "bg", "cl", "lb"), layer, BWD)
    ins += [mod] * 2 + w_ins + [h0, s0]
    specs += [mod_spec(cur), mod_spec(ahead(1))] + w_specs + _state_specs(nb)
    seq_outs = [(D, BF16), (AW, F32)] + ([(AW, F32), (BW, F32), (BW, F32)] if want_out else []) + [(BW, BF16)]
    out_shape = tuple([jax.ShapeDtypeStruct((bsz, seq, w), dt) for w, dt in seq_outs]
                      + [jax.ShapeDtypeStruct((bsz, 1, AW), F32), jax.ShapeDtypeStruct((bsz, BH, HP, HP), F32)])
    out_specs = tuple([_seq_spec(nb, rs, w, cur) for w, _ in seq_outs] + _state_specs(nb))
    scratch = ([pltpu.VMEM((T, KB_COLS), F32), pltpu.VMEM((T, D), BF16)]
               + [pltpu.VMEM((T + 2 * SUBLANES, AW), F32), pltpu.VMEM((T, 2 * AW), F32), pltpu.VMEM((T, BW), F32)]
               + _scan_scratch()
               + [pltpu.VMEM((SUBLANES, AW), F32), pltpu.VMEM((nb, BH, HP, HP), F32),
                  pltpu.VMEM((SUBLANES, AW), F32)]
               + _gla_scratch())
    return pl.pallas_call(
        functools.partial(_kb_body, nb=nb, nc=nc, add_pos=add_pos, want_out=want_out),
        grid=(bsz // nb, nc), in_specs=specs, out_specs=out_specs, out_shape=out_shape,
        scratch_shapes=scratch, compiler_params=_params(), name="bwd_sweep")(*ins)


def _run_kf(x, kb_out, mod, mod_row, weights, layer, fnw, h0, s0, pos, nb, final):
    bsz, seq, _ = x.shape
    rs = T // nb
    nc = seq // rs
    add_pos = pos is not None
    cur, ahead, fixed = _step_maps(bsz // nb, nc, reverse=False)
    mrow = (lambda g: g) if mod_row is None else (lambda g: mod_row)
    hbf, xc, hb, ob, q, v = kb_out
    ins, specs = [x], [_seq_spec(nb, rs, D, cur)]
    if add_pos:
        rowtab, coltab, _ = pos
        ins += [rowtab, coltab]
        specs += [pl.BlockSpec((NSEG, D // 2), lambda ib, ic: (cur(ib, ic)[1], 0)), _const_spec(coltab.shape)]
    w_ins, w_specs = _weight_operands(
        weights, ("wf", "wg", "bg", "cl", "lb", "bnw", "cnw", "cws", "cbias", "wo"), layer, FWD)
    ins += [hbf, hbf, xc, hb, ob, q, v, mod] + w_ins + [fnw, h0, s0]
    specs += [_seq_spec(nb, rs, D, fixed(0)), _seq_spec(nb, rs, D, ahead(1))]
    specs += [_seq_spec(nb, rs, w, cur) for w in (AW, AW, BW, BW, BW)]
    specs += [pl.BlockSpec((1, 1, 3 * D), lambda ib, ic: (mrow(ib), 0, 0))]
    specs += w_specs + [_const_spec(fnw.shape)] + _state_specs(nb)
    out_shape = (jax.ShapeDtypeStruct((bsz, seq, D), F32),
                 jax.ShapeDtypeStruct((bsz, 1, AW), F32), jax.ShapeDtypeStruct((bsz, BH, HP, HP), F32))
    out_specs = tuple([_seq_spec(nb, rs, D, cur)] + _state_specs(nb))
    scratch = ([pltpu.VMEM((T, KF_COLS), F32), pltpu.VMEM((T, AW + BW), F32), pltpu.VMEM((T, D), BF16)]
               + [pltpu.VMEM((T, 2 * AW), F32), pltpu.VMEM((T, BW), F32), pltpu.VMEM((T, YW), BF16),
                  pltpu.VMEM((T, D), F32)]
               + _scan_scratch()
               + [pltpu.VMEM((SUBLANES, AW), F32), pltpu.VMEM((nb, BH, HP, HP), F32),
                  pltpu.VMEM((T, AW), BF16), pltpu.VMEM((CCH, CW), BF16), pltpu.VMEM((CH * CCH, CW), F32)]
               + _gla_scratch())
    return pl.pallas_call(
        functools.partial(_kf_body, nb=nb, nc=nc, add_pos=add_pos, final=final),
        grid=(bsz // nb, nc), in_specs=specs, out_specs=out_specs, out_shape=out_shape,
        scratch_shapes=scratch, compiler_params=_params(), name="fwd_sweep")(*ins)


def _run_kf_states(hbf, xc, v, weights, layer, h0, s0, nb):
    bsz, seq, _ = hbf.shape
    rs = T // nb
    nc = seq // rs
    cur, ahead, fixed = _step_maps(bsz // nb, nc, reverse=False)
    w_ins, w_specs = _weight_operands(weights, ("wf", "wg", "bg", "cl", "lb"), layer, FWD)
    ins = [hbf, hbf, xc, v] + w_ins + [h0, s0]
    specs = ([_seq_spec(nb, rs, D, fixed(0)), _seq_spec(nb, rs, D, ahead(1)),
              _seq_spec(nb, rs, AW, cur), _seq_spec(nb, rs, BW, cur)] + w_specs + _state_specs(nb))
    out_shape = (jax.ShapeDtypeStruct((bsz, 1, AW), F32), jax.ShapeDtypeStruct((bsz, BH, HP, HP), F32))
    scratch = ([pltpu.VMEM((T, BW), F32), pltpu.VMEM((T, D), BF16), pltpu.VMEM((T, 2 * AW), F32),
                pltpu.VMEM((T, BW), F32)]
               + _scan_scratch()
               + [pltpu.VMEM((SUBLANES, AW), F32), pltpu.VMEM((nb, BH, HP, HP), F32), pltpu.VMEM((T, AW), BF16)]
               + _gla_scratch())
    return pl.pallas_call(
        functools.partial(_kf_states_body, nb=nb, nc=nc),
        grid=(bsz // nb, nc), in_specs=specs, out_specs=tuple(_state_specs(nb)), out_shape=out_shape,
        scratch_shapes=scratch, compiler_params=_params(), name="fwd_states")(*ins)


def _pad_heads(w, fill=0.0):
    lead = w.shape[:-1]
    w = w.reshape(lead + (BH, BDK))
    w = jnp.pad(w, [(0, 0)] * len(lead) + [(0, 0), (0, HP - BDK)], constant_values=fill)
    return w.reshape(lead + (BW,))


def _prepare_weights(norm_w, w_in, a_conv_w, a_conv_b, a_wr, a_br, a_wi, a_bi, a_lambda, lbs, b_norm_w,
                     c_norm_w, c_ws, c_bs, w_out):
    depth = w_in.shape[0]
    nbw = BH * BDK
    w_in, w_out = w_in.astype(BF16), w_out.astype(BF16)
    zero_cols = jnp.zeros((depth, D, HP - BDK), BF16)
    zero_rows = jnp.zeros((depth, HP - BDK, D), BF16)

    def group(g):
        base = 2 * AW + g * nbw
        return [p for h in range(BH) for p in (w_in[:, :, base + h * BDK:base + (h + 1) * BDK], zero_cols)]
    wo_b = [p for h in range(BH) for p in (w_out[:, AW + h * BDK:AW + (h + 1) * BDK], zero_rows)]
    eye = jnp.eye(A_HEADS, dtype=F32)

    def block_diag(w):
        return jnp.einsum("ldhij,hg->ldhigj", w, eye).reshape(depth, 2, AW, AW)
    return {
        "nw": norm_w[:, None, :],
        "wb": jnp.concatenate([w_in[:, :, :AW]] + group(0) + group(2) + group(3), axis=-1),
        "wf": jnp.concatenate([w_in[:, :, AW:2 * AW]] + group(1) + group(4) + [w_in[:, :, 2 * AW + 5 * nbw:]],
                              axis=-1),
        "cw": a_conv_w, "cb": a_conv_b[:, None, :],
        "lb": _pad_heads(lbs[:depth], fill=1.0)[:, None, :],
        "bnw": _pad_heads(b_norm_w)[:, None, :],
        "cnw": c_norm_w[:, None, :],
        "cws": c_ws.reshape(depth, CH * CCH, CCH).astype(BF16),
        "cbias": jnp.repeat(jnp.swapaxes(c_bs, 1, 2), CHD, axis=2),
        "wo": jnp.concatenate(wo_b + [w_out[:, :AW], w_out[:, AW + nbw:]], axis=1),
        "wg": jnp.concatenate([block_diag(a_wr), block_diag(a_wi)], axis=-1).astype(BF16),
        "bg": jnp.concatenate([a_br, a_bi], axis=-1)[:, :, None, :],
        "cl": (-RG_C * jax.nn.softplus(-a_lambda))[:, :, None, :],
    }


FWD, BWD = 0, 1
PER_DIRECTION = ("wg", "bg", "cl")


def _weight_operands(weights, names, layer, direction):
    arrays, specs = [], []
    for n in names:
        w = weights[n]
        lead = (layer, direction) if n in PER_DIRECTION else (layer,)
        rest = w.shape[len(lead):]
        arrays.append(w)
        specs.append(pl.BlockSpec((None,) * len(lead) + rest,
                                  lambda ib, ic, lead=lead, rest=rest: lead + (0,) * len(rest),
                                  pipeline_mode=pl.Buffered(1)))
    return arrays, specs


def _pos_tables(rows):
    qd = D // 4
    omega = 1.0 / (10000.0 ** (jnp.arange(qd, dtype=F32) / qd))
    ar = jnp.arange(rows, dtype=F32)[:, None] * omega
    ac = jnp.arange(GRID_W, dtype=F32)[:, None] * omega
    rowtab = jnp.concatenate([jnp.sin(ar), jnp.cos(ar)], axis=-1)
    coltab = jnp.concatenate([jnp.sin(ac), jnp.cos(ac)], axis=-1)
    nc = rows * GRID_W // T
    hrow = jnp.broadcast_to(rowtab[jnp.maximum(jnp.arange(nc) * NSEG - 1, 0)][:, None, :], (nc, SUBLANES, D // 2))
    hcol = jnp.broadcast_to(coltab[GRID_W - SUBLANES:][None], (nc, SUBLANES, D // 2))
    return rowtab, coltab, jnp.concatenate([hrow, hcol], axis=-1)


def kernel(x, c, ctx, c_ctx, norm_w, w_mod, b_mod, w_in, a_conv_w, a_conv_b, a_wr, a_br, a_wi, a_bi, a_lambda,
           b_lb_logits, b_norm_w, c_norm_w, c_ws, c_bs, w_out, final_norm_w):
    bsz, n_lat, _ = x.shape
    depth = w_in.shape[0]
    assert n_lat % T == 0 and T % ctx.shape[1] == 0 and bsz % (T // ctx.shape[1]) == 0
    nb_ctx = T // ctx.shape[1]
    pos = _pos_tables(n_lat // GRID_W)

    mrows = 2 * SUBLANES
    cs = jnp.zeros((mrows, D), F32).at[:bsz].set(c).at[bsz].set(c_ctx)
    nblk = 3
    mod = pl.pallas_call(
        _mod_body, grid=(depth, nblk),
        in_specs=[pl.BlockSpec((mrows, D), lambda l, j: (0, 0)),
                  pl.BlockSpec((1, D, D), lambda l, j: (l, 0, j)),
                  pl.BlockSpec((1, 1, D), lambda l, j: (l, 0, j))],
        out_specs=pl.BlockSpec((1, mrows, D), lambda l, j: (l, 0, j)),
        out_shape=jax.ShapeDtypeStruct((depth, mrows, 3 * D), F32),
        name="modulation")(cs, w_mod, b_mod[:, None, :])
    mod = mod.reshape(depth, mrows, 1, 3 * D)

    lbs = jnp.cumsum(jax.nn.softmax(b_lb_logits, axis=0), axis=0)
    fnw = final_norm_w[None, :]
    h_zero = jnp.zeros((bsz, 1, AW), F32)
    s_zero = jnp.zeros((bsz, BH, HP, HP), F32)
    weights = _prepare_weights(norm_w, w_in, a_conv_w, a_conv_b, a_wr, a_br, a_wi, a_bi, a_lambda, lbs, b_norm_w,
                               c_norm_w, c_ws, c_bs, w_out)
    for l in range(depth):
        lpos = pos if l == 0 else None
        last = l == depth - 1
        kb = _run_kb(ctx, mod[l], bsz, weights, l, h_zero, s_zero, None, nb_ctx, want_out=not last)
        hb_c, sb_c = kb[-2], kb[-1]
        if last:
            hbf_c, xc_c, v_c = kb[:3]
            hf_c, sf_c = _run_kf_states(hbf_c, xc_c, v_c, weights, l, h_zero, s_zero, nb_ctx)
        else:
            ctx, hf_c, sf_c = _run_kf(ctx, kb[:6], mod[l], bsz, weights, l, fnw, h_zero, s_zero, None, nb_ctx, False)
        kb = _run_kb(x, mod[l], None, weights, l, hb_c, sb_c, lpos, 1)
        x, _, _ = _run_kf(x, kb[:6], mod[l], None, weights, l, fnw, hf_c, sf_c, lpos, 1, last)
    return x
```

```python
import functools

import jax
import jax.numpy as jnp
import numpy as np
from jax import lax
from jax.experimental import pallas as pl
from jax.experimental.pallas import tpu as pltpu

D = 1024
GRID_W = 64
EPS = 1e-6
RG_C = 8.0
AW = 384
A_HEADS = 8
A_HD = 48
BH = 4
BDK = 96
CW = 256
CH = 4
CHD = 64
CCH = 128

LANES = 128
SUBLANES = 8
HP = LANES
BW = BH * HP
YW = AW + BW + CW
NA = AW // LANES
T = 512
SEG = 64
NSEG = T // SEG
PITCH = SEG + SUBLANES
KB_COLS = AW + 3 * BW
KF_COLS = AW + 2 * BW + 3 * CW
FILL_COLS = 256
FILL_ROWS = T // 2
ROW_TILE = 16
MLP_TILE = 32
VMEM_LIMIT = 58 * 1024 * 1024

F32 = jnp.float32
BF16 = jnp.bfloat16

C_CONV_SEG, C_COEFF_TILE, C_SCAN_STEP, C_FIX_TILE, C_YA_TILE = 100, 50, 12, 8, 15
C_GATES_TILE, C_CUMSUM_STEP, C_GLA_A_TILE, C_GLA_C_TILE, C_SILU_TILE = 20, 8, 40, 15, 12
C_PRE_TILE, C_MLP_TILE, C_GLA_B_TILE, C_GLA_D_TILE = 25, 90, 40, 40
C_PROJ_PIECE, C_OUT_PIECE, C_NORM_PIECE = FILL_ROWS, 3 * FILL_ROWS // 4, 130
PACE = 1.4


def _silu_of_twice(hx):
    return hx * (jnp.tanh(hx) + 1.0)


def _silu(x):
    return _silu_of_twice(0.5 * x)


def _gelu(x):
    c = np.float32(np.sqrt(2.0 / np.pi))
    hx = 0.5 * x
    return hx * (jnp.tanh(x * (c + (0.044715 * c) * (x * x))) + 1.0)


def _nt(a, b):
    return lax.dot_general(a, b, (((1,), (1,)), ((), ())), preferred_element_type=F32)


def _tn(a, b):
    return lax.dot_general(a, b, (((0,), (0,)), ((), ())), preferred_element_type=F32)


def _norm_mod(x, nw, sh, sc):
    ms = jnp.mean(x * x, axis=-1, keepdims=True)
    return (x * lax.rsqrt(ms + EPS) * nw) * (1.0 + sc) + sh


def _pos_rows(rt_ref, ct_ref, r0):
    s, c0 = divmod(r0, SEG)
    return jnp.concatenate([jnp.broadcast_to(rt_ref[s:s + 1, :], (ROW_TILE, D // 2)),
                            ct_ref[c0:c0 + ROW_TILE, :]], axis=1)


class _Fill:
    def __init__(self, items=(), rate=1.0):
        self.items = list(items)
        self.rate = rate
        self.credit = 0.0
        self.spent = 0.0

    def __call__(self, credit):
        self.credit += credit
        while self.items and self.spent + 0.5 * self.items[0][0] <= self.credit * self.rate:
            cost, thunk = self.items.pop(0)
            self.spent += cost
            thunk()

    def extend(self, items):
        self.items.extend(items)

    def flush(self):
        while self.items:
            self.items.pop(0)[1]()


def _cost(items):
    return sum(c for c, _ in items)


def _emit(items, fill=None):
    for cost, thunk in items:
        thunk()
        if fill is not None:
            fill(cost)


def _paired(items, partner_items):
    partner = _Fill(partner_items, _cost(partner_items) / max(_cost(items), 1))
    _emit(items, partner)
    partner.flush()


class _Flat:
    def __init__(self, ref, width):
        self.ref = ref
        self.rs = ref.shape[1]
        self.width = width

    def _split(self, idx):
        rows, lanes = idx
        start, size = rows.start, rows.size
        q, off = divmod(start, self.rs)
        assert off + size <= self.rs
        return q, pl.ds(off, size), lanes

    def __getitem__(self, idx):
        q, r, lanes = self._split(idx)
        return self.ref[q, r, lanes]

    def __setitem__(self, idx, val):
        q, r, lanes = self._split(idx)
        self.ref[q, r, lanes] = val


def _tiles():
    return [(s, k) for s in range(NSEG) for k in range(BH)]


class _AScan:
    def __init__(self, a_st, u_st, h_st, p_st, hc_ref, nb, reverse):
        self.refs = (a_st, u_st, h_st, p_st)
        self.hc_ref, self.nb, self.reverse = hc_ref, nb, reverse
        self.hs = [jnp.zeros((NSEG, LANES), F32)] * NA
        self.ps = [jnp.ones((NSEG, LANES), F32)] * NA

    def coeff_items(self, g_scr, xc_ref, cl_ref):
        a_st, u_st = self.refs[:2]

        def item(s, k):
            def run():
                rows, lanes = pl.ds(s * SEG, SEG), slice(k * LANES, (k + 1) * LANES)
                tr = jnp.tanh(g_scr[rows, lanes])
                ti = jnp.tanh(g_scr[rows, AW + k * LANES:AW + (k + 1) * LANES])
                hcl = cl_ref[:, lanes]
                la = hcl * tr + hcl
                th = jnp.tanh(la)
                om4 = (-0.5 * th) / (1.0 - th)
                a_st[k, pl.ds(s * PITCH, SEG), :] = jnp.exp(la)
                u_st[k, pl.ds(s * PITCH, SEG), :] = jnp.sqrt(om4) * ((ti + 1.0) * xc_ref[rows, lanes])
            return C_COEFF_TILE, run
        return [item(s, k) for s in range(NSEG) for k in range(NA)]

    def step_items(self):
        a_st, u_st, h_st, p_st = self.refs

        def item(j):
            def run():
                for k in range(NA):
                    a = a_st[k, pl.ds(j, NSEG, stride=PITCH), :]
                    u = u_st[k, pl.ds(j, NSEG, stride=PITCH), :]
                    self.hs[k] = a * self.hs[k] + u
                    self.ps[k] = a * self.ps[k]
                    h_st[k, pl.ds(j, NSEG, stride=PITCH), :] = self.hs[k]
                    p_st[k, pl.ds(j, NSEG, stride=PITCH), :] = self.ps[k]
            return C_SCAN_STEP, run
        order = range(SEG - 1, -1, -1) if self.reverse else range(SEG)
        return [item(j) for j in order]

    def chain(self):
        spq = NSEG // self.nb
        cin = [None] * NSEG
        for q in range(self.nb):
            c = [self.hc_ref[q:q + 1, k * LANES:(k + 1) * LANES] for k in range(NA)]
            order = range((q + 1) * spq - 1, q * spq - 1, -1) if self.reverse else range(q * spq, (q + 1) * spq)
            for s in order:
                cin[s] = c
                c = [self.ps[k][s:s + 1, :] * c[k] + self.hs[k][s:s + 1, :] for k in range(NA)]
            for k in range(NA):
                self.hc_ref[q:q + 1, k * LANES:(k + 1) * LANES] = c[k]
        return cin

    def full(self, cin, s, k):
        _, _, h_st, p_st = self.refs
        st_rows = pl.ds(s * PITCH, SEG)
        return h_st[k, st_rows, :] + p_st[k, st_rows, :] * cin[s][k]


class _Gla:
    def __init__(self, reverse, nb, k_scr, lf_st, g_st, s_scr, stage, states_only=False):
        self.reverse, self.nb, self.states_only = reverse, nb, states_only
        self.k_scr, self.lf_st, self.g_st, self.s_scr = k_scr, lf_st, g_st, s_scr
        self.kg_s, self.ke_s, self.lhs_s, self.kv_s, self.rhs_s = stage
        self.gs = [jnp.zeros((NSEG, LANES), F32)] * BH
        order = range(NSEG - 1, -1, -1) if reverse else range(NSEG)
        self.order = list(order)
        self.tiles = [(n, k) for n in order for k in range(BH)]

    def gate_items(self, p_scr, col0, lb_ref):
        def item(s, k):
            def run():
                rows, lanes = pl.ds(s * SEG, SEG), slice(k * HP, (k + 1) * HP)
                lb = lb_ref[:, lanes]
                c0, c1 = 0.5 + 0.5 * lb, 0.5 - 0.5 * lb
                ct = c1 * jnp.tanh(p_scr[rows, col0 + k * HP:col0 + (k + 1) * HP])
                self.k_scr[rows, lanes] = c1 - ct
                self.lf_st[k, pl.ds(s * PITCH, SEG), :] = jnp.log(c0 + ct)
            return C_GATES_TILE, run
        return [item(s, k) for s, k in _tiles()]

    def cumsum_items(self):
        def item(j):
            def run():
                for k in range(BH):
                    self.gs[k] = self.gs[k] + self.lf_st[k, pl.ds(j, NSEG, stride=PITCH), :]
                    self.g_st[k, pl.ds(j, NSEG, stride=PITCH), :] = self.gs[k]
            return C_CUMSUM_STEP, run
        order = range(SEG - 1, -1, -1) if self.reverse else range(SEG)
        return [item(j) for j in order]

    def a_items(self, load_q, load_v):
        def item(n, k):
            def run():
                rows, lanes = pl.ds(n * SEG, SEG), slice(k * HP, (k + 1) * HP)
                g = self.g_st[k, pl.ds(n * PITCH, SEG), :]
                tot = self.gs[k][n:n + 1, :]
                kk = self.k_scr[rows, lanes]
                if not self.states_only:
                    self.lhs_s[k, rows, 0:HP] = (load_q(n, k) * jnp.exp(g)).astype(BF16)
                    self.rhs_s[n, k, HP:HP + SEG, :] = load_v(n, k)
                    self.kg_s[rows, lanes] = (kk * jnp.exp(-g)).astype(BF16)
                self.ke_s[rows, lanes] = (kk * jnp.exp(tot - g)).astype(BF16)
            return C_GLA_A_TILE, run
        return [item(n, k) for n, k in self.tiles]

    def b_items(self, load_v):
        r_i = lax.broadcasted_iota(jnp.int32, (SEG, SEG), 0)
        c_i = lax.broadcasted_iota(jnp.int32, (SEG, SEG), 1)
        keep = (c_i >= r_i) if self.reverse else (c_i <= r_i)

        def item(n, k):
            def run():
                rows, lanes = pl.ds(n * SEG, SEG), slice(k * HP, (k + 1) * HP)
                if not self.states_only:
                    self.lhs_s[k, rows, HP:HP + SEG] = jnp.where(
                        keep, _nt(self.lhs_s[k, rows, 0:HP], self.kg_s[rows, lanes]), 0.0).astype(BF16)
                self.kv_s[n, k] = _tn(load_v(n, k), self.ke_s[rows, lanes])
            return C_GLA_B_TILE, run
        return [item(n, k) for n, k in self.tiles]

    def c_items(self):
        cps = NSEG // self.nb
        state = {}

        def item(n, k):
            def run():
                sq = n // cps
                st = state.get(k)
                if st is None:
                    st = self.s_scr[sq, k]
                if not self.states_only:
                    self.rhs_s[n, k, 0:HP, :] = st.T.astype(BF16)
                st = jnp.exp(self.gs[k][n:n + 1, :]) * st + self.kv_s[n, k]
                last = (n % cps == 0) if self.reverse else (n % cps == cps - 1)
                if last:
                    self.s_scr[sq, k] = st
                    st = None
                state[k] = st
            return C_GLA_C_TILE, run
        return [item(n, k) for n in self.order for k in range(BH)]

    def d_items(self, emit_out):
        def item(n, k):
            def run():
                rows, lanes = pl.ds(n * SEG, SEG), slice(k * HP, (k + 1) * HP)
                o = jnp.dot(self.lhs_s[k, rows, :], self.rhs_s[n, k], preferred_element_type=F32)
                emit_out(n, k, o)
            return C_GLA_D_TILE, run
        return [item(n, k) for n, k in self.tiles]


def _gla_scratch():
    return ([pltpu.VMEM((T, BW), BF16)] * 2
            + [pltpu.VMEM((BH, T, HP + SEG), BF16), pltpu.VMEM((NSEG, BH, HP, HP), F32),
               pltpu.VMEM((NSEG, BH, HP + SEG, HP), BF16)])


def _dot_items(lhs_rows, w_ref, dst, col_lo, col_hi, cost):
    def piece(r0, c0, c1):
        def run():
            dst[r0:r0 + FILL_ROWS, c0:c1] = jnp.dot(lhs_rows(r0, r0 + FILL_ROWS), w_ref[:, c0:c1],
                                                    preferred_element_type=F32)
        return cost, run
    return [piece(r0, c0, min(c0 + FILL_COLS, col_hi))
            for c0 in range(col_lo, col_hi, FILL_COLS) for r0 in range(0, T, FILL_ROWS)]


def _kb_body(*refs, nb, nc, add_pos, want_out):
    it = iter(refs)
    x0_ref, xn_ref = next(it), next(it)
    xh_ref = next(it) if nc > 1 else None
    if add_pos:
        rt0_ref, rtn_ref, ct_ref, hp_ref = next(it), next(it), next(it), next(it)
    else:
        rt0_ref = rtn_ref = ct_ref = hp_ref = None
    (mod_ref, modn_ref, nw_ref, wb_ref, cw_ref, cb_ref, wg_ref, bg_ref, cl_ref, lb_ref, h0_ref, s0_ref,
     hbf_ref, xc_ref) = (next(it) for _ in range(14))
    hb_ref, ob_ref, q_ref = (next(it), next(it), next(it)) if want_out else (None, None, None)
    (v_ref, hfin_ref, sfin_ref,
     p_scr, h_scr, xa_ext, g_scr, k_scr, a_st, u_st, h_st, p_st, lf_st, g_st, hc_ref, s_scr, xaf,
     *gla_stage) = it

    ic = pl.program_id(1)
    flat = pl.program_id(0) * nc + ic
    cc = nc - 1 - ic

    @pl.when(ic == 0)
    def _():
        for q in range(nb):
            hc_ref[q:q + 1, :] = h0_ref[q]
        s_scr[...] = s0_ref[...]
        xaf[...] = jnp.zeros_like(xaf)

    def norm_items(x_r, rt_r, m_r):
        x_flat = _Flat(x_r, D)

        def item(s):
            def run():
                rows = pl.ds(s * SEG, SEG)
                xin = x_flat[rows, slice(None)]
                if add_pos:
                    xin = xin + jnp.concatenate(
                        [_pos_rows(rt_r, ct_ref, r0) for r0 in range(s * SEG, (s + 1) * SEG, ROW_TILE)], axis=0)
                h_scr[rows, :] = _norm_mod(xin, nw_ref[...], m_r[0, :, 0:D], m_r[0, :, D:2 * D]).astype(BF16)
            return C_NORM_PIECE, run
        return [item(s) for s in range(NSEG)]

    def proj_items():
        return _dot_items(lambda r0, r1: h_scr[r0:r1, :], wb_ref, p_scr, 0, KB_COLS, C_PROJ_PIECE)

    gla = _Gla(True, nb, k_scr, lf_st, g_st, s_scr, gla_stage, states_only=not want_out)
    v_flat = _Flat(v_ref, BW)
    if want_out:
        q_flat, ob_flat = _Flat(q_ref, BW), _Flat(ob_ref, BW)

    def hgrn_input_items():
        def qv_item(s, k):
            def run():
                rows, lanes = pl.ds(s * SEG, SEG), slice(k * HP, (k + 1) * HP)
                if want_out:
                    q_flat[rows, lanes] = _silu_of_twice(p_scr[rows, AW + k * HP:AW + (k + 1) * HP])
                v_flat[rows, lanes] = p_scr[rows, AW + 2 * BW + k * HP:AW + 2 * BW + (k + 1) * HP].astype(BF16)
            return C_PRE_TILE, run
        return [qv_item(s, k) for s, k in _tiles()] + gla.gate_items(p_scr, AW + BW, lb_ref)

    @pl.when(flat == 0)
    def _():
        _emit(norm_items(x0_ref, rt0_ref, mod_ref) + proj_items())

    def step():
        xc_flat = _Flat(xc_ref, AW)
        hb_flat = _Flat(hb_ref, AW) if want_out else None
        scan = _AScan(a_st, u_st, h_st, p_st, hc_ref, nb, reverse=True)

        hbf_ref[...] = h_scr[...].reshape(hbf_ref.shape)
        if nc > 1:
            xh = xh_ref[0]
            if add_pos:
                xh = xh + hp_ref[0]
            hh = _norm_mod(xh, nw_ref[...], mod_ref[0, :, 0:D], mod_ref[0, :, D:2 * D]).astype(BF16)
            left = jnp.dot(hh, wb_ref[:, 0:AW], preferred_element_type=F32)
            xa_ext[0:SUBLANES, :] = jnp.where(cc > 0, left, 0.0)
        else:
            xa_ext[0:SUBLANES, :] = jnp.zeros((SUBLANES, AW), F32)
        xa_ext[pl.ds(SUBLANES, T), :] = p_scr[:, 0:AW]
        xa_ext[pl.ds(SUBLANES + T, SUBLANES), :] = xaf[...]
        xaf[...] = p_scr[0:SUBLANES, 0:AW]

        def conv_item(s):
            def run():
                spq = NSEG // nb
                row = lax.broadcasted_iota(jnp.int32, (SEG, 1), 0)
                acc = jnp.broadcast_to(cb_ref[...], (SEG, AW))
                for j in range(4):
                    tap = xa_ext[pl.ds(s * SEG + SUBLANES - 2 + j, SEG), :]
                    if nb > 1 and s % spq == 0 and j < 2:
                        tap = jnp.where(row + (j - 2) >= 0, tap, 0.0)
                    if nb > 1 and s % spq == spq - 1 and j == 3:
                        tap = jnp.where(row + 1 < SEG, tap, 0.0)
                    acc = acc + cw_ref[j:j + 1, :] * tap
                xc_flat[pl.ds(s * SEG, SEG), slice(None)] = acc
            return C_CONV_SEG, run

        def load_q(n, k):
            return q_flat[pl.ds(n * SEG, SEG), slice(k * HP, (k + 1) * HP)]

        def load_v(n, k):
            return v_flat[pl.ds(n * SEG, SEG), slice(k * HP, (k + 1) * HP)]

        def emit_out(n, k, o):
            ob_flat[pl.ds(n * SEG, SEG), slice(k * HP, (k + 1) * HP)] = o

        _emit(hgrn_input_items())
        _emit(norm_items(xn_ref, rtn_ref, modn_ref))

        conv = [conv_item(s) for s in range(NSEG)]
        coeff, steps, cums = scan.coeff_items(g_scr, xc_flat, cl_ref), scan.step_items(), gla.cumsum_items()
        proj = proj_items()
        credit = (_cost(conv) + _cost(coeff) + _cost(steps) + _cost(cums)
                  + BH * NSEG * (C_GLA_A_TILE + C_GLA_C_TILE) + NA * NSEG * C_FIX_TILE)
        mx = _Fill(proj, PACE * _cost(proj) / credit)
        _emit(conv, mx)
        g_scr[...] = jnp.dot(xc_ref[...].reshape(T, AW).astype(BF16), wg_ref[...],
                             preferred_element_type=F32) + bg_ref[...]
        _emit(coeff, mx)
        _emit(steps, mx)
        cin = scan.chain()
        for q in range(nb):
            hfin_ref[q] = hc_ref[q:q + 1, :]

        def fix_item(s, k):
            def run():
                hb_flat[pl.ds(s * SEG, SEG), k * LANES:(k + 1) * LANES] = scan.full(cin, s, k)
            return C_FIX_TILE, run
        if want_out:
            _emit([fix_item(s, k) for s in range(NSEG) for k in range(NA)], mx)

        _emit(cums, mx)
        _emit(gla.a_items(load_q, load_v), mx)
        _emit(gla.b_items(load_v))
        _emit(gla.c_items(), mx)
        if want_out:
            _emit(gla.d_items(emit_out))
        sfin_ref[...] = s_scr[...]
        mx.flush()

    step()


def _kf_body(*refs, nb, nc, add_pos, final):
    it = iter(refs)
    x_ref = next(it)
    if add_pos:
        rt_ref, ct_ref = next(it), next(it)
    (hbf0_ref, hbfn_ref, xc_ref, hb_ref, ob_ref, q_ref, v_ref, mod_ref, wf_ref, wg_ref, bg_ref, cl_ref, lb_ref,
     bnw_ref, cnw_ref, cws_ref, cbias_ref, wo_ref, fnw_ref, h0_ref, s0_ref,
     xo_ref, hfin_ref, sfin_ref,
     p_scr, sg_scr, hn_scr, g_scr, k_scr, y_scr, part_scr, a_st, u_st, h_st, p_st, lf_st, g_st, hc_ref, s_scr,
     xcb_scr, vn_scr, mix_scr, *gla_stage) = it

    ic = pl.program_id(1)
    flat = pl.program_id(0) * nc + ic

    @pl.when(ic == 0)
    def _():
        for q in range(nb):
            hc_ref[q:q + 1, :] = h0_ref[q]
        s_scr[...] = s0_ref[...]

    o_ag, o_ff, o_bg = 0, AW, AW + BW
    o_cu, o_cv, o_cg = AW + 2 * BW, AW + 2 * BW + CW, AW + 2 * BW + 2 * CW

    def proj_items(col_lo, col_hi):
        return _dot_items(lambda r0, r1: hn_scr[r0:r1, :], wf_ref, p_scr, col_lo, col_hi, C_PROJ_PIECE)

    @pl.when(flat == 0)
    def _():
        hn_scr[...] = hbf0_ref[...].reshape(T, D)
        _emit(proj_items(0, KF_COLS))

    def step():
        xc_flat, hb_flat = _Flat(xc_ref, AW), _Flat(hb_ref, AW)
        q_flat, v_flat, ob_flat = _Flat(q_ref, BW), _Flat(v_ref, BW), _Flat(ob_ref, BW)
        scan = _AScan(a_st, u_st, h_st, p_st, hc_ref, nb, reverse=False)
        gla = _Gla(False, nb, k_scr, lf_st, g_st, s_scr, gla_stage)

        hn_scr[...] = hbfn_ref[...].reshape(T, D)
        for s in range(NSEG):
            rows = pl.ds(s * SEG, SEG)
            xcb_scr[rows, :] = xc_flat[rows, slice(None)].astype(BF16)

        def silu_item(s, c0):
            def run():
                rows = pl.ds(s * SEG, SEG)
                dst = c0 - o_ag if c0 < o_ff else AW + c0 - o_bg
                sg_scr[rows, dst:dst + LANES] = _silu_of_twice(p_scr[rows, c0:c0 + LANES])
            return C_SILU_TILE, run
        gate_cols = list(range(o_ag, o_ag + AW, LANES)) + list(range(o_bg, o_bg + BW, LANES))

        def mlp_items(m):
            lane = lax.broadcasted_iota(jnp.int32, (MLP_TILE, CW), 1)

            def norm_item(r0):
                def run():
                    v = _gelu(p_scr[pl.ds(m * CCH + r0, MLP_TILE), o_cv:o_cv + CW])
                    mu = jnp.mean(v, axis=-1, keepdims=True)
                    dv = v - mu
                    var = jnp.mean(dv * dv, axis=-1, keepdims=True)
                    vn_scr[r0:r0 + MLP_TILE, :] = (dv * lax.rsqrt(var + EPS) * cnw_ref[...]).astype(BF16)
                return C_MLP_TILE, run

            def mix():
                mix_scr[...] = jnp.dot(cws_ref[...], vn_scr[...], preferred_element_type=F32)

            def out_item(r0):
                def run():
                    rows = pl.ds(m * CCH + r0, MLP_TILE)
                    mixed = mix_scr[r0:r0 + MLP_TILE, :]
                    for h in range(1, CH):
                        mixed = jnp.where(lane >= h * CHD, mix_scr[h * CCH + r0:h * CCH + r0 + MLP_TILE, :], mixed)
                    u = _gelu(p_scr[rows, o_cu:o_cu + CW])
                    yc = (u * (mixed + cbias_ref[r0:r0 + MLP_TILE, :])
                          * _silu_of_twice(p_scr[rows, o_cg:o_cg + CW]))
                    y_scr[rows, BW + AW:YW] = yc.astype(BF16)
                return C_MLP_TILE, run
            tiles = range(0, CCH, MLP_TILE)
            return [norm_item(r0) for r0 in tiles] + [(0, mix)] + [out_item(r0) for r0 in tiles]

        def load_q(n, k):
            return q_flat[pl.ds(n * SEG, SEG), slice(k * HP, (k + 1) * HP)]

        def load_v(n, k):
            return v_flat[pl.ds(n * SEG, SEG), slice(k * HP, (k + 1) * HP)]

        def emit_out(n, k, o):
            rows, lanes = pl.ds(n * SEG, SEG), slice(k * HP, (k + 1) * HP)
            o = o + ob_flat[rows, lanes]
            ms = jnp.sum(o * o, axis=-1, keepdims=True) * (1.0 / BDK)
            yb = (o * lax.rsqrt(ms + EPS) * bnw_ref[:, lanes]) * sg_scr[rows, AW + k * HP:AW + (k + 1) * HP]
            y_scr[rows, lanes] = yb.astype(BF16)

        g_scr[...] = jnp.dot(xcb_scr[...], wg_ref[...], preferred_element_type=F32) + bg_ref[...]
        _emit([it for m in range(T // CCH) for it in mlp_items(m)])

        pre = [silu_item(s, c0) for s in range(NSEG) for c0 in gate_cols] + gla.gate_items(p_scr, o_ff, lb_ref)
        coeff, steps, cums = scan.coeff_items(g_scr, xc_flat, cl_ref), scan.step_items(), gla.cumsum_items()
        n_proj = -(-KF_COLS // FILL_COLS) * (T // FILL_ROWS)
        n_out = (D // FILL_COLS) * (T // FILL_ROWS)
        credit = (_cost(pre) + _cost(coeff) + _cost(steps) + _cost(cums)
                  + BH * NSEG * (C_GLA_A_TILE + C_GLA_C_TILE + C_GLA_D_TILE) + NA * NSEG * C_YA_TILE)
        mx = _Fill(proj_items(o_cu, KF_COLS), PACE * (n_proj * C_PROJ_PIECE + n_out * C_OUT_PIECE) / credit)
        _emit(pre, mx)
        mx.extend(proj_items(0, o_cu))

        _emit(coeff, mx)
        _emit(steps, mx)
        cin = scan.chain()
        for q in range(nb):
            hfin_ref[q] = hc_ref[q:q + 1, :]

        def ya_item(s, k):
            def run():
                rows, lanes = pl.ds(s * SEG, SEG), slice(k * LANES, (k + 1) * LANES)
                ya = (scan.full(cin, s, k) + hb_flat[rows, lanes]) * sg_scr[rows, lanes]
                y_scr[rows, BW + k * LANES:BW + (k + 1) * LANES] = ya.astype(BF16)
            return C_YA_TILE, run
        _emit([ya_item(s, k) for s in range(NSEG) for k in range(NA)], mx)
        mx.extend(_dot_items(lambda r0, r1: y_scr[r0:r1, BW:YW], wo_ref.at[BW:YW, :], part_scr, 0, D, C_OUT_PIECE))

        _emit(cums, mx)
        _emit(gla.a_items(load_q, load_v), mx)
        _emit(gla.b_items(load_v))
        _emit(gla.c_items(), mx)
        _emit(gla.d_items(emit_out), mx)
        sfin_ref[...] = s_scr[...]
        mx.flush()

        gm = mod_ref[0, :, 2 * D:3 * D]
        x_flat, xo_flat = _Flat(x_ref, D), _Flat(xo_ref, D)
        def out_item(r0, c0):
            def run():
                rows, cols = pl.ds(r0, FILL_ROWS), slice(c0, c0 + FILL_COLS)
                part_scr[rows, cols] = part_scr[rows, cols] + jnp.dot(y_scr[rows, 0:BW], wo_ref[0:BW, cols],
                                                                      preferred_element_type=F32)
            return C_OUT_PIECE, run

        def residual_item(r0):
            def run():
                rows = pl.ds(r0, ROW_TILE)
                xin = x_flat[rows, slice(None)]
                if add_pos:
                    xin = xin + _pos_rows(rt_ref, ct_ref, r0)
                xn = xin + gm * part_scr[rows, :]
                if final:
                    ms = jnp.mean(xn * xn, axis=-1, keepdims=True)
                    xn = xn * lax.rsqrt(ms + EPS) * fnw_ref[...]
                xo_flat[rows, slice(None)] = xn
            return C_OUT_PIECE * (D // FILL_COLS) * ROW_TILE // FILL_ROWS, run

        pending = []
        for r0 in range(0, T, FILL_ROWS):
            _paired([out_item(r0, c0) for c0 in range(0, D, FILL_COLS)], pending)
            pending = [residual_item(r) for r in range(r0, r0 + FILL_ROWS, ROW_TILE)]
        _emit(pending)

    step()


def _kf_states_body(hbf0_ref, hbfn_ref, xc_ref, v_ref, wf_ref, wg_ref, bg_ref, cl_ref, lb_ref, h0_ref, s0_ref,
                    hfin_ref, sfin_ref,
                    p_scr, hn_scr, g_scr, k_scr, a_st, u_st, h_st, p_st, lf_st, g_st, hc_ref, s_scr, xcb_scr,
                    *gla_stage, nb, nc):
    ic = pl.program_id(1)
    flat = pl.program_id(0) * nc + ic

    @pl.when(ic == 0)
    def _():
        for q in range(nb):
            hc_ref[q:q + 1, :] = h0_ref[q]
        s_scr[...] = s0_ref[...]

    def proj_items():
        return _dot_items(lambda r0, r1: hn_scr[r0:r1, :], wf_ref.at[:, AW:AW + BW], p_scr, 0, BW, C_PROJ_PIECE)

    @pl.when(flat == 0)
    def _():
        hn_scr[...] = hbf0_ref[...].reshape(T, D)
        _emit(proj_items())

    xc_flat, v_flat = _Flat(xc_ref, AW), _Flat(v_ref, BW)
    scan = _AScan(a_st, u_st, h_st, p_st, hc_ref, nb, reverse=False)
    gla = _Gla(False, nb, k_scr, lf_st, g_st, s_scr, gla_stage, states_only=True)

    hn_scr[...] = hbfn_ref[...].reshape(T, D)
    for s in range(NSEG):
        rows = pl.ds(s * SEG, SEG)
        xcb_scr[rows, :] = xc_flat[rows, slice(None)].astype(BF16)
    g_scr[...] = jnp.dot(xcb_scr[...], wg_ref[...], preferred_element_type=F32) + bg_ref[...]
    _emit(gla.gate_items(p_scr, 0, lb_ref))

    coeff, steps, cums = scan.coeff_items(g_scr, xc_flat, cl_ref), scan.step_items(), gla.cumsum_items()
    proj = proj_items()
    mx = _Fill(proj, _cost(proj) / (_cost(coeff) + _cost(steps) + _cost(cums)))
    _emit(coeff, mx)
    _emit(steps, mx)
    scan.chain()
    for q in range(nb):
        hfin_ref[q] = hc_ref[q:q + 1, :]
    _emit(cums, mx)

    def load_v(n, k):
        return v_flat[pl.ds(n * SEG, SEG), slice(k * HP, (k + 1) * HP)]
    _emit(gla.a_items(None, load_v))
    _emit(gla.b_items(load_v))
    _emit(gla.c_items())
    sfin_ref[...] = s_scr[...]
    mx.flush()


def _mod_body(cs_ref, w_ref, b_ref, o_ref):
    s = _silu(cs_ref[...])
    o_ref[0] = jnp.dot(s, w_ref[0], preferred_element_type=F32, precision=lax.Precision.HIGHEST) + b_ref[0]


def _const_spec(shape):
    nd = len(shape)
    return pl.BlockSpec(shape, lambda ib, ic: (0,) * nd, pipeline_mode=pl.Buffered(1))


def _seq_spec(nb, rs, width, bc):
    return pl.BlockSpec((nb, rs, width), lambda ib, ic: bc(ib, ic) + (0,))


def _step_maps(ngroups, nc, reverse):
    last = ngroups * nc - 1

    def at(f):
        return f // nc, (nc - 1 - f % nc) if reverse else f % nc

    def cur(ib, ic):
        return at(ib * nc + ic)

    def ahead(d):
        return lambda ib, ic: at(jnp.minimum(ib * nc + ic + d, last))

    def fixed(f):
        return lambda ib, ic: at(min(f, last))
    return cur, ahead, fixed


def _scan_scratch():
    return ([pltpu.VMEM((NA, NSEG * PITCH, LANES), F32)] * 4
            + [pltpu.VMEM((BH, NSEG * PITCH, LANES), F32)] * 2)


def _params():
    return pltpu.CompilerParams(dimension_semantics=("arbitrary", "arbitrary"), vmem_limit_bytes=VMEM_LIMIT)


def _state_specs(nb):
    return [pl.BlockSpec((nb, 1, AW), lambda ib, ic: (ib, 0, 0)),
            pl.BlockSpec((nb, BH, HP, HP), lambda ib, ic: (ib, 0, 0, 0))]


def _run_kb(x, mod, mod_row, weights, layer, h0, s0, pos, nb, want_out=True):
    bsz, seq, _ = x.shape
    rs = T // nb
    nc = seq // rs
    add_pos = pos is not None
    cur, ahead, fixed = _step_maps(bsz // nb, nc, reverse=True)
    steps2 = (fixed(0), ahead(1))
    mrow = (lambda g: g) if mod_row is None else (lambda g: mod_row)
    mod_spec = lambda bc: pl.BlockSpec((1, 1, 3 * D), lambda ib, ic: (mrow(bc(ib, ic)[0]), 0, 0))
    ins, specs = [x] * 2, [_seq_spec(nb, rs, D, bc) for bc in steps2]
    if nc > 1:
        ins.append(x)
        specs.append(pl.BlockSpec(
            (1, SUBLANES, D), lambda ib, ic: (ib, jnp.maximum(cur(ib, ic)[1] * (rs // SUBLANES) - 1, 0), 0)))
    if add_pos:
        rowtab, coltab, hpos = pos
        row_spec = lambda bc: pl.BlockSpec((NSEG, D // 2), lambda ib, ic: (bc(ib, ic)[1], 0))
        ins += [rowtab] * 2 + [coltab, hpos]
        specs += [row_spec(bc) for bc in steps2] + [
            _const_spec(coltab.shape), pl.BlockSpec((1, SUBLANES, D), lambda ib, ic: (cur(ib, ic)[1], 0, 0))]
    w_ins, w_specs = _weight_operands(weights, ("nw", "wb", "cw", "cb", "wg", "bg", "cl", "lb"), layer, BWD)
    ins += [mod] * 2 + w_ins + [h0, s0]
    specs += [mod_spec(cur), mod_spec(ahead(1))] + w_specs + _state_specs(nb)
    seq_outs = [(D, BF16), (AW, F32)] + ([(AW, F32), (BW, F32), (BW, F32)] if want_out else []) + [(BW, BF16)]
    out_shape = tuple([jax.ShapeDtypeStruct((bsz, seq, w), dt) for w, dt in seq_outs]
                      + [jax.ShapeDtypeStruct((bsz, 1, AW), F32), jax.ShapeDtypeStruct((bsz, BH, HP, HP), F32)])
    out_specs = tuple([_seq_spec(nb, rs, w, cur) for w, _ in seq_outs] + _state_specs(nb))
    scratch = ([pltpu.VMEM((T, KB_COLS), F32), pltpu.VMEM((T, D), BF16)]
               + [pltpu.VMEM((T + 2 * SUBLANES, AW), F32), pltpu.VMEM((T, 2 * AW), F32), pltpu.VMEM((T, BW), F32)]
               + _scan_scratch()
               + [pltpu.VMEM((SUBLANES, AW), F32), pltpu.VMEM((nb, BH, HP, HP), F32),
                  pltpu.VMEM((SUBLANES, AW), F32)]
               + _gla_scratch())
    return pl.pallas_call(
        functools.partial(_kb_body, nb=nb, nc=nc, add_pos=add_pos, want_out=want_out),
        grid=(bsz // nb, nc), in_specs=specs, out_specs=out_specs, out_shape=out_shape,
        scratch_shapes=scratch, compiler_params=_params(), name="bwd_sweep")(*ins)


def _run_kf(x, kb_out, mod, mod_row, weights, layer, fnw, h0, s0, pos, nb, final):
    bsz, seq, _ = x.shape
    rs = T // nb
    nc = seq // rs
    add_pos = pos is not None
    cur, ahead, fixed = _step_maps(bsz // nb, nc, reverse=False)
    mrow = (lambda g: g) if mod_row is None else (lambda g: mod_row)
    hbf, xc, hb, ob, q, v = kb_out
    ins, specs = [x], [_seq_spec(nb, rs, D, cur)]
    if add_pos:
        rowtab, coltab, _ = pos
        ins += [rowtab, coltab]
        specs += [pl.BlockSpec((NSEG, D // 2), lambda ib, ic: (cur(ib, ic)[1], 0)), _const_spec(coltab.shape)]
    w_ins, w_specs = _weight_operands(
        weights, ("wf", "wg", "bg", "cl", "lb", "bnw", "cnw", "cws", "cbias", "wo"), layer, FWD)
    ins += [hbf, hbf, xc, hb, ob, q, v, mod] + w_ins + [fnw, h0, s0]
    specs += [_seq_spec(nb, rs, D, fixed(0)), _seq_spec(nb, rs, D, ahead(1))]
    specs += [_seq_spec(nb, rs, w, cur) for w in (AW, AW, BW, BW, BW)]
    specs += [pl.BlockSpec((1, 1, 3 * D), lambda ib, ic: (mrow(ib), 0, 0))]
    specs += w_specs + [_const_spec(fnw.shape)] + _state_specs(nb)
    out_shape = (jax.ShapeDtypeStruct((bsz, seq, D), F32),
                 jax.ShapeDtypeStruct((bsz, 1, AW), F32), jax.ShapeDtypeStruct((bsz, BH, HP, HP), F32))
    out_specs = tuple([_seq_spec(nb, rs, D, cur)] + _state_specs(nb))
    scratch = ([pltpu.VMEM((T, KF_COLS), F32), pltpu.VMEM((T, AW + BW), F32), pltpu.VMEM((T, D), BF16)]
               + [pltpu.VMEM((T, 2 * AW), F32), pltpu.VMEM((T, BW), F32), pltpu.VMEM((T, YW), BF16),
                  pltpu.VMEM((T, D), F32)]
               + _scan_scratch()
               + [pltpu.VMEM((SUBLANES, AW), F32), pltpu.VMEM((nb, BH, HP, HP), F32),
                  pltpu.VMEM((T, AW), BF16), pltpu.VMEM((CCH, CW), BF16), pltpu.VMEM((CH * CCH, CW), F32)]
               + _gla_scratch())
    return pl.pallas_call(
        functools.partial(_kf_body, nb=nb, nc=nc, add_pos=add_pos, final=final),
        grid=(bsz // nb, nc), in_specs=specs, out_specs=out_specs, out_shape=out_shape,
        scratch_shapes=scratch, compiler_params=_params(), name="fwd_sweep")(*ins)


def _run_kf_states(hbf, xc, v, weights, layer, h0, s0, nb):
    bsz, seq, _ = hbf.shape
    rs = T // nb
    nc = seq // rs
    cur, ahead, fixed = _step_maps(bsz // nb, nc, reverse=False)
    w_ins, w_specs = _weight_operands(weights, ("wf", "wg", "bg", "cl", "lb"), layer, FWD)
    ins = [hbf, hbf, xc, v] + w_ins + [h0, s0]
    specs = ([_seq_spec(nb, rs, D, fixed(0)), _seq_spec(nb, rs, D, ahead(1)),
              _seq_spec(nb, rs, AW, cur), _seq_spec(nb, rs, BW, cur)] + w_specs + _state_specs(nb))
    out_shape = (jax.ShapeDtypeStruct((bsz, 1, AW), F32), jax.ShapeDtypeStruct((bsz, BH, HP, HP), F32))
    scratch = ([pltpu.VMEM((T, BW), F32), pltpu.VMEM((T, D), BF16), pltpu.VMEM((T, 2 * AW), F32),
                pltpu.VMEM((T, BW), F32)]
               + _scan_scratch()
               + [pltpu.VMEM((SUBLANES, AW), F32), pltpu.VMEM((nb, BH, HP, HP), F32), pltpu.VMEM((T, AW), BF16)]
               + _gla_scratch())
    return pl.pallas_call(
        functools.partial(_kf_states_body, nb=nb, nc=nc),
        grid=(bsz // nb, nc), in_specs=specs, out_specs=tuple(_state_specs(nb)), out_shape=out_shape,
        scratch_shapes=scratch, compiler_params=_params(), name="fwd_states")(*ins)


def _pad_heads(w, fill=0.0):
    lead = w.shape[:-1]
    w = w.reshape(lead + (BH, BDK))
    w = jnp.pad(w, [(0, 0)] * len(lead) + [(0, 0), (0, HP - BDK)], constant_values=fill)
    return w.reshape(lead + (BW,))


def _prepare_weights(norm_w, w_in, a_conv_w, a_conv_b, a_wr, a_br, a_wi, a_bi, a_lambda, lbs, b_norm_w,
                     c_norm_w, c_ws, c_bs, w_out):
    depth = w_in.shape[0]
    nbw = BH * BDK
    w_in, w_out = w_in.astype(BF16), w_out.astype(BF16)
    zero_cols = jnp.zeros((depth, D, HP - BDK), BF16)
    zero_rows = jnp.zeros((depth, HP - BDK, D), BF16)

    half = jnp.asarray(0.5, BF16)

    def group(g, scale=None):
        base = 2 * AW + g * nbw
        cols = [w_in[:, :, base + h * BDK:base + (h + 1) * BDK] for h in range(BH)]
        return [p for c in cols for p in (c if scale is None else c * scale, zero_cols)]
    wo_b = [p for h in range(BH) for p in (w_out[:, AW + h * BDK:AW + (h + 1) * BDK], zero_rows)]
    c_base = 2 * AW + 5 * nbw
    eye = jnp.eye(A_HEADS, dtype=F32)

    def block_diag(w):
        return jnp.einsum("ldhij,hg->ldhigj", w, eye).reshape(depth, 2, AW, AW)
    return {
        "nw": norm_w[:, None, :],
        "wb": jnp.concatenate([w_in[:, :, :AW]] + group(0, half) + group(2, half) + group(3), axis=-1),
        "wf": jnp.concatenate([w_in[:, :, AW:2 * AW] * half] + group(1, half) + group(4, half)
                              + [w_in[:, :, c_base:c_base + 2 * CW], w_in[:, :, c_base + 2 * CW:] * half], axis=-1),
        "cw": a_conv_w, "cb": a_conv_b[:, None, :],
        "lb": _pad_heads(lbs[:depth], fill=1.0)[:, None, :],
        "bnw": _pad_heads(b_norm_w)[:, None, :],
        "cnw": c_norm_w[:, None, :],
        "cws": c_ws.reshape(depth, CH * CCH, CCH).astype(BF16),
        "cbias": jnp.repeat(jnp.swapaxes(c_bs, 1, 2), CHD, axis=2),
        "wo": jnp.concatenate(wo_b + [w_out[:, :AW], w_out[:, AW + nbw:]], axis=1),
        "wg": (0.5 * jnp.concatenate([block_diag(a_wr), block_diag(a_wi)], axis=-1)).astype(BF16),
        "bg": 0.5 * jnp.concatenate([a_br, a_bi], axis=-1)[:, :, None, :],
        "cl": (-0.5 * RG_C * jax.nn.softplus(-a_lambda))[:, :, None, :],
    }


FWD, BWD = 0, 1
PER_DIRECTION = ("wg", "bg", "cl")


def _weight_operands(weights, names, layer, direction):
    arrays, specs = [], []
    for n in names:
        w = weights[n]
        lead = (layer, direction) if n in PER_DIRECTION else (layer,)
        rest = w.shape[len(lead):]
        arrays.append(w)
        specs.append(pl.BlockSpec((None,) * len(lead) + rest,
                                  lambda ib, ic, lead=lead, rest=rest: lead + (0,) * len(rest),
                                  pipeline_mode=pl.Buffered(1)))
    return arrays, specs


def _pos_tables(rows):
    qd = D // 4
    omega = 1.0 / (10000.0 ** (jnp.arange(qd, dtype=F32) / qd))
    ar = jnp.arange(rows, dtype=F32)[:, None] * omega
    ac = jnp.arange(GRID_W, dtype=F32)[:, None] * omega
    rowtab = jnp.concatenate([jnp.sin(ar), jnp.cos(ar)], axis=-1)
    coltab = jnp.concatenate([jnp.sin(ac), jnp.cos(ac)], axis=-1)
    nc = rows * GRID_W // T
    hrow = jnp.broadcast_to(rowtab[jnp.maximum(jnp.arange(nc) * NSEG - 1, 0)][:, None, :], (nc, SUBLANES, D // 2))
    hcol = jnp.broadcast_to(coltab[GRID_W - SUBLANES:][None], (nc, SUBLANES, D // 2))
    return rowtab, coltab, jnp.concatenate([hrow, hcol], axis=-1)


def kernel(x, c, ctx, c_ctx, norm_w, w_mod, b_mod, w_in, a_conv_w, a_conv_b, a_wr, a_br, a_wi, a_bi, a_lambda,
           b_lb_logits, b_norm_w, c_norm_w, c_ws, c_bs, w_out, final_norm_w):
    bsz, n_lat, _ = x.shape
    depth = w_in.shape[0]
    assert n_lat % T == 0 and T % ctx.shape[1] == 0 and bsz % (T // ctx.shape[1]) == 0
    nb_ctx = T // ctx.shape[1]
    pos = _pos_tables(n_lat // GRID_W)

    mrows = 2 * SUBLANES
    cs = jnp.zeros((mrows, D), F32).at[:bsz].set(c).at[bsz].set(c_ctx)
    nblk = 3
    mod = pl.pallas_call(
        _mod_body, grid=(depth, nblk),
        in_specs=[pl.BlockSpec((mrows, D), lambda l, j: (0, 0)),
                  pl.BlockSpec((1, D, D), lambda l, j: (l, 0, j)),
                  pl.BlockSpec((1, 1, D), lambda l, j: (l, 0, j))],
        out_specs=pl.BlockSpec((1, mrows, D), lambda l, j: (l, 0, j)),
        out_shape=jax.ShapeDtypeStruct((depth, mrows, 3 * D), F32),
        name="modulation")(cs, w_mod, b_mod[:, None, :])
    mod = mod.reshape(depth, mrows, 1, 3 * D)

    lbs = jnp.cumsum(jax.nn.softmax(b_lb_logits, axis=0), axis=0)
    fnw = final_norm_w[None, :]
    h_zero = jnp.zeros((bsz, 1, AW), F32)
    s_zero = jnp.zeros((bsz, BH, HP, HP), F32)
    weights = _prepare_weights(norm_w, w_in, a_conv_w, a_conv_b, a_wr, a_br, a_wi, a_bi, a_lambda, lbs, b_norm_w,
                               c_norm_w, c_ws, c_bs, w_out)
    for l in range(depth):
        lpos = pos if l == 0 else None
        last = l == depth - 1
        kb = _run_kb(ctx, mod[l], bsz, weights, l, h_zero, s_zero, None, nb_ctx, want_out=not last)
        hb_c, sb_c = kb[-2], kb[-1]
        if last:
            hbf_c, xc_c, v_c = kb[:3]
            hf_c, sf_c = _run_kf_states(hbf_c, xc_c, v_c, weights, l, h_zero, s_zero, nb_ctx)
        else:
            ctx, hf_c, sf_c = _run_kf(ctx, kb[:6], mod[l], bsz, weights, l, fnw, h_zero, s_zero, None, nb_ctx, False)
        kb = _run_kb(x, mod[l], None, weights, l, hb_c, sb_c, lpos, 1)
        x, _, _ = _run_kf(x, kb[:6], mod[l], None, weights, l, fnw, hf_c, sf_c, lpos, 1, last)
    return x
```

```python
import functools

import jax
import jax.numpy as jnp
import numpy as np
from jax import lax
from jax.experimental import pallas as pl
from jax.experimental.pallas import tpu as pltpu

D = 1024
GRID_W = 64
EPS = 1e-6
RG_C = 8.0
AW = 384
A_HEADS = 8
A_HD = 48
BH = 4
BDK = 96
CW = 256
CH = 4
CHD = 64
CCH = 128

LANES = 128
SUBLANES = 8
HP = LANES
BW = BH * HP
YW = AW + BW + CW
NA = AW // LANES
T = 512
SEG = 64
NSEG = T // SEG
PITCH = SEG + SUBLANES
KB_COLS = AW + 3 * BW
KF_COLS = AW + 2 * BW + 3 * CW
FILL_COLS = 256
FILL_ROWS = T // 2
ROW_TILE = 16
MLP_TILE = 32
VMEM_LIMIT = 58 * 1024 * 1024

F32 = jnp.float32
BF16 = jnp.bfloat16

C_CONV_SEG, C_COEFF_TILE, C_SCAN_STEP, C_FIX_TILE, C_YA_TILE = 100, 50, 12, 8, 15
C_GATES_TILE, C_CUMSUM_STEP, C_GLA_A_TILE, C_GLA_C_TILE, C_SILU_TILE = 20, 8, 40, 15, 12
C_PRE_TILE, C_MLP_TILE, C_GLA_B_TILE, C_GLA_D_TILE = 25, 90, 40, 40
C_PROJ_PIECE, C_OUT_PIECE, C_NORM_PIECE = FILL_ROWS, 3 * FILL_ROWS // 4, 130
PACE = 1.4


def _silu_of_twice(hx):
    return hx * (jnp.tanh(hx) + 1.0)


def _silu(x):
    return _silu_of_twice(0.5 * x)


def _gelu(x):
    c = np.float32(np.sqrt(2.0 / np.pi))
    hx = 0.5 * x
    return hx * (jnp.tanh(x * (c + (0.044715 * c) * (x * x))) + 1.0)


def _nt(a, b):
    return lax.dot_general(a, b, (((1,), (1,)), ((), ())), preferred_element_type=F32)


def _tn(a, b):
    return lax.dot_general(a, b, (((0,), (0,)), ((), ())), preferred_element_type=F32)


def _norm_mod(x, nw, sh, sc):
    ms = jnp.mean(x * x, axis=-1, keepdims=True)
    return (x * lax.rsqrt(ms + EPS) * nw) * (1.0 + sc) + sh


def _pos_rows(rt_ref, ct_ref, r0):
    s, c0 = divmod(r0, SEG)
    return jnp.concatenate([jnp.broadcast_to(rt_ref[s:s + 1, :], (ROW_TILE, D // 2)),
                            ct_ref[c0:c0 + ROW_TILE, :]], axis=1)


class _Fill:
    def __init__(self, items=(), rate=1.0):
        self.items = list(items)
        self.rate = rate
        self.credit = 0.0
        self.spent = 0.0

    def __call__(self, credit):
        self.credit += credit
        while self.items and self.spent + 0.5 * self.items[0][0] <= self.credit * self.rate:
            cost, thunk = self.items.pop(0)
            self.spent += cost
            thunk()

    def extend(self, items):
        self.items.extend(items)

    def flush(self):
        while self.items:
            self.items.pop(0)[1]()


def _cost(items):
    return sum(c for c, _ in items)


def _emit(items, fill=None):
    for cost, thunk in items:
        thunk()
        if fill is not None:
            fill(cost)


def _paired(items, partner_items):
    partner = _Fill(partner_items, _cost(partner_items) / max(_cost(items), 1))
    _emit(items, partner)
    partner.flush()


class _Flat:
    def __init__(self, ref, width):
        self.ref = ref
        self.rs = ref.shape[1]
        self.width = width

    def _split(self, idx):
        rows, lanes = idx
        start, size = rows.start, rows.size
        q, off = divmod(start, self.rs)
        assert off + size <= self.rs
        return q, pl.ds(off, size), lanes

    def __getitem__(self, idx):
        q, r, lanes = self._split(idx)
        return self.ref[q, r, lanes]

    def __setitem__(self, idx, val):
        q, r, lanes = self._split(idx)
        self.ref[q, r, lanes] = val


def _tiles():
    return [(s, k) for s in range(NSEG) for k in range(BH)]


class _AScan:
    def __init__(self, a_st, u_st, h_st, p_st, hc_ref, nb, reverse):
        self.refs = (a_st, u_st, h_st, p_st)
        self.hc_ref, self.nb, self.reverse = hc_ref, nb, reverse
        self.hs = [jnp.zeros((NSEG, LANES), F32)] * NA
        self.ps = [jnp.ones((NSEG, LANES), F32)] * NA

    def coeff_items(self, g_scr, xc_ref, cl_ref):
        a_st, u_st = self.refs[:2]

        def item(s, k):
            def run():
                rows, lanes = pl.ds(s * SEG, SEG), slice(k * LANES, (k + 1) * LANES)
                tr = jnp.tanh(g_scr[rows, lanes])
                ti = jnp.tanh(g_scr[rows, AW + k * LANES:AW + (k + 1) * LANES])
                hcl = cl_ref[:, lanes]
                la = hcl * tr + hcl
                th = jnp.tanh(la)
                om4 = (-0.5 * th) / (1.0 - th)
                a_st[k, pl.ds(s * PITCH, SEG), :] = jnp.exp(la)
                u_st[k, pl.ds(s * PITCH, SEG), :] = jnp.sqrt(om4) * ((ti + 1.0) * xc_ref[rows, lanes])
            return C_COEFF_TILE, run
        return [item(s, k) for s in range(NSEG) for k in range(NA)]

    def step_items(self):
        a_st, u_st, h_st, p_st = self.refs

        def item(j):
            def run():
                for k in range(NA):
                    a = a_st[k, pl.ds(j, NSEG, stride=PITCH), :]
                    u = u_st[k, pl.ds(j, NSEG, stride=PITCH), :]
                    self.hs[k] = a * self.hs[k] + u
                    self.ps[k] = a * self.ps[k]
                    h_st[k, pl.ds(j, NSEG, stride=PITCH), :] = self.hs[k]
                    p_st[k, pl.ds(j, NSEG, stride=PITCH), :] = self.ps[k]
            return C_SCAN_STEP, run
        order = range(SEG - 1, -1, -1) if self.reverse else range(SEG)
        return [item(j) for j in order]

    def chain(self):
        spq = NSEG // self.nb
        cin = [None] * NSEG
        for q in range(self.nb):
            c = [self.hc_ref[q:q + 1, k * LANES:(k + 1) * LANES] for k in range(NA)]
            order = range((q + 1) * spq - 1, q * spq - 1, -1) if self.reverse else range(q * spq, (q + 1) * spq)
            for s in order:
                cin[s] = c
                c = [self.ps[k][s:s + 1, :] * c[k] + self.hs[k][s:s + 1, :] for k in range(NA)]
            for k in range(NA):
                self.hc_ref[q:q + 1, k * LANES:(k + 1) * LANES] = c[k]
        return cin

    def full(self, cin, s, k):
        _, _, h_st, p_st = self.refs
        st_rows = pl.ds(s * PITCH, SEG)
        return h_st[k, st_rows, :] + p_st[k, st_rows, :] * cin[s][k]


class _Gla:
    def __init__(self, reverse, nb, k_scr, lf_st, g_st, s_scr, stage, states_only=False):
        self.reverse, self.nb, self.states_only = reverse, nb, states_only
        self.k_scr, self.lf_st, self.g_st, self.s_scr = k_scr, lf_st, g_st, s_scr
        self.kg_s, self.ke_s, self.lhs_s, self.kv_s, self.rhs_s = stage
        self.gs = [jnp.zeros((NSEG, LANES), F32)] * BH
        order = range(NSEG - 1, -1, -1) if reverse else range(NSEG)
        self.order = list(order)
        self.tiles = [(n, k) for n in order for k in range(BH)]

    def gate_items(self, p_scr, col0, lb_ref):
        def item(s, k):
            def run():
                rows, lanes = pl.ds(s * SEG, SEG), slice(k * HP, (k + 1) * HP)
                lb = lb_ref[:, lanes]
                c0, c1 = 0.5 + 0.5 * lb, 0.5 - 0.5 * lb
                ct = c1 * jnp.tanh(p_scr[rows, col0 + k * HP:col0 + (k + 1) * HP])
                self.k_scr[rows, lanes] = c1 - ct
                self.lf_st[k, pl.ds(s * PITCH, SEG), :] = jnp.log(c0 + ct)
            return C_GATES_TILE, run
        return [item(s, k) for s, k in _tiles()]

    def cumsum_items(self):
        def item(j):
            def run():
                for k in range(BH):
                    self.gs[k] = self.gs[k] + self.lf_st[k, pl.ds(j, NSEG, stride=PITCH), :]
                    self.g_st[k, pl.ds(j, NSEG, stride=PITCH), :] = self.gs[k]
            return C_CUMSUM_STEP, run
        order = range(SEG - 1, -1, -1) if self.reverse else range(SEG)
        return [item(j) for j in order]

    def a_items(self, load_q, load_v):
        def item(n, k):
            def run():
                rows, lanes = pl.ds(n * SEG, SEG), slice(k * HP, (k + 1) * HP)
                g = self.g_st[k, pl.ds(n * PITCH, SEG), :]
                tot = self.gs[k][n:n + 1, :]
                kk = self.k_scr[rows, lanes]
                if not self.states_only:
                    self.lhs_s[k, rows, 0:HP] = (load_q(n, k) * jnp.exp(g)).astype(BF16)
                    self.rhs_s[n, k, HP:HP + SEG, :] = load_v(n, k)
                    self.kg_s[rows, lanes] = (kk * jnp.exp(-g)).astype(BF16)
                self.ke_s[rows, lanes] = (kk * jnp.exp(tot - g)).astype(BF16)
            return C_GLA_A_TILE, run
        return [item(n, k) for n, k in self.tiles]

    def b_items(self, load_v):
        r_i = lax.broadcasted_iota(jnp.int32, (SEG, SEG), 0)
        c_i = lax.broadcasted_iota(jnp.int32, (SEG, SEG), 1)
        keep = (c_i >= r_i) if self.reverse else (c_i <= r_i)

        def item(n, k):
            def run():
                rows, lanes = pl.ds(n * SEG, SEG), slice(k * HP, (k + 1) * HP)
                if not self.states_only:
                    self.lhs_s[k, rows, HP:HP + SEG] = jnp.where(
                        keep, _nt(self.lhs_s[k, rows, 0:HP], self.kg_s[rows, lanes]), 0.0).astype(BF16)
                self.kv_s[n, k] = _tn(load_v(n, k), self.ke_s[rows, lanes])
            return C_GLA_B_TILE, run
        return [item(n, k) for n, k in self.tiles]

    def c_items(self):
        cps = NSEG // self.nb
        state = {}

        def item(n, k):
            def run():
                sq = n // cps
                st = state.get(k)
                if st is None:
                    st = self.s_scr[sq, k]
                if not self.states_only:
                    self.rhs_s[n, k, 0:HP, :] = st.T.astype(BF16)
                st = jnp.exp(self.gs[k][n:n + 1, :]) * st + self.kv_s[n, k]
                last = (n % cps == 0) if self.reverse else (n % cps == cps - 1)
                if last:
                    self.s_scr[sq, k] = st
                    st = None
                state[k] = st
            return C_GLA_C_TILE, run
        return [item(n, k) for n in self.order for k in range(BH)]

    def d_items(self, emit_out):
        def item(n, k):
            def run():
                rows, lanes = pl.ds(n * SEG, SEG), slice(k * HP, (k + 1) * HP)
                o = jnp.dot(self.lhs_s[k, rows, :], self.rhs_s[n, k], preferred_element_type=F32)
                emit_out(n, k, o)
            return C_GLA_D_TILE, run
        return [item(n, k) for n, k in self.tiles]


def _gla_scratch():
    return ([pltpu.VMEM((T, BW), BF16)] * 2
            + [pltpu.VMEM((BH, T, HP + SEG), BF16), pltpu.VMEM((NSEG, BH, HP, HP), F32),
               pltpu.VMEM((NSEG, BH, HP + SEG, HP), BF16)])


def _dot_items(lhs_rows, w_ref, dst, col_lo, col_hi, cost):
    def piece(r0, c0, c1):
        def run():
            dst[r0:r0 + FILL_ROWS, c0:c1] = jnp.dot(lhs_rows(r0, r0 + FILL_ROWS), w_ref[:, c0:c1],
                                                    preferred_element_type=F32)
        return cost, run
    return [piece(r0, c0, min(c0 + FILL_COLS, col_hi))
            for c0 in range(col_lo, col_hi, FILL_COLS) for r0 in range(0, T, FILL_ROWS)]


def _kb_body(*refs, nb, nc, add_pos, want_out):
    it = iter(refs)
    x0_ref, xn_ref = next(it), next(it)
    xh_ref = next(it) if nc > 1 else None
    if add_pos:
        rt0_ref, rtn_ref, ct_ref, hp_ref = next(it), next(it), next(it), next(it)
    else:
        rt0_ref = rtn_ref = ct_ref = hp_ref = None
    (mod_ref, modn_ref, nw_ref, wb_ref, cw_ref, cb_ref, wg_ref, bg_ref, cl_ref, lb_ref, h0_ref, s0_ref,
     hbf_ref, xc_ref) = (next(it) for _ in range(14))
    hb_ref, ob_ref, q_ref = (next(it), next(it), next(it)) if want_out else (None, None, None)
    (v_ref, hfin_ref, sfin_ref,
     p_scr, h_scr, xa_ext, g_scr, k_scr, a_st, u_st, h_st, p_st, lf_st, g_st, hc_ref, s_scr, xaf,
     *gla_stage) = it

    ic = pl.program_id(1)
    flat = pl.program_id(0) * nc + ic
    cc = nc - 1 - ic

    @pl.when(ic == 0)
    def _():
        for q in range(nb):
            hc_ref[q:q + 1, :] = h0_ref[q]
        s_scr[...] = s0_ref[...]
        xaf[...] = jnp.zeros_like(xaf)

    def norm_items(x_r, rt_r, m_r):
        x_flat = _Flat(x_r, D)

        def item(s):
            def run():
                rows = pl.ds(s * SEG, SEG)
                xin = x_flat[rows, slice(None)]
                if add_pos:
                    xin = xin + jnp.concatenate(
                        [_pos_rows(rt_r, ct_ref, r0) for r0 in range(s * SEG, (s + 1) * SEG, ROW_TILE)], axis=0)
                h_scr[rows, :] = _norm_mod(xin, nw_ref[...], m_r[0, :, 0:D], m_r[0, :, D:2 * D]).astype(BF16)
            return C_NORM_PIECE, run
        return [item(s) for s in range(NSEG)]

    def proj_items():
        return _dot_items(lambda r0, r1: h_scr[r0:r1, :], wb_ref, p_scr, 0, KB_COLS, C_PROJ_PIECE)

    gla = _Gla(True, nb, k_scr, lf_st, g_st, s_scr, gla_stage, states_only=not want_out)
    v_flat = _Flat(v_ref, BW)
    if want_out:
        q_flat, ob_flat = _Flat(q_ref, BW), _Flat(ob_ref, BW)

    def hgrn_input_items():
        def qv_item(s, k):
            def run():
                rows, lanes = pl.ds(s * SEG, SEG), slice(k * HP, (k + 1) * HP)
                if want_out:
                    q_flat[rows, lanes] = _silu_of_twice(p_scr[rows, AW + k * HP:AW + (k + 1) * HP])
                v_flat[rows, lanes] = p_scr[rows, AW + 2 * BW + k * HP:AW + 2 * BW + (k + 1) * HP].astype(BF16)
            return C_PRE_TILE, run
        return [qv_item(s, k) for s, k in _tiles()] + gla.gate_items(p_scr, AW + BW, lb_ref)

    @pl.when(flat == 0)
    def _():
        _emit(norm_items(x0_ref, rt0_ref, mod_ref) + proj_items())

    def step():
        xc_flat = _Flat(xc_ref, AW)
        hb_flat = _Flat(hb_ref, AW) if want_out else None
        scan = _AScan(a_st, u_st, h_st, p_st, hc_ref, nb, reverse=True)

        hbf_ref[...] = h_scr[...].reshape(hbf_ref.shape)
        if nc > 1:
            xh = xh_ref[0]
            if add_pos:
                xh = xh + hp_ref[0]
            hh = _norm_mod(xh, nw_ref[...], mod_ref[0, :, 0:D], mod_ref[0, :, D:2 * D]).astype(BF16)
            left = jnp.dot(hh, wb_ref[:, 0:AW], preferred_element_type=F32)
            xa_ext[0:SUBLANES, :] = jnp.where(cc > 0, left, 0.0)
        else:
            xa_ext[0:SUBLANES, :] = jnp.zeros((SUBLANES, AW), F32)
        xa_ext[pl.ds(SUBLANES, T), :] = p_scr[:, 0:AW]
        xa_ext[pl.ds(SUBLANES + T, SUBLANES), :] = xaf[...]
        xaf[...] = p_scr[0:SUBLANES, 0:AW]

        def conv_item(s):
            def run():
                spq = NSEG // nb
                row = lax.broadcasted_iota(jnp.int32, (SEG, 1), 0)
                acc = jnp.broadcast_to(cb_ref[...], (SEG, AW))
                for j in range(4):
                    tap = xa_ext[pl.ds(s * SEG + SUBLANES - 2 + j, SEG), :]
                    if nb > 1 and s % spq == 0 and j < 2:
                        tap = jnp.where(row + (j - 2) >= 0, tap, 0.0)
                    if nb > 1 and s % spq == spq - 1 and j == 3:
                        tap = jnp.where(row + 1 < SEG, tap, 0.0)
                    acc = acc + cw_ref[j:j + 1, :] * tap
                xc_flat[pl.ds(s * SEG, SEG), slice(None)] = acc
            return C_CONV_SEG, run

        def load_q(n, k):
            return q_flat[pl.ds(n * SEG, SEG), slice(k * HP, (k + 1) * HP)]

        def load_v(n, k):
            return v_flat[pl.ds(n * SEG, SEG), slice(k * HP, (k + 1) * HP)]

        def emit_out(n, k, o):
            ob_flat[pl.ds(n * SEG, SEG), slice(k * HP, (k + 1) * HP)] = o

        _emit(hgrn_input_items())
        _emit(norm_items(xn_ref, rtn_ref, modn_ref))

        conv = [conv_item(s) for s in range(NSEG)]
        coeff, steps, cums = scan.coeff_items(g_scr, xc_flat, cl_ref), scan.step_items(), gla.cumsum_items()
        proj = proj_items()
        credit = (_cost(conv) + _cost(coeff) + _cost(steps) + _cost(cums)
                  + BH * NSEG * (C_GLA_A_TILE + C_GLA_C_TILE) + NA * NSEG * C_FIX_TILE)
        mx = _Fill(proj, PACE * _cost(proj) / credit)
        _emit(conv, mx)
        g_scr[...] = jnp.dot(xc_ref[...].reshape(T, AW).astype(BF16), wg_ref[...],
                             preferred_element_type=F32) + bg_ref[...]
        _emit(coeff, mx)
        _emit(steps, mx)
        cin = scan.chain()
        for q in range(nb):
            hfin_ref[q] = hc_ref[q:q + 1, :]

        def fix_item(s, k):
            def run():
                hb_flat[pl.ds(s * SEG, SEG), k * LANES:(k + 1) * LANES] = scan.full(cin, s, k)
            return C_FIX_TILE, run
        if want_out:
            _emit([fix_item(s, k) for s in range(NSEG) for k in range(NA)], mx)

        _emit(cums, mx)
        _emit(gla.a_items(load_q, load_v), mx)
        _emit(gla.b_items(load_v))
        _emit(gla.c_items(), mx)
        if want_out:
            _emit(gla.d_items(emit_out))
        sfin_ref[...] = s_scr[...]
        mx.flush()

    step()


def _kf_body(*refs, nb, nc, add_pos, final):
    it = iter(refs)
    x_ref = next(it)
    if add_pos:
        rt_ref, ct_ref = next(it), next(it)
    (hbf0_ref, hbfn_ref, xc_ref, hb_ref, ob_ref, q_ref, v_ref, mod_ref, wf_ref, wg_ref, bg_ref, cl_ref, lb_ref,
     bnw_ref, cnw_ref, cws_ref, cbias_ref, wo_ref, fnw_ref, h0_ref, s0_ref,
     xo_ref, hfin_ref, sfin_ref,
     p_scr, sg_scr, hn_scr, g_scr, k_scr, y_scr, part_scr, a_st, u_st, h_st, p_st, lf_st, g_st, hc_ref, s_scr,
     xcb_scr, vn_scr, mix_scr, *gla_stage) = it

    ic = pl.program_id(1)
    flat = pl.program_id(0) * nc + ic

    @pl.when(ic == 0)
    def _():
        for q in range(nb):
            hc_ref[q:q + 1, :] = h0_ref[q]
        s_scr[...] = s0_ref[...]

    o_ag, o_ff, o_bg = 0, AW, AW + BW
    o_cu, o_cv, o_cg = AW + 2 * BW, AW + 2 * BW + CW, AW + 2 * BW + 2 * CW

    def proj_items(col_lo, col_hi):
        return _dot_items(lambda r0, r1: hn_scr[r0:r1, :], wf_ref, p_scr, col_lo, col_hi, C_PROJ_PIECE)

    @pl.when(flat == 0)
    def _():
        hn_scr[...] = hbf0_ref[...].reshape(T, D)
        _emit(proj_items(0, KF_COLS))

    def step():
        xc_flat, hb_flat = _Flat(xc_ref, AW), _Flat(hb_ref, AW)
        q_flat, v_flat, ob_flat = _Flat(q_ref, BW), _Flat(v_ref, BW), _Flat(ob_ref, BW)
        scan = _AScan(a_st, u_st, h_st, p_st, hc_ref, nb, reverse=False)
        gla = _Gla(False, nb, k_scr, lf_st, g_st, s_scr, gla_stage)

        hn_scr[...] = hbfn_ref[...].reshape(T, D)
        for s in range(NSEG):
            rows = pl.ds(s * SEG, SEG)
            xcb_scr[rows, :] = xc_flat[rows, slice(None)].astype(BF16)

        def silu_item(s, c0):
            def run():
                rows = pl.ds(s * SEG, SEG)
                dst = c0 - o_ag if c0 < o_ff else AW + c0 - o_bg
                sg_scr[rows, dst:dst + LANES] = _silu_of_twice(p_scr[rows, c0:c0 + LANES])
            return C_SILU_TILE, run
        gate_cols = list(range(o_ag, o_ag + AW, LANES)) + list(range(o_bg, o_bg + BW, LANES))

        def mlp_items(m):
            lane = lax.broadcasted_iota(jnp.int32, (MLP_TILE, CW), 1)

            def norm_item(r0):
                def run():
                    v = _gelu(p_scr[pl.ds(m * CCH + r0, MLP_TILE), o_cv:o_cv + CW])
                    mu = jnp.mean(v, axis=-1, keepdims=True)
                    dv = v - mu
                    var = jnp.mean(dv * dv, axis=-1, keepdims=True)
                    vn_scr[r0:r0 + MLP_TILE, :] = (dv * lax.rsqrt(var + EPS) * cnw_ref[...]).astype(BF16)
                return C_MLP_TILE, run

            def mix():
                mix_scr[...] = jnp.dot(cws_ref[...], vn_scr[...], preferred_element_type=F32)

            def out_item(r0):
                def run():
                    rows = pl.ds(m * CCH + r0, MLP_TILE)
                    mixed = mix_scr[r0:r0 + MLP_TILE, :]
                    for h in range(1, CH):
                        mixed = jnp.where(lane >= h * CHD, mix_scr[h * CCH + r0:h * CCH + r0 + MLP_TILE, :], mixed)
                    u = _gelu(p_scr[rows, o_cu:o_cu + CW])
                    yc = (u * (mixed + cbias_ref[r0:r0 + MLP_TILE, :])
                          * _silu_of_twice(p_scr[rows, o_cg:o_cg + CW]))
                    y_scr[rows, BW + AW:YW] = yc.astype(BF16)
                return C_MLP_TILE, run
            tiles = range(0, CCH, MLP_TILE)
            return [norm_item(r0) for r0 in tiles] + [(0, mix)] + [out_item(r0) for r0 in tiles]

        def load_q(n, k):
            return q_flat[pl.ds(n * SEG, SEG), slice(k * HP, (k + 1) * HP)]

        def load_v(n, k):
            return v_flat[pl.ds(n * SEG, SEG), slice(k * HP, (k + 1) * HP)]

        def emit_out(n, k, o):
            rows, lanes = pl.ds(n * SEG, SEG), slice(k * HP, (k + 1) * HP)
            o = o + ob_flat[rows, lanes]
            ms = jnp.sum(o * o, axis=-1, keepdims=True) * (1.0 / BDK)
            yb = (o * lax.rsqrt(ms + EPS) * bnw_ref[:, lanes]) * sg_scr[rows, AW + k * HP:AW + (k + 1) * HP]
            y_scr[rows, lanes] = yb.astype(BF16)

        g_scr[...] = jnp.dot(xcb_scr[...], wg_ref[...], preferred_element_type=F32) + bg_ref[...]
        _emit([it for m in range(T // CCH) for it in mlp_items(m)])

        pre = [silu_item(s, c0) for s in range(NSEG) for c0 in gate_cols] + gla.gate_items(p_scr, o_ff, lb_ref)
        coeff, steps, cums = scan.coeff_items(g_scr, xc_flat, cl_ref), scan.step_items(), gla.cumsum_items()
        n_proj = -(-KF_COLS // FILL_COLS) * (T // FILL_ROWS)
        n_out = (D // FILL_COLS) * (T // FILL_ROWS)
        credit = (_cost(pre) + _cost(coeff) + _cost(steps) + _cost(cums)
                  + BH * NSEG * (C_GLA_A_TILE + C_GLA_C_TILE + C_GLA_D_TILE) + NA * NSEG * C_YA_TILE)
        mx = _Fill(proj_items(o_cu, KF_COLS), PACE * (n_proj * C_PROJ_PIECE + n_out * C_OUT_PIECE) / credit)
        _emit(pre, mx)
        mx.extend(proj_items(0, o_cu))

        _emit(coeff, mx)
        _emit(steps, mx)
        cin = scan.chain()
        for q in range(nb):
            hfin_ref[q] = hc_ref[q:q + 1, :]

        def ya_item(s, k):
            def run():
                rows, lanes = pl.ds(s * SEG, SEG), slice(k * LANES, (k + 1) * LANES)
                ya = (scan.full(cin, s, k) + hb_flat[rows, lanes]) * sg_scr[rows, lanes]
                y_scr[rows, BW + k * LANES:BW + (k + 1) * LANES] = ya.astype(BF16)
            return C_YA_TILE, run
        _emit([ya_item(s, k) for s in range(NSEG) for k in range(NA)], mx)
        mx.extend(_dot_items(lambda r0, r1: y_scr[r0:r1, BW:YW], wo_ref.at[BW:YW, :], part_scr, 0, D, C_OUT_PIECE))

        _emit(cums, mx)
        _emit(gla.a_items(load_q, load_v), mx)
        _emit(gla.b_items(load_v))
        _emit(gla.c_items(), mx)
        _emit(gla.d_items(emit_out), mx)
        sfin_ref[...] = s_scr[...]
        mx.flush()

        gm = mod_ref[0, :, 2 * D:3 * D]
        x_flat, xo_flat = _Flat(x_ref, D), _Flat(xo_ref, D)
        def out_item(r0, c0):
            def run():
                rows, cols = pl.ds(r0, FILL_ROWS), slice(c0, c0 + FILL_COLS)
                part_scr[rows, cols] = part_scr[rows, cols] + jnp.dot(y_scr[rows, 0:BW], wo_ref[0:BW, cols],
                                                                      preferred_element_type=F32)
            return C_OUT_PIECE, run

        def residual_item(r0):
            def run():
                rows = pl.ds(r0, ROW_TILE)
                xin = x_flat[rows, slice(None)]
                if add_pos:
                    xin = xin + _pos_rows(rt_ref, ct_ref, r0)
                xn = xin + gm * part_scr[rows, :]
                if final:
                    ms = jnp.mean(xn * xn, axis=-1, keepdims=True)
                    xn = xn * lax.rsqrt(ms + EPS) * fnw_ref[...]
                xo_flat[rows, slice(None)] = xn
            return C_OUT_PIECE * (D // FILL_COLS) * ROW_TILE // FILL_ROWS, run

        pending = []
        for r0 in range(0, T, FILL_ROWS):
            _paired([out_item(r0, c0) for c0 in range(0, D, FILL_COLS)], pending)
            pending = [residual_item(r) for r in range(r0, r0 + FILL_ROWS, ROW_TILE)]
        _emit(pending)

    step()


def _kf_states_body(hbf0_ref, hbfn_ref, xc_ref, v_ref, wf_ref, wg_ref, bg_ref, cl_ref, lb_ref, h0_ref, s0_ref,
                    hfin_ref, sfin_ref,
                    p_scr, hn_scr, g_scr, k_scr, a_st, u_st, h_st, p_st, lf_st, g_st, hc_ref, s_scr, xcb_scr,
                    *gla_stage, nb, nc):
    ic = pl.program_id(1)
    flat = pl.program_id(0) * nc + ic

    @pl.when(ic == 0)
    def _():
        for q in range(nb):
            hc_ref[q:q + 1, :] = h0_ref[q]
        s_scr[...] = s0_ref[...]

    def proj_items():
        return _dot_items(lambda r0, r1: hn_scr[r0:r1, :], wf_ref.at[:, AW:AW + BW], p_scr, 0, BW, C_PROJ_PIECE)

    @pl.when(flat == 0)
    def _():
        hn_scr[...] = hbf0_ref[...].reshape(T, D)
        _emit(proj_items())

    xc_flat, v_flat = _Flat(xc_ref, AW), _Flat(v_ref, BW)
    scan = _AScan(a_st, u_st, h_st, p_st, hc_ref, nb, reverse=False)
    gla = _Gla(False, nb, k_scr, lf_st, g_st, s_scr, gla_stage, states_only=True)

    hn_scr[...] = hbfn_ref[...].reshape(T, D)
    for s in range(NSEG):
        rows = pl.ds(s * SEG, SEG)
        xcb_scr[rows, :] = xc_flat[rows, slice(None)].astype(BF16)
    g_scr[...] = jnp.dot(xcb_scr[...], wg_ref[...], preferred_element_type=F32) + bg_ref[...]
    _emit(gla.gate_items(p_scr, 0, lb_ref))

    coeff, steps, cums = scan.coeff_items(g_scr, xc_flat, cl_ref), scan.step_items(), gla.cumsum_items()
    proj = proj_items()
    mx = _Fill(proj, _cost(proj) / (_cost(coeff) + _cost(steps) + _cost(cums)))
    _emit(coeff, mx)
    _emit(steps, mx)
    scan.chain()
    for q in range(nb):
        hfin_ref[q] = hc_ref[q:q + 1, :]
    _emit(cums, mx)

    def load_v(n, k):
        return v_flat[pl.ds(n * SEG, SEG), slice(k * HP, (k + 1) * HP)]
    _emit(gla.a_items(None, load_v))
    _emit(gla.b_items(load_v))
    _emit(gla.c_items())
    sfin_ref[...] = s_scr[...]
    mx.flush()


def _mod_body(cs_ref, w_ref, b_ref, o_ref):
    s = _silu(cs_ref[...])
    o_ref[0] = jnp.dot(s, w_ref[0], preferred_element_type=F32, precision=lax.Precision.HIGHEST) + b_ref[0]


def _const_spec(shape):
    nd = len(shape)
    return pl.BlockSpec(shape, lambda ib, ic: (0,) * nd, pipeline_mode=pl.Buffered(1))


def _seq_spec(nb, rs, width, bc):
    return pl.BlockSpec((nb, rs, width), lambda ib, ic: bc(ib, ic) + (0,))


def _step_maps(ngroups, nc, reverse):
    last = ngroups * nc - 1

    def at(f):
        return f // nc, (nc - 1 - f % nc) if reverse else f % nc

    def cur(ib, ic):
        return at(ib * nc + ic)

    def ahead(d):
        return lambda ib, ic: at(jnp.minimum(ib * nc + ic + d, last))

    def fixed(f):
        return lambda ib, ic: at(min(f, last))
    return cur, ahead, fixed


def _scan_scratch():
    return ([pltpu.VMEM((NA, NSEG * PITCH, LANES), F32)] * 4
            + [pltpu.VMEM((BH, NSEG * PITCH, LANES), F32)] * 2)


def _params():
    return pltpu.CompilerParams(dimension_semantics=("arbitrary", "arbitrary"), vmem_limit_bytes=VMEM_LIMIT)


def _state_specs(nb):
    return [pl.BlockSpec((nb, 1, AW), lambda ib, ic: (ib, 0, 0)),
            pl.BlockSpec((nb, BH, HP, HP), lambda ib, ic: (ib, 0, 0, 0))]


def _run_kb(x, mod, mod_row, weights, layer, h0, s0, pos, nb, want_out=True):
    bsz, seq, _ = x.shape
    rs = T // nb
    nc = seq // rs
    add_pos = pos is not None
    cur, ahead, fixed = _step_maps(bsz // nb, nc, reverse=True)
    steps2 = (fixed(0), ahead(1))
    mrow = (lambda g: g) if mod_row is None else (lambda g: mod_row)
    mod_spec = lambda bc: pl.BlockSpec((1, 1, 3 * D), lambda ib, ic: (mrow(bc(ib, ic)[0]), 0, 0))
    ins, specs = [x] * 2, [_seq_spec(nb, rs, D, bc) for bc in steps2]
    if nc > 1:
        ins.append(x)
        specs.append(pl.BlockSpec(
            (1, SUBLANES, D), lambda ib, ic: (ib, jnp.maximum(cur(ib, ic)[1] * (rs // SUBLANES) - 1, 0), 0)))
    if add_pos:
        rowtab, coltab, hpos = pos
        row_spec = lambda bc: pl.BlockSpec((NSEG, D // 2), lambda ib, ic: (bc(ib, ic)[1], 0))
        ins += [rowtab] * 2 + [coltab, hpos]
        specs += [row_spec(bc) for bc in steps2] + [
            _const_spec(coltab.shape), pl.BlockSpec((1, SUBLANES, D), lambda ib, ic: (cur(ib, ic)[1], 0, 0))]
    w_ins, w_specs = _weight_operands(weights, ("nw", "wb", "cw", "cb", "wg", "bg", "cl", "lb"), layer, BWD)
    ins += [mod] * 2 + w_ins + [h0, s0]
    specs += [mod_spec(cur), mod_spec(ahead(1))] + w_specs + _state_specs(nb)
    seq_outs = [(D, BF16), (AW, F32)] + ([(AW, F32), (BW, F32), (BW, F32)] if want_out else []) + [(BW, BF16)]
    out_shape = tuple([jax.ShapeDtypeStruct((bsz, seq, w), dt) for w, dt in seq_outs]
                      + [jax.ShapeDtypeStruct((bsz, 1, AW), F32), jax.ShapeDtypeStruct((bsz, BH, HP, HP), F32)])
    out_specs = tuple([_seq_spec(nb, rs, w, cur) for w, _ in seq_outs] + _state_specs(nb))
    scratch = ([pltpu.VMEM((T, KB_COLS), F32), pltpu.VMEM((T, D), BF16)]
               + [pltpu.VMEM((T + 2 * SUBLANES, AW), F32), pltpu.VMEM((T, 2 * AW), F32), pltpu.VMEM((T, BW), F32)]
               + _scan_scratch()
               + [pltpu.VMEM((SUBLANES, AW), F32), pltpu.VMEM((nb, BH, HP, HP), F32),
                  pltpu.VMEM((SUBLANES, AW), F32)]
               + _gla_scratch())
    return pl.pallas_call(
        functools.partial(_kb_body, nb=nb, nc=nc, add_pos=add_pos, want_out=want_out),
        grid=(bsz // nb, nc), in_specs=specs, out_specs=out_specs, out_shape=out_shape,
        scratch_shapes=scratch, compiler_params=_params(), name="bwd_sweep")(*ins)


def _run_kf(x, kb_out, mod, mod_row, weights, layer, fnw, h0, s0, pos, nb, final):
    bsz, seq, _ = x.shape
    rs = T // nb
    nc = seq // rs
    add_pos = pos is not None
    cur, ahead, fixed = _step_maps(bsz // nb, nc, reverse=False)
    mrow = (lambda g: g) if mod_row is None else (lambda g: mod_row)
    hbf, xc, hb, ob, q, v = kb_out
    ins, specs = [x], [_seq_spec(nb, rs, D, cur)]
    if add_pos:
        rowtab, coltab, _ = pos
        ins += [rowtab, coltab]
        specs += [pl.BlockSpec((NSEG, D // 2), lambda ib, ic: (cur(ib, ic)[1], 0)), _const_spec(coltab.shape)]
    w_ins, w_specs = _weight_operands(
        weights, ("wf", "wg", "bg", "cl", "lb", "bnw", "cnw", "cws", "cbias", "wo"), layer, FWD)
    ins += [hbf, hbf, xc, hb, ob, q, v, mod] + w_ins + [fnw, h0, s0]
    specs += [_seq_spec(nb, rs, D, fixed(0)), _seq_spec(nb, rs, D, ahead(1))]
    specs += [_seq_spec(nb, rs, w, cur) for w in (AW, AW, BW, BW, BW)]
    specs += [pl.BlockSpec((1, 1, 3 * D), lambda ib, ic: (mrow(ib), 0, 0))]
    specs += w_specs + [_const_spec(fnw.shape)] + _state_specs(nb)
    out_shape = (jax.ShapeDtypeStruct((bsz, seq, D), F32),
                 jax.ShapeDtypeStruct((bsz, 1, AW), F32), jax.ShapeDtypeStruct((bsz, BH, HP, HP), F32))
    out_specs = tuple([_seq_spec(nb, rs, D, cur)] + _state_specs(nb))
    scratch = ([pltpu.VMEM((T, KF_COLS), F32), pltpu.VMEM((T, AW + BW), F32), pltpu.VMEM((T, D), BF16)]
               + [pltpu.VMEM((T, 2 * AW), F32), pltpu.VMEM((T, BW), F32), pltpu.VMEM((T, YW), BF16),
                  pltpu.VMEM((T, D), F32)]
               + _scan_scratch()
               + [pltpu.VMEM((SUBLANES, AW), F32), pltpu.VMEM((nb, BH, HP, HP), F32),
                  pltpu.VMEM((T, AW), BF16), pltpu.VMEM((CCH, CW), BF16), pltpu.VMEM((CH * CCH, CW), F32)]
               + _gla_scratch())
    return pl.pallas_call(
        functools.partial(_kf_body, nb=nb, nc=nc, add_pos=add_pos, final=final),
        grid=(bsz // nb, nc), in_specs=specs, out_specs=out_specs, out_shape=out_shape,
        scratch_shapes=scratch, compiler_params=_params(), name="fwd_sweep")(*ins)


def _run_kf_states(hbf, xc, v, weights, layer, h0, s0, nb):
    bsz, seq, _ = hbf.shape
    rs = T // nb
    nc = seq // rs
    cur, ahead, fixed = _step_maps(bsz // nb, nc, reverse=False)
    w_ins, w_specs = _weight_operands(weights, ("wf", "wg", "bg", "cl", "lb"), layer, FWD)
    ins = [hbf, hbf, xc, v] + w_ins + [h0, s0]
    specs = ([_seq_spec(nb, rs, D, fixed(0)), _seq_spec(nb, rs, D, ahead(1)),
              _seq_spec(nb, rs, AW, cur), _seq_spec(nb, rs, BW, cur)] + w_specs + _state_specs(nb))
    out_shape = (jax.ShapeDtypeStruct((bsz, 1, AW), F32), jax.ShapeDtypeStruct((bsz, BH, HP, HP), F32))
    scratch = ([pltpu.VMEM((T, BW), F32), pltpu.VMEM((T, D), BF16), pltpu.VMEM((T, 2 * AW), F32),
                pltpu.VMEM((T, BW), F32)]
               + _scan_scratch()
               + [pltpu.VMEM((SUBLANES, AW), F32), pltpu.VMEM((nb, BH, HP, HP), F32), pltpu.VMEM((T, AW), BF16)]
               + _gla_scratch())
    return pl.pallas_call(
        functools.partial(_kf_states_body, nb=nb, nc=nc),
        grid=(bsz // nb, nc), in_specs=specs, out_specs=tuple(_state_specs(nb)), out_shape=out_shape,
        scratch_shapes=scratch, compiler_params=_params(), name="fwd_states")(*ins)


def _pad_heads(w, fill=0.0):
    lead = w.shape[:-1]
    w = w.reshape(lead + (BH, BDK))
    w = jnp.pad(w, [(0, 0)] * len(lead) + [(0, 0), (0, HP - BDK)], constant_values=fill)
    return w.reshape(lead + (BW,))


NBW = BH * BDK
IN_B0 = 2 * AW
IN_C0 = IN_B0 + 5 * NBW
REPACK_ROWS = 256


def _in_proj_plan():
    plan = [("b", 0, AW, 0, 1.0), ("f", AW, AW, 0, 0.5),
            ("f", IN_C0, 2 * CW, AW + 2 * BW, 1.0), ("f", IN_C0 + 2 * CW, CW, AW + 2 * BW + 2 * CW, 0.5)]
    for g, (dst, slot, scale) in enumerate((("b", 0, 0.5), ("f", 0, 0.5), ("b", 1, 0.5), ("b", 2, 1.0), ("f", 1, 0.5))):
        for h in range(BH):
            plan.append((dst, IN_B0 + g * NBW + h * BDK, BDK, AW + slot * BW + h * HP, scale))
    return plan


def _repack_in_body(w_ref, wb_ref, wf_ref):
    out = {"b": wb_ref, "f": wf_ref}
    for dst, src0, width, dst0, scale in _in_proj_plan():
        chunk = w_ref[0, :, src0:src0 + width]
        out[dst][0, :, dst0:dst0 + width] = (chunk if scale == 1.0 else chunk * scale).astype(BF16)
        if width == BDK:
            out[dst][0, :, dst0 + BDK:dst0 + HP] = jnp.zeros((REPACK_ROWS, HP - BDK), BF16)


def _repack_out_body(w_ref, wo_ref):
    for h in range(BH):
        wo_ref[0, h * HP:h * HP + BDK, :] = w_ref[0, AW + h * BDK:AW + (h + 1) * BDK, :].astype(BF16)
        wo_ref[0, h * HP + BDK:(h + 1) * HP, :] = jnp.zeros((HP - BDK, D), BF16)
    wo_ref[0, BW:BW + AW, :] = w_ref[0, 0:AW, :].astype(BF16)
    wo_ref[0, BW + AW:YW, :] = w_ref[0, AW + NBW:, :].astype(BF16)


def _repack_projections(w_in, w_out):
    depth = w_in.shape[0]
    wb, wf = pl.pallas_call(
        _repack_in_body, grid=(depth, D // REPACK_ROWS),
        in_specs=[pl.BlockSpec((1, REPACK_ROWS, w_in.shape[2]), lambda l, i: (l, i, 0))],
        out_specs=(pl.BlockSpec((1, REPACK_ROWS, KB_COLS), lambda l, i: (l, i, 0)),
                   pl.BlockSpec((1, REPACK_ROWS, KF_COLS), lambda l, i: (l, i, 0))),
        out_shape=(jax.ShapeDtypeStruct((depth, D, KB_COLS), BF16), jax.ShapeDtypeStruct((depth, D, KF_COLS), BF16)),
        name="repack_in_proj")(w_in)
    wo = pl.pallas_call(
        _repack_out_body, grid=(depth,),
        in_specs=[pl.BlockSpec((1,) + w_out.shape[1:], lambda l: (l, 0, 0))],
        out_specs=pl.BlockSpec((1, YW, D), lambda l: (l, 0, 0)),
        out_shape=jax.ShapeDtypeStruct((depth, YW, D), BF16),
        name="repack_out_proj")(w_out)
    return wb, wf, wo


def _prepare_weights(norm_w, w_in, a_conv_w, a_conv_b, a_wr, a_br, a_wi, a_bi, a_lambda, lbs, b_norm_w,
                     c_norm_w, c_ws, c_bs, w_out):
    depth = w_in.shape[0]
    wb, wf, wo = _repack_projections(w_in, w_out)
    eye = jnp.eye(A_HEADS, dtype=F32)

    def block_diag(w):
        return jnp.einsum("ldhij,hg->ldhigj", w, eye).reshape(depth, 2, AW, AW)
    return {
        "nw": norm_w[:, None, :],
        "wb": wb, "wf": wf, "wo": wo,
        "cw": a_conv_w, "cb": a_conv_b[:, None, :],
        "lb": _pad_heads(lbs[:depth], fill=1.0)[:, None, :],
        "bnw": _pad_heads(b_norm_w)[:, None, :],
        "cnw": c_norm_w[:, None, :],
        "cws": c_ws.reshape(depth, CH * CCH, CCH).astype(BF16),
        "cbias": jnp.repeat(jnp.swapaxes(c_bs, 1, 2), CHD, axis=2),
        "wg": (0.5 * jnp.concatenate([block_diag(a_wr), block_diag(a_wi)], axis=-1)).astype(BF16),
        "bg": 0.5 * jnp.concatenate([a_br, a_bi], axis=-1)[:, :, None, :],
        "cl": (-0.5 * RG_C * jax.nn.softplus(-a_lambda))[:, :, None, :],
    }


FWD, BWD = 0, 1
PER_DIRECTION = ("wg", "bg", "cl")


def _weight_operands(weights, names, layer, direction):
    arrays, specs = [], []
    for n in names:
        w = weights[n]
        lead = (layer, direction) if n in PER_DIRECTION else (layer,)
        rest = w.shape[len(lead):]
        arrays.append(w)
        specs.append(pl.BlockSpec((None,) * len(lead) + rest,
                                  lambda ib, ic, lead=lead, rest=rest: lead + (0,) * len(rest),
                                  pipeline_mode=pl.Buffered(1)))
    return arrays, specs


def _pos_tables(rows):
    qd = D // 4
    omega = 1.0 / (10000.0 ** (jnp.arange(qd, dtype=F32) / qd))
    ar = jnp.arange(rows, dtype=F32)[:, None] * omega
    ac = jnp.arange(GRID_W, dtype=F32)[:, None] * omega
    rowtab = jnp.concatenate([jnp.sin(ar), jnp.cos(ar)], axis=-1)
    coltab = jnp.concatenate([jnp.sin(ac), jnp.cos(ac)], axis=-1)
    nc = rows * GRID_W // T
    hrow = jnp.broadcast_to(rowtab[jnp.maximum(jnp.arange(nc) * NSEG - 1, 0)][:, None, :], (nc, SUBLANES, D // 2))
    hcol = jnp.broadcast_to(coltab[GRID_W - SUBLANES:][None], (nc, SUBLANES, D // 2))
    return rowtab, coltab, jnp.concatenate([hrow, hcol], axis=-1)


def kernel(x, c, ctx, c_ctx, norm_w, w_mod, b_mod, w_in, a_conv_w, a_conv_b, a_wr, a_br, a_wi, a_bi, a_lambda,
           b_lb_logits, b_norm_w, c_norm_w, c_ws, c_bs, w_out, final_norm_w):
    bsz, n_lat, _ = x.shape
    depth = w_in.shape[0]
    assert n_lat % T == 0 and T % ctx.shape[1] == 0 and bsz % (T // ctx.shape[1]) == 0
    nb_ctx = T // ctx.shape[1]
    pos = _pos_tables(n_lat // GRID_W)

    mrows = 2 * SUBLANES
    cs = jnp.zeros((mrows, D), F32).at[:bsz].set(c).at[bsz].set(c_ctx)
    nblk = 3
    mod = pl.pallas_call(
        _mod_body, grid=(depth, nblk),
        in_specs=[pl.BlockSpec((mrows, D), lambda l, j: (0, 0)),
                  pl.BlockSpec((1, D, D), lambda l, j: (l, 0, j)),
                  pl.BlockSpec((1, 1, D), lambda l, j: (l, 0, j))],
        out_specs=pl.BlockSpec((1, mrows, D), lambda l, j: (l, 0, j)),
        out_shape=jax.ShapeDtypeStruct((depth, mrows, 3 * D), F32),
        name="modulation")(cs, w_mod, b_mod[:, None, :])
    mod = mod.reshape(depth, mrows, 1, 3 * D)

    lbs = jnp.cumsum(jax.nn.softmax(b_lb_logits, axis=0), axis=0)
    fnw = final_norm_w[None, :]
    h_zero = jnp.zeros((bsz, 1, AW), F32)
    s_zero = jnp.zeros((bsz, BH, HP, HP), F32)
    weights = _prepare_weights(norm_w, w_in, a_conv_w, a_conv_b, a_wr, a_br, a_wi, a_bi, a_lambda, lbs, b_norm_w,
                               c_norm_w, c_ws, c_bs, w_out)
    for l in range(depth):
        lpos = pos if l == 0 else None
        last = l == depth - 1
        kb = _run_kb(ctx, mod[l], bsz, weights, l, h_zero, s_zero, None, nb_ctx, want_out=not last)
        hb_c, sb_c = kb[-2], kb[-1]
        if last:
            hbf_c, xc_c, v_c = kb[:3]
            hf_c, sf_c = _run_kf_states(hbf_c, xc_c, v_c, weights, l, h_zero, s_zero, nb_ctx)
        else:
            ctx, hf_c, sf_c = _run_kf(ctx, kb[:6], mod[l], bsz, weights, l, fnw, h_zero, s_zero, None, nb_ctx, False)
        kb = _run_kb(x, mod[l], None, weights, l, hb_c, sb_c, lpos, 1)
        x, _, _ = _run_kf(x, kb[:6], mod[l], None, weights, l, fnw, hf_c, sf_c, lpos, 1, last)
    return x
```

```python
import functools

import jax
import jax.numpy as jnp
import numpy as np
from jax import lax
from jax.experimental import pallas as pl
from jax.experimental.pallas import tpu as pltpu

D = 1024
GRID_W = 64
EPS = 1e-6
RG_C = 8.0
AW = 384
A_HEADS = 8
A_HD = 48
BH = 4
BDK = 96
CW = 256
CH = 4
CHD = 64
CCH = 128

LANES = 128
SUBLANES = 8
HP = LANES
BW = BH * HP
YW = AW + BW + CW
NA = AW // LANES
T = 512
SEG = 64
NSEG = T // SEG
PITCH = SEG + SUBLANES
KB_COLS = AW + 3 * BW
KF_COLS = AW + 2 * BW + 3 * CW
FILL_COLS = 256
FILL_ROWS = T // 2
ROW_TILE = 16
MLP_TILE = 32
VMEM_LIMIT = 58 * 1024 * 1024

F32 = jnp.float32
BF16 = jnp.bfloat16

C_CONV_SEG, C_COEFF_TILE, C_SCAN_STEP, C_FIX_TILE, C_YA_TILE = 100, 50, 12, 8, 15
C_GATES_TILE, C_CUMSUM_STEP, C_GLA_A_TILE, C_GLA_C_TILE, C_SILU_TILE = 20, 8, 40, 15, 12
C_PRE_TILE, C_MLP_TILE, C_GLA_B_TILE, C_GLA_D_TILE = 25, 90, 40, 40
C_PROJ_PIECE, C_OUT_PIECE, C_NORM_PIECE = FILL_ROWS, 3 * FILL_ROWS // 4, 130
PACE = 1.4


def _silu_of_twice(hx):
    return hx * (jnp.tanh(hx) + 1.0)


def _silu(x):
    return _silu_of_twice(0.5 * x)


def _gelu(x):
    c = np.float32(np.sqrt(2.0 / np.pi))
    hx = 0.5 * x
    return hx * (jnp.tanh(x * (c + (0.044715 * c) * (x * x))) + 1.0)


def _nt(a, b):
    return lax.dot_general(a, b, (((1,), (1,)), ((), ())), preferred_element_type=F32)


def _tn(a, b):
    return lax.dot_general(a, b, (((0,), (0,)), ((), ())), preferred_element_type=F32)


def _norm_mod(x, nw, sh, sc):
    ms = jnp.mean(x * x, axis=-1, keepdims=True)
    return (x * lax.rsqrt(ms + EPS)) * (nw * (1.0 + sc)) + sh


def _pos_rows(rt_ref, ct_ref, r0):
    s, c0 = divmod(r0, SEG)
    return jnp.concatenate([jnp.broadcast_to(rt_ref[s:s + 1, :], (ROW_TILE, D // 2)),
                            ct_ref[c0:c0 + ROW_TILE, :]], axis=1)


class _Fill:
    def __init__(self, items=(), rate=1.0):
        self.items = list(items)
        self.rate = rate
        self.credit = 0.0
        self.spent = 0.0

    def __call__(self, credit):
        self.credit += credit
        while self.items and self.spent + 0.5 * self.items[0][0] <= self.credit * self.rate:
            cost, thunk = self.items.pop(0)
            self.spent += cost
            thunk()

    def extend(self, items):
        self.items.extend(items)

    def flush(self):
        while self.items:
            self.items.pop(0)[1]()


def _cost(items):
    return sum(c for c, _ in items)


def _emit(items, fill=None):
    for cost, thunk in items:
        thunk()
        if fill is not None:
            fill(cost)


def _paired(items, partner_items):
    partner = _Fill(partner_items, _cost(partner_items) / max(_cost(items), 1))
    _emit(items, partner)
    partner.flush()


class _Flat:
    def __init__(self, ref, width):
        self.ref = ref
        self.rs = ref.shape[1]
        self.width = width

    def _split(self, idx):
        rows, lanes = idx
        start, size = rows.start, rows.size
        q, off = divmod(start, self.rs)
        assert off + size <= self.rs
        return q, pl.ds(off, size), lanes

    def __getitem__(self, idx):
        q, r, lanes = self._split(idx)
        return self.ref[q, r, lanes]

    def __setitem__(self, idx, val):
        q, r, lanes = self._split(idx)
        self.ref[q, r, lanes] = val


def _tiles():
    return [(s, k) for s in range(NSEG) for k in range(BH)]


class _AScan:
    def __init__(self, a_st, u_st, h_st, p_st, hc_ref, nb, reverse):
        self.refs = (a_st, u_st, h_st, p_st)
        self.hc_ref, self.nb, self.reverse = hc_ref, nb, reverse
        self.hs = [jnp.zeros((NSEG, LANES), F32)] * NA
        self.ps = [jnp.ones((NSEG, LANES), F32)] * NA

    def coeff_items(self, g_scr, xc_ref, cl_ref):
        a_st, u_st = self.refs[:2]

        def item(s, k):
            def run():
                rows, lanes = pl.ds(s * SEG, SEG), slice(k * LANES, (k + 1) * LANES)
                tr = jnp.tanh(g_scr[rows, lanes])
                ti = jnp.tanh(g_scr[rows, AW + k * LANES:AW + (k + 1) * LANES])
                hcl = cl_ref[:, lanes]
                la = hcl * tr + hcl
                th = jnp.tanh(la)
                om4 = (-0.5 * th) / (1.0 - th)
                a_st[k, pl.ds(s * PITCH, SEG), :] = jnp.exp(la)
                root = jnp.where(om4 > 0.0, om4 * lax.rsqrt(om4), 0.0)
                u_st[k, pl.ds(s * PITCH, SEG), :] = root * ((ti + 1.0) * xc_ref[rows, lanes])
            return C_COEFF_TILE, run
        return [item(s, k) for s in range(NSEG) for k in range(NA)]

    def step_items(self):
        a_st, u_st, h_st, p_st = self.refs

        def item(j):
            def run():
                for k in range(NA):
                    a = a_st[k, pl.ds(j, NSEG, stride=PITCH), :]
                    u = u_st[k, pl.ds(j, NSEG, stride=PITCH), :]
                    self.hs[k] = a * self.hs[k] + u
                    self.ps[k] = a * self.ps[k]
                    h_st[k, pl.ds(j, NSEG, stride=PITCH), :] = self.hs[k]
                    p_st[k, pl.ds(j, NSEG, stride=PITCH), :] = self.ps[k]
            return C_SCAN_STEP, run
        order = range(SEG - 1, -1, -1) if self.reverse else range(SEG)
        return [item(j) for j in order]

    def chain(self):
        spq = NSEG // self.nb
        cin = [None] * NSEG
        for q in range(self.nb):
            c = [self.hc_ref[q:q + 1, k * LANES:(k + 1) * LANES] for k in range(NA)]
            order = range((q + 1) * spq - 1, q * spq - 1, -1) if self.reverse else range(q * spq, (q + 1) * spq)
            for s in order:
                cin[s] = c
                c = [self.ps[k][s:s + 1, :] * c[k] + self.hs[k][s:s + 1, :] for k in range(NA)]
            for k in range(NA):
                self.hc_ref[q:q + 1, k * LANES:(k + 1) * LANES] = c[k]
        return cin

    def full(self, cin, s, k):
        _, _, h_st, p_st = self.refs
        st_rows = pl.ds(s * PITCH, SEG)
        return h_st[k, st_rows, :] + p_st[k, st_rows, :] * cin[s][k]


class _Gla:
    def __init__(self, reverse, nb, k_scr, lf_st, g_st, s_scr, stage, states_only=False):
        self.reverse, self.nb, self.states_only = reverse, nb, states_only
        self.k_scr, self.lf_st, self.g_st, self.s_scr = k_scr, lf_st, g_st, s_scr
        self.kg_s, self.ke_s, self.lhs_s, self.kv_s, self.rhs_s = stage
        self.gs = [jnp.zeros((NSEG, LANES), F32)] * BH
        order = range(NSEG - 1, -1, -1) if reverse else range(NSEG)
        self.order = list(order)
        self.tiles = [(n, k) for n in order for k in range(BH)]

    def gate_items(self, p_scr, col0, lb_ref):
        def item(s, k):
            def run():
                rows, lanes = pl.ds(s * SEG, SEG), slice(k * HP, (k + 1) * HP)
                lb = lb_ref[:, lanes]
                c0, c1 = 0.5 + 0.5 * lb, 0.5 - 0.5 * lb
                ct = c1 * jnp.tanh(p_scr[rows, col0 + k * HP:col0 + (k + 1) * HP])
                self.k_scr[rows, lanes] = c1 - ct
                self.lf_st[k, pl.ds(s * PITCH, SEG), :] = jnp.log2(c0 + ct)
            return C_GATES_TILE, run
        return [item(s, k) for s, k in _tiles()]

    def cumsum_items(self):
        def item(j):
            def run():
                for k in range(BH):
                    self.gs[k] = self.gs[k] + self.lf_st[k, pl.ds(j, NSEG, stride=PITCH), :]
                    self.g_st[k, pl.ds(j, NSEG, stride=PITCH), :] = self.gs[k]
            return C_CUMSUM_STEP, run
        order = range(SEG - 1, -1, -1) if self.reverse else range(SEG)
        return [item(j) for j in order]

    def a_items(self, load_q, load_v):
        def item(n, k):
            def run():
                rows, lanes = pl.ds(n * SEG, SEG), slice(k * HP, (k + 1) * HP)
                g = self.g_st[k, pl.ds(n * PITCH, SEG), :]
                tot = self.gs[k][n:n + 1, :]
                kk = self.k_scr[rows, lanes]
                if not self.states_only:
                    self.lhs_s[k, rows, 0:HP] = (load_q(n, k) * jnp.exp2(g)).astype(BF16)
                    self.rhs_s[n, k, HP:HP + SEG, :] = load_v(n, k)
                    self.kg_s[rows, lanes] = (kk * jnp.exp2(-g)).astype(BF16)
                self.ke_s[rows, lanes] = (kk * jnp.exp2(tot - g)).astype(BF16)
            return C_GLA_A_TILE, run
        return [item(n, k) for n, k in self.tiles]

    def b_items(self, load_v):
        r_i = lax.broadcasted_iota(jnp.int32, (SEG, SEG), 0)
        c_i = lax.broadcasted_iota(jnp.int32, (SEG, SEG), 1)
        keep = (c_i >= r_i) if self.reverse else (c_i <= r_i)

        def item(n, k):
            def run():
                rows, lanes = pl.ds(n * SEG, SEG), slice(k * HP, (k + 1) * HP)
                if not self.states_only:
                    self.lhs_s[k, rows, HP:HP + SEG] = jnp.where(
                        keep, _nt(self.lhs_s[k, rows, 0:HP], self.kg_s[rows, lanes]), 0.0).astype(BF16)
                self.kv_s[n, k] = _tn(load_v(n, k), self.ke_s[rows, lanes])
            return C_GLA_B_TILE, run
        return [item(n, k) for n, k in self.tiles]

    def c_items(self):
        cps = NSEG // self.nb
        state = {}

        def item(n, k):
            def run():
                sq = n // cps
                st = state.get(k)
                if st is None:
                    st = self.s_scr[sq, k]
                if not self.states_only:
                    self.rhs_s[n, k, 0:HP, :] = st.T.astype(BF16)
                st = jnp.exp2(self.gs[k][n:n + 1, :]) * st + self.kv_s[n, k]
                last = (n % cps == 0) if self.reverse else (n % cps == cps - 1)
                if last:
                    self.s_scr[sq, k] = st
                    st = None
                state[k] = st
            return C_GLA_C_TILE, run
        return [item(n, k) for n in self.order for k in range(BH)]

    def d_items(self, emit_out):
        def item(n, k):
            def run():
                rows, lanes = pl.ds(n * SEG, SEG), slice(k * HP, (k + 1) * HP)
                o = jnp.dot(self.lhs_s[k, rows, :], self.rhs_s[n, k], preferred_element_type=F32)
                emit_out(n, k, o)
            return C_GLA_D_TILE, run
        return [item(n, k) for n, k in self.tiles]


def _gla_scratch():
    return ([pltpu.VMEM((T, BW), BF16)] * 2
            + [pltpu.VMEM((BH, T, HP + SEG), BF16), pltpu.VMEM((NSEG, BH, HP, HP), F32),
               pltpu.VMEM((NSEG, BH, HP + SEG, HP), BF16)])


def _dot_items(lhs_rows, w_ref, dst, col_lo, col_hi, cost):
    def piece(r0, c0, c1):
        def run():
            dst[r0:r0 + FILL_ROWS, c0:c1] = jnp.dot(lhs_rows(r0, r0 + FILL_ROWS), w_ref[:, c0:c1],
                                                    preferred_element_type=F32)
        return cost, run
    return [piece(r0, c0, min(c0 + FILL_COLS, col_hi))
            for c0 in range(col_lo, col_hi, FILL_COLS) for r0 in range(0, T, FILL_ROWS)]


def _kb_body(*refs, nb, nc, add_pos, want_out):
    it = iter(refs)
    x0_ref, xn_ref = next(it), next(it)
    xh_ref = next(it) if nc > 1 else None
    if add_pos:
        rt0_ref, rtn_ref, ct_ref, hp_ref = next(it), next(it), next(it), next(it)
    else:
        rt0_ref = rtn_ref = ct_ref = hp_ref = None
    (mod_ref, modn_ref, nw_ref, wb_ref, cw_ref, cb_ref, wg_ref, bg_ref, cl_ref, lb_ref, h0_ref, s0_ref,
     hbf_ref, xc_ref) = (next(it) for _ in range(14))
    hb_ref, ob_ref, q_ref = (next(it), next(it), next(it)) if want_out else (None, None, None)
    (v_ref, hfin_ref, sfin_ref,
     p_scr, h_scr, xa_ext, g_scr, k_scr, a_st, u_st, h_st, p_st, lf_st, g_st, hc_ref, s_scr, xaf,
     *gla_stage) = it

    ic = pl.program_id(1)
    flat = pl.program_id(0) * nc + ic
    cc = nc - 1 - ic

    @pl.when(ic == 0)
    def _():
        for q in range(nb):
            hc_ref[q:q + 1, :] = h0_ref[q]
        s_scr[...] = s0_ref[...]
        xaf[...] = jnp.zeros_like(xaf)

    def norm_items(x_r, rt_r, m_r):
        x_flat = _Flat(x_r, D)

        def item(s):
            def run():
                rows = pl.ds(s * SEG, SEG)
                xin = x_flat[rows, slice(None)]
                if add_pos:
                    xin = xin + jnp.concatenate(
                        [_pos_rows(rt_r, ct_ref, r0) for r0 in range(s * SEG, (s + 1) * SEG, ROW_TILE)], axis=0)
                h_scr[rows, :] = _norm_mod(xin, nw_ref[...], m_r[0, :, 0:D], m_r[0, :, D:2 * D]).astype(BF16)
            return C_NORM_PIECE, run
        return [item(s) for s in range(NSEG)]

    def proj_items():
        return _dot_items(lambda r0, r1: h_scr[r0:r1, :], wb_ref, p_scr, 0, KB_COLS, C_PROJ_PIECE)

    gla = _Gla(True, nb, k_scr, lf_st, g_st, s_scr, gla_stage, states_only=not want_out)
    v_flat = _Flat(v_ref, BW)
    if want_out:
        q_flat, ob_flat = _Flat(q_ref, BW), _Flat(ob_ref, BW)

    def hgrn_input_items():
        def qv_item(s, k):
            def run():
                rows, lanes = pl.ds(s * SEG, SEG), slice(k * HP, (k + 1) * HP)
                if want_out:
                    q_flat[rows, lanes] = _silu_of_twice(p_scr[rows, AW + k * HP:AW + (k + 1) * HP])
                v_flat[rows, lanes] = p_scr[rows, AW + 2 * BW + k * HP:AW + 2 * BW + (k + 1) * HP].astype(BF16)
            return C_PRE_TILE, run
        return [qv_item(s, k) for s, k in _tiles()] + gla.gate_items(p_scr, AW + BW, lb_ref)

    @pl.when(flat == 0)
    def _():
        _emit(norm_items(x0_ref, rt0_ref, mod_ref) + proj_items())

    def step():
        xc_flat = _Flat(xc_ref, AW)
        hb_flat = _Flat(hb_ref, AW) if want_out else None
        scan = _AScan(a_st, u_st, h_st, p_st, hc_ref, nb, reverse=True)

        hbf_ref[...] = h_scr[...].reshape(hbf_ref.shape)
        if nc > 1:
            xh = xh_ref[0]
            if add_pos:
                xh = xh + hp_ref[0]
            hh = _norm_mod(xh, nw_ref[...], mod_ref[0, :, 0:D], mod_ref[0, :, D:2 * D]).astype(BF16)
            left = jnp.dot(hh, wb_ref[:, 0:AW], preferred_element_type=F32)
            xa_ext[0:SUBLANES, :] = jnp.where(cc > 0, left, 0.0)
        else:
            xa_ext[0:SUBLANES, :] = jnp.zeros((SUBLANES, AW), F32)
        xa_ext[pl.ds(SUBLANES, T), :] = p_scr[:, 0:AW]
        xa_ext[pl.ds(SUBLANES + T, SUBLANES), :] = xaf[...]
        xaf[...] = p_scr[0:SUBLANES, 0:AW]

        def conv_item(s):
            def run():
                spq = NSEG // nb
                row = lax.broadcasted_iota(jnp.int32, (SEG, 1), 0)
                acc = jnp.broadcast_to(cb_ref[...], (SEG, AW))
                for j in range(4):
                    tap = xa_ext[pl.ds(s * SEG + SUBLANES - 2 + j, SEG), :]
                    if nb > 1 and s % spq == 0 and j < 2:
                        tap = jnp.where(row + (j - 2) >= 0, tap, 0.0)
                    if nb > 1 and s % spq == spq - 1 and j == 3:
                        tap = jnp.where(row + 1 < SEG, tap, 0.0)
                    acc = acc + cw_ref[j:j + 1, :] * tap
                xc_flat[pl.ds(s * SEG, SEG), slice(None)] = acc
            return C_CONV_SEG, run

        def load_q(n, k):
            return q_flat[pl.ds(n * SEG, SEG), slice(k * HP, (k + 1) * HP)]

        def load_v(n, k):
            return v_flat[pl.ds(n * SEG, SEG), slice(k * HP, (k + 1) * HP)]

        def emit_out(n, k, o):
            ob_flat[pl.ds(n * SEG, SEG), slice(k * HP, (k + 1) * HP)] = o

        _emit(hgrn_input_items())
        _emit(norm_items(xn_ref, rtn_ref, modn_ref))

        conv = [conv_item(s) for s in range(NSEG)]
        coeff, steps, cums = scan.coeff_items(g_scr, xc_flat, cl_ref), scan.step_items(), gla.cumsum_items()
        proj = proj_items()
        credit = (_cost(conv) + _cost(coeff) + _cost(steps) + _cost(cums)
                  + BH * NSEG * (C_GLA_A_TILE + C_GLA_C_TILE) + NA * NSEG * C_FIX_TILE)
        mx = _Fill(proj, PACE * _cost(proj) / credit)
        _emit(conv, mx)
        g_scr[...] = jnp.dot(xc_ref[...].reshape(T, AW).astype(BF16), wg_ref[...],
                             preferred_element_type=F32) + bg_ref[...]
        _emit(coeff, mx)
        _emit(steps, mx)
        cin = scan.chain()
        for q in range(nb):
            hfin_ref[q] = hc_ref[q:q + 1, :]

        def fix_item(s, k):
            def run():
                hb_flat[pl.ds(s * SEG, SEG), k * LANES:(k + 1) * LANES] = scan.full(cin, s, k)
            return C_FIX_TILE, run
        if want_out:
            _emit([fix_item(s, k) for s in range(NSEG) for k in range(NA)], mx)

        _emit(cums, mx)
        _emit(gla.a_items(load_q, load_v), mx)
        _emit(gla.b_items(load_v))
        _emit(gla.c_items(), mx)
        if want_out:
            _emit(gla.d_items(emit_out))
        sfin_ref[...] = s_scr[...]
        mx.flush()

    step()


def _kf_body(*refs, nb, nc, add_pos, final):
    it = iter(refs)
    x_ref = next(it)
    if add_pos:
        rt_ref, ct_ref = next(it), next(it)
    (hbf0_ref, hbfn_ref, xc_ref, hb_ref, ob_ref, q_ref, v_ref, mod_ref, wf_ref, wg_ref, bg_ref, cl_ref, lb_ref,
     bnw_ref, cnw_ref, cws_ref, cbias_ref, wo_ref, fnw_ref, h0_ref, s0_ref,
     xo_ref, hfin_ref, sfin_ref,
     p_scr, sg_scr, hn_scr, g_scr, k_scr, y_scr, part_scr, a_st, u_st, h_st, p_st, lf_st, g_st, hc_ref, s_scr,
     xcb_scr, vn_scr, mix_scr, *gla_stage) = it

    ic = pl.program_id(1)
    flat = pl.program_id(0) * nc + ic

    @pl.when(ic == 0)
    def _():
        for q in range(nb):
            hc_ref[q:q + 1, :] = h0_ref[q]
        s_scr[...] = s0_ref[...]

    o_ag, o_ff, o_bg = 0, AW, AW + BW
    o_cu, o_cv, o_cg = AW + 2 * BW, AW + 2 * BW + CW, AW + 2 * BW + 2 * CW

    def proj_items(col_lo, col_hi):
        return _dot_items(lambda r0, r1: hn_scr[r0:r1, :], wf_ref, p_scr, col_lo, col_hi, C_PROJ_PIECE)

    @pl.when(flat == 0)
    def _():
        hn_scr[...] = hbf0_ref[...].reshape(T, D)
        _emit(proj_items(0, KF_COLS))

    def step():
        xc_flat, hb_flat = _Flat(xc_ref, AW), _Flat(hb_ref, AW)
        q_flat, v_flat, ob_flat = _Flat(q_ref, BW), _Flat(v_ref, BW), _Flat(ob_ref, BW)
        scan = _AScan(a_st, u_st, h_st, p_st, hc_ref, nb, reverse=False)
        gla = _Gla(False, nb, k_scr, lf_st, g_st, s_scr, gla_stage)

        hn_scr[...] = hbfn_ref[...].reshape(T, D)
        for s in range(NSEG):
            rows = pl.ds(s * SEG, SEG)
            xcb_scr[rows, :] = xc_flat[rows, slice(None)].astype(BF16)

        def silu_item(s, c0):
            def run():
                rows = pl.ds(s * SEG, SEG)
                dst = c0 - o_ag if c0 < o_ff else AW + c0 - o_bg
                sg_scr[rows, dst:dst + LANES] = _silu_of_twice(p_scr[rows, c0:c0 + LANES])
            return C_SILU_TILE, run
        gate_cols = list(range(o_ag, o_ag + AW, LANES)) + list(range(o_bg, o_bg + BW, LANES))

        def mlp_items(m):
            lane = lax.broadcasted_iota(jnp.int32, (MLP_TILE, CW), 1)

            def norm_item(r0):
                def run():
                    v = _gelu(p_scr[pl.ds(m * CCH + r0, MLP_TILE), o_cv:o_cv + CW])
                    mu = jnp.mean(v, axis=-1, keepdims=True)
                    dv = v - mu
                    var = jnp.mean(dv * dv, axis=-1, keepdims=True)
                    vn_scr[r0:r0 + MLP_TILE, :] = (dv * lax.rsqrt(var + EPS) * cnw_ref[...]).astype(BF16)
                return C_MLP_TILE, run

            def mix():
                mix_scr[...] = jnp.dot(cws_ref[...], vn_scr[...], preferred_element_type=F32)

            def out_item(r0):
                def run():
                    rows = pl.ds(m * CCH + r0, MLP_TILE)
                    mixed = mix_scr[r0:r0 + MLP_TILE, :]
                    for h in range(1, CH):
                        mixed = jnp.where(lane >= h * CHD, mix_scr[h * CCH + r0:h * CCH + r0 + MLP_TILE, :], mixed)
                    u = _gelu(p_scr[rows, o_cu:o_cu + CW])
                    yc = (u * (mixed + cbias_ref[r0:r0 + MLP_TILE, :])
                          * _silu_of_twice(p_scr[rows, o_cg:o_cg + CW]))
                    y_scr[rows, BW + AW:YW] = yc.astype(BF16)
                return C_MLP_TILE, run
            tiles = range(0, CCH, MLP_TILE)
            return [norm_item(r0) for r0 in tiles] + [(0, mix)] + [out_item(r0) for r0 in tiles]

        def load_q(n, k):
            return q_flat[pl.ds(n * SEG, SEG), slice(k * HP, (k + 1) * HP)]

        def load_v(n, k):
            return v_flat[pl.ds(n * SEG, SEG), slice(k * HP, (k + 1) * HP)]

        def emit_out(n, k, o):
            rows, lanes = pl.ds(n * SEG, SEG), slice(k * HP, (k + 1) * HP)
            o = o + ob_flat[rows, lanes]
            ms = jnp.sum(o * o, axis=-1, keepdims=True) * (1.0 / BDK)
            yb = (o * lax.rsqrt(ms + EPS) * bnw_ref[:, lanes]) * sg_scr[rows, AW + k * HP:AW + (k + 1) * HP]
            y_scr[rows, lanes] = yb.astype(BF16)

        g_scr[...] = jnp.dot(xcb_scr[...], wg_ref[...], preferred_element_type=F32) + bg_ref[...]
        _emit([it for m in range(T // CCH) for it in mlp_items(m)])

        pre = [silu_item(s, c0) for s in range(NSEG) for c0 in gate_cols] + gla.gate_items(p_scr, o_ff, lb_ref)
        coeff, steps, cums = scan.coeff_items(g_scr, xc_flat, cl_ref), scan.step_items(), gla.cumsum_items()
        n_proj = -(-KF_COLS // FILL_COLS) * (T // FILL_ROWS)
        n_out = (D // FILL_COLS) * (T // FILL_ROWS)
        credit = (_cost(pre) + _cost(coeff) + _cost(steps) + _cost(cums)
                  + BH * NSEG * (C_GLA_A_TILE + C_GLA_C_TILE + C_GLA_D_TILE) + NA * NSEG * C_YA_TILE)
        mx = _Fill(proj_items(o_cu, KF_COLS), PACE * (n_proj * C_PROJ_PIECE + n_out * C_OUT_PIECE) / credit)
        _emit(pre, mx)
        mx.extend(proj_items(0, o_cu))

        _emit(coeff, mx)
        _emit(steps, mx)
        cin = scan.chain()
        for q in range(nb):
            hfin_ref[q] = hc_ref[q:q + 1, :]

        def ya_item(s, k):
            def run():
                rows, lanes = pl.ds(s * SEG, SEG), slice(k * LANES, (k + 1) * LANES)
                ya = (scan.full(cin, s, k) + hb_flat[rows, lanes]) * sg_scr[rows, lanes]
                y_scr[rows, BW + k * LANES:BW + (k + 1) * LANES] = ya.astype(BF16)
            return C_YA_TILE, run
        _emit([ya_item(s, k) for s in range(NSEG) for k in range(NA)], mx)
        mx.extend(_dot_items(lambda r0, r1: y_scr[r0:r1, BW:YW], wo_ref.at[BW:YW, :], part_scr, 0, D, C_OUT_PIECE))

        _emit(cums, mx)
        _emit(gla.a_items(load_q, load_v), mx)
        _emit(gla.b_items(load_v))
        _emit(gla.c_items(), mx)
        _emit(gla.d_items(emit_out), mx)
        sfin_ref[...] = s_scr[...]
        mx.flush()

        gm = mod_ref[0, :, 2 * D:3 * D]
        x_flat, xo_flat = _Flat(x_ref, D), _Flat(xo_ref, D)
        def out_item(r0, c0):
            def run():
                rows, cols = pl.ds(r0, FILL_ROWS), slice(c0, c0 + FILL_COLS)
                part_scr[rows, cols] = part_scr[rows, cols] + jnp.dot(y_scr[rows, 0:BW], wo_ref[0:BW, cols],
                                                                      preferred_element_type=F32)
            return C_OUT_PIECE, run

        def residual_item(r0):
            def run():
                rows = pl.ds(r0, ROW_TILE)
                xin = x_flat[rows, slice(None)]
                if add_pos:
                    xin = xin + _pos_rows(rt_ref, ct_ref, r0)
                xn = xin + gm * part_scr[rows, :]
                if final:
                    ms = jnp.mean(xn * xn, axis=-1, keepdims=True)
                    xn = xn * lax.rsqrt(ms + EPS) * fnw_ref[...]
                xo_flat[rows, slice(None)] = xn
            return C_OUT_PIECE * (D // FILL_COLS) * ROW_TILE // FILL_ROWS, run

        pending = []
        for r0 in range(0, T, FILL_ROWS):
            _paired([out_item(r0, c0) for c0 in range(0, D, FILL_COLS)], pending)
            pending = [residual_item(r) for r in range(r0, r0 + FILL_ROWS, ROW_TILE)]
        _emit(pending)

    step()


def _kf_states_body(hbf0_ref, hbfn_ref, xc_ref, v_ref, wf_ref, wg_ref, bg_ref, cl_ref, lb_ref, h0_ref, s0_ref,
                    hfin_ref, sfin_ref,
                    p_scr, hn_scr, g_scr, k_scr, a_st, u_st, h_st, p_st, lf_st, g_st, hc_ref, s_scr, xcb_scr,
                    *gla_stage, nb, nc):
    ic = pl.program_id(1)
    flat = pl.program_id(0) * nc + ic

    @pl.when(ic == 0)
    def _():
        for q in range(nb):
            hc_ref[q:q + 1, :] = h0_ref[q]
        s_scr[...] = s0_ref[...]

    def proj_items():
        return _dot_items(lambda r0, r1: hn_scr[r0:r1, :], wf_ref.at[:, AW:AW + BW], p_scr, 0, BW, C_PROJ_PIECE)

    @pl.when(flat == 0)
    def _():
        hn_scr[...] = hbf0_ref[...].reshape(T, D)
        _emit(proj_items())

    xc_flat, v_flat = _Flat(xc_ref, AW), _Flat(v_ref, BW)
    scan = _AScan(a_st, u_st, h_st, p_st, hc_ref, nb, reverse=False)
    gla = _Gla(False, nb, k_scr, lf_st, g_st, s_scr, gla_stage, states_only=True)

    hn_scr[...] = hbfn_ref[...].reshape(T, D)
    for s in range(NSEG):
        rows = pl.ds(s * SEG, SEG)
        xcb_scr[rows, :] = xc_flat[rows, slice(None)].astype(BF16)
    g_scr[...] = jnp.dot(xcb_scr[...], wg_ref[...], preferred_element_type=F32) + bg_ref[...]
    _emit(gla.gate_items(p_scr, 0, lb_ref))

    coeff, steps, cums = scan.coeff_items(g_scr, xc_flat, cl_ref), scan.step_items(), gla.cumsum_items()
    proj = proj_items()
    mx = _Fill(proj, _cost(proj) / (_cost(coeff) + _cost(steps) + _cost(cums)))
    _emit(coeff, mx)
    _emit(steps, mx)
    scan.chain()
    for q in range(nb):
        hfin_ref[q] = hc_ref[q:q + 1, :]
    _emit(cums, mx)

    def load_v(n, k):
        return v_flat[pl.ds(n * SEG, SEG), slice(k * HP, (k + 1) * HP)]
    _emit(gla.a_items(None, load_v))
    _emit(gla.b_items(load_v))
    _emit(gla.c_items())
    sfin_ref[...] = s_scr[...]
    mx.flush()


def _mod_body(cs_ref, w_ref, b_ref, o_ref):
    s = _silu(cs_ref[...])
    o_ref[0] = jnp.dot(s, w_ref[0], preferred_element_type=F32, precision=lax.Precision.HIGHEST) + b_ref[0]


def _const_spec(shape):
    nd = len(shape)
    return pl.BlockSpec(shape, lambda ib, ic: (0,) * nd, pipeline_mode=pl.Buffered(1))


def _seq_spec(nb, rs, width, bc):
    return pl.BlockSpec((nb, rs, width), lambda ib, ic: bc(ib, ic) + (0,))


def _step_maps(ngroups, nc, reverse):
    last = ngroups * nc - 1

    def at(f):
        return f // nc, (nc - 1 - f % nc) if reverse else f % nc

    def cur(ib, ic):
        return at(ib * nc + ic)

    def ahead(d):
        return lambda ib, ic: at(jnp.minimum(ib * nc + ic + d, last))

    def fixed(f):
        return lambda ib, ic: at(min(f, last))
    return cur, ahead, fixed


def _scan_scratch():
    return ([pltpu.VMEM((NA, NSEG * PITCH, LANES), F32)] * 4
            + [pltpu.VMEM((BH, NSEG * PITCH, LANES), F32)] * 2)


def _params():
    return pltpu.CompilerParams(dimension_semantics=("arbitrary", "arbitrary"), vmem_limit_bytes=VMEM_LIMIT)


def _state_specs(nb):
    return [pl.BlockSpec((nb, 1, AW), lambda ib, ic: (ib, 0, 0)),
            pl.BlockSpec((nb, BH, HP, HP), lambda ib, ic: (ib, 0, 0, 0))]


def _run_kb(x, mod, mod_row, weights, layer, h0, s0, pos, nb, want_out=True):
    bsz, seq, _ = x.shape
    rs = T // nb
    nc = seq // rs
    add_pos = pos is not None
    cur, ahead, fixed = _step_maps(bsz // nb, nc, reverse=True)
    steps2 = (fixed(0), ahead(1))
    mrow = (lambda g: g) if mod_row is None else (lambda g: mod_row)
    mod_spec = lambda bc: pl.BlockSpec((1, 1, 3 * D), lambda ib, ic: (mrow(bc(ib, ic)[0]), 0, 0))
    ins, specs = [x] * 2, [_seq_spec(nb, rs, D, bc) for bc in steps2]
    if nc > 1:
        ins.append(x)
        specs.append(pl.BlockSpec(
            (1, SUBLANES, D), lambda ib, ic: (ib, jnp.maximum(cur(ib, ic)[1] * (rs // SUBLANES) - 1, 0), 0)))
    if add_pos:
        rowtab, coltab, hpos = pos
        row_spec = lambda bc: pl.BlockSpec((NSEG, D // 2), lambda ib, ic: (bc(ib, ic)[1], 0))
        ins += [rowtab] * 2 + [coltab, hpos]
        specs += [row_spec(bc) for bc in steps2] + [
            _const_spec(coltab.shape), pl.BlockSpec((1, SUBLANES, D), lambda ib, ic: (cur(ib, ic)[1], 0, 0))]
    w_ins, w_specs = _weight_operands(weights, ("nw", "wb", "cw", "cb", "wg", "bg", "cl", "lb"), layer, BWD)
    ins += [mod] * 2 + w_ins + [h0, s0]
    specs += [mod_spec(cur), mod_spec(ahead(1))] + w_specs + _state_specs(nb)
    seq_outs = [(D, BF16), (AW, F32)] + ([(AW, F32), (BW, F32), (BW, F32)] if want_out else []) + [(BW, BF16)]
    out_shape = tuple([jax.ShapeDtypeStruct((bsz, seq, w), dt) for w, dt in seq_outs]
                      + [jax.ShapeDtypeStruct((bsz, 1, AW), F32), jax.ShapeDtypeStruct((bsz, BH, HP, HP), F32)])
    out_specs = tuple([_seq_spec(nb, rs, w, cur) for w, _ in seq_outs] + _state_specs(nb))
    scratch = ([pltpu.VMEM((T, KB_COLS), F32), pltpu.VMEM((T, D), BF16)]
               + [pltpu.VMEM((T + 2 * SUBLANES, AW), F32), pltpu.VMEM((T, 2 * AW), F32), pltpu.VMEM((T, BW), F32)]
               + _scan_scratch()
               + [pltpu.VMEM((SUBLANES, AW), F32), pltpu.VMEM((nb, BH, HP, HP), F32),
                  pltpu.VMEM((SUBLANES, AW), F32)]
               + _gla_scratch())
    return pl.pallas_call(
        functools.partial(_kb_body, nb=nb, nc=nc, add_pos=add_pos, want_out=want_out),
        grid=(bsz // nb, nc), in_specs=specs, out_specs=out_specs, out_shape=out_shape,
        scratch_shapes=scratch, compiler_params=_params(), name="bwd_sweep")(*ins)


def _run_kf(x, kb_out, mod, mod_row, weights, layer, fnw, h0, s0, pos, nb, final):
    bsz, seq, _ = x.shape
    rs = T // nb
    nc = seq // rs
    add_pos = pos is not None
    cur, ahead, fixed = _step_maps(bsz // nb, nc, reverse=False)
    mrow = (lambda g: g) if mod_row is None else (lambda g: mod_row)
    hbf, xc, hb, ob, q, v = kb_out
    ins, specs = [x], [_seq_spec(nb, rs, D, cur)]
    if add_pos:
        rowtab, coltab, _ = pos
        ins += [rowtab, coltab]
        specs += [pl.BlockSpec((NSEG, D // 2), lambda ib, ic: (cur(ib, ic)[1], 0)), _const_spec(coltab.shape)]
    w_ins, w_specs = _weight_operands(
        weights, ("wf", "wg", "bg", "cl", "lb", "bnw", "cnw", "cws", "cbias", "wo"), layer, FWD)
    ins += [hbf, hbf, xc, hb, ob, q, v, mod] + w_ins + [fnw, h0, s0]
    specs += [_seq_spec(nb, rs, D, fixed(0)), _seq_spec(nb, rs, D, ahead(1))]
    specs += [_seq_spec(nb, rs, w, cur) for w in (AW, AW, BW, BW, BW)]
    specs += [pl.BlockSpec((1, 1, 3 * D), lambda ib, ic: (mrow(ib), 0, 0))]
    specs += w_specs + [_const_spec(fnw.shape)] + _state_specs(nb)
    out_shape = (jax.ShapeDtypeStruct((bsz, seq, D), F32),
                 jax.ShapeDtypeStruct((bsz, 1, AW), F32), jax.ShapeDtypeStruct((bsz, BH, HP, HP), F32))
    out_specs = tuple([_seq_spec(nb, rs, D, cur)] + _state_specs(nb))
    scratch = ([pltpu.VMEM((T, KF_COLS), F32), pltpu.VMEM((T, AW + BW), F32), pltpu.VMEM((T, D), BF16)]
               + [pltpu.VMEM((T, 2 * AW), F32), pltpu.VMEM((T, BW), F32), pltpu.VMEM((T, YW), BF16),
                  pltpu.VMEM((T, D), F32)]
               + _scan_scratch()
               + [pltpu.VMEM((SUBLANES, AW), F32), pltpu.VMEM((nb, BH, HP, HP), F32),
                  pltpu.VMEM((T, AW), BF16), pltpu.VMEM((CCH, CW), BF16), pltpu.VMEM((CH * CCH, CW), F32)]
               + _gla_scratch())
    return pl.pallas_call(
        functools.partial(_kf_body, nb=nb, nc=nc, add_pos=add_pos, final=final),
        grid=(bsz // nb, nc), in_specs=specs, out_specs=out_specs, out_shape=out_shape,
        scratch_shapes=scratch, compiler_params=_params(), name="fwd_sweep")(*ins)


def _run_kf_states(hbf, xc, v, weights, layer, h0, s0, nb):
    bsz, seq, _ = hbf.shape
    rs = T // nb
    nc = seq // rs
    cur, ahead, fixed = _step_maps(bsz // nb, nc, reverse=False)
    w_ins, w_specs = _weight_operands(weights, ("wf", "wg", "bg", "cl", "lb"), layer, FWD)
    ins = [hbf, hbf, xc, v] + w_ins + [h0, s0]
    specs = ([_seq_spec(nb, rs, D, fixed(0)), _seq_spec(nb, rs, D, ahead(1)),
              _seq_spec(nb, rs, AW, cur), _seq_spec(nb, rs, BW, cur)] + w_specs + _state_specs(nb))
    out_shape = (jax.ShapeDtypeStruct((bsz, 1, AW), F32), jax.ShapeDtypeStruct((bsz, BH, HP, HP), F32))
    scratch = ([pltpu.VMEM((T, BW), F32), pltpu.VMEM((T, D), BF16), pltpu.VMEM((T, 2 * AW), F32),
                pltpu.VMEM((T, BW), F32)]
               + _scan_scratch()
               + [pltpu.VMEM((SUBLANES, AW), F32), pltpu.VMEM((nb, BH, HP, HP), F32), pltpu.VMEM((T, AW), BF16)]
               + _gla_scratch())
    return pl.pallas_call(
        functools.partial(_kf_states_body, nb=nb, nc=nc),
        grid=(bsz // nb, nc), in_specs=specs, out_specs=tuple(_state_specs(nb)), out_shape=out_shape,
        scratch_shapes=scratch, compiler_params=_params(), name="fwd_states")(*ins)


def _pad_heads(w, fill=0.0):
    lead = w.shape[:-1]
    w = w.reshape(lead + (BH, BDK))
    w = jnp.pad(w, [(0, 0)] * len(lead) + [(0, 0), (0, HP - BDK)], constant_values=fill)
    return w.reshape(lead + (BW,))


NBW = BH * BDK
IN_B0 = 2 * AW
IN_C0 = IN_B0 + 5 * NBW
REPACK_ROWS = 256


def _in_proj_plan():
    plan = [("b", 0, AW, 0, 1.0), ("f", AW, AW, 0, 0.5),
            ("f", IN_C0, 2 * CW, AW + 2 * BW, 1.0), ("f", IN_C0 + 2 * CW, CW, AW + 2 * BW + 2 * CW, 0.5)]
    for g, (dst, slot, scale) in enumerate((("b", 0, 0.5), ("f", 0, 0.5), ("b", 1, 0.5), ("b", 2, 1.0), ("f", 1, 0.5))):
        for h in range(BH):
            plan.append((dst, IN_B0 + g * NBW + h * BDK, BDK, AW + slot * BW + h * HP, scale))
    return plan


def _repack_in_body(w_ref, wb_ref, wf_ref):
    out = {"b": wb_ref, "f": wf_ref}
    for dst, src0, width, dst0, scale in _in_proj_plan():
        chunk = w_ref[0, :, src0:src0 + width]
        out[dst][0, :, dst0:dst0 + width] = (chunk if scale == 1.0 else chunk * scale).astype(BF16)
        if width == BDK:
            out[dst][0, :, dst0 + BDK:dst0 + HP] = jnp.zeros((REPACK_ROWS, HP - BDK), BF16)


def _repack_out_body(w_ref, wo_ref):
    for h in range(BH):
        wo_ref[0, h * HP:h * HP + BDK, :] = w_ref[0, AW + h * BDK:AW + (h + 1) * BDK, :].astype(BF16)
        wo_ref[0, h * HP + BDK:(h + 1) * HP, :] = jnp.zeros((HP - BDK, D), BF16)
    wo_ref[0, BW:BW + AW, :] = w_ref[0, 0:AW, :].astype(BF16)
    wo_ref[0, BW + AW:YW, :] = w_ref[0, AW + NBW:, :].astype(BF16)


def _repack_projections(w_in, w_out):
    depth = w_in.shape[0]
    wb, wf = pl.pallas_call(
        _repack_in_body, grid=(depth, D // REPACK_ROWS),
        in_specs=[pl.BlockSpec((1, REPACK_ROWS, w_in.shape[2]), lambda l, i: (l, i, 0))],
        out_specs=(pl.BlockSpec((1, REPACK_ROWS, KB_COLS), lambda l, i: (l, i, 0)),
                   pl.BlockSpec((1, REPACK_ROWS, KF_COLS), lambda l, i: (l, i, 0))),
        out_shape=(jax.ShapeDtypeStruct((depth, D, KB_COLS), BF16), jax.ShapeDtypeStruct((depth, D, KF_COLS), BF16)),
        name="repack_in_proj")(w_in)
    wo = pl.pallas_call(
        _repack_out_body, grid=(depth,),
        in_specs=[pl.BlockSpec((1,) + w_out.shape[1:], lambda l: (l, 0, 0))],
        out_specs=pl.BlockSpec((1, YW, D), lambda l: (l, 0, 0)),
        out_shape=jax.ShapeDtypeStruct((depth, YW, D), BF16),
        name="repack_out_proj")(w_out)
    return wb, wf, wo


def _prepare_weights(norm_w, w_in, a_conv_w, a_conv_b, a_wr, a_br, a_wi, a_bi, a_lambda, lbs, b_norm_w,
                     c_norm_w, c_ws, c_bs, w_out):
    depth = w_in.shape[0]
    wb, wf, wo = _repack_projections(w_in, w_out)
    eye = jnp.eye(A_HEADS, dtype=F32)

    def block_diag(w):
        return jnp.einsum("ldhij,hg->ldhigj", w, eye).reshape(depth, 2, AW, AW)
    return {
        "nw": norm_w[:, None, :],
        "wb": wb, "wf": wf, "wo": wo,
        "cw": a_conv_w, "cb": a_conv_b[:, None, :],
        "lb": _pad_heads(lbs[:depth], fill=1.0)[:, None, :],
        "bnw": _pad_heads(b_norm_w)[:, None, :],
        "cnw": c_norm_w[:, None, :],
        "cws": c_ws.reshape(depth, CH * CCH, CCH).astype(BF16),
        "cbias": jnp.repeat(jnp.swapaxes(c_bs, 1, 2), CHD, axis=2),
        "wg": (0.5 * jnp.concatenate([block_diag(a_wr), block_diag(a_wi)], axis=-1)).astype(BF16),
        "bg": 0.5 * jnp.concatenate([a_br, a_bi], axis=-1)[:, :, None, :],
        "cl": (-0.5 * RG_C * jax.nn.softplus(-a_lambda))[:, :, None, :],
    }


FWD, BWD = 0, 1
PER_DIRECTION = ("wg", "bg", "cl")


def _weight_operands(weights, names, layer, direction):
    arrays, specs = [], []
    for n in names:
        w = weights[n]
        lead = (layer, direction) if n in PER_DIRECTION else (layer,)
        rest = w.shape[len(lead):]
        arrays.append(w)
        specs.append(pl.BlockSpec((None,) * len(lead) + rest,
                                  lambda ib, ic, lead=lead, rest=rest: lead + (0,) * len(rest),
                                  pipeline_mode=pl.Buffered(1)))
    return arrays, specs


def _pos_tables(rows):
    qd = D // 4
    omega = 1.0 / (10000.0 ** (jnp.arange(qd, dtype=F32) / qd))
    ar = jnp.arange(rows, dtype=F32)[:, None] * omega
    ac = jnp.arange(GRID_W, dtype=F32)[:, None] * omega
    rowtab = jnp.concatenate([jnp.sin(ar), jnp.cos(ar)], axis=-1)
    coltab = jnp.concatenate([jnp.sin(ac), jnp.cos(ac)], axis=-1)
    nc = rows * GRID_W // T
    hrow = jnp.broadcast_to(rowtab[jnp.maximum(jnp.arange(nc) * NSEG - 1, 0)][:, None, :], (nc, SUBLANES, D // 2))
    hcol = jnp.broadcast_to(coltab[GRID_W - SUBLANES:][None], (nc, SUBLANES, D // 2))
    return rowtab, coltab, jnp.concatenate([hrow, hcol], axis=-1)


def kernel(x, c, ctx, c_ctx, norm_w, w_mod, b_mod, w_in, a_conv_w, a_conv_b, a_wr, a_br, a_wi, a_bi, a_lambda,
           b_lb_logits, b_norm_w, c_norm_w, c_ws, c_bs, w_out, final_norm_w):
    bsz, n_lat, _ = x.shape
    depth = w_in.shape[0]
    assert n_lat % T == 0 and T % ctx.shape[1] == 0 and bsz % (T // ctx.shape[1]) == 0
    nb_ctx = T // ctx.shape[1]
    pos = _pos_tables(n_lat // GRID_W)

    mrows = 2 * SUBLANES
    cs = jnp.zeros((mrows, D), F32).at[:bsz].set(c).at[bsz].set(c_ctx)
    nblk = 3
    mod = pl.pallas_call(
        _mod_body, grid=(depth, nblk),
        in_specs=[pl.BlockSpec((mrows, D), lambda l, j: (0, 0)),
                  pl.BlockSpec((1, D, D), lambda l, j: (l, 0, j)),
                  pl.BlockSpec((1, 1, D), lambda l, j: (l, 0, j))],
        out_specs=pl.BlockSpec((1, mrows, D), lambda l, j: (l, 0, j)),
        out_shape=jax.ShapeDtypeStruct((depth, mrows, 3 * D), F32),
        name="modulation")(cs, w_mod, b_mod[:, None, :])
    mod = mod.reshape(depth, mrows, 1, 3 * D)

    lbs = jnp.cumsum(jax.nn.softmax(b_lb_logits, axis=0), axis=0)
    fnw = final_norm_w[None, :]
    h_zero = jnp.zeros((bsz, 1, AW), F32)
    s_zero = jnp.zeros((bsz, BH, HP, HP), F32)
    weights = _prepare_weights(norm_w, w_in, a_conv_w, a_conv_b, a_wr, a_br, a_wi, a_bi, a_lambda, lbs, b_norm_w,
                               c_norm_w, c_ws, c_bs, w_out)
    for l in range(depth):
        lpos = pos if l == 0 else None
        last = l == depth - 1
        kb = _run_kb(ctx, mod[l], bsz, weights, l, h_zero, s_zero, None, nb_ctx, want_out=not last)
        hb_c, sb_c = kb[-2], kb[-1]
        if last:
            hbf_c, xc_c, v_c = kb[:3]
            hf_c, sf_c = _run_kf_states(hbf_c, xc_c, v_c, weights, l, h_zero, s_zero, nb_ctx)
        else:
            ctx, hf_c, sf_c = _run_kf(ctx, kb[:6], mod[l], bsz, weights, l, fnw, h_zero, s_zero, None, nb_ctx, False)
        kb = _run_kb(x, mod[l], None, weights, l, hb_c, sb_c, lpos, 1)
        x, _, _ = _run_kf(x, kb[:6], mod[l], None, weights, l, fnw, hf_c, sf_c, lpos, 1, last)
    return x
```

```python
import functools

import jax
import jax.numpy as jnp
import numpy as np
from jax import lax
from jax.experimental import pallas as pl
from jax.experimental.pallas import tpu as pltpu

D = 1024
GRID_W = 64
EPS = 1e-6
RG_C = 8.0
AW = 384
A_HEADS = 8
A_HD = 48
BH = 4
BDK = 96
CW = 256
CH = 4
CHD = 64
CCH = 128

LANES = 128
SUBLANES = 8
HP = LANES
BW = BH * HP
NBW = BH * BDK
YW = AW + BW + CW
NA = AW // LANES
T = 512
SEG = 64
NSEG = T // SEG
PITCH = SEG + SUBLANES
KB_COLS = AW + 3 * BW
KF_COLS = AW + 2 * BW + 3 * CW
FILL_COLS = 512
FILL_ROWS = T // 2
ROW_TILE = 16
MLP_TILE = 32
VMEM_LIMIT = 58 * 1024 * 1024

F32 = jnp.float32
BF16 = jnp.bfloat16

C_CONV_SEG, C_COEFF_TILE, C_SCAN_STEP, C_FIX_TILE, C_YA_TILE = 100, 50, 12, 8, 15
C_GATES_TILE, C_CUMSUM_STEP, C_GLA_A_TILE, C_GLA_C_TILE, C_SILU_TILE = 20, 8, 40, 15, 12
C_PRE_TILE, C_MLP_TILE, C_GLA_B_TILE, C_GLA_D_TILE = 25, 90, 40, 40
C_PROJ_PIECE, C_OUT_PIECE, C_NORM_PIECE = FILL_ROWS * FILL_COLS // 256, 3 * FILL_ROWS * FILL_COLS // 1024, 130
PACE = 1.4


def _silu_of_twice(hx):
    return hx * (jnp.tanh(hx) + 1.0)


def _silu(x):
    return _silu_of_twice(0.5 * x)


def _gelu(x):
    c = np.float32(np.sqrt(2.0 / np.pi))
    hx = 0.5 * x
    return hx * (jnp.tanh(x * (c + (0.044715 * c) * (x * x))) + 1.0)


def _nt(a, b):
    return lax.dot_general(a, b, (((1,), (1,)), ((), ())), preferred_element_type=F32)


def _tn(a, b):
    return lax.dot_general(a, b, (((0,), (0,)), ((), ())), preferred_element_type=F32)


def _norm_mod(x, nw, sh, sc):
    ms = jnp.mean(x * x, axis=-1, keepdims=True)
    return (x * lax.rsqrt(ms + EPS)) * (nw * (1.0 + sc)) + sh


def _pos_rows(rt_ref, ct_ref, r0):
    s, c0 = divmod(r0, SEG)
    return jnp.concatenate([jnp.broadcast_to(rt_ref[s:s + 1, :], (ROW_TILE, D // 2)),
                            ct_ref[c0:c0 + ROW_TILE, :]], axis=1)


class _Fill:
    def __init__(self, items=(), rate=1.0):
        self.items = list(items)
        self.rate = rate
        self.credit = 0.0
        self.spent = 0.0

    def __call__(self, credit):
        self.credit += credit
        while self.items and self.spent + 0.5 * self.items[0][0] <= self.credit * self.rate:
            cost, thunk = self.items.pop(0)
            self.spent += cost
            thunk()

    def extend(self, items):
        self.items.extend(items)

    def flush(self):
        while self.items:
            self.items.pop(0)[1]()


def _cost(items):
    return sum(c for c, _ in items)


def _emit(items, fill=None):
    for cost, thunk in items:
        thunk()
        if fill is not None:
            fill(cost)


def _paired(items, partner_items):
    partner = _Fill(partner_items, _cost(partner_items) / max(_cost(items), 1))
    _emit(items, partner)
    partner.flush()


class _Flat:
    def __init__(self, ref, width):
        self.ref = ref
        self.rs = ref.shape[1]
        self.width = width

    def _split(self, idx):
        rows, lanes = idx
        start, size = rows.start, rows.size
        q, off = divmod(start, self.rs)
        assert off + size <= self.rs
        return q, pl.ds(off, size), lanes

    def __getitem__(self, idx):
        q, r, lanes = self._split(idx)
        return self.ref[q, r, lanes]

    def __setitem__(self, idx, val):
        q, r, lanes = self._split(idx)
        self.ref[q, r, lanes] = val


def _tiles():
    return [(s, k) for s in range(NSEG) for k in range(BH)]


class _AScan:
    def __init__(self, a_st, u_st, h_st, p_st, hc_ref, nb, reverse):
        self.refs = (a_st, u_st, h_st, p_st)
        self.hc_ref, self.nb, self.reverse = hc_ref, nb, reverse
        self.hs = [jnp.zeros((NSEG, LANES), F32)] * NA
        self.ps = [jnp.ones((NSEG, LANES), F32)] * NA

    def coeff_items(self, g_scr, xc_ref, cl_ref):
        a_st, u_st = self.refs[:2]

        def item(s, k):
            def run():
                rows, lanes = pl.ds(s * SEG, SEG), slice(k * LANES, (k + 1) * LANES)
                tr = jnp.tanh(g_scr[rows, lanes])
                ti = jnp.tanh(g_scr[rows, AW + k * LANES:AW + (k + 1) * LANES])
                hcl = cl_ref[:, lanes]
                la = hcl * tr + hcl
                th = jnp.tanh(la)
                om4 = (-0.5 * th) / (1.0 - th)
                a_st[k, pl.ds(s * PITCH, SEG), :] = jnp.exp(la)
                root = jnp.where(om4 > 0.0, om4 * lax.rsqrt(om4), 0.0)
                u_st[k, pl.ds(s * PITCH, SEG), :] = root * ((ti + 1.0) * xc_ref[rows, lanes])
            return C_COEFF_TILE, run
        return [item(s, k) for s in range(NSEG) for k in range(NA)]

    def step_items(self):
        a_st, u_st, h_st, p_st = self.refs

        def item(j):
            def run():
                for k in range(NA):
                    a = a_st[k, pl.ds(j, NSEG, stride=PITCH), :]
                    u = u_st[k, pl.ds(j, NSEG, stride=PITCH), :]
                    self.hs[k] = a * self.hs[k] + u
                    self.ps[k] = a * self.ps[k]
                    h_st[k, pl.ds(j, NSEG, stride=PITCH), :] = self.hs[k]
                    p_st[k, pl.ds(j, NSEG, stride=PITCH), :] = self.ps[k]
            return C_SCAN_STEP, run
        order = range(SEG - 1, -1, -1) if self.reverse else range(SEG)
        return [item(j) for j in order]

    def chain(self):
        spq = NSEG // self.nb
        cin = [None] * NSEG
        for q in range(self.nb):
            c = [self.hc_ref[q:q + 1, k * LANES:(k + 1) * LANES] for k in range(NA)]
            order = range((q + 1) * spq - 1, q * spq - 1, -1) if self.reverse else range(q * spq, (q + 1) * spq)
            for s in order:
                cin[s] = c
                c = [self.ps[k][s:s + 1, :] * c[k] + self.hs[k][s:s + 1, :] for k in range(NA)]
            for k in range(NA):
                self.hc_ref[q:q + 1, k * LANES:(k + 1) * LANES] = c[k]
        return cin

    def full(self, cin, s, k):
        _, _, h_st, p_st = self.refs
        st_rows = pl.ds(s * PITCH, SEG)
        return h_st[k, st_rows, :] + p_st[k, st_rows, :] * cin[s][k]


class _Gla:
    def __init__(self, reverse, nb, k_scr, lf_st, g_st, s_scr, stage, states_only=False):
        self.reverse, self.nb, self.states_only = reverse, nb, states_only
        self.k_scr, self.lf_st, self.g_st, self.s_scr = k_scr, lf_st, g_st, s_scr
        self.kg_s, self.ke_s, self.lhs_s, self.kv_s, self.rhs_s = stage
        self.gs = [jnp.zeros((NSEG, LANES), F32)] * BH
        order = range(NSEG - 1, -1, -1) if reverse else range(NSEG)
        self.order = list(order)
        self.tiles = [(n, k) for n in order for k in range(BH)]

    def gate_items(self, p_scr, col0, lb_ref):
        def item(s, k):
            def run():
                rows, lanes = pl.ds(s * SEG, SEG), slice(k * HP, (k + 1) * HP)
                lb = lb_ref[:, lanes]
                c0, c1 = 0.5 + 0.5 * lb, 0.5 - 0.5 * lb
                ct = c1 * jnp.tanh(p_scr[rows, col0 + k * HP:col0 + (k + 1) * HP])
                self.k_scr[rows, lanes] = c1 - ct
                self.lf_st[k, pl.ds(s * PITCH, SEG), :] = jnp.log2(c0 + ct)
            return C_GATES_TILE, run
        return [item(s, k) for s, k in _tiles()]

    def cumsum_items(self):
        def item(j):
            def run():
                for k in range(BH):
                    self.gs[k] = self.gs[k] + self.lf_st[k, pl.ds(j, NSEG, stride=PITCH), :]
                    self.g_st[k, pl.ds(j, NSEG, stride=PITCH), :] = self.gs[k]
            return C_CUMSUM_STEP, run
        order = range(SEG - 1, -1, -1) if self.reverse else range(SEG)
        return [item(j) for j in order]

    def a_items(self, load_q, load_v):
        def item(n, k):
            def run():
                rows, lanes = pl.ds(n * SEG, SEG), slice(k * HP, (k + 1) * HP)
                g = self.g_st[k, pl.ds(n * PITCH, SEG), :]
                tot = self.gs[k][n:n + 1, :]
                kk = self.k_scr[rows, lanes]
                if not self.states_only:
                    self.lhs_s[k, rows, 0:HP] = (load_q(n, k) * jnp.exp2(g)).astype(BF16)
                    self.rhs_s[n, k, HP:HP + SEG, :] = load_v(n, k)
                    self.kg_s[rows, lanes] = (kk * jnp.exp2(-g)).astype(BF16)
                self.ke_s[rows, lanes] = (kk * jnp.exp2(tot - g)).astype(BF16)
            return C_GLA_A_TILE, run
        return [item(n, k) for n, k in self.tiles]

    def b_items(self, load_v):
        r_i = lax.broadcasted_iota(jnp.int32, (SEG, SEG), 0)
        c_i = lax.broadcasted_iota(jnp.int32, (SEG, SEG), 1)
        keep = (c_i >= r_i) if self.reverse else (c_i <= r_i)

        def item(n, k):
            def run():
                rows, lanes = pl.ds(n * SEG, SEG), slice(k * HP, (k + 1) * HP)
                if not self.states_only:
                    self.lhs_s[k, rows, HP:HP + SEG] = jnp.where(
                        keep, _nt(self.lhs_s[k, rows, 0:HP], self.kg_s[rows, lanes]), 0.0).astype(BF16)
                self.kv_s[n, k] = _tn(load_v(n, k), self.ke_s[rows, lanes])
            return C_GLA_B_TILE, run
        return [item(n, k) for n, k in self.tiles]

    def c_items(self):
        cps = NSEG // self.nb
        state = {}

        def item(n, k):
            def run():
                sq = n // cps
                st = state.get(k)
                if st is None:
                    st = self.s_scr[sq, k]
                if not self.states_only:
                    self.rhs_s[n, k, 0:HP, :] = st.T.astype(BF16)
                st = jnp.exp2(self.gs[k][n:n + 1, :]) * st + self.kv_s[n, k]
                last = (n % cps == 0) if self.reverse else (n % cps == cps - 1)
                if last:
                    self.s_scr[sq, k] = st
                    st = None
                state[k] = st
            return C_GLA_C_TILE, run
        return [item(n, k) for n in self.order for k in range(BH)]

    def d_items(self, emit_out):
        def item(n, k):
            def run():
                rows, lanes = pl.ds(n * SEG, SEG), slice(k * HP, (k + 1) * HP)
                o = jnp.dot(self.lhs_s[k, rows, :], self.rhs_s[n, k], preferred_element_type=F32)
                emit_out(n, k, o)
            return C_GLA_D_TILE, run
        return [item(n, k) for n, k in self.tiles]


def _gla_scratch():
    return ([pltpu.VMEM((T, BW), BF16)] * 2
            + [pltpu.VMEM((BH, T, HP + SEG), BF16), pltpu.VMEM((NSEG, BH, HP, HP), F32),
               pltpu.VMEM((NSEG, BH, HP + SEG, HP), BF16)])


def _dot_items(lhs_rows, w_ref, dst, col_lo, col_hi, cost):
    def piece(r0, c0, c1):
        def run():
            dst[r0:r0 + FILL_ROWS, c0:c1] = jnp.dot(lhs_rows(r0, r0 + FILL_ROWS), w_ref[:, c0:c1],
                                                    preferred_element_type=F32)
        return cost, run
    return [piece(r0, c0, min(c0 + FILL_COLS, col_hi))
            for c0 in range(col_lo, col_hi, FILL_COLS) for r0 in range(0, T, FILL_ROWS)]


def _kb_body(*refs, nb, nc, add_pos, want_out):
    it = iter(refs)
    x0_ref, xn_ref = next(it), next(it)
    xh_ref = next(it) if nc > 1 else None
    if add_pos:
        rt0_ref, rtn_ref, ct_ref, hp_ref = next(it), next(it), next(it), next(it)
    else:
        rt0_ref = rtn_ref = ct_ref = hp_ref = None
    (mod_ref, modn_ref, nw_ref, wb_ref, cw_ref, cb_ref, wg_ref, bg_ref, cl_ref, lb_ref, h0_ref, s0_ref,
     hbf_ref, xc_ref) = (next(it) for _ in range(14))
    hb_ref, ob_ref, q_ref = (next(it), next(it), next(it)) if want_out else (None, None, None)
    (v_ref, hfin_ref, sfin_ref,
     p_scr, h_scr, xa_ext, g_scr, k_scr, a_st, u_st, h_st, p_st, lf_st, g_st, hc_ref, s_scr, xaf,
     *gla_stage) = it

    ic = pl.program_id(1)
    flat = pl.program_id(0) * nc + ic
    cc = nc - 1 - ic

    @pl.when(ic == 0)
    def _():
        for q in range(nb):
            hc_ref[q:q + 1, :] = h0_ref[q]
        s_scr[...] = s0_ref[...]
        xaf[...] = jnp.zeros_like(xaf)

    def norm_items(x_r, rt_r, m_r):
        x_flat = _Flat(x_r, D)

        def item(s):
            def run():
                rows = pl.ds(s * SEG, SEG)
                xin = x_flat[rows, slice(None)]
                if add_pos:
                    xin = xin + jnp.concatenate(
                        [_pos_rows(rt_r, ct_ref, r0) for r0 in range(s * SEG, (s + 1) * SEG, ROW_TILE)], axis=0)
                h_scr[rows, :] = _norm_mod(xin, nw_ref[...], m_r[0, :, 0:D], m_r[0, :, D:2 * D]).astype(BF16)
            return C_NORM_PIECE, run
        return [item(s) for s in range(NSEG)]

    def proj_items():
        return _dot_items(lambda r0, r1: h_scr[r0:r1, :], wb_ref, p_scr, 0, KB_COLS, C_PROJ_PIECE)

    gla = _Gla(True, nb, k_scr, lf_st, g_st, s_scr, gla_stage, states_only=not want_out)
    v_flat = _Flat(v_ref, BW)
    if want_out:
        q_flat, ob_flat = _Flat(q_ref, BW), _Flat(ob_ref, BW)

    def hgrn_input_items():
        def qv_item(s, k):
            def run():
                rows, lanes = pl.ds(s * SEG, SEG), slice(k * HP, (k + 1) * HP)
                if want_out:
                    q_flat[rows, lanes] = _silu_of_twice(p_scr[rows, AW + k * HP:AW + (k + 1) * HP])
                v_flat[rows, lanes] = p_scr[rows, AW + 2 * BW + k * HP:AW + 2 * BW + (k + 1) * HP].astype(BF16)
            return C_PRE_TILE, run
        return [qv_item(s, k) for s, k in _tiles()] + gla.gate_items(p_scr, AW + BW, lb_ref)

    @pl.when(flat == 0)
    def _():
        _emit(norm_items(x0_ref, rt0_ref, mod_ref) + proj_items())

    def step():
        xc_flat = _Flat(xc_ref, AW)
        hb_flat = _Flat(hb_ref, AW) if want_out else None
        scan = _AScan(a_st, u_st, h_st, p_st, hc_ref, nb, reverse=True)

        hbf_ref[...] = h_scr[...].reshape(hbf_ref.shape)
        if nc > 1:
            xh = xh_ref[0]
            if add_pos:
                xh = xh + hp_ref[0]
            hh = _norm_mod(xh, nw_ref[...], mod_ref[0, :, 0:D], mod_ref[0, :, D:2 * D]).astype(BF16)
            left = jnp.dot(hh, wb_ref[:, 0:AW], preferred_element_type=F32)
            xa_ext[0:SUBLANES, :] = jnp.where(cc > 0, left, 0.0)
        else:
            xa_ext[0:SUBLANES, :] = jnp.zeros((SUBLANES, AW), F32)
        xa_ext[pl.ds(SUBLANES, T), :] = p_scr[:, 0:AW]
        xa_ext[pl.ds(SUBLANES + T, SUBLANES), :] = xaf[...]
        xaf[...] = p_scr[0:SUBLANES, 0:AW]

        def conv_item(s):
            def run():
                spq = NSEG // nb
                row = lax.broadcasted_iota(jnp.int32, (SEG, 1), 0)
                acc = jnp.broadcast_to(cb_ref[...], (SEG, AW))
                for j in range(4):
                    tap = xa_ext[pl.ds(s * SEG + SUBLANES - 2 + j, SEG), :]
                    if nb > 1 and s % spq == 0 and j < 2:
                        tap = jnp.where(row + (j - 2) >= 0, tap, 0.0)
                    if nb > 1 and s % spq == spq - 1 and j == 3:
                        tap = jnp.where(row + 1 < SEG, tap, 0.0)
                    acc = acc + cw_ref[j:j + 1, :] * tap
                xc_flat[pl.ds(s * SEG, SEG), slice(None)] = acc
            return C_CONV_SEG, run

        def load_q(n, k):
            return q_flat[pl.ds(n * SEG, SEG), slice(k * HP, (k + 1) * HP)]

        def load_v(n, k):
            return v_flat[pl.ds(n * SEG, SEG), slice(k * HP, (k + 1) * HP)]

        def emit_out(n, k, o):
            ob_flat[pl.ds(n * SEG, SEG), slice(k * HP, (k + 1) * HP)] = o

        _emit(hgrn_input_items())
        _emit(norm_items(xn_ref, rtn_ref, modn_ref))

        conv = [conv_item(s) for s in range(NSEG)]
        coeff, steps, cums = scan.coeff_items(g_scr, xc_flat, cl_ref), scan.step_items(), gla.cumsum_items()
        proj = proj_items()
        credit = (_cost(conv) + _cost(coeff) + _cost(steps) + _cost(cums)
                  + BH * NSEG * (C_GLA_A_TILE + C_GLA_C_TILE) + NA * NSEG * C_FIX_TILE)
        mx = _Fill(proj, PACE * _cost(proj) / credit)
        _emit(conv, mx)
        g_scr[...] = jnp.dot(xc_ref[...].reshape(T, AW).astype(BF16), wg_ref[...],
                             preferred_element_type=F32) + bg_ref[...]
        _emit(coeff, mx)
        _emit(steps, mx)
        cin = scan.chain()
        for q in range(nb):
            hfin_ref[q] = hc_ref[q:q + 1, :]

        def fix_item(s, k):
            def run():
                hb_flat[pl.ds(s * SEG, SEG), k * LANES:(k + 1) * LANES] = scan.full(cin, s, k)
            return C_FIX_TILE, run
        if want_out:
            _emit([fix_item(s, k) for s in range(NSEG) for k in range(NA)], mx)

        _emit(cums, mx)
        _emit(gla.a_items(load_q, load_v), mx)
        _emit(gla.b_items(load_v))
        _emit(gla.c_items(), mx)
        if want_out:
            _emit(gla.d_items(emit_out))
        sfin_ref[...] = s_scr[...]
        mx.flush()

    step()


def _kf_body(*refs, nb, nc, add_pos, final):
    it = iter(refs)
    x_ref = next(it)
    if add_pos:
        rt_ref, ct_ref = next(it), next(it)
    (hbf0_ref, hbfn_ref, xc_ref, hb_ref, ob_ref, q_ref, v_ref, mod_ref, wf_ref, wg_ref, bg_ref, cl_ref, lb_ref,
     bnw_ref, cnw_ref, cws_ref, cbias_ref, wo_ref, fnw_ref, h0_ref, s0_ref,
     xo_ref, hfin_ref, sfin_ref,
     p_scr, sg_scr, hn_scr, g_scr, k_scr, y_scr, part_scr, a_st, u_st, h_st, p_st, lf_st, g_st, hc_ref, s_scr,
     xcb_scr, vn_scr, mix_scr, *gla_stage) = it

    ic = pl.program_id(1)
    flat = pl.program_id(0) * nc + ic

    @pl.when(ic == 0)
    def _():
        for q in range(nb):
            hc_ref[q:q + 1, :] = h0_ref[q]
        s_scr[...] = s0_ref[...]

    o_ag, o_ff, o_bg = 0, AW, AW + BW
    o_cu, o_cv, o_cg = AW + 2 * BW, AW + 2 * BW + CW, AW + 2 * BW + 2 * CW

    def proj_items(col_lo, col_hi):
        return _dot_items(lambda r0, r1: hn_scr[r0:r1, :], wf_ref, p_scr, col_lo, col_hi, C_PROJ_PIECE)

    @pl.when(flat == 0)
    def _():
        hn_scr[...] = hbf0_ref[...].reshape(T, D)
        _emit(proj_items(0, KF_COLS))

    def step():
        xc_flat, hb_flat = _Flat(xc_ref, AW), _Flat(hb_ref, AW)
        q_flat, v_flat, ob_flat = _Flat(q_ref, BW), _Flat(v_ref, BW), _Flat(ob_ref, BW)
        scan = _AScan(a_st, u_st, h_st, p_st, hc_ref, nb, reverse=False)
        gla = _Gla(False, nb, k_scr, lf_st, g_st, s_scr, gla_stage)

        hn_scr[...] = hbfn_ref[...].reshape(T, D)
        for s in range(NSEG):
            rows = pl.ds(s * SEG, SEG)
            xcb_scr[rows, :] = xc_flat[rows, slice(None)].astype(BF16)

        def silu_item(s, c0):
            def run():
                rows = pl.ds(s * SEG, SEG)
                dst = c0 - o_ag if c0 < o_ff else AW + c0 - o_bg
                sg_scr[rows, dst:dst + LANES] = _silu_of_twice(p_scr[rows, c0:c0 + LANES])
            return C_SILU_TILE, run
        gate_cols = list(range(o_ag, o_ag + AW, LANES)) + list(range(o_bg, o_bg + BW, LANES))

        def mlp_items(m):
            lane = lax.broadcasted_iota(jnp.int32, (MLP_TILE, CW), 1)

            def norm_item(r0):
                def run():
                    v = _gelu(p_scr[pl.ds(m * CCH + r0, MLP_TILE), o_cv:o_cv + CW])
                    mu = jnp.mean(v, axis=-1, keepdims=True)
                    dv = v - mu
                    var = jnp.mean(dv * dv, axis=-1, keepdims=True)
                    vn_scr[r0:r0 + MLP_TILE, :] = (dv * lax.rsqrt(var + EPS) * cnw_ref[...]).astype(BF16)
                return C_MLP_TILE, run

            def mix():
                mix_scr[...] = jnp.dot(cws_ref[...], vn_scr[...], preferred_element_type=F32)

            def out_item(r0):
                def run():
                    rows = pl.ds(m * CCH + r0, MLP_TILE)
                    mixed = mix_scr[r0:r0 + MLP_TILE, :]
                    for h in range(1, CH):
                        mixed = jnp.where(lane >= h * CHD, mix_scr[h * CCH + r0:h * CCH + r0 + MLP_TILE, :], mixed)
                    u = _gelu(p_scr[rows, o_cu:o_cu + CW])
                    yc = (u * (mixed + cbias_ref[r0:r0 + MLP_TILE, :])
                          * _silu_of_twice(p_scr[rows, o_cg:o_cg + CW]))
                    y_scr[rows, BW + AW:YW] = yc.astype(BF16)
                return C_MLP_TILE, run
            tiles = range(0, CCH, MLP_TILE)
            return [norm_item(r0) for r0 in tiles] + [(0, mix)] + [out_item(r0) for r0 in tiles]

        def load_q(n, k):
            return q_flat[pl.ds(n * SEG, SEG), slice(k * HP, (k + 1) * HP)]

        def load_v(n, k):
            return v_flat[pl.ds(n * SEG, SEG), slice(k * HP, (k + 1) * HP)]

        def emit_out(n, k, o):
            rows, lanes = pl.ds(n * SEG, SEG), slice(k * HP, (k + 1) * HP)
            o = o + ob_flat[rows, lanes]
            ms = jnp.sum(o * o, axis=-1, keepdims=True) * (1.0 / BDK)
            yb = (o * lax.rsqrt(ms + EPS) * bnw_ref[:, lanes]) * sg_scr[rows, AW + k * HP:AW + (k + 1) * HP]
            y_scr[rows, lanes] = yb.astype(BF16)

        g_scr[...] = jnp.dot(xcb_scr[...], wg_ref[...], preferred_element_type=F32) + bg_ref[...]
        _emit([it for m in range(T // CCH) for it in mlp_items(m)])

        pre = [silu_item(s, c0) for s in range(NSEG) for c0 in gate_cols] + gla.gate_items(p_scr, o_ff, lb_ref)
        coeff, steps, cums = scan.coeff_items(g_scr, xc_flat, cl_ref), scan.step_items(), gla.cumsum_items()
        n_proj = -(-KF_COLS // FILL_COLS) * (T // FILL_ROWS)
        n_out = (D // FILL_COLS) * (T // FILL_ROWS)
        credit = (_cost(pre) + _cost(coeff) + _cost(steps) + _cost(cums)
                  + BH * NSEG * (C_GLA_A_TILE + C_GLA_C_TILE + C_GLA_D_TILE) + NA * NSEG * C_YA_TILE)
        mx = _Fill(proj_items(o_cu, KF_COLS), PACE * (n_proj * C_PROJ_PIECE + n_out * C_OUT_PIECE) / credit)
        _emit(pre, mx)
        mx.extend(proj_items(0, o_cu))

        _emit(coeff, mx)
        _emit(steps, mx)
        cin = scan.chain()
        for q in range(nb):
            hfin_ref[q] = hc_ref[q:q + 1, :]

        def ya_item(s, k):
            def run():
                rows, lanes = pl.ds(s * SEG, SEG), slice(k * LANES, (k + 1) * LANES)
                ya = (scan.full(cin, s, k) + hb_flat[rows, lanes]) * sg_scr[rows, lanes]
                y_scr[rows, BW + k * LANES:BW + (k + 1) * LANES] = ya.astype(BF16)
            return C_YA_TILE, run
        _emit([ya_item(s, k) for s in range(NSEG) for k in range(NA)], mx)
        mx.extend(_dot_items(lambda r0, r1: y_scr[r0:r1, BW:YW], wo_ref.at[BW:YW, :], part_scr, 0, D, C_OUT_PIECE))

        _emit(cums, mx)
        _emit(gla.a_items(load_q, load_v), mx)
        _emit(gla.b_items(load_v))
        _emit(gla.c_items(), mx)
        _emit(gla.d_items(emit_out), mx)
        sfin_ref[...] = s_scr[...]
        mx.flush()

        gm = mod_ref[0, :, 2 * D:3 * D]
        x_flat, xo_flat = _Flat(x_ref, D), _Flat(xo_ref, D)
        def out_item(r0, c0):
            def run():
                rows, cols = pl.ds(r0, FILL_ROWS), slice(c0, min(c0 + FILL_COLS, D))
                part_scr[rows, cols] = part_scr[rows, cols] + jnp.dot(y_scr[rows, 0:BW], wo_ref[0:BW, cols],
                                                                      preferred_element_type=F32)
            return C_OUT_PIECE, run

        def residual_item(r0):
            def run():
                rows = pl.ds(r0, ROW_TILE)
                xin = x_flat[rows, slice(None)]
                if add_pos:
                    xin = xin + _pos_rows(rt_ref, ct_ref, r0)
                xn = xin + gm * part_scr[rows, :]
                if final:
                    ms = jnp.mean(xn * xn, axis=-1, keepdims=True)
                    xn = xn * lax.rsqrt(ms + EPS) * fnw_ref[...]
                xo_flat[rows, slice(None)] = xn
            return C_OUT_PIECE * (D // FILL_COLS) * ROW_TILE // FILL_ROWS, run

        pending = []
        for r0 in range(0, T, FILL_ROWS):
            _paired([out_item(r0, c0) for c0 in range(0, D, FILL_COLS)], pending)
            pending = [residual_item(r) for r in range(r0, r0 + FILL_ROWS, ROW_TILE)]
        _emit(pending)

    step()


def _kf_states_body(hbf0_ref, hbfn_ref, xc_ref, v_ref, wf_ref, wg_ref, bg_ref, cl_ref, lb_ref, h0_ref, s0_ref,
                    hfin_ref, sfin_ref,
                    p_scr, hn_scr, g_scr, k_scr, a_st, u_st, h_st, p_st, lf_st, g_st, hc_ref, s_scr, xcb_scr,
                    *gla_stage, nb, nc):
    ic = pl.program_id(1)
    flat = pl.program_id(0) * nc + ic

    @pl.when(ic == 0)
    def _():
        for q in range(nb):
            hc_ref[q:q + 1, :] = h0_ref[q]
        s_scr[...] = s0_ref[...]

    def proj_items():
        return _dot_items(lambda r0, r1: hn_scr[r0:r1, :], wf_ref.at[:, AW:AW + BW], p_scr, 0, BW, C_PROJ_PIECE)

    @pl.when(flat == 0)
    def _():
        hn_scr[...] = hbf0_ref[...].reshape(T, D)
        _emit(proj_items())

    xc_flat, v_flat = _Flat(xc_ref, AW), _Flat(v_ref, BW)
    scan = _AScan(a_st, u_st, h_st, p_st, hc_ref, nb, reverse=False)
    gla = _Gla(False, nb, k_scr, lf_st, g_st, s_scr, gla_stage, states_only=True)

    hn_scr[...] = hbfn_ref[...].reshape(T, D)
    for s in range(NSEG):
        rows = pl.ds(s * SEG, SEG)
        xcb_scr[rows, :] = xc_flat[rows, slice(None)].astype(BF16)
    g_scr[...] = jnp.dot(xcb_scr[...], wg_ref[...], preferred_element_type=F32) + bg_ref[...]
    _emit(gla.gate_items(p_scr, 0, lb_ref))

    coeff, steps, cums = scan.coeff_items(g_scr, xc_flat, cl_ref), scan.step_items(), gla.cumsum_items()
    proj = proj_items()
    mx = _Fill(proj, _cost(proj) / (_cost(coeff) + _cost(steps) + _cost(cums)))
    _emit(coeff, mx)
    _emit(steps, mx)
    scan.chain()
    for q in range(nb):
        hfin_ref[q] = hc_ref[q:q + 1, :]
    _emit(cums, mx)

    def load_v(n, k):
        return v_flat[pl.ds(n * SEG, SEG), slice(k * HP, (k + 1) * HP)]
    _emit(gla.a_items(None, load_v))
    _emit(gla.b_items(load_v))
    _emit(gla.c_items())
    sfin_ref[...] = s_scr[...]
    mx.flush()


def _mod_body(cs_ref, w_ref, b_ref, o_ref):
    s = _silu(cs_ref[...])
    o_ref[0] = jnp.dot(s, w_ref[0], preferred_element_type=F32, precision=lax.Precision.HIGHEST) + b_ref[0]


def _const_spec(shape):
    nd = len(shape)
    return pl.BlockSpec(shape, lambda ib, ic: (0,) * nd, pipeline_mode=pl.Buffered(1))


def _seq_spec(nb, rs, width, bc):
    return pl.BlockSpec((nb, rs, width), lambda ib, ic: bc(ib, ic) + (0,))


def _step_maps(ngroups, nc, reverse):
    last = ngroups * nc - 1

    def at(f):
        return f // nc, (nc - 1 - f % nc) if reverse else f % nc

    def cur(ib, ic):
        return at(ib * nc + ic)

    def ahead(d):
        return lambda ib, ic: at(jnp.minimum(ib * nc + ic + d, last))

    def fixed(f):
        return lambda ib, ic: at(min(f, last))
    return cur, ahead, fixed


def _scan_scratch():
    return ([pltpu.VMEM((NA, NSEG * PITCH, LANES), F32)] * 4
            + [pltpu.VMEM((BH, NSEG * PITCH, LANES), F32)] * 2)


def _params():
    return pltpu.CompilerParams(dimension_semantics=("arbitrary", "arbitrary"), vmem_limit_bytes=VMEM_LIMIT)


def _state_specs(nb):
    return [pl.BlockSpec((nb, 1, AW), lambda ib, ic: (ib, 0, 0)),
            pl.BlockSpec((nb, BH, HP, HP), lambda ib, ic: (ib, 0, 0, 0))]


def _run_kb(x, mod, mod_row, weights, layer, h0, s0, pos, nb, want_out=True):
    bsz, seq, _ = x.shape
    rs = T // nb
    nc = seq // rs
    add_pos = pos is not None
    cur, ahead, fixed = _step_maps(bsz // nb, nc, reverse=True)
    steps2 = (fixed(0), ahead(1))
    mrow = (lambda g: g) if mod_row is None else (lambda g: mod_row)
    mod_spec = lambda bc: pl.BlockSpec((1, 1, 3 * D), lambda ib, ic: (mrow(bc(ib, ic)[0]), 0, 0))
    ins, specs = [x] * 2, [_seq_spec(nb, rs, D, bc) for bc in steps2]
    if nc > 1:
        ins.append(x)
        specs.append(pl.BlockSpec(
            (1, SUBLANES, D), lambda ib, ic: (ib, jnp.maximum(cur(ib, ic)[1] * (rs // SUBLANES) - 1, 0), 0)))
    if add_pos:
        rowtab, coltab, hpos = pos
        row_spec = lambda bc: pl.BlockSpec((NSEG, D // 2), lambda ib, ic: (bc(ib, ic)[1], 0))
        ins += [rowtab] * 2 + [coltab, hpos]
        specs += [row_spec(bc) for bc in steps2] + [
            _const_spec(coltab.shape), pl.BlockSpec((1, SUBLANES, D), lambda ib, ic: (cur(ib, ic)[1], 0, 0))]
    w_ins, w_specs = _weight_operands(weights, ("nw", "wb", "cw", "cb", "wg", "bg", "cl", "lb"), layer, BWD)
    ins += [mod] * 2 + w_ins + [h0, s0]
    specs += [mod_spec(cur), mod_spec(ahead(1))] + w_specs + _state_specs(nb)
    seq_outs = [(D, BF16), (AW, F32)] + ([(AW, F32), (BW, F32), (BW, F32)] if want_out else []) + [(BW, BF16)]
    out_shape = tuple([jax.ShapeDtypeStruct((bsz, seq, w), dt) for w, dt in seq_outs]
                      + [jax.ShapeDtypeStruct((bsz, 1, AW), F32), jax.ShapeDtypeStruct((bsz, BH, HP, HP), F32)])
    out_specs = tuple([_seq_spec(nb, rs, w, cur) for w, _ in seq_outs] + _state_specs(nb))
    scratch = ([pltpu.VMEM((T, KB_COLS), F32), pltpu.VMEM((T, D), BF16)]
               + [pltpu.VMEM((T + 2 * SUBLANES, AW), F32), pltpu.VMEM((T, 2 * AW), F32), pltpu.VMEM((T, BW), F32)]
               + _scan_scratch()
               + [pltpu.VMEM((SUBLANES, AW), F32), pltpu.VMEM((nb, BH, HP, HP), F32),
                  pltpu.VMEM((SUBLANES, AW), F32)]
               + _gla_scratch())
    return pl.pallas_call(
        functools.partial(_kb_body, nb=nb, nc=nc, add_pos=add_pos, want_out=want_out),
        grid=(bsz // nb, nc), in_specs=specs, out_specs=out_specs, out_shape=out_shape,
        scratch_shapes=scratch, compiler_params=_params(), name="bwd_sweep")(*ins)


def _run_kf(x, kb_out, mod, mod_row, weights, layer, fnw, h0, s0, pos, nb, final):
    bsz, seq, _ = x.shape
    rs = T // nb
    nc = seq // rs
    add_pos = pos is not None
    cur, ahead, fixed = _step_maps(bsz // nb, nc, reverse=False)
    mrow = (lambda g: g) if mod_row is None else (lambda g: mod_row)
    hbf, xc, hb, ob, q, v = kb_out
    ins, specs = [x], [_seq_spec(nb, rs, D, cur)]
    if add_pos:
        rowtab, coltab, _ = pos
        ins += [rowtab, coltab]
        specs += [pl.BlockSpec((NSEG, D // 2), lambda ib, ic: (cur(ib, ic)[1], 0)), _const_spec(coltab.shape)]
    w_ins, w_specs = _weight_operands(
        weights, ("wf", "wg", "bg", "cl", "lb", "bnw", "cnw", "cws", "cbias", "wo"), layer, FWD)
    ins += [hbf, hbf, xc, hb, ob, q, v, mod] + w_ins + [fnw, h0, s0]
    specs += [_seq_spec(nb, rs, D, fixed(0)), _seq_spec(nb, rs, D, ahead(1))]
    specs += [_seq_spec(nb, rs, w, cur) for w in (AW, AW, BW, BW, BW)]
    specs += [pl.BlockSpec((1, 1, 3 * D), lambda ib, ic: (mrow(ib), 0, 0))]
    specs += w_specs + [_const_spec(fnw.shape)] + _state_specs(nb)
    out_shape = (jax.ShapeDtypeStruct((bsz, seq, D), F32),
                 jax.ShapeDtypeStruct((bsz, 1, AW), F32), jax.ShapeDtypeStruct((bsz, BH, HP, HP), F32))
    out_specs = tuple([_seq_spec(nb, rs, D, cur)] + _state_specs(nb))
    scratch = ([pltpu.VMEM((T, KF_COLS), F32), pltpu.VMEM((T, AW + BW), F32), pltpu.VMEM((T, D), BF16)]
               + [pltpu.VMEM((T, 2 * AW), F32), pltpu.VMEM((T, BW), F32), pltpu.VMEM((T, YW), BF16),
                  pltpu.VMEM((T, D), F32)]
               + _scan_scratch()
               + [pltpu.VMEM((SUBLANES, AW), F32), pltpu.VMEM((nb, BH, HP, HP), F32),
                  pltpu.VMEM((T, AW), BF16), pltpu.VMEM((CCH, CW), BF16), pltpu.VMEM((CH * CCH, CW), F32)]
               + _gla_scratch())
    return pl.pallas_call(
        functools.partial(_kf_body, nb=nb, nc=nc, add_pos=add_pos, final=final),
        grid=(bsz // nb, nc), in_specs=specs, out_specs=out_specs, out_shape=out_shape,
        scratch_shapes=scratch, compiler_params=_params(), name="fwd_sweep")(*ins)


def _run_kf_states(hbf, xc, v, weights, layer, h0, s0, nb):
    bsz, seq, _ = hbf.shape
    rs = T // nb
    nc = seq // rs
    cur, ahead, fixed = _step_maps(bsz // nb, nc, reverse=False)
    w_ins, w_specs = _weight_operands(weights, ("wf", "wg", "bg", "cl", "lb"), layer, FWD)
    ins = [hbf, hbf, xc, v] + w_ins + [h0, s0]
    specs = ([_seq_spec(nb, rs, D, fixed(0)), _seq_spec(nb, rs, D, ahead(1)),
              _seq_spec(nb, rs, AW, cur), _seq_spec(nb, rs, BW, cur)] + w_specs + _state_specs(nb))
    out_shape = (jax.ShapeDtypeStruct((bsz, 1, AW), F32), jax.ShapeDtypeStruct((bsz, BH, HP, HP), F32))
    scratch = ([pltpu.VMEM((T, BW), F32), pltpu.VMEM((T, D), BF16), pltpu.VMEM((T, 2 * AW), F32),
                pltpu.VMEM((T, BW), F32)]
               + _scan_scratch()
               + [pltpu.VMEM((SUBLANES, AW), F32), pltpu.VMEM((nb, BH, HP, HP), F32), pltpu.VMEM((T, AW), BF16)]
               + _gla_scratch())
    return pl.pallas_call(
        functools.partial(_kf_states_body, nb=nb, nc=nc),
        grid=(bsz // nb, nc), in_specs=specs, out_specs=tuple(_state_specs(nb)), out_shape=out_shape,
        scratch_shapes=scratch, compiler_params=_params(), name="fwd_states")(*ins)


def _pad_heads(w, fill=0.0):
    lead = w.shape[:-1]
    w = w.reshape(lead + (BH, BDK))
    w = jnp.pad(w, [(0, 0)] * len(lead) + [(0, 0), (0, HP - BDK)], constant_values=fill)
    return w.reshape(lead + (BW,))


IN_B0 = 2 * AW
IN_C0 = IN_B0 + 5 * NBW
REPACK_ROWS = 256


def _in_proj_plan():
    plan = [("b", 0, AW, 0, 1.0), ("f", AW, AW, 0, 0.5),
            ("f", IN_C0, 2 * CW, AW + 2 * BW, 1.0), ("f", IN_C0 + 2 * CW, CW, AW + 2 * BW + 2 * CW, 0.5)]
    for g, (dst, slot, scale) in enumerate((("b", 0, 0.5), ("f", 0, 0.5), ("b", 1, 0.5), ("b", 2, 1.0), ("f", 1, 0.5))):
        for h in range(BH):
            plan.append((dst, IN_B0 + g * NBW + h * BDK, BDK, AW + slot * BW + h * HP, scale))
    return plan


def _repack_in_body(w_ref, wb_ref, wf_ref):
    out = {"b": wb_ref, "f": wf_ref}
    for dst, src0, width, dst0, scale in _in_proj_plan():
        chunk = w_ref[0, :, src0:src0 + width]
        out[dst][0, :, dst0:dst0 + width] = (chunk if scale == 1.0 else chunk * scale).astype(BF16)
        if width == BDK:
            out[dst][0, :, dst0 + BDK:dst0 + HP] = jnp.zeros((REPACK_ROWS, HP - BDK), BF16)


def _repack_out_body(w_ref, wo_ref):
    for h in range(BH):
        wo_ref[0, h * HP:h * HP + BDK, :] = w_ref[0, AW + h * BDK:AW + (h + 1) * BDK, :].astype(BF16)
        wo_ref[0, h * HP + BDK:(h + 1) * HP, :] = jnp.zeros((HP - BDK, D), BF16)
    wo_ref[0, BW:BW + AW, :] = w_ref[0, 0:AW, :].astype(BF16)
    wo_ref[0, BW + AW:YW, :] = w_ref[0, AW + NBW:, :].astype(BF16)


def _repack_projections(w_in, w_out):
    depth = w_in.shape[0]
    wb, wf = pl.pallas_call(
        _repack_in_body, grid=(depth, D // REPACK_ROWS),
        in_specs=[pl.BlockSpec((1, REPACK_ROWS, w_in.shape[2]), lambda l, i: (l, i, 0))],
        out_specs=(pl.BlockSpec((1, REPACK_ROWS, KB_COLS), lambda l, i: (l, i, 0)),
                   pl.BlockSpec((1, REPACK_ROWS, KF_COLS), lambda l, i: (l, i, 0))),
        out_shape=(jax.ShapeDtypeStruct((depth, D, KB_COLS), BF16), jax.ShapeDtypeStruct((depth, D, KF_COLS), BF16)),
        name="repack_in_proj")(w_in)
    wo = pl.pallas_call(
        _repack_out_body, grid=(depth,),
        in_specs=[pl.BlockSpec((1,) + w_out.shape[1:], lambda l: (l, 0, 0))],
        out_specs=pl.BlockSpec((1, YW, D), lambda l: (l, 0, 0)),
        out_shape=jax.ShapeDtypeStruct((depth, YW, D), BF16),
        name="repack_out_proj")(w_out)
    return wb, wf, wo


def _prepare_weights(norm_w, w_in, a_conv_w, a_conv_b, a_wr, a_br, a_wi, a_bi, a_lambda, lbs, b_norm_w,
                     c_norm_w, c_ws, c_bs, w_out):
    depth = w_in.shape[0]
    wb, wf, wo = _repack_projections(w_in, w_out)
    eye = jnp.eye(A_HEADS, dtype=F32)

    def block_diag(w):
        return jnp.einsum("ldhij,hg->ldhigj", w, eye).reshape(depth, 2, AW, AW)
    return {
        "nw": norm_w[:, None, :],
        "wb": wb, "wf": wf, "wo": wo,
        "cw": a_conv_w, "cb": a_conv_b[:, None, :],
        "lb": _pad_heads(lbs[:depth], fill=1.0)[:, None, :],
        "bnw": _pad_heads(b_norm_w)[:, None, :],
        "cnw": c_norm_w[:, None, :],
        "cws": c_ws.reshape(depth, CH * CCH, CCH).astype(BF16),
        "cbias": jnp.repeat(jnp.swapaxes(c_bs, 1, 2), CHD, axis=2),
        "wg": (0.5 * jnp.concatenate([block_diag(a_wr), block_diag(a_wi)], axis=-1)).astype(BF16),
        "bg": 0.5 * jnp.concatenate([a_br, a_bi], axis=-1)[:, :, None, :],
        "cl": (-0.5 * RG_C * jax.nn.softplus(-a_lambda))[:, :, None, :],
    }


FWD, BWD = 0, 1
PER_DIRECTION = ("wg", "bg", "cl")


def _weight_operands(weights, names, layer, direction):
    arrays, specs = [], []
    for n in names:
        w = weights[n]
        lead = (layer, direction) if n in PER_DIRECTION else (layer,)
        rest = w.shape[len(lead):]
        arrays.append(w)
        specs.append(pl.BlockSpec((None,) * len(lead) + rest,
                                  lambda ib, ic, lead=lead, rest=rest: lead + (0,) * len(rest),
                                  pipeline_mode=pl.Buffered(1)))
    return arrays, specs


def _pos_tables(rows):
    qd = D // 4
    omega = 1.0 / (10000.0 ** (jnp.arange(qd, dtype=F32) / qd))
    ar = jnp.arange(rows, dtype=F32)[:, None] * omega
    ac = jnp.arange(GRID_W, dtype=F32)[:, None] * omega
    rowtab = jnp.concatenate([jnp.sin(ar), jnp.cos(ar)], axis=-1)
    coltab = jnp.concatenate([jnp.sin(ac), jnp.cos(ac)], axis=-1)
    nc = rows * GRID_W // T
    hrow = jnp.broadcast_to(rowtab[jnp.maximum(jnp.arange(nc) * NSEG - 1, 0)][:, None, :], (nc, SUBLANES, D // 2))
    hcol = jnp.broadcast_to(coltab[GRID_W - SUBLANES:][None], (nc, SUBLANES, D // 2))
    return rowtab, coltab, jnp.concatenate([hrow, hcol], axis=-1)


def kernel(x, c, ctx, c_ctx, norm_w, w_mod, b_mod, w_in, a_conv_w, a_conv_b, a_wr, a_br, a_wi, a_bi, a_lambda,
           b_lb_logits, b_norm_w, c_norm_w, c_ws, c_bs, w_out, final_norm_w):
    bsz, n_lat, _ = x.shape
    depth = w_in.shape[0]
    assert n_lat % T == 0 and T % ctx.shape[1] == 0 and bsz % (T // ctx.shape[1]) == 0
    nb_ctx = T // ctx.shape[1]
    pos = _pos_tables(n_lat // GRID_W)

    mrows = 2 * SUBLANES
    cs = jnp.zeros((mrows, D), F32).at[:bsz].set(c).at[bsz].set(c_ctx)
    nblk = 3
    mod = pl.pallas_call(
        _mod_body, grid=(depth, nblk),
        in_specs=[pl.BlockSpec((mrows, D), lambda l, j: (0, 0)),
                  pl.BlockSpec((1, D, D), lambda l, j: (l, 0, j)),
                  pl.BlockSpec((1, 1, D), lambda l, j: (l, 0, j))],
        out_specs=pl.BlockSpec((1, mrows, D), lambda l, j: (l, 0, j)),
        out_shape=jax.ShapeDtypeStruct((depth, mrows, 3 * D), F32),
        name="modulation")(cs, w_mod, b_mod[:, None, :])
    mod = mod.reshape(depth, mrows, 1, 3 * D)

    lbs = jnp.cumsum(jax.nn.softmax(b_lb_logits, axis=0), axis=0)
    fnw = final_norm_w[None, :]
    h_zero = jnp.zeros((bsz, 1, AW), F32)
    s_zero = jnp.zeros((bsz, BH, HP, HP), F32)
    weights = _prepare_weights(norm_w, w_in, a_conv_w, a_conv_b, a_wr, a_br, a_wi, a_bi, a_lambda, lbs, b_norm_w,
                               c_norm_w, c_ws, c_bs, w_out)
    for l in range(depth):
        lpos = pos if l == 0 else None
        last = l == depth - 1
        kb = _run_kb(ctx, mod[l], bsz, weights, l, h_zero, s_zero, None, nb_ctx, want_out=not last)
        hb_c, sb_c = kb[-2], kb[-1]
        if last:
            hbf_c, xc_c, v_c = kb[:3]
            hf_c, sf_c = _run_kf_states(hbf_c, xc_c, v_c, weights, l, h_zero, s_zero, nb_ctx)
        else:
            ctx, hf_c, sf_c = _run_kf(ctx, kb[:6], mod[l], bsz, weights, l, fnw, h_zero, s_zero, None, nb_ctx, False)
        kb = _run_kb(x, mod[l], None, weights, l, hb_c, sb_c, lpos, 1)
        x, _, _ = _run_kf(x, kb[:6], mod[l], None, weights, l, fnw, hf_c, sf_c, lpos, 1, last)
    return x
```

```python
import functools

import jax
import jax.numpy as jnp
import numpy as np
from jax import lax
from jax.experimental import pallas as pl
from jax.experimental.pallas import tpu as pltpu

D = 1024
GRID_W = 64
EPS = 1e-6
RG_C = 8.0
AW = 384
A_HEADS = 8
A_HD = 48
BH = 4
BDK = 96
CW = 256
CH = 4
CHD = 64
CCH = 128

LANES = 128
SUBLANES = 8
HP = LANES
BW = BH * HP
NBW = BH * BDK
YW = AW + BW + CW
NA = AW // LANES
T = 512
SEG = 64
NSEG = T // SEG
PITCH = SEG + SUBLANES
KB_COLS = AW + 3 * BW
KF_COLS = AW + 2 * BW + 3 * CW
FILL_COLS = 512
FILL_ROWS = T // 2
ROW_TILE = 16
MLP_TILE = 32
VMEM_LIMIT = 58 * 1024 * 1024

F32 = jnp.float32
BF16 = jnp.bfloat16

C_CONV_SEG, C_COEFF_TILE, C_SCAN_STEP, C_FIX_TILE, C_YA_TILE = 100, 50, 12, 8, 15
C_GATES_TILE, C_CUMSUM_STEP, C_GLA_A_TILE, C_GLA_C_TILE, C_SILU_TILE = 20, 8, 40, 15, 12
C_PRE_TILE, C_MLP_TILE, C_GLA_B_TILE, C_GLA_D_TILE = 25, 90, 40, 40
C_PROJ_PIECE, C_OUT_PIECE, C_NORM_PIECE = FILL_ROWS * FILL_COLS // 256, 3 * FILL_ROWS * FILL_COLS // 1024, 130
PACE = 1.4


def _silu_of_twice(hx):
    return hx * (jnp.tanh(hx) + 1.0)


def _silu(x):
    return _silu_of_twice(0.5 * x)


def _gelu(x):
    c = np.float32(np.sqrt(2.0 / np.pi))
    hx = 0.5 * x
    return hx * (jnp.tanh(x * (c + (0.044715 * c) * (x * x))) + 1.0)


def _nt(a, b):
    return lax.dot_general(a, b, (((1,), (1,)), ((), ())), preferred_element_type=F32)


def _tn(a, b):
    return lax.dot_general(a, b, (((0,), (0,)), ((), ())), preferred_element_type=F32)


def _norm_mod(x, nw, sh, sc):
    ms = jnp.mean(x * x, axis=-1, keepdims=True)
    return (x * lax.rsqrt(ms + EPS)) * (nw * (1.0 + sc)) + sh


def _pos_rows(rt_ref, ct_ref, r0):
    s, c0 = divmod(r0, SEG)
    return jnp.concatenate([jnp.broadcast_to(rt_ref[s:s + 1, :], (ROW_TILE, D // 2)),
                            ct_ref[c0:c0 + ROW_TILE, :]], axis=1)


class _Fill:
    def __init__(self, items=(), rate=1.0):
        self.items = list(items)
        self.rate = rate
        self.credit = 0.0
        self.spent = 0.0

    def __call__(self, credit):
        self.credit += credit
        while self.items and self.spent + 0.5 * self.items[0][0] <= self.credit * self.rate:
            cost, thunk = self.items.pop(0)
            self.spent += cost
            thunk()

    def extend(self, items):
        self.items.extend(items)

    def flush(self):
        while self.items:
            self.items.pop(0)[1]()


def _cost(items):
    return sum(c for c, _ in items)


def _emit(items, fill=None):
    for cost, thunk in items:
        thunk()
        if fill is not None:
            fill(cost)


def _paired(items, partner_items):
    partner = _Fill(partner_items, _cost(partner_items) / max(_cost(items), 1))
    _emit(items, partner)
    partner.flush()


class _Flat:
    def __init__(self, ref, width):
        self.ref = ref
        self.rs = ref.shape[1]
        self.width = width

    def _split(self, idx):
        rows, lanes = idx
        start, size = rows.start, rows.size
        q, off = divmod(start, self.rs)
        assert off + size <= self.rs
        return q, pl.ds(off, size), lanes

    def __getitem__(self, idx):
        q, r, lanes = self._split(idx)
        return self.ref[q, r, lanes]

    def __setitem__(self, idx, val):
        q, r, lanes = self._split(idx)
        self.ref[q, r, lanes] = val


def _tiles():
    return [(s, k) for s in range(NSEG) for k in range(BH)]


class _AScan:
    def __init__(self, a_st, u_st, h_st, p_st, hc_ref, nb, reverse):
        self.refs = (a_st, u_st, h_st, p_st)
        self.hc_ref, self.nb, self.reverse = hc_ref, nb, reverse
        self.hs = [jnp.zeros((NSEG, LANES), F32)] * NA
        self.ps = [jnp.ones((NSEG, LANES), F32)] * NA

    def coeff_items(self, g_scr, xc_ref, cl_ref):
        a_st, u_st = self.refs[:2]

        def item(s, k):
            def run():
                rows, lanes = pl.ds(s * SEG, SEG), slice(k * LANES, (k + 1) * LANES)
                tr = jnp.tanh(g_scr[rows, lanes])
                ti = jnp.tanh(g_scr[rows, AW + k * LANES:AW + (k + 1) * LANES])
                hcl = cl_ref[:, lanes]
                la = hcl * tr + hcl
                th = jnp.tanh(la)
                om4 = (-0.5 * th) / (1.0 - th)
                a_st[k, pl.ds(s * PITCH, SEG), :] = jnp.exp(la)
                root = jnp.where(om4 > 0.0, om4 * lax.rsqrt(om4), 0.0)
                u_st[k, pl.ds(s * PITCH, SEG), :] = root * ((ti + 1.0) * xc_ref[rows, lanes])
            return C_COEFF_TILE, run
        return [item(s, k) for s in range(NSEG) for k in range(NA)]

    def step_items(self):
        a_st, u_st, h_st, p_st = self.refs

        def item(j):
            def run():
                for k in range(NA):
                    a = a_st[k, pl.ds(j, NSEG, stride=PITCH), :]
                    u = u_st[k, pl.ds(j, NSEG, stride=PITCH), :]
                    self.hs[k] = a * self.hs[k] + u
                    self.ps[k] = a * self.ps[k]
                    h_st[k, pl.ds(j, NSEG, stride=PITCH), :] = self.hs[k]
                    p_st[k, pl.ds(j, NSEG, stride=PITCH), :] = self.ps[k]
            return C_SCAN_STEP, run
        order = range(SEG - 1, -1, -1) if self.reverse else range(SEG)
        return [item(j) for j in order]

    def chain(self):
        spq = NSEG // self.nb
        cin = [None] * NSEG
        for q in range(self.nb):
            c = [self.hc_ref[q:q + 1, k * LANES:(k + 1) * LANES] for k in range(NA)]
            order = range((q + 1) * spq - 1, q * spq - 1, -1) if self.reverse else range(q * spq, (q + 1) * spq)
            for s in order:
                cin[s] = c
                c = [self.ps[k][s:s + 1, :] * c[k] + self.hs[k][s:s + 1, :] for k in range(NA)]
            for k in range(NA):
                self.hc_ref[q:q + 1, k * LANES:(k + 1) * LANES] = c[k]
        return cin

    def full(self, cin, s, k):
        _, _, h_st, p_st = self.refs
        st_rows = pl.ds(s * PITCH, SEG)
        return h_st[k, st_rows, :] + p_st[k, st_rows, :] * cin[s][k]


class _Gla:
    def __init__(self, reverse, nb, k_scr, lf_st, g_st, s_scr, stage, states_only=False):
        self.reverse, self.nb, self.states_only = reverse, nb, states_only
        self.k_scr, self.lf_st, self.g_st, self.s_scr = k_scr, lf_st, g_st, s_scr
        self.kg_s, self.ke_s, self.lhs_s, self.kv_s, self.rhs_s = stage
        self.gs = [jnp.zeros((NSEG, LANES), F32)] * BH
        order = range(NSEG - 1, -1, -1) if reverse else range(NSEG)
        self.order = list(order)
        self.tiles = [(n, k) for n in order for k in range(BH)]

    def gate_items(self, p_scr, col0, lb_ref):
        def item(s, k):
            def run():
                rows, lanes = pl.ds(s * SEG, SEG), slice(k * HP, (k + 1) * HP)
                lb = lb_ref[:, lanes]
                c0, c1 = 0.5 + 0.5 * lb, 0.5 - 0.5 * lb
                ct = c1 * jnp.tanh(p_scr[rows, col0 + k * HP:col0 + (k + 1) * HP])
                self.k_scr[rows, lanes] = c1 - ct
                self.lf_st[k, pl.ds(s * PITCH, SEG), :] = jnp.log2(c0 + ct)
            return C_GATES_TILE, run
        return [item(s, k) for s, k in _tiles()]

    def cumsum_items(self):
        def item(j):
            def run():
                for k in range(BH):
                    self.gs[k] = self.gs[k] + self.lf_st[k, pl.ds(j, NSEG, stride=PITCH), :]
                    self.g_st[k, pl.ds(j, NSEG, stride=PITCH), :] = self.gs[k]
            return C_CUMSUM_STEP, run
        order = range(SEG - 1, -1, -1) if self.reverse else range(SEG)
        return [item(j) for j in order]

    def a_items(self, load_q, load_v):
        def item(n, k):
            def run():
                rows, lanes = pl.ds(n * SEG, SEG), slice(k * HP, (k + 1) * HP)
                g = self.g_st[k, pl.ds(n * PITCH, SEG), :]
                tot = self.gs[k][n:n + 1, :]
                kk = self.k_scr[rows, lanes]
                if self.states_only:
                    self.ke_s[rows, lanes] = (kk * jnp.exp2(tot - g)).astype(BF16)
                else:
                    self.lhs_s[k, rows, 0:HP] = (load_q(n, k) * jnp.exp2(g)).astype(BF16)
                    self.rhs_s[n, k, HP:HP + SEG, :] = load_v(n, k)
                    kg = kk * jnp.exp2(-g)
                    self.kg_s[rows, lanes] = kg.astype(BF16)
                    self.ke_s[rows, lanes] = (kg * jnp.exp2(tot)).astype(BF16)
            return C_GLA_A_TILE, run
        return [item(n, k) for n, k in self.tiles]

    def b_items(self, load_v):
        r_i = lax.broadcasted_iota(jnp.int32, (SEG, SEG), 0)
        c_i = lax.broadcasted_iota(jnp.int32, (SEG, SEG), 1)
        keep = (c_i >= r_i) if self.reverse else (c_i <= r_i)

        def item(n, k):
            def run():
                rows, lanes = pl.ds(n * SEG, SEG), slice(k * HP, (k + 1) * HP)
                if not self.states_only:
                    self.lhs_s[k, rows, HP:HP + SEG] = jnp.where(
                        keep, _nt(self.lhs_s[k, rows, 0:HP], self.kg_s[rows, lanes]), 0.0).astype(BF16)
                self.kv_s[n, k] = _tn(load_v(n, k), self.ke_s[rows, lanes])
            return C_GLA_B_TILE, run
        return [item(n, k) for n, k in self.tiles]

    def c_items(self):
        cps = NSEG // self.nb
        state = {}

        def item(n, k):
            def run():
                sq = n // cps
                st = state.get(k)
                if st is None:
                    st = self.s_scr[sq, k]
                if not self.states_only:
                    self.rhs_s[n, k, 0:HP, :] = st.T.astype(BF16)
                st = jnp.exp2(self.gs[k][n:n + 1, :]) * st + self.kv_s[n, k]
                last = (n % cps == 0) if self.reverse else (n % cps == cps - 1)
                if last:
                    self.s_scr[sq, k] = st
                    st = None
                state[k] = st
            return C_GLA_C_TILE, run
        return [item(n, k) for n in self.order for k in range(BH)]

    def d_items(self, emit_out):
        def item(n, k):
            def run():
                rows, lanes = pl.ds(n * SEG, SEG), slice(k * HP, (k + 1) * HP)
                o = jnp.dot(self.lhs_s[k, rows, :], self.rhs_s[n, k], preferred_element_type=F32)
                emit_out(n, k, o)
            return C_GLA_D_TILE, run
        return [item(n, k) for n, k in self.tiles]


def _gla_scratch():
    return ([pltpu.VMEM((T, BW), BF16)] * 2
            + [pltpu.VMEM((BH, T, HP + SEG), BF16), pltpu.VMEM((NSEG, BH, HP, HP), F32),
               pltpu.VMEM((NSEG, BH, HP + SEG, HP), BF16)])


def _dot_items(lhs_rows, w_ref, dst, col_lo, col_hi, cost):
    def piece(r0, c0, c1):
        def run():
            dst[r0:r0 + FILL_ROWS, c0:c1] = jnp.dot(lhs_rows(r0, r0 + FILL_ROWS), w_ref[:, c0:c1],
                                                    preferred_element_type=F32)
        return cost, run
    return [piece(r0, c0, min(c0 + FILL_COLS, col_hi))
            for c0 in range(col_lo, col_hi, FILL_COLS) for r0 in range(0, T, FILL_ROWS)]


def _kb_body(*refs, nb, nc, add_pos, want_out):
    it = iter(refs)
    x0_ref, xn_ref = next(it), next(it)
    xh_ref = next(it) if nc > 1 else None
    if add_pos:
        rt0_ref, rtn_ref, ct_ref, hp_ref = next(it), next(it), next(it), next(it)
    else:
        rt0_ref = rtn_ref = ct_ref = hp_ref = None
    (mod_ref, modn_ref, nw_ref, wb_ref, cw_ref, cb_ref, wg_ref, bg_ref, cl_ref, lb_ref, h0_ref, s0_ref,
     hbf_ref, xc_ref) = (next(it) for _ in range(14))
    hb_ref, ob_ref, q_ref = (next(it), next(it), next(it)) if want_out else (None, None, None)
    (v_ref, hfin_ref, sfin_ref,
     p_scr, h_scr, xa_ext, g_scr, k_scr, a_st, u_st, h_st, p_st, lf_st, g_st, hc_ref, s_scr, xaf,
     *gla_stage) = it

    ic = pl.program_id(1)
    flat = pl.program_id(0) * nc + ic
    cc = nc - 1 - ic

    @pl.when(ic == 0)
    def _():
        for q in range(nb):
            hc_ref[q:q + 1, :] = h0_ref[q]
        s_scr[...] = s0_ref[...]
        xaf[...] = jnp.zeros_like(xaf)

    def norm_items(x_r, rt_r, m_r):
        x_flat = _Flat(x_r, D)

        def item(s):
            def run():
                rows = pl.ds(s * SEG, SEG)
                xin = x_flat[rows, slice(None)]
                if add_pos:
                    xin = xin + jnp.concatenate(
                        [_pos_rows(rt_r, ct_ref, r0) for r0 in range(s * SEG, (s + 1) * SEG, ROW_TILE)], axis=0)
                h_scr[rows, :] = _norm_mod(xin, nw_ref[...], m_r[0, :, 0:D], m_r[0, :, D:2 * D]).astype(BF16)
            return C_NORM_PIECE, run
        return [item(s) for s in range(NSEG)]

    def proj_items():
        return _dot_items(lambda r0, r1: h_scr[r0:r1, :], wb_ref, p_scr, 0, KB_COLS, C_PROJ_PIECE)

    gla = _Gla(True, nb, k_scr, lf_st, g_st, s_scr, gla_stage, states_only=not want_out)
    v_flat = _Flat(v_ref, BW)
    if want_out:
        q_flat, ob_flat = _Flat(q_ref, BW), _Flat(ob_ref, BW)

    def hgrn_input_items():
        def qv_item(s, k):
            def run():
                rows, lanes = pl.ds(s * SEG, SEG), slice(k * HP, (k + 1) * HP)
                if want_out:
                    q_flat[rows, lanes] = _silu_of_twice(p_scr[rows, AW + k * HP:AW + (k + 1) * HP])
                v_flat[rows, lanes] = p_scr[rows, AW + 2 * BW + k * HP:AW + 2 * BW + (k + 1) * HP].astype(BF16)
            return C_PRE_TILE, run
        return [qv_item(s, k) for s, k in _tiles()] + gla.gate_items(p_scr, AW + BW, lb_ref)

    @pl.when(flat == 0)
    def _():
        _emit(norm_items(x0_ref, rt0_ref, mod_ref) + proj_items())

    def step():
        xc_flat = _Flat(xc_ref, AW)
        hb_flat = _Flat(hb_ref, AW) if want_out else None
        scan = _AScan(a_st, u_st, h_st, p_st, hc_ref, nb, reverse=True)

        hbf_ref[...] = h_scr[...].reshape(hbf_ref.shape)
        if nc > 1:
            xh = xh_ref[0]
            if add_pos:
                xh = xh + hp_ref[0]
            hh = _norm_mod(xh, nw_ref[...], mod_ref[0, :, 0:D], mod_ref[0, :, D:2 * D]).astype(BF16)
            left = jnp.dot(hh, wb_ref[:, 0:AW], preferred_element_type=F32)
            xa_ext[0:SUBLANES, :] = jnp.where(cc > 0, left, 0.0)
        else:
            xa_ext[0:SUBLANES, :] = jnp.zeros((SUBLANES, AW), F32)
        xa_ext[pl.ds(SUBLANES, T), :] = p_scr[:, 0:AW]
        xa_ext[pl.ds(SUBLANES + T, SUBLANES), :] = xaf[...]
        xaf[...] = p_scr[0:SUBLANES, 0:AW]

        def conv_item(s):
            def run():
                spq = NSEG // nb
                row = lax.broadcasted_iota(jnp.int32, (SEG, 1), 0)
                acc = jnp.broadcast_to(cb_ref[...], (SEG, AW))
                for j in range(4):
                    tap = xa_ext[pl.ds(s * SEG + SUBLANES - 2 + j, SEG), :]
                    if nb > 1 and s % spq == 0 and j < 2:
                        tap = jnp.where(row + (j - 2) >= 0, tap, 0.0)
                    if nb > 1 and s % spq == spq - 1 and j == 3:
                        tap = jnp.where(row + 1 < SEG, tap, 0.0)
                    acc = acc + cw_ref[j:j + 1, :] * tap
                xc_flat[pl.ds(s * SEG, SEG), slice(None)] = acc
            return C_CONV_SEG, run

        def load_q(n, k):
            return q_flat[pl.ds(n * SEG, SEG), slice(k * HP, (k + 1) * HP)]

        def load_v(n, k):
            return v_flat[pl.ds(n * SEG, SEG), slice(k * HP, (k + 1) * HP)]

        def emit_out(n, k, o):
            ob_flat[pl.ds(n * SEG, SEG), slice(k * HP, (k + 1) * HP)] = o

        _emit(hgrn_input_items())
        _emit(norm_items(xn_ref, rtn_ref, modn_ref))

        conv = [conv_item(s) for s in range(NSEG)]
        coeff, steps, cums = scan.coeff_items(g_scr, xc_flat, cl_ref), scan.step_items(), gla.cumsum_items()
        proj = proj_items()
        credit = (_cost(conv) + _cost(coeff) + _cost(steps) + _cost(cums)
                  + BH * NSEG * (C_GLA_A_TILE + C_GLA_C_TILE) + NA * NSEG * C_FIX_TILE)
        mx = _Fill(proj, PACE * _cost(proj) / credit)
        _emit(conv, mx)
        g_scr[...] = jnp.dot(xc_ref[...].reshape(T, AW).astype(BF16), wg_ref[...],
                             preferred_element_type=F32) + bg_ref[...]
        _emit(coeff, mx)
        _emit(steps, mx)
        cin = scan.chain()
        for q in range(nb):
            hfin_ref[q] = hc_ref[q:q + 1, :]

        def fix_item(s, k):
            def run():
                hb_flat[pl.ds(s * SEG, SEG), k * LANES:(k + 1) * LANES] = scan.full(cin, s, k)
            return C_FIX_TILE, run
        if want_out:
            _emit([fix_item(s, k) for s in range(NSEG) for k in range(NA)], mx)

        _emit(cums, mx)
        _emit(gla.a_items(load_q, load_v), mx)
        _emit(gla.b_items(load_v))
        _emit(gla.c_items(), mx)
        if want_out:
            _emit(gla.d_items(emit_out))
        sfin_ref[...] = s_scr[...]
        mx.flush()

    step()


def _kf_body(*refs, nb, nc, add_pos, final):
    it = iter(refs)
    x_ref = next(it)
    if add_pos:
        rt_ref, ct_ref = next(it), next(it)
    (hbf0_ref, hbfn_ref, xc_ref, hb_ref, ob_ref, q_ref, v_ref, mod_ref, wf_ref, wg_ref, bg_ref, cl_ref, lb_ref,
     bnw_ref, cnw_ref, cws_ref, cbias_ref, wo_ref, fnw_ref, h0_ref, s0_ref,
     xo_ref, hfin_ref, sfin_ref,
     p_scr, sg_scr, hn_scr, g_scr, k_scr, y_scr, part_scr, a_st, u_st, h_st, p_st, lf_st, g_st, hc_ref, s_scr,
     xcb_scr, vn_scr, mix_scr, *gla_stage) = it

    ic = pl.program_id(1)
    flat = pl.program_id(0) * nc + ic

    @pl.when(ic == 0)
    def _():
        for q in range(nb):
            hc_ref[q:q + 1, :] = h0_ref[q]
        s_scr[...] = s0_ref[...]

    o_ag, o_ff, o_bg = 0, AW, AW + BW
    o_cu, o_cv, o_cg = AW + 2 * BW, AW + 2 * BW + CW, AW + 2 * BW + 2 * CW

    def proj_items(col_lo, col_hi):
        return _dot_items(lambda r0, r1: hn_scr[r0:r1, :], wf_ref, p_scr, col_lo, col_hi, C_PROJ_PIECE)

    @pl.when(flat == 0)
    def _():
        hn_scr[...] = hbf0_ref[...].reshape(T, D)
        _emit(proj_items(0, KF_COLS))

    def step():
        xc_flat, hb_flat = _Flat(xc_ref, AW), _Flat(hb_ref, AW)
        q_flat, v_flat, ob_flat = _Flat(q_ref, BW), _Flat(v_ref, BW), _Flat(ob_ref, BW)
        scan = _AScan(a_st, u_st, h_st, p_st, hc_ref, nb, reverse=False)
        gla = _Gla(False, nb, k_scr, lf_st, g_st, s_scr, gla_stage)

        hn_scr[...] = hbfn_ref[...].reshape(T, D)
        for s in range(NSEG):
            rows = pl.ds(s * SEG, SEG)
            xcb_scr[rows, :] = xc_flat[rows, slice(None)].astype(BF16)

        def silu_item(s, c0):
            def run():
                rows = pl.ds(s * SEG, SEG)
                dst = c0 - o_ag if c0 < o_ff else AW + c0 - o_bg
                sg_scr[rows, dst:dst + LANES] = _silu_of_twice(p_scr[rows, c0:c0 + LANES])
            return C_SILU_TILE, run
        gate_cols = list(range(o_ag, o_ag + AW, LANES)) + list(range(o_bg, o_bg + BW, LANES))

        def mlp_items(m):
            lane = lax.broadcasted_iota(jnp.int32, (MLP_TILE, CW), 1)

            def norm_item(r0):
                def run():
                    v = _gelu(p_scr[pl.ds(m * CCH + r0, MLP_TILE), o_cv:o_cv + CW])
                    mu = jnp.mean(v, axis=-1, keepdims=True)
                    dv = v - mu
                    var = jnp.mean(dv * dv, axis=-1, keepdims=True)
                    vn_scr[r0:r0 + MLP_TILE, :] = (dv * lax.rsqrt(var + EPS) * cnw_ref[...]).astype(BF16)
                return C_MLP_TILE, run

            def mix():
                mix_scr[...] = jnp.dot(cws_ref[...], vn_scr[...], preferred_element_type=F32)

            def out_item(r0):
                def run():
                    rows = pl.ds(m * CCH + r0, MLP_TILE)
                    mixed = mix_scr[r0:r0 + MLP_TILE, :]
                    for h in range(1, CH):
                        mixed = jnp.where(lane >= h * CHD, mix_scr[h * CCH + r0:h * CCH + r0 + MLP_TILE, :], mixed)
                    u = _gelu(p_scr[rows, o_cu:o_cu + CW])
                    yc = (u * (mixed + cbias_ref[r0:r0 + MLP_TILE, :])
                          * _silu_of_twice(p_scr[rows, o_cg:o_cg + CW]))
                    y_scr[rows, BW + AW:YW] = yc.astype(BF16)
                return C_MLP_TILE, run
            tiles = range(0, CCH, MLP_TILE)
            return [norm_item(r0) for r0 in tiles] + [(0, mix)] + [out_item(r0) for r0 in tiles]

        def load_q(n, k):
            return q_flat[pl.ds(n * SEG, SEG), slice(k * HP, (k + 1) * HP)]

        def load_v(n, k):
            return v_flat[pl.ds(n * SEG, SEG), slice(k * HP, (k + 1) * HP)]

        def emit_out(n, k, o):
            rows, lanes = pl.ds(n * SEG, SEG), slice(k * HP, (k + 1) * HP)
            o = o + ob_flat[rows, lanes]
            ms = jnp.sum(o * o, axis=-1, keepdims=True) * (1.0 / BDK)
            yb = (o * lax.rsqrt(ms + EPS) * bnw_ref[:, lanes]) * sg_scr[rows, AW + k * HP:AW + (k + 1) * HP]
            y_scr[rows, lanes] = yb.astype(BF16)

        g_scr[...] = jnp.dot(xcb_scr[...], wg_ref[...], preferred_element_type=F32) + bg_ref[...]
        _emit([it for m in range(T // CCH) for it in mlp_items(m)])

        pre = [silu_item(s, c0) for s in range(NSEG) for c0 in gate_cols] + gla.gate_items(p_scr, o_ff, lb_ref)
        coeff, steps, cums = scan.coeff_items(g_scr, xc_flat, cl_ref), scan.step_items(), gla.cumsum_items()
        n_proj = -(-KF_COLS // FILL_COLS) * (T // FILL_ROWS)
        n_out = (D // FILL_COLS) * (T // FILL_ROWS)
        credit = (_cost(pre) + _cost(coeff) + _cost(steps) + _cost(cums)
                  + BH * NSEG * (C_GLA_A_TILE + C_GLA_C_TILE + C_GLA_D_TILE) + NA * NSEG * C_YA_TILE)
        mx = _Fill(proj_items(o_cu, KF_COLS), PACE * (n_proj * C_PROJ_PIECE + n_out * C_OUT_PIECE) / credit)
        _emit(pre, mx)
        mx.extend(proj_items(0, o_cu))

        _emit(coeff, mx)
        _emit(steps, mx)
        cin = scan.chain()
        for q in range(nb):
            hfin_ref[q] = hc_ref[q:q + 1, :]

        def ya_item(s, k):
            def run():
                rows, lanes = pl.ds(s * SEG, SEG), slice(k * LANES, (k + 1) * LANES)
                ya = (scan.full(cin, s, k) + hb_flat[rows, lanes]) * sg_scr[rows, lanes]
                y_scr[rows, BW + k * LANES:BW + (k + 1) * LANES] = ya.astype(BF16)
            return C_YA_TILE, run
        _emit([ya_item(s, k) for s in range(NSEG) for k in range(NA)], mx)
        mx.extend(_dot_items(lambda r0, r1: y_scr[r0:r1, BW:YW], wo_ref.at[BW:YW, :], part_scr, 0, D, C_OUT_PIECE))

        _emit(cums, mx)
        _emit(gla.a_items(load_q, load_v), mx)
        _emit(gla.b_items(load_v))
        _emit(gla.c_items(), mx)
        _emit(gla.d_items(emit_out), mx)
        sfin_ref[...] = s_scr[...]
        mx.flush()

        gm = mod_ref[0, :, 2 * D:3 * D]
        x_flat, xo_flat = _Flat(x_ref, D), _Flat(xo_ref, D)
        def out_item(r0, c0):
            def run():
                rows, cols = pl.ds(r0, FILL_ROWS), slice(c0, min(c0 + FILL_COLS, D))
                part_scr[rows, cols] = part_scr[rows, cols] + jnp.dot(y_scr[rows, 0:BW], wo_ref[0:BW, cols],
                                                                      preferred_element_type=F32)
            return C_OUT_PIECE, run

        def residual_item(r0):
            def run():
                rows = pl.ds(r0, ROW_TILE)
                xin = x_flat[rows, slice(None)]
                if add_pos:
                    xin = xin + _pos_rows(rt_ref, ct_ref, r0)
                xn = xin + gm * part_scr[rows, :]
                if final:
                    ms = jnp.mean(xn * xn, axis=-1, keepdims=True)
                    xn = xn * lax.rsqrt(ms + EPS) * fnw_ref[...]
                xo_flat[rows, slice(None)] = xn
            return C_OUT_PIECE * (D // FILL_COLS) * ROW_TILE // FILL_ROWS, run

        pending = []
        for r0 in range(0, T, FILL_ROWS):
            _paired([out_item(r0, c0) for c0 in range(0, D, FILL_COLS)], pending)
            pending = [residual_item(r) for r in range(r0, r0 + FILL_ROWS, ROW_TILE)]
        _emit(pending)

    step()


def _kf_states_body(hbf0_ref, hbfn_ref, xc_ref, v_ref, wf_ref, wg_ref, bg_ref, cl_ref, lb_ref, h0_ref, s0_ref,
                    hfin_ref, sfin_ref,
                    p_scr, hn_scr, g_scr, k_scr, a_st, u_st, h_st, p_st, lf_st, g_st, hc_ref, s_scr, xcb_scr,
                    *gla_stage, nb, nc):
    ic = pl.program_id(1)
    flat = pl.program_id(0) * nc + ic

    @pl.when(ic == 0)
    def _():
        for q in range(nb):
            hc_ref[q:q + 1, :] = h0_ref[q]
        s_scr[...] = s0_ref[...]

    def proj_items():
        return _dot_items(lambda r0, r1: hn_scr[r0:r1, :], wf_ref.at[:, AW:AW + BW], p_scr, 0, BW, C_PROJ_PIECE)

    @pl.when(flat == 0)
    def _():
        hn_scr[...] = hbf0_ref[...].reshape(T, D)
        _emit(proj_items())

    xc_flat, v_flat = _Flat(xc_ref, AW), _Flat(v_ref, BW)
    scan = _AScan(a_st, u_st, h_st, p_st, hc_ref, nb, reverse=False)
    gla = _Gla(False, nb, k_scr, lf_st, g_st, s_scr, gla_stage, states_only=True)

    hn_scr[...] = hbfn_ref[...].reshape(T, D)
    for s in range(NSEG):
        rows = pl.ds(s * SEG, SEG)
        xcb_scr[rows, :] = xc_flat[rows, slice(None)].astype(BF16)
    g_scr[...] = jnp.dot(xcb_scr[...], wg_ref[...], preferred_element_type=F32) + bg_ref[...]
    _emit(gla.gate_items(p_scr, 0, lb_ref))

    coeff, steps, cums = scan.coeff_items(g_scr, xc_flat, cl_ref), scan.step_items(), gla.cumsum_items()
    proj = proj_items()
    mx = _Fill(proj, _cost(proj) / (_cost(coeff) + _cost(steps) + _cost(cums)))
    _emit(coeff, mx)
    _emit(steps, mx)
    scan.chain()
    for q in range(nb):
        hfin_ref[q] = hc_ref[q:q + 1, :]
    _emit(cums, mx)

    def load_v(n, k):
        return v_flat[pl.ds(n * SEG, SEG), slice(k * HP, (k + 1) * HP)]
    _emit(gla.a_items(None, load_v))
    _emit(gla.b_items(load_v))
    _emit(gla.c_items())
    sfin_ref[...] = s_scr[...]
    mx.flush()


def _mod_body(cs_ref, w_ref, b_ref, o_ref):
    s = _silu(cs_ref[...])
    o_ref[0] = jnp.dot(s, w_ref[0], preferred_element_type=F32, precision=lax.Precision.HIGHEST) + b_ref[0]


def _const_spec(shape):
    nd = len(shape)
    return pl.BlockSpec(shape, lambda ib, ic: (0,) * nd, pipeline_mode=pl.Buffered(1))


def _seq_spec(nb, rs, width, bc):
    return pl.BlockSpec((nb, rs, width), lambda ib, ic: bc(ib, ic) + (0,))


def _step_maps(ngroups, nc, reverse):
    last = ngroups * nc - 1

    def at(f):
        return f // nc, (nc - 1 - f % nc) if reverse else f % nc

    def cur(ib, ic):
        return at(ib * nc + ic)

    def ahead(d):
        return lambda ib, ic: at(jnp.minimum(ib * nc + ic + d, last))

    def fixed(f):
        return lambda ib, ic: at(min(f, last))
    return cur, ahead, fixed


def _scan_scratch():
    return ([pltpu.VMEM((NA, NSEG * PITCH, LANES), F32)] * 4
            + [pltpu.VMEM((BH, NSEG * PITCH, LANES), F32)] * 2)


def _params():
    return pltpu.CompilerParams(dimension_semantics=("arbitrary", "arbitrary"), vmem_limit_bytes=VMEM_LIMIT)


def _state_specs(nb):
    return [pl.BlockSpec((nb, 1, AW), lambda ib, ic: (ib, 0, 0)),
            pl.BlockSpec((nb, BH, HP, HP), lambda ib, ic: (ib, 0, 0, 0))]


def _run_kb(x, mod, mod_row, weights, layer, h0, s0, pos, nb, want_out=True):
    bsz, seq, _ = x.shape
    rs = T // nb
    nc = seq // rs
    add_pos = pos is not None
    cur, ahead, fixed = _step_maps(bsz // nb, nc, reverse=True)
    steps2 = (fixed(0), ahead(1))
    mrow = (lambda g: g) if mod_row is None else (lambda g: mod_row)
    mod_spec = lambda bc: pl.BlockSpec((1, 1, 3 * D), lambda ib, ic: (mrow(bc(ib, ic)[0]), 0, 0))
    ins, specs = [x] * 2, [_seq_spec(nb, rs, D, bc) for bc in steps2]
    if nc > 1:
        ins.append(x)
        specs.append(pl.BlockSpec(
            (1, SUBLANES, D), lambda ib, ic: (ib, jnp.maximum(cur(ib, ic)[1] * (rs // SUBLANES) - 1, 0), 0)))
    if add_pos:
        rowtab, coltab, hpos = pos
        row_spec = lambda bc: pl.BlockSpec((NSEG, D // 2), lambda ib, ic: (bc(ib, ic)[1], 0))
        ins += [rowtab] * 2 + [coltab, hpos]
        specs += [row_spec(bc) for bc in steps2] + [
            _const_spec(coltab.shape), pl.BlockSpec((1, SUBLANES, D), lambda ib, ic: (cur(ib, ic)[1], 0, 0))]
    w_ins, w_specs = _weight_operands(weights, ("nw", "wb", "cw", "cb", "wg", "bg", "cl", "lb"), layer, BWD)
    ins += [mod] * 2 + w_ins + [h0, s0]
    specs += [mod_spec(cur), mod_spec(ahead(1))] + w_specs + _state_specs(nb)
    seq_outs = [(D, BF16), (AW, F32)] + ([(AW, F32), (BW, F32), (BW, F32)] if want_out else []) + [(BW, BF16)]
    out_shape = tuple([jax.ShapeDtypeStruct((bsz, seq, w), dt) for w, dt in seq_outs]
                      + [jax.ShapeDtypeStruct((bsz, 1, AW), F32), jax.ShapeDtypeStruct((bsz, BH, HP, HP), F32)])
    out_specs = tuple([_seq_spec(nb, rs, w, cur) for w, _ in seq_outs] + _state_specs(nb))
    scratch = ([pltpu.VMEM((T, KB_COLS), F32), pltpu.VMEM((T, D), BF16)]
               + [pltpu.VMEM((T + 2 * SUBLANES, AW), F32), pltpu.VMEM((T, 2 * AW), F32), pltpu.VMEM((T, BW), F32)]
               + _scan_scratch()
               + [pltpu.VMEM((SUBLANES, AW), F32), pltpu.VMEM((nb, BH, HP, HP), F32),
                  pltpu.VMEM((SUBLANES, AW), F32)]
               + _gla_scratch())
    return pl.pallas_call(
        functools.partial(_kb_body, nb=nb, nc=nc, add_pos=add_pos, want_out=want_out),
        grid=(bsz // nb, nc), in_specs=specs, out_specs=out_specs, out_shape=out_shape,
        scratch_shapes=scratch, compiler_params=_params(), name="bwd_sweep")(*ins)


def _run_kf(x, kb_out, mod, mod_row, weights, layer, fnw, h0, s0, pos, nb, final):
    bsz, seq, _ = x.shape
    rs = T // nb
    nc = seq // rs
    add_pos = pos is not None
    cur, ahead, fixed = _step_maps(bsz // nb, nc, reverse=False)
    mrow = (lambda g: g) if mod_row is None else (lambda g: mod_row)
    hbf, xc, hb, ob, q, v = kb_out
    ins, specs = [x], [_seq_spec(nb, rs, D, cur)]
    if add_pos:
        rowtab, coltab, _ = pos
        ins += [rowtab, coltab]
        specs += [pl.BlockSpec((NSEG, D // 2), lambda ib, ic: (cur(ib, ic)[1], 0)), _const_spec(coltab.shape)]
    w_ins, w_specs = _weight_operands(
        weights, ("wf", "wg", "bg", "cl", "lb", "bnw", "cnw", "cws", "cbias", "wo"), layer, FWD)
    ins += [hbf, hbf, xc, hb, ob, q, v, mod] + w_ins + [fnw, h0, s0]
    specs += [_seq_spec(nb, rs, D, fixed(0)), _seq_spec(nb, rs, D, ahead(1))]
    specs += [_seq_spec(nb, rs, w, cur) for w in (AW, AW, BW, BW, BW)]
    specs += [pl.BlockSpec((1, 1, 3 * D), lambda ib, ic: (mrow(ib), 0, 0))]
    specs += w_specs + [_const_spec(fnw.shape)] + _state_specs(nb)
    out_shape = (jax.ShapeDtypeStruct((bsz, seq, D), F32),
                 jax.ShapeDtypeStruct((bsz, 1, AW), F32), jax.ShapeDtypeStruct((bsz, BH, HP, HP), F32))
    out_specs = tuple([_seq_spec(nb, rs, D, cur)] + _state_specs(nb))
    scratch = ([pltpu.VMEM((T, KF_COLS), F32), pltpu.VMEM((T, AW + BW), F32), pltpu.VMEM((T, D), BF16)]
               + [pltpu.VMEM((T, 2 * AW), F32), pltpu.VMEM((T, BW), F32), pltpu.VMEM((T, YW), BF16),
                  pltpu.VMEM((T, D), F32)]
               + _scan_scratch()
               + [pltpu.VMEM((SUBLANES, AW), F32), pltpu.VMEM((nb, BH, HP, HP), F32),
                  pltpu.VMEM((T, AW), BF16), pltpu.VMEM((CCH, CW), BF16), pltpu.VMEM((CH * CCH, CW), F32)]
               + _gla_scratch())
    return pl.pallas_call(
        functools.partial(_kf_body, nb=nb, nc=nc, add_pos=add_pos, final=final),
        grid=(bsz // nb, nc), in_specs=specs, out_specs=out_specs, out_shape=out_shape,
        scratch_shapes=scratch, compiler_params=_params(), name="fwd_sweep")(*ins)


def _run_kf_states(hbf, xc, v, weights, layer, h0, s0, nb):
    bsz, seq, _ = hbf.shape
    rs = T // nb
    nc = seq // rs
    cur, ahead, fixed = _step_maps(bsz // nb, nc, reverse=False)
    w_ins, w_specs = _weight_operands(weights, ("wf", "wg", "bg", "cl", "lb"), layer, FWD)
    ins = [hbf, hbf, xc, v] + w_ins + [h0, s0]
    specs = ([_seq_spec(nb, rs, D, fixed(0)), _seq_spec(nb, rs, D, ahead(1)),
              _seq_spec(nb, rs, AW, cur), _seq_spec(nb, rs, BW, cur)] + w_specs + _state_specs(nb))
    out_shape = (jax.ShapeDtypeStruct((bsz, 1, AW), F32), jax.ShapeDtypeStruct((bsz, BH, HP, HP), F32))
    scratch = ([pltpu.VMEM((T, BW), F32), pltpu.VMEM((T, D), BF16), pltpu.VMEM((T, 2 * AW), F32),
                pltpu.VMEM((T, BW), F32)]
               + _scan_scratch()
               + [pltpu.VMEM((SUBLANES, AW), F32), pltpu.VMEM((nb, BH, HP, HP), F32), pltpu.VMEM((T, AW), BF16)]
               + _gla_scratch())
    return pl.pallas_call(
        functools.partial(_kf_states_body, nb=nb, nc=nc),
        grid=(bsz // nb, nc), in_specs=specs, out_specs=tuple(_state_specs(nb)), out_shape=out_shape,
        scratch_shapes=scratch, compiler_params=_params(), name="fwd_states")(*ins)


def _pad_heads(w, fill=0.0):
    lead = w.shape[:-1]
    w = w.reshape(lead + (BH, BDK))
    w = jnp.pad(w, [(0, 0)] * len(lead) + [(0, 0), (0, HP - BDK)], constant_values=fill)
    return w.reshape(lead + (BW,))


IN_B0 = 2 * AW
IN_C0 = IN_B0 + 5 * NBW
REPACK_ROWS = 256


def _in_proj_plan():
    plan = [("b", 0, AW, 0, 1.0), ("f", AW, AW, 0, 0.5),
            ("f", IN_C0, 2 * CW, AW + 2 * BW, 1.0), ("f", IN_C0 + 2 * CW, CW, AW + 2 * BW + 2 * CW, 0.5)]
    for g, (dst, slot, scale) in enumerate((("b", 0, 0.5), ("f", 0, 0.5), ("b", 1, 0.5), ("b", 2, 1.0), ("f", 1, 0.5))):
        for h in range(BH):
            plan.append((dst, IN_B0 + g * NBW + h * BDK, BDK, AW + slot * BW + h * HP, scale))
    return plan


def _repack_in_body(w_ref, wb_ref, wf_ref):
    out = {"b": wb_ref, "f": wf_ref}
    for dst, src0, width, dst0, scale in _in_proj_plan():
        chunk = w_ref[0, :, src0:src0 + width]
        out[dst][0, :, dst0:dst0 + width] = (chunk if scale == 1.0 else chunk * scale).astype(BF16)
        if width == BDK:
            out[dst][0, :, dst0 + BDK:dst0 + HP] = jnp.zeros((REPACK_ROWS, HP - BDK), BF16)


def _repack_out_body(w_ref, wo_ref):
    for h in range(BH):
        wo_ref[0, h * HP:h * HP + BDK, :] = w_ref[0, AW + h * BDK:AW + (h + 1) * BDK, :].astype(BF16)
        wo_ref[0, h * HP + BDK:(h + 1) * HP, :] = jnp.zeros((HP - BDK, D), BF16)
    wo_ref[0, BW:BW + AW, :] = w_ref[0, 0:AW, :].astype(BF16)
    wo_ref[0, BW + AW:YW, :] = w_ref[0, AW + NBW:, :].astype(BF16)


def _repack_projections(w_in, w_out):
    depth = w_in.shape[0]
    wb, wf = pl.pallas_call(
        _repack_in_body, grid=(depth, D // REPACK_ROWS),
        in_specs=[pl.BlockSpec((1, REPACK_ROWS, w_in.shape[2]), lambda l, i: (l, i, 0))],
        out_specs=(pl.BlockSpec((1, REPACK_ROWS, KB_COLS), lambda l, i: (l, i, 0)),
                   pl.BlockSpec((1, REPACK_ROWS, KF_COLS), lambda l, i: (l, i, 0))),
        out_shape=(jax.ShapeDtypeStruct((depth, D, KB_COLS), BF16), jax.ShapeDtypeStruct((depth, D, KF_COLS), BF16)),
        name="repack_in_proj")(w_in)
    wo = pl.pallas_call(
        _repack_out_body, grid=(depth,),
        in_specs=[pl.BlockSpec((1,) + w_out.shape[1:], lambda l: (l, 0, 0))],
        out_specs=pl.BlockSpec((1, YW, D), lambda l: (l, 0, 0)),
        out_shape=jax.ShapeDtypeStruct((depth, YW, D), BF16),
        name="repack_out_proj")(w_out)
    return wb, wf, wo


def _prepare_weights(norm_w, w_in, a_conv_w, a_conv_b, a_wr, a_br, a_wi, a_bi, a_lambda, lbs, b_norm_w,
                     c_norm_w, c_ws, c_bs, w_out):
    depth = w_in.shape[0]
    wb, wf, wo = _repack_projections(w_in, w_out)
    eye = jnp.eye(A_HEADS, dtype=F32)

    def block_diag(w):
        return jnp.einsum("ldhij,hg->ldhigj", w, eye).reshape(depth, 2, AW, AW)
    return {
        "nw": norm_w[:, None, :],
        "wb": wb, "wf": wf, "wo": wo,
        "cw": a_conv_w, "cb": a_conv_b[:, None, :],
        "lb": _pad_heads(lbs[:depth], fill=1.0)[:, None, :],
        "bnw": _pad_heads(b_norm_w)[:, None, :],
        "cnw": c_norm_w[:, None, :],
        "cws": c_ws.reshape(depth, CH * CCH, CCH).astype(BF16),
        "cbias": jnp.repeat(jnp.swapaxes(c_bs, 1, 2), CHD, axis=2),
        "wg": (0.5 * jnp.concatenate([block_diag(a_wr), block_diag(a_wi)], axis=-1)).astype(BF16),
        "bg": 0.5 * jnp.concatenate([a_br, a_bi], axis=-1)[:, :, None, :],
        "cl": (-0.5 * RG_C * jax.nn.softplus(-a_lambda))[:, :, None, :],
    }


FWD, BWD = 0, 1
PER_DIRECTION = ("wg", "bg", "cl")


def _weight_operands(weights, names, layer, direction):
    arrays, specs = [], []
    for n in names:
        w = weights[n]
        lead = (layer, direction) if n in PER_DIRECTION else (layer,)
        rest = w.shape[len(lead):]
        arrays.append(w)
        specs.append(pl.BlockSpec((None,) * len(lead) + rest,
                                  lambda ib, ic, lead=lead, rest=rest: lead + (0,) * len(rest),
                                  pipeline_mode=pl.Buffered(1)))
    return arrays, specs


def _pos_tables(rows):
    qd = D // 4
    omega = 1.0 / (10000.0 ** (jnp.arange(qd, dtype=F32) / qd))
    ar = jnp.arange(rows, dtype=F32)[:, None] * omega
    ac = jnp.arange(GRID_W, dtype=F32)[:, None] * omega
    rowtab = jnp.concatenate([jnp.sin(ar), jnp.cos(ar)], axis=-1)
    coltab = jnp.concatenate([jnp.sin(ac), jnp.cos(ac)], axis=-1)
    nc = rows * GRID_W // T
    hrow = jnp.broadcast_to(rowtab[jnp.maximum(jnp.arange(nc) * NSEG - 1, 0)][:, None, :], (nc, SUBLANES, D // 2))
    hcol = jnp.broadcast_to(coltab[GRID_W - SUBLANES:][None], (nc, SUBLANES, D // 2))
    return rowtab, coltab, jnp.concatenate([hrow, hcol], axis=-1)


def kernel(x, c, ctx, c_ctx, norm_w, w_mod, b_mod, w_in, a_conv_w, a_conv_b, a_wr, a_br, a_wi, a_bi, a_lambda,
           b_lb_logits, b_norm_w, c_norm_w, c_ws, c_bs, w_out, final_norm_w):
    bsz, n_lat, _ = x.shape
    depth = w_in.shape[0]
    assert n_lat % T == 0 and T % ctx.shape[1] == 0 and bsz % (T // ctx.shape[1]) == 0
    nb_ctx = T // ctx.shape[1]
    pos = _pos_tables(n_lat // GRID_W)

    mrows = 2 * SUBLANES
    cs = jnp.zeros((mrows, D), F32).at[:bsz].set(c).at[bsz].set(c_ctx)
    nblk = 3
    mod = pl.pallas_call(
        _mod_body, grid=(depth, nblk),
        in_specs=[pl.BlockSpec((mrows, D), lambda l, j: (0, 0)),
                  pl.BlockSpec((1, D, D), lambda l, j: (l, 0, j)),
                  pl.BlockSpec((1, 1, D), lambda l, j: (l, 0, j))],
        out_specs=pl.BlockSpec((1, mrows, D), lambda l, j: (l, 0, j)),
        out_shape=jax.ShapeDtypeStruct((depth, mrows, 3 * D), F32),
        name="modulation")(cs, w_mod, b_mod[:, None, :])
    mod = mod.reshape(depth, mrows, 1, 3 * D)

    lbs = jnp.cumsum(jax.nn.softmax(b_lb_logits, axis=0), axis=0)
    fnw = final_norm_w[None, :]
    h_zero = jnp.zeros((bsz, 1, AW), F32)
    s_zero = jnp.zeros((bsz, BH, HP, HP), F32)
    weights = _prepare_weights(norm_w, w_in, a_conv_w, a_conv_b, a_wr, a_br, a_wi, a_bi, a_lambda, lbs, b_norm_w,
                               c_norm_w, c_ws, c_bs, w_out)
    for l in range(depth):
        lpos = pos if l == 0 else None
        last = l == depth - 1
        kb = _run_kb(ctx, mod[l], bsz, weights, l, h_zero, s_zero, None, nb_ctx, want_out=not last)
        hb_c, sb_c = kb[-2], kb[-1]
        if last:
            hbf_c, xc_c, v_c = kb[:3]
            hf_c, sf_c = _run_kf_states(hbf_c, xc_c, v_c, weights, l, h_zero, s_zero, nb_ctx)
        else:
            ctx, hf_c, sf_c = _run_kf(ctx, kb[:6], mod[l], bsz, weights, l, fnw, h_zero, s_zero, None, nb_ctx, False)
        kb = _run_kb(x, mod[l], None, weights, l, hb_c, sb_c, lpos, 1)
        x, _, _ = _run_kf(x, kb[:6], mod[l], None, weights, l, fnw, hf_c, sf_c, lpos, 1, last)
    return x
```

```python
import functools

import jax
import jax.numpy as jnp
import numpy as np
from jax import lax
from jax.experimental import pallas as pl
from jax.experimental.pallas import tpu as pltpu

D = 1024
GRID_W = 64
EPS = 1e-6
RG_C = 8.0
AW = 384
A_HEADS = 8
A_HD = 48
BH = 4
BDK = 96
CW = 256
CH = 4
CHD = 64
CCH = 128

LANES = 128
SUBLANES = 8
HP = LANES
BW = BH * HP
NBW = BH * BDK
YW = AW + BW + CW
NA = AW // LANES
T = 512
SEG = 64
NSEG = T // SEG
PITCH = SEG + SUBLANES
KB_COLS = AW + 3 * BW
KF_COLS = AW + 2 * BW + 3 * CW
FILL_COLS = 512
FILL_ROWS = T // 2
ROW_TILE = 16
MLP_TILE = 32
VMEM_LIMIT = 58 * 1024 * 1024

F32 = jnp.float32
BF16 = jnp.bfloat16

C_CONV_SEG, C_COEFF_TILE, C_SCAN_STEP, C_FIX_TILE, C_YA_TILE = 100, 50, 12, 8, 15
C_GATES_TILE, C_CUMSUM_STEP, C_GLA_A_TILE, C_GLA_C_TILE, C_SILU_TILE = 20, 8, 40, 15, 12
C_PRE_TILE, C_MLP_TILE, C_GLA_B_TILE, C_GLA_D_TILE = 25, 90, 40, 40
C_PROJ_PIECE, C_OUT_PIECE, C_NORM_PIECE = FILL_ROWS * FILL_COLS // 256, 3 * FILL_ROWS * FILL_COLS // 1024, 130
PACE = 1.4


def _silu_of_twice(hx):
    return hx * (jnp.tanh(hx) + 1.0)


def _silu(x):
    return _silu_of_twice(0.5 * x)


def _gelu(x):
    c = np.float32(np.sqrt(2.0 / np.pi))
    hx = 0.5 * x
    return hx * (jnp.tanh(x * (c + (0.044715 * c) * (x * x))) + 1.0)


def _nt(a, b):
    return lax.dot_general(a, b, (((1,), (1,)), ((), ())), preferred_element_type=F32)


def _tn(a, b):
    return lax.dot_general(a, b, (((0,), (0,)), ((), ())), preferred_element_type=F32)


def _norm_mod(x, nw, sh, sc):
    ms = jnp.mean(x * x, axis=-1, keepdims=True)
    return (x * lax.rsqrt(ms + EPS)) * (nw * (1.0 + sc)) + sh


def _pos_rows(rt_ref, ct_ref, r0):
    s, c0 = divmod(r0, SEG)
    return jnp.concatenate([jnp.broadcast_to(rt_ref[s:s + 1, :], (ROW_TILE, D // 2)),
                            ct_ref[c0:c0 + ROW_TILE, :]], axis=1)


class _Fill:
    def __init__(self, items=(), rate=1.0):
        self.items = list(items)
        self.rate = rate
        self.credit = 0.0
        self.spent = 0.0

    def __call__(self, credit):
        self.credit += credit
        while self.items and self.spent + 0.5 * self.items[0][0] <= self.credit * self.rate:
            cost, thunk = self.items.pop(0)
            self.spent += cost
            thunk()

    def extend(self, items):
        self.items.extend(items)

    def flush(self):
        while self.items:
            self.items.pop(0)[1]()


def _cost(items):
    return sum(c for c, _ in items)


def _emit(items, fill=None):
    for cost, thunk in items:
        thunk()
        if fill is not None:
            fill(cost)


def _paired(items, partner_items):
    partner = _Fill(partner_items, _cost(partner_items) / max(_cost(items), 1))
    _emit(items, partner)
    partner.flush()


class _Flat:
    def __init__(self, ref, width):
        self.ref = ref
        self.rs = ref.shape[1]
        self.width = width

    def _split(self, idx):
        rows, lanes = idx
        start, size = rows.start, rows.size
        q, off = divmod(start, self.rs)
        assert off + size <= self.rs
        return q, pl.ds(off, size), lanes

    def __getitem__(self, idx):
        q, r, lanes = self._split(idx)
        return self.ref[q, r, lanes]

    def __setitem__(self, idx, val):
        q, r, lanes = self._split(idx)
        self.ref[q, r, lanes] = val


def _tiles():
    return [(s, k) for s in range(NSEG) for k in range(BH)]


class _AScan:
    def __init__(self, a_st, u_st, h_st, p_st, hc_ref, nb, reverse):
        self.refs = (a_st, u_st, h_st, p_st)
        self.hc_ref, self.nb, self.reverse = hc_ref, nb, reverse
        self.hs = [jnp.zeros((NSEG, LANES), F32)] * NA
        self.ps = [jnp.ones((NSEG, LANES), F32)] * NA

    def coeff_items(self, g_scr, xc_ref, cl_ref):
        a_st, u_st = self.refs[:2]

        def item(s, k):
            def run():
                rows, lanes = pl.ds(s * SEG, SEG), slice(k * LANES, (k + 1) * LANES)
                tr = jnp.tanh(g_scr[rows, lanes])
                ti = jnp.tanh(g_scr[rows, AW + k * LANES:AW + (k + 1) * LANES])
                hcl = cl_ref[:, lanes]
                la = hcl * tr + hcl
                th = jnp.tanh(la)
                om4 = (-0.5 * th) / (1.0 - th)
                a_st[k, pl.ds(s * PITCH, SEG), :] = jnp.exp(la)
                root = jnp.where(om4 > 0.0, om4 * lax.rsqrt(om4), 0.0)
                u_st[k, pl.ds(s * PITCH, SEG), :] = root * ((ti + 1.0) * xc_ref[rows, lanes])
            return C_COEFF_TILE, run
        return [item(s, k) for s in range(NSEG) for k in range(NA)]

    def step_items(self):
        a_st, u_st, h_st, p_st = self.refs

        def item(j):
            def run():
                for k in range(NA):
                    a = a_st[k, pl.ds(j, NSEG, stride=PITCH), :]
                    u = u_st[k, pl.ds(j, NSEG, stride=PITCH), :]
                    self.hs[k] = a * self.hs[k] + u
                    self.ps[k] = a * self.ps[k]
                    h_st[k, pl.ds(j, NSEG, stride=PITCH), :] = self.hs[k]
                    p_st[k, pl.ds(j, NSEG, stride=PITCH), :] = self.ps[k]
            return C_SCAN_STEP, run
        order = range(SEG - 1, -1, -1) if self.reverse else range(SEG)
        return [item(j) for j in order]

    def chain(self):
        spq = NSEG // self.nb
        cin = [None] * NSEG
        for q in range(self.nb):
            c = [self.hc_ref[q:q + 1, k * LANES:(k + 1) * LANES] for k in range(NA)]
            order = range((q + 1) * spq - 1, q * spq - 1, -1) if self.reverse else range(q * spq, (q + 1) * spq)
            for s in order:
                cin[s] = c
                c = [self.ps[k][s:s + 1, :] * c[k] + self.hs[k][s:s + 1, :] for k in range(NA)]
            for k in range(NA):
                self.hc_ref[q:q + 1, k * LANES:(k + 1) * LANES] = c[k]
        return cin

    def full(self, cin, s, k):
        _, _, h_st, p_st = self.refs
        st_rows = pl.ds(s * PITCH, SEG)
        return h_st[k, st_rows, :] + p_st[k, st_rows, :] * cin[s][k]


class _Gla:
    def __init__(self, reverse, nb, k_scr, lf_st, g_st, s_scr, stage, states_only=False):
        self.reverse, self.nb, self.states_only = reverse, nb, states_only
        self.k_scr, self.lf_st, self.g_st, self.s_scr = k_scr, lf_st, g_st, s_scr
        self.kg_s, self.ke_s, self.lhs_s, self.kv_s, self.rhs_s = stage
        self.gs = [jnp.zeros((NSEG, LANES), F32)] * BH
        order = range(NSEG - 1, -1, -1) if reverse else range(NSEG)
        self.order = list(order)
        self.tiles = [(n, k) for n in order for k in range(BH)]

    def gate_items(self, p_scr, col0, lb_ref):
        def item(s, k):
            def run():
                rows, lanes = pl.ds(s * SEG, SEG), slice(k * HP, (k + 1) * HP)
                lb = lb_ref[:, lanes]
                c0, c1 = 0.5 + 0.5 * lb, 0.5 - 0.5 * lb
                ct = c1 * jnp.tanh(p_scr[rows, col0 + k * HP:col0 + (k + 1) * HP])
                self.k_scr[rows, lanes] = c1 - ct
                self.lf_st[k, pl.ds(s * PITCH, SEG), :] = jnp.log2(c0 + ct)
            return C_GATES_TILE, run
        return [item(s, k) for s, k in _tiles()]

    def cumsum_items(self):
        def item(j):
            def run():
                for k in range(BH):
                    self.gs[k] = self.gs[k] + self.lf_st[k, pl.ds(j, NSEG, stride=PITCH), :]
                    self.g_st[k, pl.ds(j, NSEG, stride=PITCH), :] = self.gs[k]
            return C_CUMSUM_STEP, run
        order = range(SEG - 1, -1, -1) if self.reverse else range(SEG)
        return [item(j) for j in order]

    def a_items(self, load_q, load_v):
        def item(n, k):
            def run():
                rows, lanes = pl.ds(n * SEG, SEG), slice(k * HP, (k + 1) * HP)
                g = self.g_st[k, pl.ds(n * PITCH, SEG), :]
                tot = self.gs[k][n:n + 1, :]
                kk = self.k_scr[rows, lanes]
                if self.states_only:
                    self.ke_s[rows, lanes] = (kk * jnp.exp2(tot - g)).astype(BF16)
                else:
                    self.lhs_s[k, rows, 0:HP] = (load_q(n, k) * jnp.exp2(g)).astype(BF16)
                    self.rhs_s[n, k, HP:HP + SEG, :] = load_v(n, k)
                    kg = kk * jnp.exp2(-g)
                    self.kg_s[rows, lanes] = kg.astype(BF16)
                    self.ke_s[rows, lanes] = (kg * jnp.exp2(tot)).astype(BF16)
            return C_GLA_A_TILE, run
        return [item(n, k) for n, k in self.tiles]

    def b_items(self, load_v):
        r_i = lax.broadcasted_iota(jnp.int32, (SEG, SEG), 0)
        c_i = lax.broadcasted_iota(jnp.int32, (SEG, SEG), 1)
        keep = (c_i >= r_i) if self.reverse else (c_i <= r_i)

        def item(n, k):
            def run():
                rows, lanes = pl.ds(n * SEG, SEG), slice(k * HP, (k + 1) * HP)
                if not self.states_only:
                    self.lhs_s[k, rows, HP:HP + SEG] = jnp.where(
                        keep, _nt(self.lhs_s[k, rows, 0:HP], self.kg_s[rows, lanes]), 0.0).astype(BF16)
                self.kv_s[n, k] = _tn(load_v(n, k), self.ke_s[rows, lanes])
            return C_GLA_B_TILE, run
        return [item(n, k) for n, k in self.tiles]

    def c_items(self):
        cps = NSEG // self.nb
        state = {}

        def item(n, k):
            def run():
                sq = n // cps
                st = state.get(k)
                if st is None:
                    st = self.s_scr[sq, k]
                if not self.states_only:
                    self.rhs_s[n, k, 0:HP, :] = st.T.astype(BF16)
                st = jnp.exp2(self.gs[k][n:n + 1, :]) * st + self.kv_s[n, k]
                last = (n % cps == 0) if self.reverse else (n % cps == cps - 1)
                if last:
                    self.s_scr[sq, k] = st
                    st = None
                state[k] = st
            return C_GLA_C_TILE, run
        return [item(n, k) for n in self.order for k in range(BH)]

    def d_items(self, emit_out):
        def item(n, k):
            def run():
                rows, lanes = pl.ds(n * SEG, SEG), slice(k * HP, (k + 1) * HP)
                o = jnp.dot(self.lhs_s[k, rows, :], self.rhs_s[n, k], preferred_element_type=F32)
                emit_out(n, k, o)
            return C_GLA_D_TILE, run
        return [item(n, k) for n, k in self.tiles]


def _gla_scratch():
    return ([pltpu.VMEM((T, BW), BF16)] * 2
            + [pltpu.VMEM((BH, T, HP + SEG), BF16), pltpu.VMEM((NSEG, BH, HP, HP), F32),
               pltpu.VMEM((NSEG, BH, HP + SEG, HP), BF16)])


def _dot_items(lhs_rows, w_ref, dst, col_lo, col_hi, cost, piece_rows=FILL_ROWS, piece_cols=FILL_COLS):
    def piece(r0, c0, c1):
        def run():
            dst[r0:r0 + piece_rows, c0:c1] = jnp.dot(lhs_rows(r0, r0 + piece_rows), w_ref[:, c0:c1],
                                                     preferred_element_type=F32)
        return cost * piece_rows * piece_cols // (FILL_ROWS * FILL_COLS), run
    return [piece(r0, c0, min(c0 + piece_cols, col_hi))
            for c0 in range(col_lo, col_hi, piece_cols) for r0 in range(0, T, piece_rows)]


def _kb_body(*refs, nb, nc, add_pos, want_out):
    it = iter(refs)
    x0_ref, xn_ref = next(it), next(it)
    xh_ref = next(it) if nc > 1 else None
    if add_pos:
        rt0_ref, rtn_ref, ct_ref, hp_ref = next(it), next(it), next(it), next(it)
    else:
        rt0_ref = rtn_ref = ct_ref = hp_ref = None
    (mod_ref, modn_ref, nw_ref, wb_ref, cw_ref, cb_ref, wg_ref, bg_ref, cl_ref, lb_ref, h0_ref, s0_ref,
     hbf_ref, xc_ref) = (next(it) for _ in range(14))
    hb_ref, ob_ref, q_ref = (next(it), next(it), next(it)) if want_out else (None, None, None)
    (v_ref, hfin_ref, sfin_ref,
     p_scr, h_scr, xa_ext, g_scr, k_scr, a_st, u_st, h_st, p_st, lf_st, g_st, hc_ref, s_scr, xaf,
     *gla_stage) = it

    ic = pl.program_id(1)
    flat = pl.program_id(0) * nc + ic
    cc = nc - 1 - ic

    @pl.when(ic == 0)
    def _():
        for q in range(nb):
            hc_ref[q:q + 1, :] = h0_ref[q]
        s_scr[...] = s0_ref[...]
        xaf[...] = jnp.zeros_like(xaf)

    def norm_items(x_r, rt_r, m_r):
        x_flat = _Flat(x_r, D)

        def item(s):
            def run():
                rows = pl.ds(s * SEG, SEG)
                xin = x_flat[rows, slice(None)]
                if add_pos:
                    xin = xin + jnp.concatenate(
                        [_pos_rows(rt_r, ct_ref, r0) for r0 in range(s * SEG, (s + 1) * SEG, ROW_TILE)], axis=0)
                h_scr[rows, :] = _norm_mod(xin, nw_ref[...], m_r[0, :, 0:D], m_r[0, :, D:2 * D]).astype(BF16)
            return C_NORM_PIECE, run
        return [item(s) for s in range(NSEG)]

    def proj_items():
        return _dot_items(lambda r0, r1: h_scr[r0:r1, :], wb_ref, p_scr, 0, KB_COLS, C_PROJ_PIECE, piece_rows=T)

    gla = _Gla(True, nb, k_scr, lf_st, g_st, s_scr, gla_stage, states_only=not want_out)
    v_flat = _Flat(v_ref, BW)
    if want_out:
        q_flat, ob_flat = _Flat(q_ref, BW), _Flat(ob_ref, BW)

    def hgrn_input_items():
        def qv_item(s, k):
            def run():
                rows, lanes = pl.ds(s * SEG, SEG), slice(k * HP, (k + 1) * HP)
                if want_out:
                    q_flat[rows, lanes] = _silu_of_twice(p_scr[rows, AW + k * HP:AW + (k + 1) * HP])
                v_flat[rows, lanes] = p_scr[rows, AW + 2 * BW + k * HP:AW + 2 * BW + (k + 1) * HP].astype(BF16)
            return C_PRE_TILE, run
        return [qv_item(s, k) for s, k in _tiles()] + gla.gate_items(p_scr, AW + BW, lb_ref)

    @pl.when(flat == 0)
    def _():
        _emit(norm_items(x0_ref, rt0_ref, mod_ref) + proj_items())

    def step():
        xc_flat = _Flat(xc_ref, AW)
        hb_flat = _Flat(hb_ref, AW) if want_out else None
        scan = _AScan(a_st, u_st, h_st, p_st, hc_ref, nb, reverse=True)

        hbf_ref[...] = h_scr[...].reshape(hbf_ref.shape)
        if nc > 1:
            xh = xh_ref[0]
            if add_pos:
                xh = xh + hp_ref[0]
            hh = _norm_mod(xh, nw_ref[...], mod_ref[0, :, 0:D], mod_ref[0, :, D:2 * D]).astype(BF16)
            left = jnp.dot(hh, wb_ref[:, 0:AW], preferred_element_type=F32)
            xa_ext[0:SUBLANES, :] = jnp.where(cc > 0, left, 0.0)
        else:
            xa_ext[0:SUBLANES, :] = jnp.zeros((SUBLANES, AW), F32)
        xa_ext[pl.ds(SUBLANES, T), :] = p_scr[:, 0:AW]
        xa_ext[pl.ds(SUBLANES + T, SUBLANES), :] = xaf[...]
        xaf[...] = p_scr[0:SUBLANES, 0:AW]

        def conv_item(s):
            def run():
                spq = NSEG // nb
                row = lax.broadcasted_iota(jnp.int32, (SEG, 1), 0)
                acc = jnp.broadcast_to(cb_ref[...], (SEG, AW))
                for j in range(4):
                    tap = xa_ext[pl.ds(s * SEG + SUBLANES - 2 + j, SEG), :]
                    if nb > 1 and s % spq == 0 and j < 2:
                        tap = jnp.where(row + (j - 2) >= 0, tap, 0.0)
                    if nb > 1 and s % spq == spq - 1 and j == 3:
                        tap = jnp.where(row + 1 < SEG, tap, 0.0)
                    acc = acc + cw_ref[j:j + 1, :] * tap
                xc_flat[pl.ds(s * SEG, SEG), slice(None)] = acc
            return C_CONV_SEG, run

        def load_q(n, k):
            return q_flat[pl.ds(n * SEG, SEG), slice(k * HP, (k + 1) * HP)]

        def load_v(n, k):
            return v_flat[pl.ds(n * SEG, SEG), slice(k * HP, (k + 1) * HP)]

        def emit_out(n, k, o):
            ob_flat[pl.ds(n * SEG, SEG), slice(k * HP, (k + 1) * HP)] = o

        _emit(hgrn_input_items())
        _emit(norm_items(xn_ref, rtn_ref, modn_ref))

        conv = [conv_item(s) for s in range(NSEG)]
        coeff, steps, cums = scan.coeff_items(g_scr, xc_flat, cl_ref), scan.step_items(), gla.cumsum_items()
        proj = proj_items()
        credit = (_cost(conv) + _cost(coeff) + _cost(steps) + _cost(cums)
                  + BH * NSEG * (C_GLA_A_TILE + C_GLA_C_TILE) + NA * NSEG * C_FIX_TILE)
        mx = _Fill(proj, PACE * _cost(proj) / credit)
        _emit(conv, mx)
        g_scr[...] = jnp.dot(xc_ref[...].reshape(T, AW).astype(BF16), wg_ref[...],
                             preferred_element_type=F32) + bg_ref[...]
        _emit(coeff, mx)
        _emit(steps, mx)
        cin = scan.chain()
        for q in range(nb):
            hfin_ref[q] = hc_ref[q:q + 1, :]

        def fix_item(s, k):
            def run():
                hb_flat[pl.ds(s * SEG, SEG), k * LANES:(k + 1) * LANES] = scan.full(cin, s, k)
            return C_FIX_TILE, run
        if want_out:
            _emit([fix_item(s, k) for s in range(NSEG) for k in range(NA)], mx)

        _emit(cums, mx)
        _emit(gla.a_items(load_q, load_v), mx)
        _emit(gla.b_items(load_v))
        _emit(gla.c_items(), mx)
        if want_out:
            _emit(gla.d_items(emit_out))
        sfin_ref[...] = s_scr[...]
        mx.flush()

    step()


def _kf_body(*refs, nb, nc, add_pos, final):
    it = iter(refs)
    x_ref = next(it)
    if add_pos:
        rt_ref, ct_ref = next(it), next(it)
    (hbf0_ref, hbfn_ref, xc_ref, hb_ref, ob_ref, q_ref, v_ref, mod_ref, wf_ref, wg_ref, bg_ref, cl_ref, lb_ref,
     bnw_ref, cnw_ref, cws_ref, cbias_ref, wo_ref, fnw_ref, h0_ref, s0_ref,
     xo_ref, hfin_ref, sfin_ref,
     p_scr, sg_scr, hn_scr, g_scr, k_scr, y_scr, part_scr, a_st, u_st, h_st, p_st, lf_st, g_st, hc_ref, s_scr,
     xcb_scr, vn_scr, mix_scr, *gla_stage) = it

    ic = pl.program_id(1)
    flat = pl.program_id(0) * nc + ic

    @pl.when(ic == 0)
    def _():
        for q in range(nb):
            hc_ref[q:q + 1, :] = h0_ref[q]
        s_scr[...] = s0_ref[...]

    o_ag, o_ff, o_bg = 0, AW, AW + BW
    o_cu, o_cv, o_cg = AW + 2 * BW, AW + 2 * BW + CW, AW + 2 * BW + 2 * CW

    def proj_items(col_lo, col_hi):
        return _dot_items(lambda r0, r1: hn_scr[r0:r1, :], wf_ref, p_scr, col_lo, col_hi, C_PROJ_PIECE)

    @pl.when(flat == 0)
    def _():
        hn_scr[...] = hbf0_ref[...].reshape(T, D)
        _emit(proj_items(0, KF_COLS))

    def step():
        xc_flat, hb_flat = _Flat(xc_ref, AW), _Flat(hb_ref, AW)
        q_flat, v_flat, ob_flat = _Flat(q_ref, BW), _Flat(v_ref, BW), _Flat(ob_ref, BW)
        scan = _AScan(a_st, u_st, h_st, p_st, hc_ref, nb, reverse=False)
        gla = _Gla(False, nb, k_scr, lf_st, g_st, s_scr, gla_stage)

        hn_scr[...] = hbfn_ref[...].reshape(T, D)
        for s in range(NSEG):
            rows = pl.ds(s * SEG, SEG)
            xcb_scr[rows, :] = xc_flat[rows, slice(None)].astype(BF16)

        def silu_item(s, c0):
            def run():
                rows = pl.ds(s * SEG, SEG)
                dst = c0 - o_ag if c0 < o_ff else AW + c0 - o_bg
                sg_scr[rows, dst:dst + LANES] = _silu_of_twice(p_scr[rows, c0:c0 + LANES])
            return C_SILU_TILE, run
        gate_cols = list(range(o_ag, o_ag + AW, LANES)) + list(range(o_bg, o_bg + BW, LANES))

        def mlp_items(m):
            lane = lax.broadcasted_iota(jnp.int32, (MLP_TILE, CW), 1)

            def norm_item(r0):
                def run():
                    v = _gelu(p_scr[pl.ds(m * CCH + r0, MLP_TILE), o_cv:o_cv + CW])
                    mu = jnp.mean(v, axis=-1, keepdims=True)
                    dv = v - mu
                    var = jnp.mean(dv * dv, axis=-1, keepdims=True)
                    vn_scr[r0:r0 + MLP_TILE, :] = (dv * lax.rsqrt(var + EPS) * cnw_ref[...]).astype(BF16)
                return C_MLP_TILE, run

            def mix():
                mix_scr[...] = jnp.dot(cws_ref[...], vn_scr[...], preferred_element_type=F32)

            def out_item(r0):
                def run():
                    rows = pl.ds(m * CCH + r0, MLP_TILE)
                    mixed = mix_scr[r0:r0 + MLP_TILE, :]
                    for h in range(1, CH):
                        mixed = jnp.where(lane >= h * CHD, mix_scr[h * CCH + r0:h * CCH + r0 + MLP_TILE, :], mixed)
                    u = _gelu(p_scr[rows, o_cu:o_cu + CW])
                    yc = (u * (mixed + cbias_ref[r0:r0 + MLP_TILE, :])
                          * _silu_of_twice(p_scr[rows, o_cg:o_cg + CW]))
                    y_scr[rows, BW + AW:YW] = yc.astype(BF16)
                return C_MLP_TILE, run
            tiles = range(0, CCH, MLP_TILE)
            return [norm_item(r0) for r0 in tiles] + [(0, mix)] + [out_item(r0) for r0 in tiles]

        def load_q(n, k):
            return q_flat[pl.ds(n * SEG, SEG), slice(k * HP, (k + 1) * HP)]

        def load_v(n, k):
            return v_flat[pl.ds(n * SEG, SEG), slice(k * HP, (k + 1) * HP)]

        def emit_out(n, k, o):
            rows, lanes = pl.ds(n * SEG, SEG), slice(k * HP, (k + 1) * HP)
            o = o + ob_flat[rows, lanes]
            ms = jnp.sum(o * o, axis=-1, keepdims=True) * (1.0 / BDK)
            yb = (o * lax.rsqrt(ms + EPS) * bnw_ref[:, lanes]) * sg_scr[rows, AW + k * HP:AW + (k + 1) * HP]
            y_scr[rows, lanes] = yb.astype(BF16)

        g_scr[...] = jnp.dot(xcb_scr[...], wg_ref[...], preferred_element_type=F32) + bg_ref[...]
        _emit([it for m in range(T // CCH) for it in mlp_items(m)])

        pre = [silu_item(s, c0) for s in range(NSEG) for c0 in gate_cols] + gla.gate_items(p_scr, o_ff, lb_ref)
        coeff, steps, cums = scan.coeff_items(g_scr, xc_flat, cl_ref), scan.step_items(), gla.cumsum_items()
        n_proj = -(-KF_COLS // FILL_COLS) * (T // FILL_ROWS)
        n_out = (D // FILL_COLS) * (T // FILL_ROWS)
        credit = (_cost(pre) + _cost(coeff) + _cost(steps) + _cost(cums)
                  + BH * NSEG * (C_GLA_A_TILE + C_GLA_C_TILE + C_GLA_D_TILE) + NA * NSEG * C_YA_TILE)
        mx = _Fill(proj_items(o_cu, KF_COLS), PACE * (n_proj * C_PROJ_PIECE + n_out * C_OUT_PIECE) / credit)
        _emit(pre, mx)
        mx.extend(proj_items(0, o_cu))

        _emit(coeff, mx)
        _emit(steps, mx)
        cin = scan.chain()
        for q in range(nb):
            hfin_ref[q] = hc_ref[q:q + 1, :]

        def ya_item(s, k):
            def run():
                rows, lanes = pl.ds(s * SEG, SEG), slice(k * LANES, (k + 1) * LANES)
                ya = (scan.full(cin, s, k) + hb_flat[rows, lanes]) * sg_scr[rows, lanes]
                y_scr[rows, BW + k * LANES:BW + (k + 1) * LANES] = ya.astype(BF16)
            return C_YA_TILE, run
        _emit([ya_item(s, k) for s in range(NSEG) for k in range(NA)], mx)
        mx.extend(_dot_items(lambda r0, r1: y_scr[r0:r1, BW:YW], wo_ref.at[BW:YW, :], part_scr, 0, D, C_OUT_PIECE))

        _emit(cums, mx)
        _emit(gla.a_items(load_q, load_v), mx)
        _emit(gla.b_items(load_v))
        _emit(gla.c_items(), mx)
        _emit(gla.d_items(emit_out), mx)
        sfin_ref[...] = s_scr[...]
        mx.flush()

        gm = mod_ref[0, :, 2 * D:3 * D]
        x_flat, xo_flat = _Flat(x_ref, D), _Flat(xo_ref, D)
        def out_item(r0, c0):
            def run():
                rows, cols = pl.ds(r0, FILL_ROWS), slice(c0, min(c0 + FILL_COLS, D))
                part_scr[rows, cols] = part_scr[rows, cols] + jnp.dot(y_scr[rows, 0:BW], wo_ref[0:BW, cols],
                                                                      preferred_element_type=F32)
            return C_OUT_PIECE, run

        def residual_item(r0):
            def run():
                rows = pl.ds(r0, ROW_TILE)
                xin = x_flat[rows, slice(None)]
                if add_pos:
                    xin = xin + _pos_rows(rt_ref, ct_ref, r0)
                xn = xin + gm * part_scr[rows, :]
                if final:
                    ms = jnp.mean(xn * xn, axis=-1, keepdims=True)
                    xn = xn * lax.rsqrt(ms + EPS) * fnw_ref[...]
                xo_flat[rows, slice(None)] = xn
            return C_OUT_PIECE * (D // FILL_COLS) * ROW_TILE // FILL_ROWS, run

        pending = []
        for r0 in range(0, T, FILL_ROWS):
            _paired([out_item(r0, c0) for c0 in range(0, D, FILL_COLS)], pending)
            pending = [residual_item(r) for r in range(r0, r0 + FILL_ROWS, ROW_TILE)]
        _emit(pending)

    step()


def _kf_states_body(hbf0_ref, hbfn_ref, xc_ref, v_ref, wf_ref, wg_ref, bg_ref, cl_ref, lb_ref, h0_ref, s0_ref,
                    hfin_ref, sfin_ref,
                    p_scr, hn_scr, g_scr, k_scr, a_st, u_st, h_st, p_st, lf_st, g_st, hc_ref, s_scr, xcb_scr,
                    *gla_stage, nb, nc):
    ic = pl.program_id(1)
    flat = pl.program_id(0) * nc + ic

    @pl.when(ic == 0)
    def _():
        for q in range(nb):
            hc_ref[q:q + 1, :] = h0_ref[q]
        s_scr[...] = s0_ref[...]

    def proj_items():
        return _dot_items(lambda r0, r1: hn_scr[r0:r1, :], wf_ref.at[:, AW:AW + BW], p_scr, 0, BW, C_PROJ_PIECE)

    @pl.when(flat == 0)
    def _():
        hn_scr[...] = hbf0_ref[...].reshape(T, D)
        _emit(proj_items())

    xc_flat, v_flat = _Flat(xc_ref, AW), _Flat(v_ref, BW)
    scan = _AScan(a_st, u_st, h_st, p_st, hc_ref, nb, reverse=False)
    gla = _Gla(False, nb, k_scr, lf_st, g_st, s_scr, gla_stage, states_only=True)

    hn_scr[...] = hbfn_ref[...].reshape(T, D)
    for s in range(NSEG):
        rows = pl.ds(s * SEG, SEG)
        xcb_scr[rows, :] = xc_flat[rows, slice(None)].astype(BF16)
    g_scr[...] = jnp.dot(xcb_scr[...], wg_ref[...], preferred_element_type=F32) + bg_ref[...]
    _emit(gla.gate_items(p_scr, 0, lb_ref))

    coeff, steps, cums = scan.coeff_items(g_scr, xc_flat, cl_ref), scan.step_items(), gla.cumsum_items()
    proj = proj_items()
    mx = _Fill(proj, _cost(proj) / (_cost(coeff) + _cost(steps) + _cost(cums)))
    _emit(coeff, mx)
    _emit(steps, mx)
    scan.chain()
    for q in range(nb):
        hfin_ref[q] = hc_ref[q:q + 1, :]
    _emit(cums, mx)

    def load_v(n, k):
        return v_flat[pl.ds(n * SEG, SEG), slice(k * HP, (k + 1) * HP)]
    _emit(gla.a_items(None, load_v))
    _emit(gla.b_items(load_v))
    _emit(gla.c_items())
    sfin_ref[...] = s_scr[...]
    mx.flush()


def _mod_body(cs_ref, w_ref, b_ref, o_ref):
    s = _silu(cs_ref[...])
    o_ref[0] = jnp.dot(s, w_ref[0], preferred_element_type=F32, precision=lax.Precision.HIGHEST) + b_ref[0]


def _const_spec(shape):
    nd = len(shape)
    return pl.BlockSpec(shape, lambda ib, ic: (0,) * nd, pipeline_mode=pl.Buffered(1))


def _seq_spec(nb, rs, width, bc):
    return pl.BlockSpec((nb, rs, width), lambda ib, ic: bc(ib, ic) + (0,))


def _step_maps(ngroups, nc, reverse):
    last = ngroups * nc - 1

    def at(f):
        return f // nc, (nc - 1 - f % nc) if reverse else f % nc

    def cur(ib, ic):
        return at(ib * nc + ic)

    def ahead(d):
        return lambda ib, ic: at(jnp.minimum(ib * nc + ic + d, last))

    def fixed(f):
        return lambda ib, ic: at(min(f, last))
    return cur, ahead, fixed


def _scan_scratch():
    return ([pltpu.VMEM((NA, NSEG * PITCH, LANES), F32)] * 4
            + [pltpu.VMEM((BH, NSEG * PITCH, LANES), F32)] * 2)


def _params():
    return pltpu.CompilerParams(dimension_semantics=("arbitrary", "arbitrary"), vmem_limit_bytes=VMEM_LIMIT)


def _state_specs(nb):
    return [pl.BlockSpec((nb, 1, AW), lambda ib, ic: (ib, 0, 0)),
            pl.BlockSpec((nb, BH, HP, HP), lambda ib, ic: (ib, 0, 0, 0))]


def _run_kb(x, mod, mod_row, weights, layer, h0, s0, pos, nb, want_out=True):
    bsz, seq, _ = x.shape
    rs = T // nb
    nc = seq // rs
    add_pos = pos is not None
    cur, ahead, fixed = _step_maps(bsz // nb, nc, reverse=True)
    steps2 = (fixed(0), ahead(1))
    mrow = (lambda g: g) if mod_row is None else (lambda g: mod_row)
    mod_spec = lambda bc: pl.BlockSpec((1, 1, 3 * D), lambda ib, ic: (mrow(bc(ib, ic)[0]), 0, 0))
    ins, specs = [x] * 2, [_seq_spec(nb, rs, D, bc) for bc in steps2]
    if nc > 1:
        ins.append(x)
        specs.append(pl.BlockSpec(
            (1, SUBLANES, D), lambda ib, ic: (ib, jnp.maximum(cur(ib, ic)[1] * (rs // SUBLANES) - 1, 0), 0)))
    if add_pos:
        rowtab, coltab, hpos = pos
        row_spec = lambda bc: pl.BlockSpec((NSEG, D // 2), lambda ib, ic: (bc(ib, ic)[1], 0))
        ins += [rowtab] * 2 + [coltab, hpos]
        specs += [row_spec(bc) for bc in steps2] + [
            _const_spec(coltab.shape), pl.BlockSpec((1, SUBLANES, D), lambda ib, ic: (cur(ib, ic)[1], 0, 0))]
    w_ins, w_specs = _weight_operands(weights, ("nw", "wb", "cw", "cb", "wg", "bg", "cl", "lb"), layer, BWD)
    ins += [mod] * 2 + w_ins + [h0, s0]
    specs += [mod_spec(cur), mod_spec(ahead(1))] + w_specs + _state_specs(nb)
    seq_outs = [(D, BF16), (AW, F32)] + ([(AW, F32), (BW, F32), (BW, F32)] if want_out else []) + [(BW, BF16)]
    out_shape = tuple([jax.ShapeDtypeStruct((bsz, seq, w), dt) for w, dt in seq_outs]
                      + [jax.ShapeDtypeStruct((bsz, 1, AW), F32), jax.ShapeDtypeStruct((bsz, BH, HP, HP), F32)])
    out_specs = tuple([_seq_spec(nb, rs, w, cur) for w, _ in seq_outs] + _state_specs(nb))
    scratch = ([pltpu.VMEM((T, KB_COLS), F32), pltpu.VMEM((T, D), BF16)]
               + [pltpu.VMEM((T + 2 * SUBLANES, AW), F32), pltpu.VMEM((T, 2 * AW), F32), pltpu.VMEM((T, BW), F32)]
               + _scan_scratch()
               + [pltpu.VMEM((SUBLANES, AW), F32), pltpu.VMEM((nb, BH, HP, HP), F32),
                  pltpu.VMEM((SUBLANES, AW), F32)]
               + _gla_scratch())
    return pl.pallas_call(
        functools.partial(_kb_body, nb=nb, nc=nc, add_pos=add_pos, want_out=want_out),
        grid=(bsz // nb, nc), in_specs=specs, out_specs=out_specs, out_shape=out_shape,
        scratch_shapes=scratch, compiler_params=_params(), name="bwd_sweep")(*ins)


def _run_kf(x, kb_out, mod, mod_row, weights, layer, fnw, h0, s0, pos, nb, final):
    bsz, seq, _ = x.shape
    rs = T // nb
    nc = seq // rs
    add_pos = pos is not None
    cur, ahead, fixed = _step_maps(bsz // nb, nc, reverse=False)
    mrow = (lambda g: g) if mod_row is None else (lambda g: mod_row)
    hbf, xc, hb, ob, q, v = kb_out
    ins, specs = [x], [_seq_spec(nb, rs, D, cur)]
    if add_pos:
        rowtab, coltab, _ = pos
        ins += [rowtab, coltab]
        specs += [pl.BlockSpec((NSEG, D // 2), lambda ib, ic: (cur(ib, ic)[1], 0)), _const_spec(coltab.shape)]
    w_ins, w_specs = _weight_operands(
        weights, ("wf", "wg", "bg", "cl", "lb", "bnw", "cnw", "cws", "cbias", "wo"), layer, FWD)
    ins += [hbf, hbf, xc, hb, ob, q, v, mod] + w_ins + [fnw, h0, s0]
    specs += [_seq_spec(nb, rs, D, fixed(0)), _seq_spec(nb, rs, D, ahead(1))]
    specs += [_seq_spec(nb, rs, w, cur) for w in (AW, AW, BW, BW, BW)]
    specs += [pl.BlockSpec((1, 1, 3 * D), lambda ib, ic: (mrow(ib), 0, 0))]
    specs += w_specs + [_const_spec(fnw.shape)] + _state_specs(nb)
    out_shape = (jax.ShapeDtypeStruct((bsz, seq, D), F32),
                 jax.ShapeDtypeStruct((bsz, 1, AW), F32), jax.ShapeDtypeStruct((bsz, BH, HP, HP), F32))
    out_specs = tuple([_seq_spec(nb, rs, D, cur)] + _state_specs(nb))
    scratch = ([pltpu.VMEM((T, KF_COLS), F32), pltpu.VMEM((T, AW + BW), F32), pltpu.VMEM((T, D), BF16)]
               + [pltpu.VMEM((T, 2 * AW), F32), pltpu.VMEM((T, BW), F32), pltpu.VMEM((T, YW), BF16),
                  pltpu.VMEM((T, D), F32)]
               + _scan_scratch()
               + [pltpu.VMEM((SUBLANES, AW), F32), pltpu.VMEM((nb, BH, HP, HP), F32),
                  pltpu.VMEM((T, AW), BF16), pltpu.VMEM((CCH, CW), BF16), pltpu.VMEM((CH * CCH, CW), F32)]
               + _gla_scratch())
    return pl.pallas_call(
        functools.partial(_kf_body, nb=nb, nc=nc, add_pos=add_pos, final=final),
        grid=(bsz // nb, nc), in_specs=specs, out_specs=out_specs, out_shape=out_shape,
        scratch_shapes=scratch, compiler_params=_params(), name="fwd_sweep")(*ins)


def _run_kf_states(hbf, xc, v, weights, layer, h0, s0, nb):
    bsz, seq, _ = hbf.shape
    rs = T // nb
    nc = seq // rs
    cur, ahead, fixed = _step_maps(bsz // nb, nc, reverse=False)
    w_ins, w_specs = _weight_operands(weights, ("wf", "wg", "bg", "cl", "lb"), layer, FWD)
    ins = [hbf, hbf, xc, v] + w_ins + [h0, s0]
    specs = ([_seq_spec(nb, rs, D, fixed(0)), _seq_spec(nb, rs, D, ahead(1)),
              _seq_spec(nb, rs, AW, cur), _seq_spec(nb, rs, BW, cur)] + w_specs + _state_specs(nb))
    out_shape = (jax.ShapeDtypeStruct((bsz, 1, AW), F32), jax.ShapeDtypeStruct((bsz, BH, HP, HP), F32))
    scratch = ([pltpu.VMEM((T, BW), F32), pltpu.VMEM((T, D), BF16), pltpu.VMEM((T, 2 * AW), F32),
                pltpu.VMEM((T, BW), F32)]
               + _scan_scratch()
               + [pltpu.VMEM((SUBLANES, AW), F32), pltpu.VMEM((nb, BH, HP, HP), F32), pltpu.VMEM((T, AW), BF16)]
               + _gla_scratch())
    return pl.pallas_call(
        functools.partial(_kf_states_body, nb=nb, nc=nc),
        grid=(bsz // nb, nc), in_specs=specs, out_specs=tuple(_state_specs(nb)), out_shape=out_shape,
        scratch_shapes=scratch, compiler_params=_params(), name="fwd_states")(*ins)


def _pad_heads(w, fill=0.0):
    lead = w.shape[:-1]
    w = w.reshape(lead + (BH, BDK))
    w = jnp.pad(w, [(0, 0)] * len(lead) + [(0, 0), (0, HP - BDK)], constant_values=fill)
    return w.reshape(lead + (BW,))


IN_B0 = 2 * AW
IN_C0 = IN_B0 + 5 * NBW
REPACK_ROWS = 256


def _in_proj_plan():
    plan = [("b", 0, AW, 0, 1.0), ("f", AW, AW, 0, 0.5),
            ("f", IN_C0, 2 * CW, AW + 2 * BW, 1.0), ("f", IN_C0 + 2 * CW, CW, AW + 2 * BW + 2 * CW, 0.5)]
    for g, (dst, slot, scale) in enumerate((("b", 0, 0.5), ("f", 0, 0.5), ("b", 1, 0.5), ("b", 2, 1.0), ("f", 1, 0.5))):
        for h in range(BH):
            plan.append((dst, IN_B0 + g * NBW + h * BDK, BDK, AW + slot * BW + h * HP, scale))
    return plan


def _repack_in_body(w_ref, wb_ref, wf_ref):
    out = {"b": wb_ref, "f": wf_ref}
    for dst, src0, width, dst0, scale in _in_proj_plan():
        chunk = w_ref[0, :, src0:src0 + width]
        out[dst][0, :, dst0:dst0 + width] = (chunk if scale == 1.0 else chunk * scale).astype(BF16)
        if width == BDK:
            out[dst][0, :, dst0 + BDK:dst0 + HP] = jnp.zeros((REPACK_ROWS, HP - BDK), BF16)


def _repack_out_body(w_ref, wo_ref):
    for h in range(BH):
        wo_ref[0, h * HP:h * HP + BDK, :] = w_ref[0, AW + h * BDK:AW + (h + 1) * BDK, :].astype(BF16)
        wo_ref[0, h * HP + BDK:(h + 1) * HP, :] = jnp.zeros((HP - BDK, D), BF16)
    wo_ref[0, BW:BW + AW, :] = w_ref[0, 0:AW, :].astype(BF16)
    wo_ref[0, BW + AW:YW, :] = w_ref[0, AW + NBW:, :].astype(BF16)


def _repack_projections(w_in, w_out):
    depth = w_in.shape[0]
    wb, wf = pl.pallas_call(
        _repack_in_body, grid=(depth, D // REPACK_ROWS),
        in_specs=[pl.BlockSpec((1, REPACK_ROWS, w_in.shape[2]), lambda l, i: (l, i, 0))],
        out_specs=(pl.BlockSpec((1, REPACK_ROWS, KB_COLS), lambda l, i: (l, i, 0)),
                   pl.BlockSpec((1, REPACK_ROWS, KF_COLS), lambda l, i: (l, i, 0))),
        out_shape=(jax.ShapeDtypeStruct((depth, D, KB_COLS), BF16), jax.ShapeDtypeStruct((depth, D, KF_COLS), BF16)),
        name="repack_in_proj")(w_in)
    wo = pl.pallas_call(
        _repack_out_body, grid=(depth,),
        in_specs=[pl.BlockSpec((1,) + w_out.shape[1:], lambda l: (l, 0, 0))],
        out_specs=pl.BlockSpec((1, YW, D), lambda l: (l, 0, 0)),
        out_shape=jax.ShapeDtypeStruct((depth, YW, D), BF16),
        name="repack_out_proj")(w_out)
    return wb, wf, wo


def _prepare_weights(norm_w, w_in, a_conv_w, a_conv_b, a_wr, a_br, a_wi, a_bi, a_lambda, lbs, b_norm_w,
                     c_norm_w, c_ws, c_bs, w_out):
    depth = w_in.shape[0]
    wb, wf, wo = _repack_projections(w_in, w_out)
    eye = jnp.eye(A_HEADS, dtype=F32)

    def block_diag(w):
        return jnp.einsum("ldhij,hg->ldhigj", w, eye).reshape(depth, 2, AW, AW)
    return {
        "nw": norm_w[:, None, :],
        "wb": wb, "wf": wf, "wo": wo,
        "cw": a_conv_w, "cb": a_conv_b[:, None, :],
        "lb": _pad_heads(lbs[:depth], fill=1.0)[:, None, :],
        "bnw": _pad_heads(b_norm_w)[:, None, :],
        "cnw": c_norm_w[:, None, :],
        "cws": c_ws.reshape(depth, CH * CCH, CCH).astype(BF16),
        "cbias": jnp.repeat(jnp.swapaxes(c_bs, 1, 2), CHD, axis=2),
        "wg": (0.5 * jnp.concatenate([block_diag(a_wr), block_diag(a_wi)], axis=-1)).astype(BF16),
        "bg": 0.5 * jnp.concatenate([a_br, a_bi], axis=-1)[:, :, None, :],
        "cl": (-0.5 * RG_C * jax.nn.softplus(-a_lambda))[:, :, None, :],
    }


FWD, BWD = 0, 1
PER_DIRECTION = ("wg", "bg", "cl")


def _weight_operands(weights, names, layer, direction):
    arrays, specs = [], []
    for n in names:
        w = weights[n]
        lead = (layer, direction) if n in PER_DIRECTION else (layer,)
        rest = w.shape[len(lead):]
        arrays.append(w)
        specs.append(pl.BlockSpec((None,) * len(lead) + rest,
                                  lambda ib, ic, lead=lead, rest=rest: lead + (0,) * len(rest),
                                  pipeline_mode=pl.Buffered(1)))
    return arrays, specs


def _pos_tables(rows):
    qd = D // 4
    omega = 1.0 / (10000.0 ** (jnp.arange(qd, dtype=F32) / qd))
    ar = jnp.arange(rows, dtype=F32)[:, None] * omega
    ac = jnp.arange(GRID_W, dtype=F32)[:, None] * omega
    rowtab = jnp.concatenate([jnp.sin(ar), jnp.cos(ar)], axis=-1)
    coltab = jnp.concatenate([jnp.sin(ac), jnp.cos(ac)], axis=-1)
    nc = rows * GRID_W // T
    hrow = jnp.broadcast_to(rowtab[jnp.maximum(jnp.arange(nc) * NSEG - 1, 0)][:, None, :], (nc, SUBLANES, D // 2))
    hcol = jnp.broadcast_to(coltab[GRID_W - SUBLANES:][None], (nc, SUBLANES, D // 2))
    return rowtab, coltab, jnp.concatenate([hrow, hcol], axis=-1)


def kernel(x, c, ctx, c_ctx, norm_w, w_mod, b_mod, w_in, a_conv_w, a_conv_b, a_wr, a_br, a_wi, a_bi, a_lambda,
           b_lb_logits, b_norm_w, c_norm_w, c_ws, c_bs, w_out, final_norm_w):
    bsz, n_lat, _ = x.shape
    depth = w_in.shape[0]
    assert n_lat % T == 0 and T % ctx.shape[1] == 0 and bsz % (T // ctx.shape[1]) == 0
    nb_ctx = T // ctx.shape[1]
    pos = _pos_tables(n_lat // GRID_W)

    mrows = 2 * SUBLANES
    cs = jnp.zeros((mrows, D), F32).at[:bsz].set(c).at[bsz].set(c_ctx)
    nblk = 3
    mod = pl.pallas_call(
        _mod_body, grid=(depth, nblk),
        in_specs=[pl.BlockSpec((mrows, D), lambda l, j: (0, 0)),
                  pl.BlockSpec((1, D, D), lambda l, j: (l, 0, j)),
                  pl.BlockSpec((1, 1, D), lambda l, j: (l, 0, j))],
        out_specs=pl.BlockSpec((1, mrows, D), lambda l, j: (l, 0, j)),
        out_shape=jax.ShapeDtypeStruct((depth, mrows, 3 * D), F32),
        name="modulation")(cs, w_mod, b_mod[:, None, :])
    mod = mod.reshape(depth, mrows, 1, 3 * D)

    lbs = jnp.cumsum(jax.nn.softmax(b_lb_logits, axis=0), axis=0)
    fnw = final_norm_w[None, :]
    h_zero = jnp.zeros((bsz, 1, AW), F32)
    s_zero = jnp.zeros((bsz, BH, HP, HP), F32)
    weights = _prepare_weights(norm_w, w_in, a_conv_w, a_conv_b, a_wr, a_br, a_wi, a_bi, a_lambda, lbs, b_norm_w,
                               c_norm_w, c_ws, c_bs, w_out)
    for l in range(depth):
        lpos = pos if l == 0 else None
        last = l == depth - 1
        kb = _run_kb(ctx, mod[l], bsz, weights, l, h_zero, s_zero, None, nb_ctx, want_out=not last)
        hb_c, sb_c = kb[-2], kb[-1]
        if last:
            hbf_c, xc_c, v_c = kb[:3]
            hf_c, sf_c = _run_kf_states(hbf_c, xc_c, v_c, weights, l, h_zero, s_zero, nb_ctx)
        else:
            ctx, hf_c, sf_c = _run_kf(ctx, kb[:6], mod[l], bsz, weights, l, fnw, h_zero, s_zero, None, nb_ctx, False)
        kb = _run_kb(x, mod[l], None, weights, l, hb_c, sb_c, lpos, 1)
        x, _, _ = _run_kf(x, kb[:6], mod[l], None, weights, l, fnw, hf_c, sf_c, lpos, 1, last)
    return x
```

```python
import functools

import jax
import jax.numpy as jnp
import numpy as np
from jax import lax
from jax.experimental import pallas as pl
from jax.experimental.pallas import tpu as pltpu

D = 1024
GRID_W = 64
EPS = 1e-6
RG_C = 8.0
AW = 384
A_HEADS = 8
A_HD = 48
BH = 4
BDK = 96
CW = 256
CH = 4
CHD = 64
CCH = 128

LANES = 128
SUBLANES = 8
HP = LANES
BW = BH * HP
NBW = BH * BDK
YW = AW + BW + CW
NA = AW // LANES
T = 512
SEG = 64
NSEG = T // SEG
PITCH = SEG + SUBLANES
KB_COLS = AW + 3 * BW
KF_COLS = AW + 2 * BW + 3 * CW
FILL_COLS = 512
FILL_ROWS = T // 2
ROW_TILE = 16
MLP_TILE = 32
VMEM_LIMIT = 58 * 1024 * 1024

F32 = jnp.float32
BF16 = jnp.bfloat16

C_CONV_SEG, C_COEFF_TILE, C_SCAN_STEP, C_FIX_TILE, C_YA_TILE = 100, 50, 12, 8, 15
C_GATES_TILE, C_CUMSUM_STEP, C_GLA_A_TILE, C_GLA_C_TILE, C_SILU_TILE = 20, 8, 40, 15, 12
C_PRE_TILE, C_MLP_TILE, C_GLA_B_TILE, C_GLA_D_TILE = 25, 90, 40, 40
C_PROJ_PIECE, C_OUT_PIECE, C_NORM_PIECE = FILL_ROWS * FILL_COLS // 256, 3 * FILL_ROWS * FILL_COLS // 1024, 130
PACE = 1.4


def _silu_of_twice(hx):
    return hx * (jnp.tanh(hx) + 1.0)


def _silu(x):
    return _silu_of_twice(0.5 * x)


def _gelu(x):
    c = np.float32(np.sqrt(2.0 / np.pi))
    hx = 0.5 * x
    return hx * (jnp.tanh(x * (c + (0.044715 * c) * (x * x))) + 1.0)


def _nt(a, b):
    return lax.dot_general(a, b, (((1,), (1,)), ((), ())), preferred_element_type=F32)


def _tn(a, b):
    return lax.dot_general(a, b, (((0,), (0,)), ((), ())), preferred_element_type=F32)


def _norm_mod(x, nw, sh, sc):
    ms = jnp.mean(x * x, axis=-1, keepdims=True)
    return (x * lax.rsqrt(ms + EPS)) * (nw * (1.0 + sc)) + sh


def _pos_rows(rt_ref, ct_ref, r0):
    s, c0 = divmod(r0, SEG)
    return jnp.concatenate([jnp.broadcast_to(rt_ref[s:s + 1, :], (ROW_TILE, D // 2)),
                            ct_ref[c0:c0 + ROW_TILE, :]], axis=1)


class _Fill:
    def __init__(self, items=(), rate=1.0):
        self.items = list(items)
        self.rate = rate
        self.credit = 0.0
        self.spent = 0.0

    def __call__(self, credit):
        self.credit += credit
        while self.items and self.spent + 0.5 * self.items[0][0] <= self.credit * self.rate:
            cost, thunk = self.items.pop(0)
            self.spent += cost
            thunk()

    def extend(self, items):
        self.items.extend(items)

    def flush(self):
        while self.items:
            self.items.pop(0)[1]()


def _cost(items):
    return sum(c for c, _ in items)


def _emit(items, fill=None):
    for cost, thunk in items:
        thunk()
        if fill is not None:
            fill(cost)


def _paired(items, partner_items):
    partner = _Fill(partner_items, _cost(partner_items) / max(_cost(items), 1))
    _emit(items, partner)
    partner.flush()


class _Flat:
    def __init__(self, ref, width):
        self.ref = ref
        self.rs = ref.shape[1]
        self.width = width

    def _split(self, idx):
        rows, lanes = idx
        start, size = rows.start, rows.size
        q, off = divmod(start, self.rs)
        assert off + size <= self.rs
        return q, pl.ds(off, size), lanes

    def __getitem__(self, idx):
        q, r, lanes = self._split(idx)
        return self.ref[q, r, lanes]

    def __setitem__(self, idx, val):
        q, r, lanes = self._split(idx)
        self.ref[q, r, lanes] = val


def _tiles():
    return [(s, k) for s in range(NSEG) for k in range(BH)]


class _AScan:
    def __init__(self, a_st, u_st, h_st, p_st, hc_ref, nb, reverse):
        self.refs = (a_st, u_st, h_st, p_st)
        self.hc_ref, self.nb, self.reverse = hc_ref, nb, reverse
        self.hs = [jnp.zeros((NSEG, LANES), F32)] * NA
        self.ps = [jnp.ones((NSEG, LANES), F32)] * NA

    def coeff_items(self, g_scr, xc_ref, cl_ref):
        a_st, u_st = self.refs[:2]

        def item(s, k):
            def run():
                rows, lanes = pl.ds(s * SEG, SEG), slice(k * LANES, (k + 1) * LANES)
                tr = jnp.tanh(g_scr[rows, lanes])
                ti = jnp.tanh(g_scr[rows, AW + k * LANES:AW + (k + 1) * LANES])
                hcl = cl_ref[:, lanes]
                la = hcl * tr + hcl
                th = jnp.tanh(la)
                om4 = (-0.5 * th) / (1.0 - th)
                a_st[k, pl.ds(s * PITCH, SEG), :] = jnp.exp(la)
                root = jnp.where(om4 > 0.0, om4 * lax.rsqrt(om4), 0.0)
                u_st[k, pl.ds(s * PITCH, SEG), :] = root * ((ti + 1.0) * xc_ref[rows, lanes])
            return C_COEFF_TILE, run
        return [item(s, k) for s in range(NSEG) for k in range(NA)]

    def step_items(self):
        a_st, u_st, h_st, p_st = self.refs

        def item(j):
            def run():
                for k in range(NA):
                    a = a_st[k, pl.ds(j, NSEG, stride=PITCH), :]
                    u = u_st[k, pl.ds(j, NSEG, stride=PITCH), :]
                    self.hs[k] = a * self.hs[k] + u
                    self.ps[k] = a * self.ps[k]
                    h_st[k, pl.ds(j, NSEG, stride=PITCH), :] = self.hs[k]
                    p_st[k, pl.ds(j, NSEG, stride=PITCH), :] = self.ps[k]
            return C_SCAN_STEP, run
        order = range(SEG - 1, -1, -1) if self.reverse else range(SEG)
        return [item(j) for j in order]

    def chain(self):
        spq = NSEG // self.nb
        cin = [None] * NSEG
        for q in range(self.nb):
            c = [self.hc_ref[q:q + 1, k * LANES:(k + 1) * LANES] for k in range(NA)]
            order = range((q + 1) * spq - 1, q * spq - 1, -1) if self.reverse else range(q * spq, (q + 1) * spq)
            for s in order:
                cin[s] = c
                c = [self.ps[k][s:s + 1, :] * c[k] + self.hs[k][s:s + 1, :] for k in range(NA)]
            for k in range(NA):
                self.hc_ref[q:q + 1, k * LANES:(k + 1) * LANES] = c[k]
        return cin

    def full(self, cin, s, k):
        _, _, h_st, p_st = self.refs
        st_rows = pl.ds(s * PITCH, SEG)
        return h_st[k, st_rows, :] + p_st[k, st_rows, :] * cin[s][k]


class _Gla:
    def __init__(self, reverse, nb, k_scr, lf_st, g_st, s_scr, stage, states_only=False):
        self.reverse, self.nb, self.states_only = reverse, nb, states_only
        self.k_scr, self.lf_st, self.g_st, self.s_scr = k_scr, lf_st, g_st, s_scr
        self.kg_s, self.ke_s, self.lhs_s, self.kv_s, self.rhs_s = stage
        self.gs = [jnp.zeros((NSEG, LANES), F32)] * BH
        order = range(NSEG - 1, -1, -1) if reverse else range(NSEG)
        self.order = list(order)
        self.tiles = [(n, k) for n in order for k in range(BH)]

    def gate_items(self, p_scr, col0, lb_ref):
        def item(s, k):
            def run():
                rows, lanes = pl.ds(s * SEG, SEG), slice(k * HP, (k + 1) * HP)
                lb = lb_ref[:, lanes]
                c0, c1 = 0.5 + 0.5 * lb, 0.5 - 0.5 * lb
                ct = c1 * jnp.tanh(p_scr[rows, col0 + k * HP:col0 + (k + 1) * HP])
                self.k_scr[rows, lanes] = c1 - ct
                self.lf_st[k, pl.ds(s * PITCH, SEG), :] = jnp.log2(c0 + ct)
            return C_GATES_TILE, run
        return [item(s, k) for s, k in _tiles()]

    def cumsum_items(self):
        def item(j):
            def run():
                for k in range(BH):
                    self.gs[k] = self.gs[k] + self.lf_st[k, pl.ds(j, NSEG, stride=PITCH), :]
                    self.g_st[k, pl.ds(j, NSEG, stride=PITCH), :] = self.gs[k]
            return C_CUMSUM_STEP, run
        order = range(SEG - 1, -1, -1) if self.reverse else range(SEG)
        return [item(j) for j in order]

    def a_items(self, load_q, load_v):
        def item(n, k):
            def run():
                rows, lanes = pl.ds(n * SEG, SEG), slice(k * HP, (k + 1) * HP)
                g = self.g_st[k, pl.ds(n * PITCH, SEG), :]
                tot = self.gs[k][n:n + 1, :]
                kk = self.k_scr[rows, lanes]
                if self.states_only:
                    self.ke_s[rows, lanes] = (kk * jnp.exp2(tot - g)).astype(BF16)
                else:
                    self.lhs_s[k, rows, 0:HP] = (load_q(n, k) * jnp.exp2(g)).astype(BF16)
                    self.rhs_s[n, k, HP:HP + SEG, :] = load_v(n, k)
                    kg = kk * jnp.exp2(-g)
                    self.kg_s[rows, lanes] = kg.astype(BF16)
                    self.ke_s[rows, lanes] = (kg * jnp.exp2(tot)).astype(BF16)
            return C_GLA_A_TILE, run
        return [item(n, k) for n, k in self.tiles]

    def b_items(self, load_v):
        r_i = lax.broadcasted_iota(jnp.int32, (SEG, SEG), 0)
        c_i = lax.broadcasted_iota(jnp.int32, (SEG, SEG), 1)
        keep = (c_i >= r_i) if self.reverse else (c_i <= r_i)

        def item(n, k):
            def run():
                rows, lanes = pl.ds(n * SEG, SEG), slice(k * HP, (k + 1) * HP)
                if not self.states_only:
                    self.lhs_s[k, rows, HP:HP + SEG] = jnp.where(
                        keep, _nt(self.lhs_s[k, rows, 0:HP], self.kg_s[rows, lanes]), 0.0).astype(BF16)
                self.kv_s[n, k] = _tn(load_v(n, k), self.ke_s[rows, lanes])
            return C_GLA_B_TILE, run
        return [item(n, k) for n, k in self.tiles]

    def c_items(self):
        cps = NSEG // self.nb
        state = {}

        def item(n, k):
            def run():
                sq = n // cps
                st = state.get(k)
                if st is None:
                    st = self.s_scr[sq, k]
                if not self.states_only:
                    self.rhs_s[n, k, 0:HP, :] = st.T.astype(BF16)
                st = jnp.exp2(self.gs[k][n:n + 1, :]) * st + self.kv_s[n, k]
                last = (n % cps == 0) if self.reverse else (n % cps == cps - 1)
                if last:
                    self.s_scr[sq, k] = st
                    st = None
                state[k] = st
            return C_GLA_C_TILE, run
        return [item(n, k) for n in self.order for k in range(BH)]

    def d_items(self, emit_out):
        def item(n, k):
            def run():
                rows, lanes = pl.ds(n * SEG, SEG), slice(k * HP, (k + 1) * HP)
                o = jnp.dot(self.lhs_s[k, rows, :], self.rhs_s[n, k], preferred_element_type=F32)
                emit_out(n, k, o)
            return C_GLA_D_TILE, run
        return [item(n, k) for n, k in self.tiles]


def _gla_scratch():
    return ([pltpu.VMEM((T, BW), BF16)] * 2
            + [pltpu.VMEM((BH, T, HP + SEG), BF16), pltpu.VMEM((NSEG, BH, HP, HP), F32),
               pltpu.VMEM((NSEG, BH, HP + SEG, HP), BF16)])


def _dot_items(lhs_rows, w_ref, dst, col_lo, col_hi, cost, piece_rows=FILL_ROWS, piece_cols=FILL_COLS):
    def piece(r0, c0, c1):
        def run():
            dst[r0:r0 + piece_rows, c0:c1] = jnp.dot(lhs_rows(r0, r0 + piece_rows), w_ref[:, c0:c1],
                                                     preferred_element_type=F32)
        return cost * piece_rows * piece_cols // (FILL_ROWS * FILL_COLS), run
    return [piece(r0, c0, min(c0 + piece_cols, col_hi))
            for c0 in range(col_lo, col_hi, piece_cols) for r0 in range(0, T, piece_rows)]


def _kb_body(*refs, nb, nc, add_pos, want_out):
    it = iter(refs)
    x0_ref, xn_ref = next(it), next(it)
    xh_ref = next(it) if nc > 1 else None
    if add_pos:
        rt0_ref, rtn_ref, ct_ref, hp_ref = next(it), next(it), next(it), next(it)
    else:
        rt0_ref = rtn_ref = ct_ref = hp_ref = None
    (mod_ref, modn_ref, nw_ref, wb_ref, cw_ref, cb_ref, wg_ref, bg_ref, cl_ref, lb_ref, h0_ref, s0_ref,
     hbf_ref, xc_ref) = (next(it) for _ in range(14))
    hb_ref, ob_ref, q_ref = (next(it), next(it), next(it)) if want_out else (None, None, None)
    (v_ref, hfin_ref, sfin_ref,
     p_scr, h_scr, xa_ext, g_scr, k_scr, a_st, u_st, h_st, p_st, lf_st, g_st, hc_ref, s_scr, xaf,
     *gla_stage) = it

    ic = pl.program_id(1)
    flat = pl.program_id(0) * nc + ic
    cc = nc - 1 - ic

    @pl.when(ic == 0)
    def _():
        for q in range(nb):
            hc_ref[q:q + 1, :] = h0_ref[q]
        s_scr[...] = s0_ref[...]
        xaf[...] = jnp.zeros_like(xaf)

    def norm_items(x_r, rt_r, m_r):
        x_flat = _Flat(x_r, D)

        def item(s):
            def run():
                rows = pl.ds(s * SEG, SEG)
                xin = x_flat[rows, slice(None)]
                if add_pos:
                    xin = xin + jnp.concatenate(
                        [_pos_rows(rt_r, ct_ref, r0) for r0 in range(s * SEG, (s + 1) * SEG, ROW_TILE)], axis=0)
                h_scr[rows, :] = _norm_mod(xin, nw_ref[...], m_r[0, :, 0:D], m_r[0, :, D:2 * D]).astype(BF16)
            return C_NORM_PIECE, run
        return [item(s) for s in range(NSEG)]

    def proj_items():
        return _dot_items(lambda r0, r1: h_scr[r0:r1, :], wb_ref, p_scr, 0, KB_COLS, C_PROJ_PIECE, piece_rows=T)

    gla = _Gla(True, nb, k_scr, lf_st, g_st, s_scr, gla_stage, states_only=not want_out)
    v_flat = _Flat(v_ref, BW)
    if want_out:
        q_flat, ob_flat = _Flat(q_ref, BW), _Flat(ob_ref, BW)

    def hgrn_input_items():
        def qv_item(s, k):
            def run():
                rows, lanes = pl.ds(s * SEG, SEG), slice(k * HP, (k + 1) * HP)
                if want_out:
                    q_flat[rows, lanes] = _silu_of_twice(p_scr[rows, AW + k * HP:AW + (k + 1) * HP])
                v_flat[rows, lanes] = p_scr[rows, AW + 2 * BW + k * HP:AW + 2 * BW + (k + 1) * HP].astype(BF16)
            return C_PRE_TILE, run
        return [qv_item(s, k) for s, k in _tiles()] + gla.gate_items(p_scr, AW + BW, lb_ref)

    @pl.when(flat == 0)
    def _():
        _emit(norm_items(x0_ref, rt0_ref, mod_ref) + proj_items())

    def step():
        xc_flat = _Flat(xc_ref, AW)
        hb_flat = _Flat(hb_ref, AW) if want_out else None
        scan = _AScan(a_st, u_st, h_st, p_st, hc_ref, nb, reverse=True)

        hbf_ref[...] = h_scr[...].reshape(hbf_ref.shape)
        if nc > 1:
            xh = xh_ref[0]
            if add_pos:
                xh = xh + hp_ref[0]
            hh = _norm_mod(xh, nw_ref[...], mod_ref[0, :, 0:D], mod_ref[0, :, D:2 * D]).astype(BF16)
            left = jnp.dot(hh, wb_ref[:, 0:AW], preferred_element_type=F32)
            xa_ext[0:SUBLANES, :] = jnp.where(cc > 0, left, 0.0)
        else:
            xa_ext[0:SUBLANES, :] = jnp.zeros((SUBLANES, AW), F32)
        xa_ext[pl.ds(SUBLANES, T), :] = p_scr[:, 0:AW]
        xa_ext[pl.ds(SUBLANES + T, SUBLANES), :] = xaf[...]
        xaf[...] = p_scr[0:SUBLANES, 0:AW]

        def conv_item(s):
            def run():
                spq = NSEG // nb
                row = lax.broadcasted_iota(jnp.int32, (SEG, 1), 0)
                acc = jnp.broadcast_to(cb_ref[...], (SEG, AW))
                for j in range(4):
                    tap = xa_ext[pl.ds(s * SEG + SUBLANES - 2 + j, SEG), :]
                    if nb > 1 and s % spq == 0 and j < 2:
                        tap = jnp.where(row + (j - 2) >= 0, tap, 0.0)
                    if nb > 1 and s % spq == spq - 1 and j == 3:
                        tap = jnp.where(row + 1 < SEG, tap, 0.0)
                    acc = acc + cw_ref[j:j + 1, :] * tap
                xc_flat[pl.ds(s * SEG, SEG), slice(None)] = acc
            return C_CONV_SEG, run

        def load_q(n, k):
            return q_flat[pl.ds(n * SEG, SEG), slice(k * HP, (k + 1) * HP)]

        def load_v(n, k):
            return v_flat[pl.ds(n * SEG, SEG), slice(k * HP, (k + 1) * HP)]

        def emit_out(n, k, o):
            ob_flat[pl.ds(n * SEG, SEG), slice(k * HP, (k + 1) * HP)] = o

        _emit(hgrn_input_items())
        _emit(norm_items(xn_ref, rtn_ref, modn_ref))

        conv = [conv_item(s) for s in range(NSEG)]
        coeff, steps, cums = scan.coeff_items(g_scr, xc_flat, cl_ref), scan.step_items(), gla.cumsum_items()
        proj = proj_items()
        credit = (_cost(conv) + _cost(coeff) + _cost(steps) + _cost(cums)
                  + BH * NSEG * (C_GLA_A_TILE + C_GLA_C_TILE) + NA * NSEG * C_FIX_TILE)
        mx = _Fill(proj, PACE * _cost(proj) / credit)
        _emit(conv, mx)
        g_scr[...] = jnp.dot(xc_ref[...].reshape(T, AW).astype(BF16), wg_ref[...],
                             preferred_element_type=F32) + bg_ref[...]
        _emit(coeff, mx)
        _emit(steps, mx)
        cin = scan.chain()
        for q in range(nb):
            hfin_ref[q] = hc_ref[q:q + 1, :]

        def fix_item(s, k):
            def run():
                hb_flat[pl.ds(s * SEG, SEG), k * LANES:(k + 1) * LANES] = scan.full(cin, s, k)
            return C_FIX_TILE, run
        if want_out:
            _emit([fix_item(s, k) for s in range(NSEG) for k in range(NA)], mx)

        _emit(cums, mx)
        _emit(gla.a_items(load_q, load_v), mx)
        _emit(gla.b_items(load_v))
        _emit(gla.c_items(), mx)
        if want_out:
            _emit(gla.d_items(emit_out))
        sfin_ref[...] = s_scr[...]
        mx.flush()

    step()


def _kf_body(*refs, nb, nc, add_pos, final):
    it = iter(refs)
    x_ref = next(it)
    if add_pos:
        rt_ref, ct_ref = next(it), next(it)
    (hbf0_ref, hbfn_ref, xc_ref, hb_ref, ob_ref, q_ref, v_ref, mod_ref, wf_ref, wg_ref, bg_ref, cl_ref, lb_ref,
     bnw_ref, cnw_ref, cws_ref, cbias_ref, wo_ref, fnw_ref, h0_ref, s0_ref,
     xo_ref, hfin_ref, sfin_ref,
     p_scr, sg_scr, hn_scr, g_scr, k_scr, y_scr, part_scr, a_st, u_st, h_st, p_st, lf_st, g_st, hc_ref, s_scr,
     xcb_scr, vn_scr, mix_scr, *gla_stage) = it

    ic = pl.program_id(1)
    flat = pl.program_id(0) * nc + ic

    @pl.when(ic == 0)
    def _():
        for q in range(nb):
            hc_ref[q:q + 1, :] = h0_ref[q]
        s_scr[...] = s0_ref[...]

    o_ag, o_ff, o_bg = 0, AW, AW + BW
    o_cu, o_cv, o_cg = AW + 2 * BW, AW + 2 * BW + CW, AW + 2 * BW + 2 * CW

    def proj_items(col_lo, col_hi):
        return _dot_items(lambda r0, r1: hn_scr[r0:r1, :], wf_ref, p_scr, col_lo, col_hi, C_PROJ_PIECE,
                          piece_rows=FILL_ROWS // 2)

    @pl.when(flat == 0)
    def _():
        hn_scr[...] = hbf0_ref[...].reshape(T, D)
        _emit(proj_items(0, KF_COLS))

    def step():
        xc_flat, hb_flat = _Flat(xc_ref, AW), _Flat(hb_ref, AW)
        q_flat, v_flat, ob_flat = _Flat(q_ref, BW), _Flat(v_ref, BW), _Flat(ob_ref, BW)
        scan = _AScan(a_st, u_st, h_st, p_st, hc_ref, nb, reverse=False)
        gla = _Gla(False, nb, k_scr, lf_st, g_st, s_scr, gla_stage)

        hn_scr[...] = hbfn_ref[...].reshape(T, D)
        for s in range(NSEG):
            rows = pl.ds(s * SEG, SEG)
            xcb_scr[rows, :] = xc_flat[rows, slice(None)].astype(BF16)

        def silu_item(s, c0):
            def run():
                rows = pl.ds(s * SEG, SEG)
                dst = c0 - o_ag if c0 < o_ff else AW + c0 - o_bg
                sg_scr[rows, dst:dst + LANES] = _silu_of_twice(p_scr[rows, c0:c0 + LANES])
            return C_SILU_TILE, run
        gate_cols = list(range(o_ag, o_ag + AW, LANES)) + list(range(o_bg, o_bg + BW, LANES))

        def mlp_items(m):
            lane = lax.broadcasted_iota(jnp.int32, (MLP_TILE, CW), 1)

            def norm_item(r0):
                def run():
                    v = _gelu(p_scr[pl.ds(m * CCH + r0, MLP_TILE), o_cv:o_cv + CW])
                    mu = jnp.mean(v, axis=-1, keepdims=True)
                    dv = v - mu
                    var = jnp.mean(dv * dv, axis=-1, keepdims=True)
                    vn_scr[r0:r0 + MLP_TILE, :] = (dv * lax.rsqrt(var + EPS) * cnw_ref[...]).astype(BF16)
                return C_MLP_TILE, run

            def mix():
                mix_scr[...] = jnp.dot(cws_ref[...], vn_scr[...], preferred_element_type=F32)

            def out_item(r0):
                def run():
                    rows = pl.ds(m * CCH + r0, MLP_TILE)
                    mixed = mix_scr[r0:r0 + MLP_TILE, :]
                    for h in range(1, CH):
                        mixed = jnp.where(lane >= h * CHD, mix_scr[h * CCH + r0:h * CCH + r0 + MLP_TILE, :], mixed)
                    u = _gelu(p_scr[rows, o_cu:o_cu + CW])
                    yc = (u * (mixed + cbias_ref[r0:r0 + MLP_TILE, :])
                          * _silu_of_twice(p_scr[rows, o_cg:o_cg + CW]))
                    y_scr[rows, BW + AW:YW] = yc.astype(BF16)
                return C_MLP_TILE, run
            tiles = range(0, CCH, MLP_TILE)
            return [norm_item(r0) for r0 in tiles] + [(0, mix)] + [out_item(r0) for r0 in tiles]

        def load_q(n, k):
            return q_flat[pl.ds(n * SEG, SEG), slice(k * HP, (k + 1) * HP)]

        def load_v(n, k):
            return v_flat[pl.ds(n * SEG, SEG), slice(k * HP, (k + 1) * HP)]

        def emit_out(n, k, o):
            rows, lanes = pl.ds(n * SEG, SEG), slice(k * HP, (k + 1) * HP)
            o = o + ob_flat[rows, lanes]
            ms = jnp.sum(o * o, axis=-1, keepdims=True) * (1.0 / BDK)
            yb = (o * lax.rsqrt(ms + EPS) * bnw_ref[:, lanes]) * sg_scr[rows, AW + k * HP:AW + (k + 1) * HP]
            y_scr[rows, lanes] = yb.astype(BF16)

        g_scr[...] = jnp.dot(xcb_scr[...], wg_ref[...], preferred_element_type=F32) + bg_ref[...]
        _emit([it for m in range(T // CCH) for it in mlp_items(m)])

        pre = [silu_item(s, c0) for s in range(NSEG) for c0 in gate_cols] + gla.gate_items(p_scr, o_ff, lb_ref)
        coeff, steps, cums = scan.coeff_items(g_scr, xc_flat, cl_ref), scan.step_items(), gla.cumsum_items()
        n_proj = -(-KF_COLS // FILL_COLS) * (T // FILL_ROWS)
        n_out = (D // FILL_COLS) * (T // FILL_ROWS)
        credit = (_cost(pre) + _cost(coeff) + _cost(steps) + _cost(cums)
                  + BH * NSEG * (C_GLA_A_TILE + C_GLA_C_TILE + C_GLA_D_TILE) + NA * NSEG * C_YA_TILE)
        mx = _Fill(proj_items(o_cu, KF_COLS), PACE * (n_proj * C_PROJ_PIECE + n_out * C_OUT_PIECE) / credit)
        _emit(pre, mx)
        mx.extend(proj_items(0, o_cu))

        _emit(coeff, mx)
        _emit(steps, mx)
        cin = scan.chain()
        for q in range(nb):
            hfin_ref[q] = hc_ref[q:q + 1, :]

        def ya_item(s, k):
            def run():
                rows, lanes = pl.ds(s * SEG, SEG), slice(k * LANES, (k + 1) * LANES)
                ya = (scan.full(cin, s, k) + hb_flat[rows, lanes]) * sg_scr[rows, lanes]
                y_scr[rows, BW + k * LANES:BW + (k + 1) * LANES] = ya.astype(BF16)
            return C_YA_TILE, run
        _emit([ya_item(s, k) for s in range(NSEG) for k in range(NA)], mx)
        mx.extend(_dot_items(lambda r0, r1: y_scr[r0:r1, BW:YW], wo_ref.at[BW:YW, :], part_scr, 0, D, C_OUT_PIECE))

        _emit(cums, mx)
        _emit(gla.a_items(load_q, load_v), mx)
        _emit(gla.b_items(load_v))
        _emit(gla.c_items(), mx)
        _emit(gla.d_items(emit_out), mx)
        sfin_ref[...] = s_scr[...]
        mx.flush()

        gm = mod_ref[0, :, 2 * D:3 * D]
        x_flat, xo_flat = _Flat(x_ref, D), _Flat(xo_ref, D)
        def out_item(r0, c0):
            def run():
                rows, cols = pl.ds(r0, FILL_ROWS), slice(c0, min(c0 + FILL_COLS, D))
                part_scr[rows, cols] = part_scr[rows, cols] + jnp.dot(y_scr[rows, 0:BW], wo_ref[0:BW, cols],
                                                                      preferred_element_type=F32)
            return C_OUT_PIECE, run

        def residual_item(r0):
            def run():
                rows = pl.ds(r0, ROW_TILE)
                xin = x_flat[rows, slice(None)]
                if add_pos:
                    xin = xin + _pos_rows(rt_ref, ct_ref, r0)
                xn = xin + gm * part_scr[rows, :]
                if final:
                    ms = jnp.mean(xn * xn, axis=-1, keepdims=True)
                    xn = xn * lax.rsqrt(ms + EPS) * fnw_ref[...]
                xo_flat[rows, slice(None)] = xn
            return C_OUT_PIECE * (D // FILL_COLS) * ROW_TILE // FILL_ROWS, run

        pending = []
        for r0 in range(0, T, FILL_ROWS):
            _paired([out_item(r0, c0) for c0 in range(0, D, FILL_COLS)], pending)
            pending = [residual_item(r) for r in range(r0, r0 + FILL_ROWS, ROW_TILE)]
        _emit(pending)

    step()


def _kf_states_body(hbf0_ref, hbfn_ref, xc_ref, v_ref, wf_ref, wg_ref, bg_ref, cl_ref, lb_ref, h0_ref, s0_ref,
                    hfin_ref, sfin_ref,
                    p_scr, hn_scr, g_scr, k_scr, a_st, u_st, h_st, p_st, lf_st, g_st, hc_ref, s_scr, xcb_scr,
                    *gla_stage, nb, nc):
    ic = pl.program_id(1)
    flat = pl.program_id(0) * nc + ic

    @pl.when(ic == 0)
    def _():
        for q in range(nb):
            hc_ref[q:q + 1, :] = h0_ref[q]
        s_scr[...] = s0_ref[...]

    def proj_items():
        return _dot_items(lambda r0, r1: hn_scr[r0:r1, :], wf_ref.at[:, AW:AW + BW], p_scr, 0, BW, C_PROJ_PIECE)

    @pl.when(flat == 0)
    def _():
        hn_scr[...] = hbf0_ref[...].reshape(T, D)
        _emit(proj_items())

    xc_flat, v_flat = _Flat(xc_ref, AW), _Flat(v_ref, BW)
    scan = _AScan(a_st, u_st, h_st, p_st, hc_ref, nb, reverse=False)
    gla = _Gla(False, nb, k_scr, lf_st, g_st, s_scr, gla_stage, states_only=True)

    hn_scr[...] = hbfn_ref[...].reshape(T, D)
    for s in range(NSEG):
        rows = pl.ds(s * SEG, SEG)
        xcb_scr[rows, :] = xc_flat[rows, slice(None)].astype(BF16)
    g_scr[...] = jnp.dot(xcb_scr[...], wg_ref[...], preferred_element_type=F32) + bg_ref[...]
    _emit(gla.gate_items(p_scr, 0, lb_ref))

    coeff, steps, cums = scan.coeff_items(g_scr, xc_flat, cl_ref), scan.step_items(), gla.cumsum_items()
    proj = proj_items()
    mx = _Fill(proj, _cost(proj) / (_cost(coeff) + _cost(steps) + _cost(cums)))
    _emit(coeff, mx)
    _emit(steps, mx)
    scan.chain()
    for q in range(nb):
        hfin_ref[q] = hc_ref[q:q + 1, :]
    _emit(cums, mx)

    def load_v(n, k):
        return v_flat[pl.ds(n * SEG, SEG), slice(k * HP, (k + 1) * HP)]
    _emit(gla.a_items(None, load_v))
    _emit(gla.b_items(load_v))
    _emit(gla.c_items())
    sfin_ref[...] = s_scr[...]
    mx.flush()


def _mod_body(cs_ref, w_ref, b_ref, o_ref):
    s = _silu(cs_ref[...])
    o_ref[0] = jnp.dot(s, w_ref[0], preferred_element_type=F32, precision=lax.Precision.HIGHEST) + b_ref[0]


def _const_spec(shape):
    nd = len(shape)
    return pl.BlockSpec(shape, lambda ib, ic: (0,) * nd, pipeline_mode=pl.Buffered(1))


def _seq_spec(nb, rs, width, bc):
    return pl.BlockSpec((nb, rs, width), lambda ib, ic: bc(ib, ic) + (0,))


def _step_maps(ngroups, nc, reverse):
    last = ngroups * nc - 1

    def at(f):
        return f // nc, (nc - 1 - f % nc) if reverse else f % nc

    def cur(ib, ic):
        return at(ib * nc + ic)

    def ahead(d):
        return lambda ib, ic: at(jnp.minimum(ib * nc + ic + d, last))

    def fixed(f):
        return lambda ib, ic: at(min(f, last))
    return cur, ahead, fixed


def _scan_scratch():
    return ([pltpu.VMEM((NA, NSEG * PITCH, LANES), F32)] * 4
            + [pltpu.VMEM((BH, NSEG * PITCH, LANES), F32)] * 2)


def _params():
    return pltpu.CompilerParams(dimension_semantics=("arbitrary", "arbitrary"), vmem_limit_bytes=VMEM_LIMIT)


def _state_specs(nb):
    return [pl.BlockSpec((nb, 1, AW), lambda ib, ic: (ib, 0, 0)),
            pl.BlockSpec((nb, BH, HP, HP), lambda ib, ic: (ib, 0, 0, 0))]


def _run_kb(x, mod, mod_row, weights, layer, h0, s0, pos, nb, want_out=True):
    bsz, seq, _ = x.shape
    rs = T // nb
    nc = seq // rs
    add_pos = pos is not None
    cur, ahead, fixed = _step_maps(bsz // nb, nc, reverse=True)
    steps2 = (fixed(0), ahead(1))
    mrow = (lambda g: g) if mod_row is None else (lambda g: mod_row)
    mod_spec = lambda bc: pl.BlockSpec((1, 1, 3 * D), lambda ib, ic: (mrow(bc(ib, ic)[0]), 0, 0))
    ins, specs = [x] * 2, [_seq_spec(nb, rs, D, bc) for bc in steps2]
    if nc > 1:
        ins.append(x)
        specs.append(pl.BlockSpec(
            (1, SUBLANES, D), lambda ib, ic: (ib, jnp.maximum(cur(ib, ic)[1] * (rs // SUBLANES) - 1, 0), 0)))
    if add_pos:
        rowtab, coltab, hpos = pos
        row_spec = lambda bc: pl.BlockSpec((NSEG, D // 2), lambda ib, ic: (bc(ib, ic)[1], 0))
        ins += [rowtab] * 2 + [coltab, hpos]
        specs += [row_spec(bc) for bc in steps2] + [
            _const_spec(coltab.shape), pl.BlockSpec((1, SUBLANES, D), lambda ib, ic: (cur(ib, ic)[1], 0, 0))]
    w_ins, w_specs = _weight_operands(weights, ("nw", "wb", "cw", "cb", "wg", "bg", "cl", "lb"), layer, BWD)
    ins += [mod] * 2 + w_ins + [h0, s0]
    specs += [mod_spec(cur), mod_spec(ahead(1))] + w_specs + _state_specs(nb)
    seq_outs = [(D, BF16), (AW, F32)] + ([(AW, F32), (BW, F32), (BW, F32)] if want_out else []) + [(BW, BF16)]
    out_shape = tuple([jax.ShapeDtypeStruct((bsz, seq, w), dt) for w, dt in seq_outs]
                      + [jax.ShapeDtypeStruct((bsz, 1, AW), F32), jax.ShapeDtypeStruct((bsz, BH, HP, HP), F32)])
    out_specs = tuple([_seq_spec(nb, rs, w, cur) for w, _ in seq_outs] + _state_specs(nb))
    scratch = ([pltpu.VMEM((T, KB_COLS), F32), pltpu.VMEM((T, D), BF16)]
               + [pltpu.VMEM((T + 2 * SUBLANES, AW), F32), pltpu.VMEM((T, 2 * AW), F32), pltpu.VMEM((T, BW), F32)]
               + _scan_scratch()
               + [pltpu.VMEM((SUBLANES, AW), F32), pltpu.VMEM((nb, BH, HP, HP), F32),
                  pltpu.VMEM((SUBLANES, AW), F32)]
               + _gla_scratch())
    return pl.pallas_call(
        functools.partial(_kb_body, nb=nb, nc=nc, add_pos=add_pos, want_out=want_out),
        grid=(bsz // nb, nc), in_specs=specs, out_specs=out_specs, out_shape=out_shape,
        scratch_shapes=scratch, compiler_params=_params(), name="bwd_sweep")(*ins)


def _run_kf(x, kb_out, mod, mod_row, weights, layer, fnw, h0, s0, pos, nb, final):
    bsz, seq, _ = x.shape
    rs = T // nb
    nc = seq // rs
    add_pos = pos is not None
    cur, ahead, fixed = _step_maps(bsz // nb, nc, reverse=False)
    mrow = (lambda g: g) if mod_row is None else (lambda g: mod_row)
    hbf, xc, hb, ob, q, v = kb_out
    ins, specs = [x], [_seq_spec(nb, rs, D, cur)]
    if add_pos:
        rowtab, coltab, _ = pos
        ins += [rowtab, coltab]
        specs += [pl.BlockSpec((NSEG, D // 2), lambda ib, ic: (cur(ib, ic)[1], 0)), _const_spec(coltab.shape)]
    w_ins, w_specs = _weight_operands(
        weights, ("wf", "wg", "bg", "cl", "lb", "bnw", "cnw", "cws", "cbias", "wo"), layer, FWD)
    ins += [hbf, hbf, xc, hb, ob, q, v, mod] + w_ins + [fnw, h0, s0]
    specs += [_seq_spec(nb, rs, D, fixed(0)), _seq_spec(nb, rs, D, ahead(1))]
    specs += [_seq_spec(nb, rs, w, cur) for w in (AW, AW, BW, BW, BW)]
    specs += [pl.BlockSpec((1, 1, 3 * D), lambda ib, ic: (mrow(ib), 0, 0))]
    specs += w_specs + [_const_spec(fnw.shape)] + _state_specs(nb)
    out_shape = (jax.ShapeDtypeStruct((bsz, seq, D), F32),
                 jax.ShapeDtypeStruct((bsz, 1, AW), F32), jax.ShapeDtypeStruct((bsz, BH, HP, HP), F32))
    out_specs = tuple([_seq_spec(nb, rs, D, cur)] + _state_specs(nb))
    scratch = ([pltpu.VMEM((T, KF_COLS), F32), pltpu.VMEM((T, AW + BW), F32), pltpu.VMEM((T, D), BF16)]
               + [pltpu.VMEM((T, 2 * AW), F32), pltpu.VMEM((T, BW), F32), pltpu.VMEM((T, YW), BF16),
                  pltpu.VMEM((T, D), F32)]
               + _scan_scratch()
               + [pltpu.VMEM((SUBLANES, AW), F32), pltpu.VMEM((nb, BH, HP, HP), F32),
                  pltpu.VMEM((T, AW), BF16), pltpu.VMEM((CCH, CW), BF16), pltpu.VMEM((CH * CCH, CW), F32)]
               + _gla_scratch())
    return pl.pallas_call(
        functools.partial(_kf_body, nb=nb, nc=nc, add_pos=add_pos, final=final),
        grid=(bsz // nb, nc), in_specs=specs, out_specs=out_specs, out_shape=out_shape,
        scratch_shapes=scratch, compiler_params=_params(), name="fwd_sweep")(*ins)


def _run_kf_states(hbf, xc, v, weights, layer, h0, s0, nb):
    bsz, seq, _ = hbf.shape
    rs = T // nb
    nc = seq // rs
    cur, ahead, fixed = _step_maps(bsz // nb, nc, reverse=False)
    w_ins, w_specs = _weight_operands(weights, ("wf", "wg", "bg", "cl", "lb"), layer, FWD)
    ins = [hbf, hbf, xc, v] + w_ins + [h0, s0]
    specs = ([_seq_spec(nb, rs, D, fixed(0)), _seq_spec(nb, rs, D, ahead(1)),
              _seq_spec(nb, rs, AW, cur), _seq_spec(nb, rs, BW, cur)] + w_specs + _state_specs(nb))
    out_shape = (jax.ShapeDtypeStruct((bsz, 1, AW), F32), jax.ShapeDtypeStruct((bsz, BH, HP, HP), F32))
    scratch = ([pltpu.VMEM((T, BW), F32), pltpu.VMEM((T, D), BF16), pltpu.VMEM((T, 2 * AW), F32),
                pltpu.VMEM((T, BW), F32)]
               + _scan_scratch()
               + [pltpu.VMEM((SUBLANES, AW), F32), pltpu.VMEM((nb, BH, HP, HP), F32), pltpu.VMEM((T, AW), BF16)]
               + _gla_scratch())
    return pl.pallas_call(
        functools.partial(_kf_states_body, nb=nb, nc=nc),
        grid=(bsz // nb, nc), in_specs=specs, out_specs=tuple(_state_specs(nb)), out_shape=out_shape,
        scratch_shapes=scratch, compiler_params=_params(), name="fwd_states")(*ins)


def _pad_heads(w, fill=0.0):
    lead = w.shape[:-1]
    w = w.reshape(lead + (BH, BDK))
    w = jnp.pad(w, [(0, 0)] * len(lead) + [(0, 0), (0, HP - BDK)], constant_values=fill)
    return w.reshape(lead + (BW,))


IN_B0 = 2 * AW
IN_C0 = IN_B0 + 5 * NBW
REPACK_ROWS = 256


def _in_proj_plan():
    plan = [("b", 0, AW, 0, 1.0), ("f", AW, AW, 0, 0.5),
            ("f", IN_C0, 2 * CW, AW + 2 * BW, 1.0), ("f", IN_C0 + 2 * CW, CW, AW + 2 * BW + 2 * CW, 0.5)]
    for g, (dst, slot, scale) in enumerate((("b", 0, 0.5), ("f", 0, 0.5), ("b", 1, 0.5), ("b", 2, 1.0), ("f", 1, 0.5))):
        for h in range(BH):
            plan.append((dst, IN_B0 + g * NBW + h * BDK, BDK, AW + slot * BW + h * HP, scale))
    return plan


def _repack_in_body(w_ref, wb_ref, wf_ref):
    out = {"b": wb_ref, "f": wf_ref}
    for dst, src0, width, dst0, scale in _in_proj_plan():
        chunk = w_ref[0, :, src0:src0 + width]
        out[dst][0, :, dst0:dst0 + width] = (chunk if scale == 1.0 else chunk * scale).astype(BF16)
        if width == BDK:
            out[dst][0, :, dst0 + BDK:dst0 + HP] = jnp.zeros((REPACK_ROWS, HP - BDK), BF16)


def _repack_out_body(w_ref, wo_ref):
    for h in range(BH):
        wo_ref[0, h * HP:h * HP + BDK, :] = w_ref[0, AW + h * BDK:AW + (h + 1) * BDK, :].astype(BF16)
        wo_ref[0, h * HP + BDK:(h + 1) * HP, :] = jnp.zeros((HP - BDK, D), BF16)
    wo_ref[0, BW:BW + AW, :] = w_ref[0, 0:AW, :].astype(BF16)
    wo_ref[0, BW + AW:YW, :] = w_ref[0, AW + NBW:, :].astype(BF16)


def _repack_projections(w_in, w_out):
    depth = w_in.shape[0]
    wb, wf = pl.pallas_call(
        _repack_in_body, grid=(depth, D // REPACK_ROWS),
        in_specs=[pl.BlockSpec((1, REPACK_ROWS, w_in.shape[2]), lambda l, i: (l, i, 0))],
        out_specs=(pl.BlockSpec((1, REPACK_ROWS, KB_COLS), lambda l, i: (l, i, 0)),
                   pl.BlockSpec((1, REPACK_ROWS, KF_COLS), lambda l, i: (l, i, 0))),
        out_shape=(jax.ShapeDtypeStruct((depth, D, KB_COLS), BF16), jax.ShapeDtypeStruct((depth, D, KF_COLS), BF16)),
        name="repack_in_proj")(w_in)
    wo = pl.pallas_call(
        _repack_out_body, grid=(depth,),
        in_specs=[pl.BlockSpec((1,) + w_out.shape[1:], lambda l: (l, 0, 0))],
        out_specs=pl.BlockSpec((1, YW, D), lambda l: (l, 0, 0)),
        out_shape=jax.ShapeDtypeStruct((depth, YW, D), BF16),
        name="repack_out_proj")(w_out)
    return wb, wf, wo


def _prepare_weights(norm_w, w_in, a_conv_w, a_conv_b, a_wr, a_br, a_wi, a_bi, a_lambda, lbs, b_norm_w,
                     c_norm_w, c_ws, c_bs, w_out):
    depth = w_in.shape[0]
    wb, wf, wo = _repack_projections(w_in, w_out)
    eye = jnp.eye(A_HEADS, dtype=F32)

    def block_diag(w):
        return jnp.einsum("ldhij,hg->ldhigj", w, eye).reshape(depth, 2, AW, AW)
    return {
        "nw": norm_w[:, None, :],
        "wb": wb, "wf": wf, "wo": wo,
        "cw": a_conv_w, "cb": a_conv_b[:, None, :],
        "lb": _pad_heads(lbs[:depth], fill=1.0)[:, None, :],
        "bnw": _pad_heads(b_norm_w)[:, None, :],
        "cnw": c_norm_w[:, None, :],
        "cws": c_ws.reshape(depth, CH * CCH, CCH).astype(BF16),
        "cbias": jnp.repeat(jnp.swapaxes(c_bs, 1, 2), CHD, axis=2),
        "wg": (0.5 * jnp.concatenate([block_diag(a_wr), block_diag(a_wi)], axis=-1)).astype(BF16),
        "bg": 0.5 * jnp.concatenate([a_br, a_bi], axis=-1)[:, :, None, :],
        "cl": (-0.5 * RG_C * jax.nn.softplus(-a_lambda))[:, :, None, :],
    }


FWD, BWD = 0, 1
PER_DIRECTION = ("wg", "bg", "cl")


def _weight_operands(weights, names, layer, direction):
    arrays, specs = [], []
    for n in names:
        w = weights[n]
        lead = (layer, direction) if n in PER_DIRECTION else (layer,)
        rest = w.shape[len(lead):]
        arrays.append(w)
        specs.append(pl.BlockSpec((None,) * len(lead) + rest,
                                  lambda ib, ic, lead=lead, rest=rest: lead + (0,) * len(rest),
                                  pipeline_mode=pl.Buffered(1)))
    return arrays, specs


def _pos_tables(rows):
    qd = D // 4
    omega = 1.0 / (10000.0 ** (jnp.arange(qd, dtype=F32) / qd))
    ar = jnp.arange(rows, dtype=F32)[:, None] * omega
    ac = jnp.arange(GRID_W, dtype=F32)[:, None] * omega
    rowtab = jnp.concatenate([jnp.sin(ar), jnp.cos(ar)], axis=-1)
    coltab = jnp.concatenate([jnp.sin(ac), jnp.cos(ac)], axis=-1)
    nc = rows * GRID_W // T
    hrow = jnp.broadcast_to(rowtab[jnp.maximum(jnp.arange(nc) * NSEG - 1, 0)][:, None, :], (nc, SUBLANES, D // 2))
    hcol = jnp.broadcast_to(coltab[GRID_W - SUBLANES:][None], (nc, SUBLANES, D // 2))
    return rowtab, coltab, jnp.concatenate([hrow, hcol], axis=-1)


def kernel(x, c, ctx, c_ctx, norm_w, w_mod, b_mod, w_in, a_conv_w, a_conv_b, a_wr, a_br, a_wi, a_bi, a_lambda,
           b_lb_logits, b_norm_w, c_norm_w, c_ws, c_bs, w_out, final_norm_w):
    bsz, n_lat, _ = x.shape
    depth = w_in.shape[0]
    assert n_lat % T == 0 and T % ctx.shape[1] == 0 and bsz % (T // ctx.shape[1]) == 0
    nb_ctx = T // ctx.shape[1]
    pos = _pos_tables(n_lat // GRID_W)

    mrows = 2 * SUBLANES
    cs = jnp.zeros((mrows, D), F32).at[:bsz].set(c).at[bsz].set(c_ctx)
    nblk = 3
    mod = pl.pallas_call(
        _mod_body, grid=(depth, nblk),
        in_specs=[pl.BlockSpec((mrows, D), lambda l, j: (0, 0)),
                  pl.BlockSpec((1, D, D), lambda l, j: (l, 0, j)),
                  pl.BlockSpec((1, 1, D), lambda l, j: (l, 0, j))],
        out_specs=pl.BlockSpec((1, mrows, D), lambda l, j: (l, 0, j)),
        out_shape=jax.ShapeDtypeStruct((depth, mrows, 3 * D), F32),
        name="modulation")(cs, w_mod, b_mod[:, None, :])
    mod = mod.reshape(depth, mrows, 1, 3 * D)

    lbs = jnp.cumsum(jax.nn.softmax(b_lb_logits, axis=0), axis=0)
    fnw = final_norm_w[None, :]
    h_zero = jnp.zeros((bsz, 1, AW), F32)
    s_zero = jnp.zeros((bsz, BH, HP, HP), F32)
    weights = _prepare_weights(norm_w, w_in, a_conv_w, a_conv_b, a_wr, a_br, a_wi, a_bi, a_lambda, lbs, b_norm_w,
                               c_norm_w, c_ws, c_bs, w_out)
    for l in range(depth):
        lpos = pos if l == 0 else None
        last = l == depth - 1
        kb = _run_kb(ctx, mod[l], bsz, weights, l, h_zero, s_zero, None, nb_ctx, want_out=not last)
        hb_c, sb_c = kb[-2], kb[-1]
        if last:
            hbf_c, xc_c, v_c = kb[:3]
            hf_c, sf_c = _run_kf_states(hbf_c, xc_c, v_c, weights, l, h_zero, s_zero, nb_ctx)
        else:
            ctx, hf_c, sf_c = _run_kf(ctx, kb[:6], mod[l], bsz, weights, l, fnw, h_zero, s_zero, None, nb_ctx, False)
        kb = _run_kb(x, mod[l], None, weights, l, hb_c, sb_c, lpos, 1)
        x, _, _ = _run_kf(x, kb[:6], mod[l], None, weights, l, fnw, hf_c, sf_c, lpos, 1, last)
    return x
```

```python
import functools

import jax
import jax.numpy as jnp
import numpy as np
from jax import lax
from jax.experimental import pallas as pl
from jax.experimental.pallas import tpu as pltpu

D = 1024
GRID_W = 64
EPS = 1e-6
RG_C = 8.0
AW = 384
A_HEADS = 8
A_HD = 48
BH = 4
BDK = 96
CW = 256
CH = 4
CHD = 64
CCH = 128

LANES = 128
SUBLANES = 8
HP = LANES
BW = BH * HP
NBW = BH * BDK
YW = AW + BW + CW
NA = AW // LANES
T = 512
SEG = 64
NSEG = T // SEG
PITCH = SEG + SUBLANES
KB_COLS = AW + 3 * BW
KF_COLS = AW + 2 * BW + 3 * CW
MXU_TILE = 256
FILL_COLS = 2 * MXU_TILE
FILL_ROWS = T // 2
ROW_TILE = 16
MLP_TILE = 32
V7X_VMEM_BYTES = 64 * 1024 * 1024
VMEM_LIMIT = V7X_VMEM_BYTES - 6 * 1024 * 1024

F32 = jnp.float32
BF16 = jnp.bfloat16

C_CONV_SEG, C_COEFF_TILE, C_SCAN_STEP, C_FIX_TILE, C_YA_TILE = 100, 50, 12, 8, 15
C_GATES_TILE, C_CUMSUM_STEP, C_GLA_A_TILE, C_GLA_C_TILE, C_SILU_TILE = 20, 8, 40, 15, 12
C_PRE_TILE, C_MLP_TILE, C_GLA_B_TILE, C_GLA_D_TILE = 25, 90, 40, 40
C_PROJ_PIECE = FILL_ROWS * FILL_COLS // MXU_TILE
C_OUT_PIECE, C_NORM_PIECE = 3 * C_PROJ_PIECE // 4, 130
PACE = 1.4


def _silu_of_twice(hx):
    return hx * (jnp.tanh(hx) + 1.0)


def _silu(x):
    return _silu_of_twice(0.5 * x)


def _gelu(x):
    c = np.float32(np.sqrt(2.0 / np.pi))
    hx = 0.5 * x
    return hx * (jnp.tanh(x * (c + (0.044715 * c) * (x * x))) + 1.0)


def _nt(a, b):
    return lax.dot_general(a, b, (((1,), (1,)), ((), ())), preferred_element_type=F32)


def _tn(a, b):
    return lax.dot_general(a, b, (((0,), (0,)), ((), ())), preferred_element_type=F32)


def _norm_mod(x, nw, sh, sc):
    ms = jnp.mean(x * x, axis=-1, keepdims=True)
    return (x * lax.rsqrt(ms + EPS)) * (nw * (1.0 + sc)) + sh


def _pos_rows(rt_ref, ct_ref, r0):
    s, c0 = divmod(r0, SEG)
    return jnp.concatenate([jnp.broadcast_to(rt_ref[s:s + 1, :], (ROW_TILE, D // 2)),
                            ct_ref[c0:c0 + ROW_TILE, :]], axis=1)


class _Fill:
    def __init__(self, items=(), rate=1.0):
        self.items = list(items)
        self.rate = rate
        self.credit = 0.0
        self.spent = 0.0

    def __call__(self, credit):
        self.credit += credit
        while self.items and self.spent + 0.5 * self.items[0][0] <= self.credit * self.rate:
            cost, thunk = self.items.pop(0)
            self.spent += cost
            thunk()

    def extend(self, items):
        self.items.extend(items)

    def flush(self):
        while self.items:
            self.items.pop(0)[1]()


def _cost(items):
    return sum(c for c, _ in items)


def _emit(items, fill=None):
    for cost, thunk in items:
        thunk()
        if fill is not None:
            fill(cost)


def _paired(items, partner_items):
    partner = _Fill(partner_items, _cost(partner_items) / max(_cost(items), 1))
    _emit(items, partner)
    partner.flush()


class _Flat:
    def __init__(self, ref, width):
        self.ref = ref
        self.rs = ref.shape[1]
        self.width = width

    def _split(self, idx):
        rows, lanes = idx
        start, size = rows.start, rows.size
        q, off = divmod(start, self.rs)
        assert off + size <= self.rs
        return q, pl.ds(off, size), lanes

    def __getitem__(self, idx):
        q, r, lanes = self._split(idx)
        return self.ref[q, r, lanes]

    def __setitem__(self, idx, val):
        q, r, lanes = self._split(idx)
        self.ref[q, r, lanes] = val


def _tiles():
    return [(s, k) for s in range(NSEG) for k in range(BH)]


class _AScan:
    def __init__(self, a_st, u_st, h_st, p_st, hc_ref, nb, reverse):
        self.refs = (a_st, u_st, h_st, p_st)
        self.hc_ref, self.nb, self.reverse = hc_ref, nb, reverse
        self.hs = [jnp.zeros((NSEG, LANES), F32)] * NA
        self.ps = [jnp.ones((NSEG, LANES), F32)] * NA

    def coeff_items(self, g_scr, xc_ref, cl_ref):
        a_st, u_st = self.refs[:2]

        def item(s, k):
            def run():
                rows, lanes = pl.ds(s * SEG, SEG), slice(k * LANES, (k + 1) * LANES)
                tr = jnp.tanh(g_scr[rows, lanes])
                ti = jnp.tanh(g_scr[rows, AW + k * LANES:AW + (k + 1) * LANES])
                hcl = cl_ref[:, lanes]
                la = hcl * tr + hcl
                th = jnp.tanh(la)
                om4 = (-0.5 * th) / (1.0 - th)
                a_st[k, pl.ds(s * PITCH, SEG), :] = jnp.exp(la)
                root = jnp.where(om4 > 0.0, om4 * lax.rsqrt(om4), 0.0)
                u_st[k, pl.ds(s * PITCH, SEG), :] = root * ((ti + 1.0) * xc_ref[rows, lanes])
            return C_COEFF_TILE, run
        return [item(s, k) for s in range(NSEG) for k in range(NA)]

    def step_items(self):
        a_st, u_st, h_st, p_st = self.refs

        def item(j):
            def run():
                for k in range(NA):
                    a = a_st[k, pl.ds(j, NSEG, stride=PITCH), :]
                    u = u_st[k, pl.ds(j, NSEG, stride=PITCH), :]
                    self.hs[k] = a * self.hs[k] + u
                    self.ps[k] = a * self.ps[k]
                    h_st[k, pl.ds(j, NSEG, stride=PITCH), :] = self.hs[k]
                    p_st[k, pl.ds(j, NSEG, stride=PITCH), :] = self.ps[k]
            return C_SCAN_STEP, run
        order = range(SEG - 1, -1, -1) if self.reverse else range(SEG)
        return [item(j) for j in order]

    def chain(self):
        spq = NSEG // self.nb
        cin = [None] * NSEG
        for q in range(self.nb):
            c = [self.hc_ref[q:q + 1, k * LANES:(k + 1) * LANES] for k in range(NA)]
            order = range((q + 1) * spq - 1, q * spq - 1, -1) if self.reverse else range(q * spq, (q + 1) * spq)
            for s in order:
                cin[s] = c
                c = [self.ps[k][s:s + 1, :] * c[k] + self.hs[k][s:s + 1, :] for k in range(NA)]
            for k in range(NA):
                self.hc_ref[q:q + 1, k * LANES:(k + 1) * LANES] = c[k]
        return cin

    def full(self, cin, s, k):
        _, _, h_st, p_st = self.refs
        st_rows = pl.ds(s * PITCH, SEG)
        return h_st[k, st_rows, :] + p_st[k, st_rows, :] * cin[s][k]


class _Gla:
    def __init__(self, reverse, nb, k_scr, lf_st, g_st, s_scr, stage, states_only=False):
        self.reverse, self.nb, self.states_only = reverse, nb, states_only
        self.k_scr, self.lf_st, self.g_st, self.s_scr = k_scr, lf_st, g_st, s_scr
        self.kg_s, self.ke_s, self.lhs_s, self.kv_s, self.rhs_s = stage
        self.gs = [jnp.zeros((NSEG, LANES), F32)] * BH
        order = range(NSEG - 1, -1, -1) if reverse else range(NSEG)
        self.order = list(order)
        self.tiles = [(n, k) for n in order for k in range(BH)]

    def gate_items(self, p_scr, col0, lb_ref):
        def item(s, k):
            def run():
                rows, lanes = pl.ds(s * SEG, SEG), slice(k * HP, (k + 1) * HP)
                lb = lb_ref[:, lanes]
                c0, c1 = 0.5 + 0.5 * lb, 0.5 - 0.5 * lb
                ct = c1 * jnp.tanh(p_scr[rows, col0 + k * HP:col0 + (k + 1) * HP])
                self.k_scr[rows, lanes] = c1 - ct
                self.lf_st[k, pl.ds(s * PITCH, SEG), :] = jnp.log2(c0 + ct)
            return C_GATES_TILE, run
        return [item(s, k) for s, k in _tiles()]

    def cumsum_items(self):
        def item(j):
            def run():
                for k in range(BH):
                    self.gs[k] = self.gs[k] + self.lf_st[k, pl.ds(j, NSEG, stride=PITCH), :]
                    self.g_st[k, pl.ds(j, NSEG, stride=PITCH), :] = self.gs[k]
            return C_CUMSUM_STEP, run
        order = range(SEG - 1, -1, -1) if self.reverse else range(SEG)
        return [item(j) for j in order]

    def a_items(self, load_q, load_v):
        def item(n, k):
            def run():
                rows, lanes = pl.ds(n * SEG, SEG), slice(k * HP, (k + 1) * HP)
                g = self.g_st[k, pl.ds(n * PITCH, SEG), :]
                tot = self.gs[k][n:n + 1, :]
                kk = self.k_scr[rows, lanes]
                if self.states_only:
                    self.ke_s[rows, lanes] = (kk * jnp.exp2(tot - g)).astype(BF16)
                else:
                    self.lhs_s[k, rows, 0:HP] = (load_q(n, k) * jnp.exp2(g)).astype(BF16)
                    self.rhs_s[n, k, HP:HP + SEG, :] = load_v(n, k)
                    kg = kk * jnp.exp2(-g)
                    self.kg_s[rows, lanes] = kg.astype(BF16)
                    self.ke_s[rows, lanes] = (kg * jnp.exp2(tot)).astype(BF16)
            return C_GLA_A_TILE, run
        return [item(n, k) for n, k in self.tiles]

    def b_items(self, load_v):
        r_i = lax.broadcasted_iota(jnp.int32, (SEG, SEG), 0)
        c_i = lax.broadcasted_iota(jnp.int32, (SEG, SEG), 1)
        keep = (c_i >= r_i) if self.reverse else (c_i <= r_i)

        def item(n, k):
            def run():
                rows, lanes = pl.ds(n * SEG, SEG), slice(k * HP, (k + 1) * HP)
                if not self.states_only:
                    self.lhs_s[k, rows, HP:HP + SEG] = jnp.where(
                        keep, _nt(self.lhs_s[k, rows, 0:HP], self.kg_s[rows, lanes]), 0.0).astype(BF16)
                self.kv_s[n, k] = _tn(load_v(n, k), self.ke_s[rows, lanes])
            return C_GLA_B_TILE, run
        return [item(n, k) for n, k in self.tiles]

    def c_items(self):
        cps = NSEG // self.nb
        state = {}

        def item(n, k):
            def run():
                sq = n // cps
                st = state.get(k)
                if st is None:
                    st = self.s_scr[sq, k]
                if not self.states_only:
                    self.rhs_s[n, k, 0:HP, :] = st.T.astype(BF16)
                st = jnp.exp2(self.gs[k][n:n + 1, :]) * st + self.kv_s[n, k]
                last = (n % cps == 0) if self.reverse else (n % cps == cps - 1)
                if last:
                    self.s_scr[sq, k] = st
                    st = None
                state[k] = st
            return C_GLA_C_TILE, run
        return [item(n, k) for n in self.order for k in range(BH)]

    def d_items(self, emit_out):
        def item(n, k):
            def run():
                rows, lanes = pl.ds(n * SEG, SEG), slice(k * HP, (k + 1) * HP)
                o = jnp.dot(self.lhs_s[k, rows, :], self.rhs_s[n, k], preferred_element_type=F32)
                emit_out(n, k, o)
            return C_GLA_D_TILE, run
        return [item(n, k) for n, k in self.tiles]


def _gla_scratch():
    return ([pltpu.VMEM((T, BW), BF16)] * 2
            + [pltpu.VMEM((BH, T, HP + SEG), BF16), pltpu.VMEM((NSEG, BH, HP, HP), F32),
               pltpu.VMEM((NSEG, BH, HP + SEG, HP), BF16)])


def _dot_items(lhs_rows, w_ref, dst, col_lo, col_hi, cost, piece_rows=FILL_ROWS, piece_cols=FILL_COLS):
    def piece(r0, c0, c1):
        def run():
            dst[r0:r0 + piece_rows, c0:c1] = jnp.dot(lhs_rows(r0, r0 + piece_rows), w_ref[:, c0:c1],
                                                     preferred_element_type=F32)
        return cost * piece_rows * piece_cols // (FILL_ROWS * FILL_COLS), run
    return [piece(r0, c0, min(c0 + piece_cols, col_hi))
            for c0 in range(col_lo, col_hi, piece_cols) for r0 in range(0, T, piece_rows)]


def _kb_body(*refs, nb, nc, add_pos, want_out):
    it = iter(refs)
    x0_ref, xn_ref = next(it), next(it)
    xh_ref = next(it) if nc > 1 else None
    if add_pos:
        rt0_ref, rtn_ref, ct_ref, hp_ref = next(it), next(it), next(it), next(it)
    else:
        rt0_ref = rtn_ref = ct_ref = hp_ref = None
    (mod_ref, modn_ref, nw_ref, wb_ref, cw_ref, cb_ref, wg_ref, bg_ref, cl_ref, lb_ref, h0_ref, s0_ref,
     hbf_ref, xc_ref) = (next(it) for _ in range(14))
    hb_ref, ob_ref, q_ref = (next(it), next(it), next(it)) if want_out else (None, None, None)
    (v_ref, hfin_ref, sfin_ref,
     p_scr, h_scr, xa_ext, g_scr, k_scr, a_st, u_st, h_st, p_st, lf_st, g_st, hc_ref, s_scr, xaf,
     *gla_stage) = it

    ic = pl.program_id(1)
    flat = pl.program_id(0) * nc + ic
    cc = nc - 1 - ic

    @pl.when(ic == 0)
    def _():
        for q in range(nb):
            hc_ref[q:q + 1, :] = h0_ref[q]
        s_scr[...] = s0_ref[...]
        xaf[...] = jnp.zeros_like(xaf)

    def norm_items(x_r, rt_r, m_r):
        x_flat = _Flat(x_r, D)

        def item(s):
            def run():
                rows = pl.ds(s * SEG, SEG)
                xin = x_flat[rows, slice(None)]
                if add_pos:
                    xin = xin + jnp.concatenate(
                        [_pos_rows(rt_r, ct_ref, r0) for r0 in range(s * SEG, (s + 1) * SEG, ROW_TILE)], axis=0)
                h_scr[rows, :] = _norm_mod(xin, nw_ref[...], m_r[0, :, 0:D], m_r[0, :, D:2 * D]).astype(BF16)
            return C_NORM_PIECE, run
        return [item(s) for s in range(NSEG)]

    def proj_items():
        return _dot_items(lambda r0, r1: h_scr[r0:r1, :], wb_ref, p_scr, 0, KB_COLS, C_PROJ_PIECE, piece_rows=T)

    gla = _Gla(True, nb, k_scr, lf_st, g_st, s_scr, gla_stage, states_only=not want_out)
    v_flat = _Flat(v_ref, BW)
    if want_out:
        q_flat, ob_flat = _Flat(q_ref, BW), _Flat(ob_ref, BW)

    def hgrn_input_items():
        def qv_item(s, k):
            def run():
                rows, lanes = pl.ds(s * SEG, SEG), slice(k * HP, (k + 1) * HP)
                if want_out:
                    q_flat[rows, lanes] = _silu_of_twice(p_scr[rows, AW + k * HP:AW + (k + 1) * HP])
                v_flat[rows, lanes] = p_scr[rows, AW + 2 * BW + k * HP:AW + 2 * BW + (k + 1) * HP].astype(BF16)
            return C_PRE_TILE, run
        return [qv_item(s, k) for s, k in _tiles()] + gla.gate_items(p_scr, AW + BW, lb_ref)

    @pl.when(flat == 0)
    def _():
        _emit(norm_items(x0_ref, rt0_ref, mod_ref) + proj_items())

    def step():
        xc_flat = _Flat(xc_ref, AW)
        hb_flat = _Flat(hb_ref, AW) if want_out else None
        scan = _AScan(a_st, u_st, h_st, p_st, hc_ref, nb, reverse=True)

        hbf_ref[...] = h_scr[...].reshape(hbf_ref.shape)
        if nc > 1:
            xh = xh_ref[0]
            if add_pos:
                xh = xh + hp_ref[0]
            hh = _norm_mod(xh, nw_ref[...], mod_ref[0, :, 0:D], mod_ref[0, :, D:2 * D]).astype(BF16)
            left = jnp.dot(hh, wb_ref[:, 0:AW], preferred_element_type=F32)
            xa_ext[0:SUBLANES, :] = jnp.where(cc > 0, left, 0.0)
        else:
            xa_ext[0:SUBLANES, :] = jnp.zeros((SUBLANES, AW), F32)
        xa_ext[pl.ds(SUBLANES, T), :] = p_scr[:, 0:AW]
        xa_ext[pl.ds(SUBLANES + T, SUBLANES), :] = xaf[...]
        xaf[...] = p_scr[0:SUBLANES, 0:AW]

        def conv_item(s):
            def run():
                spq = NSEG // nb
                row = lax.broadcasted_iota(jnp.int32, (SEG, 1), 0)
                acc = jnp.broadcast_to(cb_ref[...], (SEG, AW))
                for j in range(4):
                    tap = xa_ext[pl.ds(s * SEG + SUBLANES - 2 + j, SEG), :]
                    if nb > 1 and s % spq == 0 and j < 2:
                        tap = jnp.where(row + (j - 2) >= 0, tap, 0.0)
                    if nb > 1 and s % spq == spq - 1 and j == 3:
                        tap = jnp.where(row + 1 < SEG, tap, 0.0)
                    acc = acc + cw_ref[j:j + 1, :] * tap
                xc_flat[pl.ds(s * SEG, SEG), slice(None)] = acc
            return C_CONV_SEG, run

        def load_q(n, k):
            return q_flat[pl.ds(n * SEG, SEG), slice(k * HP, (k + 1) * HP)]

        def load_v(n, k):
            return v_flat[pl.ds(n * SEG, SEG), slice(k * HP, (k + 1) * HP)]

        def emit_out(n, k, o):
            ob_flat[pl.ds(n * SEG, SEG), slice(k * HP, (k + 1) * HP)] = o

        _emit(hgrn_input_items())
        _emit(norm_items(xn_ref, rtn_ref, modn_ref))

        conv = [conv_item(s) for s in range(NSEG)]
        coeff, steps, cums = scan.coeff_items(g_scr, xc_flat, cl_ref), scan.step_items(), gla.cumsum_items()
        proj = proj_items()
        credit = (_cost(conv) + _cost(coeff) + _cost(steps) + _cost(cums)
                  + BH * NSEG * (C_GLA_A_TILE + C_GLA_C_TILE) + NA * NSEG * C_FIX_TILE)
        mx = _Fill(proj, PACE * _cost(proj) / credit)
        _emit(conv, mx)
        g_scr[...] = jnp.dot(xc_ref[...].reshape(T, AW).astype(BF16), wg_ref[...],
                             preferred_element_type=F32) + bg_ref[...]
        _emit(coeff, mx)
        _emit(steps, mx)
        cin = scan.chain()
        for q in range(nb):
            hfin_ref[q] = hc_ref[q:q + 1, :]

        def fix_item(s, k):
            def run():
                hb_flat[pl.ds(s * SEG, SEG), k * LANES:(k + 1) * LANES] = scan.full(cin, s, k)
            return C_FIX_TILE, run
        if want_out:
            _emit([fix_item(s, k) for s in range(NSEG) for k in range(NA)], mx)

        _emit(cums, mx)
        _emit(gla.a_items(load_q, load_v), mx)
        _emit(gla.b_items(load_v))
        _emit(gla.c_items(), mx)
        if want_out:
            _emit(gla.d_items(emit_out))
        sfin_ref[...] = s_scr[...]
        mx.flush()

    step()


def _kf_body(*refs, nb, nc, add_pos, final):
    it = iter(refs)
    x_ref = next(it)
    if add_pos:
        rt_ref, ct_ref = next(it), next(it)
    (hbf0_ref, hbfn_ref, xc_ref, hb_ref, ob_ref, q_ref, v_ref, mod_ref, wf_ref, wg_ref, bg_ref, cl_ref, lb_ref,
     bnw_ref, cnw_ref, cws_ref, cbias_ref, wo_ref, fnw_ref, h0_ref, s0_ref,
     xo_ref, hfin_ref, sfin_ref,
     p_scr, sg_scr, hn_scr, g_scr, k_scr, y_scr, part_scr, a_st, u_st, h_st, p_st, lf_st, g_st, hc_ref, s_scr,
     xcb_scr, vn_scr, mix_scr, *gla_stage) = it

    ic = pl.program_id(1)
    flat = pl.program_id(0) * nc + ic

    @pl.when(ic == 0)
    def _():
        for q in range(nb):
            hc_ref[q:q + 1, :] = h0_ref[q]
        s_scr[...] = s0_ref[...]

    o_ag, o_ff, o_bg = 0, AW, AW + BW
    o_cu, o_cv, o_cg = AW + 2 * BW, AW + 2 * BW + CW, AW + 2 * BW + 2 * CW

    def proj_items(col_lo, col_hi):
        return _dot_items(lambda r0, r1: hn_scr[r0:r1, :], wf_ref, p_scr, col_lo, col_hi, C_PROJ_PIECE)

    @pl.when(flat == 0)
    def _():
        hn_scr[...] = hbf0_ref[...].reshape(T, D)
        _emit(proj_items(0, KF_COLS))

    def step():
        xc_flat, hb_flat = _Flat(xc_ref, AW), _Flat(hb_ref, AW)
        q_flat, v_flat, ob_flat = _Flat(q_ref, BW), _Flat(v_ref, BW), _Flat(ob_ref, BW)
        scan = _AScan(a_st, u_st, h_st, p_st, hc_ref, nb, reverse=False)
        gla = _Gla(False, nb, k_scr, lf_st, g_st, s_scr, gla_stage)

        hn_scr[...] = hbfn_ref[...].reshape(T, D)
        for s in range(NSEG):
            rows = pl.ds(s * SEG, SEG)
            xcb_scr[rows, :] = xc_flat[rows, slice(None)].astype(BF16)

        def silu_item(s, c0):
            def run():
                rows = pl.ds(s * SEG, SEG)
                dst = c0 - o_ag if c0 < o_ff else AW + c0 - o_bg
                sg_scr[rows, dst:dst + LANES] = _silu_of_twice(p_scr[rows, c0:c0 + LANES])
            return C_SILU_TILE, run
        gate_cols = list(range(o_ag, o_ag + AW, LANES)) + list(range(o_bg, o_bg + BW, LANES))

        def mlp_items(m):
            lane = lax.broadcasted_iota(jnp.int32, (MLP_TILE, CW), 1)

            def norm_item(r0):
                def run():
                    v = _gelu(p_scr[pl.ds(m * CCH + r0, MLP_TILE), o_cv:o_cv + CW])
                    mu = jnp.mean(v, axis=-1, keepdims=True)
                    dv = v - mu
                    var = jnp.mean(dv * dv, axis=-1, keepdims=True)
                    vn_scr[r0:r0 + MLP_TILE, :] = (dv * lax.rsqrt(var + EPS) * cnw_ref[...]).astype(BF16)
                return C_MLP_TILE, run

            def mix():
                mix_scr[...] = jnp.dot(cws_ref[...], vn_scr[...], preferred_element_type=F32)

            def out_item(r0):
                def run():
                    rows = pl.ds(m * CCH + r0, MLP_TILE)
                    mixed = mix_scr[r0:r0 + MLP_TILE, :]
                    for h in range(1, CH):
                        mixed = jnp.where(lane >= h * CHD, mix_scr[h * CCH + r0:h * CCH + r0 + MLP_TILE, :], mixed)
                    u = _gelu(p_scr[rows, o_cu:o_cu + CW])
                    yc = (u * (mixed + cbias_ref[r0:r0 + MLP_TILE, :])
                          * _silu_of_twice(p_scr[rows, o_cg:o_cg + CW]))
                    y_scr[rows, BW + AW:YW] = yc.astype(BF16)
                return C_MLP_TILE, run
            tiles = range(0, CCH, MLP_TILE)
            return [norm_item(r0) for r0 in tiles] + [(0, mix)] + [out_item(r0) for r0 in tiles]

        def load_q(n, k):
            return q_flat[pl.ds(n * SEG, SEG), slice(k * HP, (k + 1) * HP)]

        def load_v(n, k):
            return v_flat[pl.ds(n * SEG, SEG), slice(k * HP, (k + 1) * HP)]

        def emit_out(n, k, o):
            rows, lanes = pl.ds(n * SEG, SEG), slice(k * HP, (k + 1) * HP)
            o = o + ob_flat[rows, lanes]
            ms = jnp.sum(o * o, axis=-1, keepdims=True) * (1.0 / BDK)
            yb = (o * lax.rsqrt(ms + EPS) * bnw_ref[:, lanes]) * sg_scr[rows, AW + k * HP:AW + (k + 1) * HP]
            y_scr[rows, lanes] = yb.astype(BF16)

        g_scr[...] = jnp.dot(xcb_scr[...], wg_ref[...], preferred_element_type=F32) + bg_ref[...]
        _emit([it for m in range(T // CCH) for it in mlp_items(m)])

        pre = [silu_item(s, c0) for s in range(NSEG) for c0 in gate_cols] + gla.gate_items(p_scr, o_ff, lb_ref)
        coeff, steps, cums = scan.coeff_items(g_scr, xc_flat, cl_ref), scan.step_items(), gla.cumsum_items()
        n_proj = -(-KF_COLS // FILL_COLS) * (T // FILL_ROWS)
        n_out = (D // FILL_COLS) * (T // FILL_ROWS)
        credit = (_cost(pre) + _cost(coeff) + _cost(steps) + _cost(cums)
                  + BH * NSEG * (C_GLA_A_TILE + C_GLA_C_TILE + C_GLA_D_TILE) + NA * NSEG * C_YA_TILE)
        mx = _Fill(proj_items(o_cu, KF_COLS), PACE * (n_proj * C_PROJ_PIECE + n_out * C_OUT_PIECE) / credit)
        _emit(pre, mx)
        mx.extend(proj_items(0, o_cu))

        _emit(coeff, mx)
        _emit(steps, mx)
        cin = scan.chain()
        for q in range(nb):
            hfin_ref[q] = hc_ref[q:q + 1, :]

        def ya_item(s, k):
            def run():
                rows, lanes = pl.ds(s * SEG, SEG), slice(k * LANES, (k + 1) * LANES)
                ya = (scan.full(cin, s, k) + hb_flat[rows, lanes]) * sg_scr[rows, lanes]
                y_scr[rows, BW + k * LANES:BW + (k + 1) * LANES] = ya.astype(BF16)
            return C_YA_TILE, run
        _emit([ya_item(s, k) for s in range(NSEG) for k in range(NA)], mx)
        mx.extend(_dot_items(lambda r0, r1: y_scr[r0:r1, BW:YW], wo_ref.at[BW:YW, :], part_scr, 0, D, C_OUT_PIECE))

        _emit(cums, mx)
        _emit(gla.a_items(load_q, load_v), mx)
        _emit(gla.b_items(load_v))
        _emit(gla.c_items(), mx)
        _emit(gla.d_items(emit_out), mx)
        sfin_ref[...] = s_scr[...]
        mx.flush()

        gm = mod_ref[0, :, 2 * D:3 * D]
        x_flat, xo_flat = _Flat(x_ref, D), _Flat(xo_ref, D)
        def out_item(r0, c0):
            def run():
                rows, cols = pl.ds(r0, FILL_ROWS), slice(c0, min(c0 + FILL_COLS, D))
                part_scr[rows, cols] = part_scr[rows, cols] + jnp.dot(y_scr[rows, 0:BW], wo_ref[0:BW, cols],
                                                                      preferred_element_type=F32)
            return C_OUT_PIECE, run

        def residual_item(r0):
            def run():
                rows = pl.ds(r0, ROW_TILE)
                xin = x_flat[rows, slice(None)]
                if add_pos:
                    xin = xin + _pos_rows(rt_ref, ct_ref, r0)
                xn = xin + gm * part_scr[rows, :]
                if final:
                    ms = jnp.mean(xn * xn, axis=-1, keepdims=True)
                    xn = xn * lax.rsqrt(ms + EPS) * fnw_ref[...]
                xo_flat[rows, slice(None)] = xn
            return C_OUT_PIECE * (D // FILL_COLS) * ROW_TILE // FILL_ROWS, run

        pending = []
        for r0 in range(0, T, FILL_ROWS):
            _paired([out_item(r0, c0) for c0 in range(0, D, FILL_COLS)], pending)
            pending = [residual_item(r) for r in range(r0, r0 + FILL_ROWS, ROW_TILE)]
        _emit(pending)

    step()


def _kf_states_body(hbf0_ref, hbfn_ref, xc_ref, v_ref, wf_ref, wg_ref, bg_ref, cl_ref, lb_ref, h0_ref, s0_ref,
                    hfin_ref, sfin_ref,
                    p_scr, hn_scr, g_scr, k_scr, a_st, u_st, h_st, p_st, lf_st, g_st, hc_ref, s_scr, xcb_scr,
                    *gla_stage, nb, nc):
    ic = pl.program_id(1)
    flat = pl.program_id(0) * nc + ic

    @pl.when(ic == 0)
    def _():
        for q in range(nb):
            hc_ref[q:q + 1, :] = h0_ref[q]
        s_scr[...] = s0_ref[...]

    def proj_items():
        return _dot_items(lambda r0, r1: hn_scr[r0:r1, :], wf_ref.at[:, AW:AW + BW], p_scr, 0, BW, C_PROJ_PIECE)

    @pl.when(flat == 0)
    def _():
        hn_scr[...] = hbf0_ref[...].reshape(T, D)
        _emit(proj_items())

    xc_flat, v_flat = _Flat(xc_ref, AW), _Flat(v_ref, BW)
    scan = _AScan(a_st, u_st, h_st, p_st, hc_ref, nb, reverse=False)
    gla = _Gla(False, nb, k_scr, lf_st, g_st, s_scr, gla_stage, states_only=True)

    hn_scr[...] = hbfn_ref[...].reshape(T, D)
    for s in range(NSEG):
        rows = pl.ds(s * SEG, SEG)
        xcb_scr[rows, :] = xc_flat[rows, slice(None)].astype(BF16)
    g_scr[...] = jnp.dot(xcb_scr[...], wg_ref[...], preferred_element_type=F32) + bg_ref[...]
    _emit(gla.gate_items(p_scr, 0, lb_ref))

    coeff, steps, cums = scan.coeff_items(g_scr, xc_flat, cl_ref), scan.step_items(), gla.cumsum_items()
    proj = proj_items()
    mx = _Fill(proj, _cost(proj) / (_cost(coeff) + _cost(steps) + _cost(cums)))
    _emit(coeff, mx)
    _emit(steps, mx)
    scan.chain()
    for q in range(nb):
        hfin_ref[q] = hc_ref[q:q + 1, :]
    _emit(cums, mx)

    def load_v(n, k):
        return v_flat[pl.ds(n * SEG, SEG), slice(k * HP, (k + 1) * HP)]
    _emit(gla.a_items(None, load_v))
    _emit(gla.b_items(load_v))
    _emit(gla.c_items())
    sfin_ref[...] = s_scr[...]
    mx.flush()


def _mod_body(cs_ref, w_ref, b_ref, o_ref):
    s = _silu(cs_ref[...])
    o_ref[0] = jnp.dot(s, w_ref[0], preferred_element_type=F32, precision=lax.Precision.HIGHEST) + b_ref[0]


def _const_spec(shape):
    nd = len(shape)
    return pl.BlockSpec(shape, lambda ib, ic: (0,) * nd, pipeline_mode=pl.Buffered(1))


def _seq_spec(nb, rs, width, bc):
    return pl.BlockSpec((nb, rs, width), lambda ib, ic: bc(ib, ic) + (0,))


def _step_maps(ngroups, nc, reverse):
    last = ngroups * nc - 1

    def at(f):
        return f // nc, (nc - 1 - f % nc) if reverse else f % nc

    def cur(ib, ic):
        return at(ib * nc + ic)

    def ahead(d):
        return lambda ib, ic: at(jnp.minimum(ib * nc + ic + d, last))

    def fixed(f):
        return lambda ib, ic: at(min(f, last))
    return cur, ahead, fixed


def _scan_scratch():
    return ([pltpu.VMEM((NA, NSEG * PITCH, LANES), F32)] * 4
            + [pltpu.VMEM((BH, NSEG * PITCH, LANES), F32)] * 2)


def _params():
    return pltpu.CompilerParams(dimension_semantics=("arbitrary", "arbitrary"), vmem_limit_bytes=VMEM_LIMIT)


def _state_specs(nb):
    return [pl.BlockSpec((nb, 1, AW), lambda ib, ic: (ib, 0, 0)),
            pl.BlockSpec((nb, BH, HP, HP), lambda ib, ic: (ib, 0, 0, 0))]


def _run_kb(x, mod, mod_row, weights, layer, h0, s0, pos, nb, want_out=True):
    bsz, seq, _ = x.shape
    rs = T // nb
    nc = seq // rs
    add_pos = pos is not None
    cur, ahead, fixed = _step_maps(bsz // nb, nc, reverse=True)
    steps2 = (fixed(0), ahead(1))
    mrow = (lambda g: g) if mod_row is None else (lambda g: mod_row)
    mod_spec = lambda bc: pl.BlockSpec((1, 1, 3 * D), lambda ib, ic: (mrow(bc(ib, ic)[0]), 0, 0))
    ins, specs = [x] * 2, [_seq_spec(nb, rs, D, bc) for bc in steps2]
    if nc > 1:
        ins.append(x)
        specs.append(pl.BlockSpec(
            (1, SUBLANES, D), lambda ib, ic: (ib, jnp.maximum(cur(ib, ic)[1] * (rs // SUBLANES) - 1, 0), 0)))
    if add_pos:
        rowtab, coltab, hpos = pos
        row_spec = lambda bc: pl.BlockSpec((NSEG, D // 2), lambda ib, ic: (bc(ib, ic)[1], 0))
        ins += [rowtab] * 2 + [coltab, hpos]
        specs += [row_spec(bc) for bc in steps2] + [
            _const_spec(coltab.shape), pl.BlockSpec((1, SUBLANES, D), lambda ib, ic: (cur(ib, ic)[1], 0, 0))]
    w_ins, w_specs = _weight_operands(weights, ("nw", "wb", "cw", "cb", "wg", "bg", "cl", "lb"), layer, BWD)
    ins += [mod] * 2 + w_ins + [h0, s0]
    specs += [mod_spec(cur), mod_spec(ahead(1))] + w_specs + _state_specs(nb)
    seq_outs = [(D, BF16), (AW, F32)] + ([(AW, F32), (BW, F32), (BW, F32)] if want_out else []) + [(BW, BF16)]
    out_shape = tuple([jax.ShapeDtypeStruct((bsz, seq, w), dt) for w, dt in seq_outs]
                      + [jax.ShapeDtypeStruct((bsz, 1, AW), F32), jax.ShapeDtypeStruct((bsz, BH, HP, HP), F32)])
    out_specs = tuple([_seq_spec(nb, rs, w, cur) for w, _ in seq_outs] + _state_specs(nb))
    scratch = ([pltpu.VMEM((T, KB_COLS), F32), pltpu.VMEM((T, D), BF16)]
               + [pltpu.VMEM((T + 2 * SUBLANES, AW), F32), pltpu.VMEM((T, 2 * AW), F32), pltpu.VMEM((T, BW), F32)]
               + _scan_scratch()
               + [pltpu.VMEM((SUBLANES, AW), F32), pltpu.VMEM((nb, BH, HP, HP), F32),
                  pltpu.VMEM((SUBLANES, AW), F32)]
               + _gla_scratch())
    return pl.pallas_call(
        functools.partial(_kb_body, nb=nb, nc=nc, add_pos=add_pos, want_out=want_out),
        grid=(bsz // nb, nc), in_specs=specs, out_specs=out_specs, out_shape=out_shape,
        scratch_shapes=scratch, compiler_params=_params(), name="bwd_sweep")(*ins)


def _run_kf(x, kb_out, mod, mod_row, weights, layer, fnw, h0, s0, pos, nb, final):
    bsz, seq, _ = x.shape
    rs = T // nb
    nc = seq // rs
    add_pos = pos is not None
    cur, ahead, fixed = _step_maps(bsz // nb, nc, reverse=False)
    mrow = (lambda g: g) if mod_row is None else (lambda g: mod_row)
    hbf, xc, hb, ob, q, v = kb_out
    ins, specs = [x], [_seq_spec(nb, rs, D, cur)]
    if add_pos:
        rowtab, coltab, _ = pos
        ins += [rowtab, coltab]
        specs += [pl.BlockSpec((NSEG, D // 2), lambda ib, ic: (cur(ib, ic)[1], 0)), _const_spec(coltab.shape)]
    w_ins, w_specs = _weight_operands(
        weights, ("wf", "wg", "bg", "cl", "lb", "bnw", "cnw", "cws", "cbias", "wo"), layer, FWD)
    ins += [hbf, hbf, xc, hb, ob, q, v, mod] + w_ins + [fnw, h0, s0]
    specs += [_seq_spec(nb, rs, D, fixed(0)), _seq_spec(nb, rs, D, ahead(1))]
    specs += [_seq_spec(nb, rs, w, cur) for w in (AW, AW, BW, BW, BW)]
    specs += [pl.BlockSpec((1, 1, 3 * D), lambda ib, ic: (mrow(ib), 0, 0))]
    specs += w_specs + [_const_spec(fnw.shape)] + _state_specs(nb)
    out_shape = (jax.ShapeDtypeStruct((bsz, seq, D), F32),
                 jax.ShapeDtypeStruct((bsz, 1, AW), F32), jax.ShapeDtypeStruct((bsz, BH, HP, HP), F32))
    out_specs = tuple([_seq_spec(nb, rs, D, cur)] + _state_specs(nb))
    scratch = ([pltpu.VMEM((T, KF_COLS), F32), pltpu.VMEM((T, AW + BW), F32), pltpu.VMEM((T, D), BF16)]
               + [pltpu.VMEM((T, 2 * AW), F32), pltpu.VMEM((T, BW), F32), pltpu.VMEM((T, YW), BF16),
                  pltpu.VMEM((T, D), F32)]
               + _scan_scratch()
               + [pltpu.VMEM((SUBLANES, AW), F32), pltpu.VMEM((nb, BH, HP, HP), F32),
                  pltpu.VMEM((T, AW), BF16), pltpu.VMEM((CCH, CW), BF16), pltpu.VMEM((CH * CCH, CW), F32)]
               + _gla_scratch())
    return pl.pallas_call(
        functools.partial(_kf_body, nb=nb, nc=nc, add_pos=add_pos, final=final),
        grid=(bsz // nb, nc), in_specs=specs, out_specs=out_specs, out_shape=out_shape,
        scratch_shapes=scratch, compiler_params=_params(), name="fwd_sweep")(*ins)


def _run_kf_states(hbf, xc, v, weights, layer, h0, s0, nb):
    bsz, seq, _ = hbf.shape
    rs = T // nb
    nc = seq // rs
    cur, ahead, fixed = _step_maps(bsz // nb, nc, reverse=False)
    w_ins, w_specs = _weight_operands(weights, ("wf", "wg", "bg", "cl", "lb"), layer, FWD)
    ins = [hbf, hbf, xc, v] + w_ins + [h0, s0]
    specs = ([_seq_spec(nb, rs, D, fixed(0)), _seq_spec(nb, rs, D, ahead(1)),
              _seq_spec(nb, rs, AW, cur), _seq_spec(nb, rs, BW, cur)] + w_specs + _state_specs(nb))
    out_shape = (jax.ShapeDtypeStruct((bsz, 1, AW), F32), jax.ShapeDtypeStruct((bsz, BH, HP, HP), F32))
    scratch = ([pltpu.VMEM((T, BW), F32), pltpu.VMEM((T, D), BF16), pltpu.VMEM((T, 2 * AW), F32),
                pltpu.VMEM((T, BW), F32)]
               + _scan_scratch()
               + [pltpu.VMEM((SUBLANES, AW), F32), pltpu.VMEM((nb, BH, HP, HP), F32), pltpu.VMEM((T, AW), BF16)]
               + _gla_scratch())
    return pl.pallas_call(
        functools.partial(_kf_states_body, nb=nb, nc=nc),
        grid=(bsz // nb, nc), in_specs=specs, out_specs=tuple(_state_specs(nb)), out_shape=out_shape,
        scratch_shapes=scratch, compiler_params=_params(), name="fwd_states")(*ins)


def _pad_heads(w, fill=0.0):
    lead = w.shape[:-1]
    w = w.reshape(lead + (BH, BDK))
    w = jnp.pad(w, [(0, 0)] * len(lead) + [(0, 0), (0, HP - BDK)], constant_values=fill)
    return w.reshape(lead + (BW,))


IN_B0 = 2 * AW
IN_C0 = IN_B0 + 5 * NBW
REPACK_ROWS = 256


def _in_proj_plan():
    plan = [("b", 0, AW, 0, 1.0), ("f", AW, AW, 0, 0.5),
            ("f", IN_C0, 2 * CW, AW + 2 * BW, 1.0), ("f", IN_C0 + 2 * CW, CW, AW + 2 * BW + 2 * CW, 0.5)]
    for g, (dst, slot, scale) in enumerate((("b", 0, 0.5), ("f", 0, 0.5), ("b", 1, 0.5), ("b", 2, 1.0), ("f", 1, 0.5))):
        for h in range(BH):
            plan.append((dst, IN_B0 + g * NBW + h * BDK, BDK, AW + slot * BW + h * HP, scale))
    return plan


def _repack_in_body(w_ref, wb_ref, wf_ref):
    out = {"b": wb_ref, "f": wf_ref}
    for dst, src0, width, dst0, scale in _in_proj_plan():
        chunk = w_ref[0, :, src0:src0 + width]
        out[dst][0, :, dst0:dst0 + width] = (chunk if scale == 1.0 else chunk * scale).astype(BF16)
        if width == BDK:
            out[dst][0, :, dst0 + BDK:dst0 + HP] = jnp.zeros((REPACK_ROWS, HP - BDK), BF16)


def _repack_out_body(w_ref, wo_ref):
    for h in range(BH):
        wo_ref[0, h * HP:h * HP + BDK, :] = w_ref[0, AW + h * BDK:AW + (h + 1) * BDK, :].astype(BF16)
        wo_ref[0, h * HP + BDK:(h + 1) * HP, :] = jnp.zeros((HP - BDK, D), BF16)
    wo_ref[0, BW:BW + AW, :] = w_ref[0, 0:AW, :].astype(BF16)
    wo_ref[0, BW + AW:YW, :] = w_ref[0, AW + NBW:, :].astype(BF16)


def _repack_projections(w_in, w_out):
    depth = w_in.shape[0]
    wb, wf = pl.pallas_call(
        _repack_in_body, grid=(depth, D // REPACK_ROWS),
        in_specs=[pl.BlockSpec((1, REPACK_ROWS, w_in.shape[2]), lambda l, i: (l, i, 0))],
        out_specs=(pl.BlockSpec((1, REPACK_ROWS, KB_COLS), lambda l, i: (l, i, 0)),
                   pl.BlockSpec((1, REPACK_ROWS, KF_COLS), lambda l, i: (l, i, 0))),
        out_shape=(jax.ShapeDtypeStruct((depth, D, KB_COLS), BF16), jax.ShapeDtypeStruct((depth, D, KF_COLS), BF16)),
        name="repack_in_proj")(w_in)
    wo = pl.pallas_call(
        _repack_out_body, grid=(depth,),
        in_specs=[pl.BlockSpec((1,) + w_out.shape[1:], lambda l: (l, 0, 0))],
        out_specs=pl.BlockSpec((1, YW, D), lambda l: (l, 0, 0)),
        out_shape=jax.ShapeDtypeStruct((depth, YW, D), BF16),
        name="repack_out_proj")(w_out)
    return wb, wf, wo


def _prepare_weights(norm_w, w_in, a_conv_w, a_conv_b, a_wr, a_br, a_wi, a_bi, a_lambda, lbs, b_norm_w,
                     c_norm_w, c_ws, c_bs, w_out):
    depth = w_in.shape[0]
    wb, wf, wo = _repack_projections(w_in, w_out)
    eye = jnp.eye(A_HEADS, dtype=F32)

    def block_diag(w):
        return jnp.einsum("ldhij,hg->ldhigj", w, eye).reshape(depth, 2, AW, AW)
    return {
        "nw": norm_w[:, None, :],
        "wb": wb, "wf": wf, "wo": wo,
        "cw": a_conv_w, "cb": a_conv_b[:, None, :],
        "lb": _pad_heads(lbs[:depth], fill=1.0)[:, None, :],
        "bnw": _pad_heads(b_norm_w)[:, None, :],
        "cnw": c_norm_w[:, None, :],
        "cws": c_ws.reshape(depth, CH * CCH, CCH).astype(BF16),
        "cbias": jnp.repeat(jnp.swapaxes(c_bs, 1, 2), CHD, axis=2),
        "wg": (0.5 * jnp.concatenate([block_diag(a_wr), block_diag(a_wi)], axis=-1)).astype(BF16),
        "bg": 0.5 * jnp.concatenate([a_br, a_bi], axis=-1)[:, :, None, :],
        "cl": (-0.5 * RG_C * jax.nn.softplus(-a_lambda))[:, :, None, :],
    }


FWD, BWD = 0, 1
PER_DIRECTION = ("wg", "bg", "cl")


def _weight_operands(weights, names, layer, direction):
    arrays, specs = [], []
    for n in names:
        w = weights[n]
        lead = (layer, direction) if n in PER_DIRECTION else (layer,)
        rest = w.shape[len(lead):]
        arrays.append(w)
        specs.append(pl.BlockSpec((None,) * len(lead) + rest,
                                  lambda ib, ic, lead=lead, rest=rest: lead + (0,) * len(rest),
                                  pipeline_mode=pl.Buffered(1)))
    return arrays, specs


def _pos_tables(rows):
    qd = D // 4
    omega = 1.0 / (10000.0 ** (jnp.arange(qd, dtype=F32) / qd))
    ar = jnp.arange(rows, dtype=F32)[:, None] * omega
    ac = jnp.arange(GRID_W, dtype=F32)[:, None] * omega
    rowtab = jnp.concatenate([jnp.sin(ar), jnp.cos(ar)], axis=-1)
    coltab = jnp.concatenate([jnp.sin(ac), jnp.cos(ac)], axis=-1)
    nc = rows * GRID_W // T
    hrow = jnp.broadcast_to(rowtab[jnp.maximum(jnp.arange(nc) * NSEG - 1, 0)][:, None, :], (nc, SUBLANES, D // 2))
    hcol = jnp.broadcast_to(coltab[GRID_W - SUBLANES:][None], (nc, SUBLANES, D // 2))
    return rowtab, coltab, jnp.concatenate([hrow, hcol], axis=-1)


def kernel(x, c, ctx, c_ctx, norm_w, w_mod, b_mod, w_in, a_conv_w, a_conv_b, a_wr, a_br, a_wi, a_bi, a_lambda,
           b_lb_logits, b_norm_w, c_norm_w, c_ws, c_bs, w_out, final_norm_w):
    bsz, n_lat, _ = x.shape
    depth = w_in.shape[0]
    assert n_lat % T == 0 and T % ctx.shape[1] == 0 and bsz % (T // ctx.shape[1]) == 0
    nb_ctx = T // ctx.shape[1]
    pos = _pos_tables(n_lat // GRID_W)

    mrows = 2 * SUBLANES
    cs = jnp.zeros((mrows, D), F32).at[:bsz].set(c).at[bsz].set(c_ctx)
    nblk = 3
    mod = pl.pallas_call(
        _mod_body, grid=(depth, nblk),
        in_specs=[pl.BlockSpec((mrows, D), lambda l, j: (0, 0)),
                  pl.BlockSpec((1, D, D), lambda l, j: (l, 0, j)),
                  pl.BlockSpec((1, 1, D), lambda l, j: (l, 0, j))],
        out_specs=pl.BlockSpec((1, mrows, D), lambda l, j: (l, 0, j)),
        out_shape=jax.ShapeDtypeStruct((depth, mrows, 3 * D), F32),
        name="modulation")(cs, w_mod, b_mod[:, None, :])
    mod = mod.reshape(depth, mrows, 1, 3 * D)

    lbs = jnp.cumsum(jax.nn.softmax(b_lb_logits, axis=0), axis=0)
    fnw = final_norm_w[None, :]
    h_zero = jnp.zeros((bsz, 1, AW), F32)
    s_zero = jnp.zeros((bsz, BH, HP, HP), F32)
    weights = _prepare_weights(norm_w, w_in, a_conv_w, a_conv_b, a_wr, a_br, a_wi, a_bi, a_lambda, lbs, b_norm_w,
                               c_norm_w, c_ws, c_bs, w_out)
    for l in range(depth):
        lpos = pos if l == 0 else None
        last = l == depth - 1
        kb = _run_kb(ctx, mod[l], bsz, weights, l, h_zero, s_zero, None, nb_ctx, want_out=not last)
        hb_c, sb_c = kb[-2], kb[-1]
        if last:
            hbf_c, xc_c, v_c = kb[:3]
            hf_c, sf_c = _run_kf_states(hbf_c, xc_c, v_c, weights, l, h_zero, s_zero, nb_ctx)
        else:
            ctx, hf_c, sf_c = _run_kf(ctx, kb[:6], mod[l], bsz, weights, l, fnw, h_zero, s_zero, None, nb_ctx, False)
        kb = _run_kb(x, mod[l], None, weights, l, hb_c, sb_c, lpos, 1)
        x, _, _ = _run_kf(x, kb[:6], mod[l], None, weights, l, fnw, hf_c, sf_c, lpos, 1, last)
    return x
```

```python
import functools

import jax
import jax.numpy as jnp
import numpy as np
from jax import lax
from jax.experimental import pallas as pl
from jax.experimental.pallas import tpu as pltpu

D = 1024
GRID_W = 64
EPS = 1e-6
RG_C = 8.0
AW = 384
A_HEADS = 8
A_HD = 48
BH = 4
BDK = 96
CW = 256
CH = 4
CHD = 64
CCH = 128

LANES = 128
SUBLANES = 8
HP = LANES
BW = BH * HP
NBW = BH * BDK
YW = AW + BW + CW
NA = AW // LANES
T = 512
SEG = 64
NSEG = T // SEG
PITCH = SEG + SUBLANES
KB_COLS = AW + 3 * BW
KF_COLS = AW + 2 * BW + 3 * CW
MXU_TILE = 256
FILL_COLS = 2 * MXU_TILE
FILL_ROWS = T // 2
ROW_TILE = 16
MLP_TILE = 32
V7X_VMEM_BYTES = 64 * 1024 * 1024
VMEM_LIMIT = V7X_VMEM_BYTES - 6 * 1024 * 1024

F32 = jnp.float32
BF16 = jnp.bfloat16

C_CONV_SEG, C_COEFF_TILE, C_SCAN_STEP, C_FIX_TILE, C_YA_TILE = 100, 50, 12, 8, 15
C_GATES_TILE, C_CUMSUM_STEP, C_GLA_A_TILE, C_GLA_C_TILE, C_SILU_TILE = 20, 8, 40, 15, 12
C_PRE_TILE, C_MLP_TILE, C_GLA_B_TILE, C_GLA_D_TILE = 25, 90, 40, 40
C_PROJ_PIECE = FILL_ROWS * FILL_COLS // MXU_TILE
C_OUT_PIECE, C_NORM_PIECE = 3 * C_PROJ_PIECE // 4, 130
PACE = 1.4


def _silu_of_twice(hx):
    return hx * (jnp.tanh(hx) + 1.0)


def _silu(x):
    return _silu_of_twice(0.5 * x)


def _gelu(x):
    c = np.float32(np.sqrt(2.0 / np.pi))
    hx = 0.5 * x
    return hx * (jnp.tanh(x * (c + (0.044715 * c) * (x * x))) + 1.0)


def _nt(a, b):
    return lax.dot_general(a, b, (((1,), (1,)), ((), ())), preferred_element_type=F32)


def _tn(a, b):
    return lax.dot_general(a, b, (((0,), (0,)), ((), ())), preferred_element_type=F32)


def _norm_mod(x, nw, sh, sc):
    ms = jnp.mean(x * x, axis=-1, keepdims=True)
    return (x * lax.rsqrt(ms + EPS)) * (nw * (1.0 + sc)) + sh


def _pos_rows(rt_ref, ct_ref, r0):
    s, c0 = divmod(r0, SEG)
    return jnp.concatenate([jnp.broadcast_to(rt_ref[s:s + 1, :], (ROW_TILE, D // 2)),
                            ct_ref[c0:c0 + ROW_TILE, :]], axis=1)


class _Fill:
    def __init__(self, items=(), rate=1.0):
        self.items = list(items)
        self.rate = rate
        self.credit = 0.0
        self.spent = 0.0

    def __call__(self, credit):
        self.credit += credit
        while self.items and self.spent + 0.5 * self.items[0][0] <= self.credit * self.rate:
            cost, thunk = self.items.pop(0)
            self.spent += cost
            thunk()

    def extend(self, items):
        self.items.extend(items)

    def flush(self):
        while self.items:
            self.items.pop(0)[1]()


def _cost(items):
    return sum(c for c, _ in items)


def _emit(items, fill=None):
    for cost, thunk in items:
        thunk()
        if fill is not None:
            fill(cost)


def _paired(items, partner_items):
    partner = _Fill(partner_items, _cost(partner_items) / max(_cost(items), 1))
    _emit(items, partner)
    partner.flush()


class _Flat:
    def __init__(self, ref, width):
        self.ref = ref
        self.rs = ref.shape[1]
        self.width = width

    def _split(self, idx):
        rows, lanes = idx
        start, size = rows.start, rows.size
        q, off = divmod(start, self.rs)
        assert off + size <= self.rs
        return q, pl.ds(off, size), lanes

    def __getitem__(self, idx):
        q, r, lanes = self._split(idx)
        return self.ref[q, r, lanes]

    def __setitem__(self, idx, val):
        q, r, lanes = self._split(idx)
        self.ref[q, r, lanes] = val


def _tiles():
    return [(s, k) for s in range(NSEG) for k in range(BH)]


class _AScan:
    def __init__(self, a_st, u_st, h_st, p_st, hc_ref, nb, reverse):
        self.refs = (a_st, u_st, h_st, p_st)
        self.hc_ref, self.nb, self.reverse = hc_ref, nb, reverse
        self.hs = [jnp.zeros((NSEG, LANES), F32)] * NA
        self.ps = [jnp.ones((NSEG, LANES), F32)] * NA

    def coeff_items(self, g_scr, xc_ref, cl_ref):
        a_st, u_st = self.refs[:2]

        def item(s, k):
            def run():
                rows, lanes = pl.ds(s * SEG, SEG), slice(k * LANES, (k + 1) * LANES)
                tr = jnp.tanh(g_scr[rows, lanes])
                ti = jnp.tanh(g_scr[rows, AW + k * LANES:AW + (k + 1) * LANES])
                hcl = cl_ref[:, lanes]
                la = hcl * tr + hcl
                th = jnp.tanh(la)
                om4 = (-0.5 * th) / (1.0 - th)
                a_st[k, pl.ds(s * PITCH, SEG), :] = jnp.exp(la)
                root = jnp.where(om4 > 0.0, om4 * lax.rsqrt(om4), 0.0)
                u_st[k, pl.ds(s * PITCH, SEG), :] = root * ((ti + 1.0) * xc_ref[rows, lanes])
            return C_COEFF_TILE, run
        return [item(s, k) for s in range(NSEG) for k in range(NA)]

    def step_items(self):
        a_st, u_st, h_st, p_st = self.refs

        def item(j):
            def run():
                for k in range(NA):
                    a = a_st[k, pl.ds(j, NSEG, stride=PITCH), :]
                    u = u_st[k, pl.ds(j, NSEG, stride=PITCH), :]
                    self.hs[k] = a * self.hs[k] + u
                    self.ps[k] = a * self.ps[k]
                    h_st[k, pl.ds(j, NSEG, stride=PITCH), :] = self.hs[k]
                    p_st[k, pl.ds(j, NSEG, stride=PITCH), :] = self.ps[k]
            return C_SCAN_STEP, run
        order = range(SEG - 1, -1, -1) if self.reverse else range(SEG)
        return [item(j) for j in order]

    def chain(self):
        spq = NSEG // self.nb
        cin = [None] * NSEG
        for q in range(self.nb):
            c = [self.hc_ref[q:q + 1, k * LANES:(k + 1) * LANES] for k in range(NA)]
            order = range((q + 1) * spq - 1, q * spq - 1, -1) if self.reverse else range(q * spq, (q + 1) * spq)
            for s in order:
                cin[s] = c
                c = [self.ps[k][s:s + 1, :] * c[k] + self.hs[k][s:s + 1, :] for k in range(NA)]
            for k in range(NA):
                self.hc_ref[q:q + 1, k * LANES:(k + 1) * LANES] = c[k]
        return cin

    def full(self, cin, s, k):
        _, _, h_st, p_st = self.refs
        st_rows = pl.ds(s * PITCH, SEG)
        return h_st[k, st_rows, :] + p_st[k, st_rows, :] * cin[s][k]


class _Gla:
    def __init__(self, reverse, nb, k_scr, lf_st, g_st, s_scr, stage, states_only=False):
        self.reverse, self.nb, self.states_only = reverse, nb, states_only
        self.k_scr, self.lf_st, self.g_st, self.s_scr = k_scr, lf_st, g_st, s_scr
        self.kg_s, self.ke_s, self.lhs_s, self.kv_s, self.rhs_s = stage
        self.gs = [jnp.zeros((NSEG, LANES), F32)] * BH
        order = range(NSEG - 1, -1, -1) if reverse else range(NSEG)
        self.order = list(order)
        self.tiles = [(n, k) for n in order for k in range(BH)]

    def gate_items(self, p_scr, col0, lb_ref):
        def item(s, k):
            def run():
                rows, lanes = pl.ds(s * SEG, SEG), slice(k * HP, (k + 1) * HP)
                lb = lb_ref[:, lanes]
                c0, c1 = 0.5 + 0.5 * lb, 0.5 - 0.5 * lb
                ct = c1 * jnp.tanh(p_scr[rows, col0 + k * HP:col0 + (k + 1) * HP])
                self.k_scr[rows, lanes] = c1 - ct
                self.lf_st[k, pl.ds(s * PITCH, SEG), :] = jnp.log2(c0 + ct)
            return C_GATES_TILE, run
        return [item(s, k) for s, k in _tiles()]

    def cumsum_items(self):
        def item(j):
            def run():
                for k in range(BH):
                    self.gs[k] = self.gs[k] + self.lf_st[k, pl.ds(j, NSEG, stride=PITCH), :]
                    self.g_st[k, pl.ds(j, NSEG, stride=PITCH), :] = self.gs[k]
            return C_CUMSUM_STEP, run
        order = range(SEG - 1, -1, -1) if self.reverse else range(SEG)
        return [item(j) for j in order]

    def a_items(self, load_q, load_v):
        def item(n, k):
            def run():
                rows, lanes = pl.ds(n * SEG, SEG), slice(k * HP, (k + 1) * HP)
                g = self.g_st[k, pl.ds(n * PITCH, SEG), :]
                tot = self.gs[k][n:n + 1, :]
                kk = self.k_scr[rows, lanes]
                if self.states_only:
                    self.ke_s[rows, lanes] = (kk * jnp.exp2(tot - g)).astype(BF16)
                else:
                    self.lhs_s[k, rows, 0:HP] = (load_q(n, k) * jnp.exp2(g)).astype(BF16)
                    self.rhs_s[n, k, HP:HP + SEG, :] = load_v(n, k)
                    kg = kk * jnp.exp2(-g)
                    self.kg_s[rows, lanes] = kg.astype(BF16)
                    self.ke_s[rows, lanes] = (kg * jnp.exp2(tot)).astype(BF16)
            return C_GLA_A_TILE, run
        return [item(n, k) for n, k in self.tiles]

    def b_items(self, load_v):
        r_i = lax.broadcasted_iota(jnp.int32, (SEG, SEG), 0)
        c_i = lax.broadcasted_iota(jnp.int32, (SEG, SEG), 1)
        keep = (c_i >= r_i) if self.reverse else (c_i <= r_i)

        def item(n, k):
            def run():
                rows, lanes = pl.ds(n * SEG, SEG), slice(k * HP, (k + 1) * HP)
                if not self.states_only:
                    self.lhs_s[k, rows, HP:HP + SEG] = jnp.where(
                        keep, _nt(self.lhs_s[k, rows, 0:HP], self.kg_s[rows, lanes]), 0.0).astype(BF16)
                self.kv_s[n, k] = _tn(load_v(n, k), self.ke_s[rows, lanes])
            return C_GLA_B_TILE, run
        return [item(n, k) for n, k in self.tiles]

    def c_items(self):
        cps = NSEG // self.nb
        state = {}

        def item(n, k):
            def run():
                sq = n // cps
                st = state.get(k)
                if st is None:
                    st = self.s_scr[sq, k]
                if not self.states_only:
                    self.rhs_s[n, k, 0:HP, :] = st.T.astype(BF16)
                st = jnp.exp2(self.gs[k][n:n + 1, :]) * st + self.kv_s[n, k]
                last = (n % cps == 0) if self.reverse else (n % cps == cps - 1)
                if last:
                    self.s_scr[sq, k] = st
                    st = None
                state[k] = st
            return C_GLA_C_TILE, run
        return [item(n, k) for n in self.order for k in range(BH)]

    def d_items(self, emit_out):
        def item(n, k):
            def run():
                rows, lanes = pl.ds(n * SEG, SEG), slice(k * HP, (k + 1) * HP)
                o = jnp.dot(self.lhs_s[k, rows, :], self.rhs_s[n, k], preferred_element_type=F32)
                emit_out(n, k, o)
            return C_GLA_D_TILE, run
        return [item(n, k) for n, k in self.tiles]


def _gla_scratch():
    return ([pltpu.VMEM((T, BW), BF16)] * 2
            + [pltpu.VMEM((BH, T, HP + SEG), BF16), pltpu.VMEM((NSEG, BH, HP, HP), F32),
               pltpu.VMEM((NSEG, BH, HP + SEG, HP), BF16)])


def _dot_items(lhs_rows, w_ref, dst, col_lo, col_hi, cost, piece_rows=FILL_ROWS, piece_cols=FILL_COLS):
    def piece(r0, c0, c1):
        def run():
            dst[r0:r0 + piece_rows, c0:c1] = jnp.dot(lhs_rows(r0, r0 + piece_rows), w_ref[:, c0:c1],
                                                     preferred_element_type=F32)
        return cost * piece_rows * piece_cols // (FILL_ROWS * FILL_COLS), run
    return [piece(r0, c0, min(c0 + piece_cols, col_hi))
            for c0 in range(col_lo, col_hi, piece_cols) for r0 in range(0, T, piece_rows)]


def _kb_body(*refs, nb, nc, add_pos, want_out):
    it = iter(refs)
    x0_ref, xn_ref = next(it), next(it)
    xh_ref = next(it) if nc > 1 else None
    if add_pos:
        rt0_ref, rtn_ref, ct_ref, hp_ref = next(it), next(it), next(it), next(it)
    else:
        rt0_ref = rtn_ref = ct_ref = hp_ref = None
    (mod_ref, modn_ref, nw_ref, wb_ref, cw_ref, cb_ref, wg_ref, bg_ref, cl_ref, lb_ref, h0_ref, s0_ref,
     hbf_ref, xc_ref) = (next(it) for _ in range(14))
    hb_ref, ob_ref, q_ref = (next(it), next(it), next(it)) if want_out else (None, None, None)
    (v_ref, hfin_ref, sfin_ref,
     p_scr, h_scr, xa_ext, g_scr, k_scr, a_st, u_st, h_st, p_st, lf_st, g_st, hc_ref, s_scr, xaf,
     *gla_stage) = it

    ic = pl.program_id(1)
    flat = pl.program_id(0) * nc + ic
    cc = nc - 1 - ic

    @pl.when(ic == 0)
    def _():
        for q in range(nb):
            hc_ref[q:q + 1, :] = h0_ref[q]
        s_scr[...] = s0_ref[...]
        xaf[...] = jnp.zeros_like(xaf)

    def norm_items(x_r, rt_r, m_r):
        x_flat = _Flat(x_r, D)

        def item(s):
            def run():
                rows = pl.ds(s * SEG, SEG)
                xin = x_flat[rows, slice(None)]
                if add_pos:
                    xin = xin + jnp.concatenate(
                        [_pos_rows(rt_r, ct_ref, r0) for r0 in range(s * SEG, (s + 1) * SEG, ROW_TILE)], axis=0)
                h_scr[rows, :] = _norm_mod(xin, nw_ref[...], m_r[0, :, 0:D], m_r[0, :, D:2 * D]).astype(BF16)
            return C_NORM_PIECE, run
        return [item(s) for s in range(NSEG)]

    def proj_items():
        return _dot_items(lambda r0, r1: h_scr[r0:r1, :], wb_ref, p_scr, 0, KB_COLS, C_PROJ_PIECE, piece_rows=T)

    gla = _Gla(True, nb, k_scr, lf_st, g_st, s_scr, gla_stage, states_only=not want_out)
    v_flat = _Flat(v_ref, BW)
    if want_out:
        q_flat, ob_flat = _Flat(q_ref, BW), _Flat(ob_ref, BW)

    def hgrn_input_items():
        def qv_item(s, k):
            def run():
                rows, lanes = pl.ds(s * SEG, SEG), slice(k * HP, (k + 1) * HP)
                if want_out:
                    q_flat[rows, lanes] = _silu_of_twice(p_scr[rows, AW + k * HP:AW + (k + 1) * HP])
                v_flat[rows, lanes] = p_scr[rows, AW + 2 * BW + k * HP:AW + 2 * BW + (k + 1) * HP].astype(BF16)
            return C_PRE_TILE, run
        return [qv_item(s, k) for s, k in _tiles()] + gla.gate_items(p_scr, AW + BW, lb_ref)

    @pl.when(flat == 0)
    def _():
        _emit(norm_items(x0_ref, rt0_ref, mod_ref) + proj_items())

    def step():
        xc_flat = _Flat(xc_ref, AW)
        hb_flat = _Flat(hb_ref, AW) if want_out else None
        scan = _AScan(a_st, u_st, h_st, p_st, hc_ref, nb, reverse=True)

        hbf_ref[...] = h_scr[...].reshape(hbf_ref.shape)
        if nc > 1:
            xh = xh_ref[0]
            if add_pos:
                xh = xh + hp_ref[0]
            hh = _norm_mod(xh, nw_ref[...], mod_ref[0, :, 0:D], mod_ref[0, :, D:2 * D]).astype(BF16)
            left = jnp.dot(hh, wb_ref[:, 0:AW], preferred_element_type=F32)
            xa_ext[0:SUBLANES, :] = jnp.where(cc > 0, left, 0.0)
        else:
            xa_ext[0:SUBLANES, :] = jnp.zeros((SUBLANES, AW), F32)
        xa_ext[pl.ds(SUBLANES, T), :] = p_scr[:, 0:AW]
        xa_ext[pl.ds(SUBLANES + T, SUBLANES), :] = xaf[...]
        xaf[...] = p_scr[0:SUBLANES, 0:AW]

        def conv_item(s):
            def run():
                spq = NSEG // nb
                row = lax.broadcasted_iota(jnp.int32, (SEG, 1), 0)
                acc = jnp.broadcast_to(cb_ref[...], (SEG, AW))
                for j in range(4):
                    tap = xa_ext[pl.ds(s * SEG + SUBLANES - 2 + j, SEG), :]
                    if nb > 1 and s % spq == 0 and j < 2:
                        tap = jnp.where(row + (j - 2) >= 0, tap, 0.0)
                    if nb > 1 and s % spq == spq - 1 and j == 3:
                        tap = jnp.where(row + 1 < SEG, tap, 0.0)
                    acc = acc + cw_ref[j:j + 1, :] * tap
                xc_flat[pl.ds(s * SEG, SEG), slice(None)] = acc
            return C_CONV_SEG, run

        def load_q(n, k):
            return q_flat[pl.ds(n * SEG, SEG), slice(k * HP, (k + 1) * HP)]

        def load_v(n, k):
            return v_flat[pl.ds(n * SEG, SEG), slice(k * HP, (k + 1) * HP)]

        def emit_out(n, k, o):
            ob_flat[pl.ds(n * SEG, SEG), slice(k * HP, (k + 1) * HP)] = o

        _emit(hgrn_input_items())
        _emit(norm_items(xn_ref, rtn_ref, modn_ref))

        conv = [conv_item(s) for s in range(NSEG)]
        coeff, steps, cums = scan.coeff_items(g_scr, xc_flat, cl_ref), scan.step_items(), gla.cumsum_items()
        proj = proj_items()
        credit = (_cost(conv) + _cost(coeff) + _cost(steps) + _cost(cums)
                  + BH * NSEG * (C_GLA_A_TILE + C_GLA_C_TILE) + NA * NSEG * C_FIX_TILE)
        mx = _Fill(proj, PACE * _cost(proj) / credit)
        _emit(conv, mx)
        g_scr[...] = jnp.dot(xc_ref[...].reshape(T, AW).astype(BF16), wg_ref[...],
                             preferred_element_type=F32) + bg_ref[...]
        _emit(coeff, mx)
        _emit(steps, mx)
        cin = scan.chain()
        for q in range(nb):
            hfin_ref[q] = hc_ref[q:q + 1, :]

        def fix_item(s, k):
            def run():
                hb_flat[pl.ds(s * SEG, SEG), k * LANES:(k + 1) * LANES] = scan.full(cin, s, k)
            return C_FIX_TILE, run
        if want_out:
            _emit([fix_item(s, k) for s in range(NSEG) for k in range(NA)], mx)

        _emit(cums, mx)
        _emit(gla.a_items(load_q, load_v), mx)
        _emit(gla.b_items(load_v))
        _emit(gla.c_items(), mx)
        if want_out:
            _emit(gla.d_items(emit_out))
        sfin_ref[...] = s_scr[...]
        mx.flush()

    step()


def _kf_body(*refs, nb, nc, add_pos, final):
    it = iter(refs)
    x_ref = next(it)
    if add_pos:
        rt_ref, ct_ref = next(it), next(it)
    (hbf0_ref, hbfn_ref, xc_ref, hb_ref, ob_ref, q_ref, v_ref, mod_ref, wf_ref, wg_ref, bg_ref, cl_ref, lb_ref,
     bnw_ref, cnw_ref, cws_ref, cbias_ref, wo_ref, fnw_ref, h0_ref, s0_ref,
     xo_ref, hfin_ref, sfin_ref,
     p_scr, sg_scr, hn_scr, g_scr, k_scr, y_scr, part_scr, a_st, u_st, h_st, p_st, lf_st, g_st, hc_ref, s_scr,
     xcb_scr, vn_scr, mix_scr, *gla_stage) = it

    ic = pl.program_id(1)
    flat = pl.program_id(0) * nc + ic

    @pl.when(ic == 0)
    def _():
        for q in range(nb):
            hc_ref[q:q + 1, :] = h0_ref[q]
        s_scr[...] = s0_ref[...]

    o_ag, o_ff, o_bg = 0, AW, AW + BW
    o_cu, o_cv, o_cg = AW + 2 * BW, AW + 2 * BW + CW, AW + 2 * BW + 2 * CW

    def proj_items(col_lo, col_hi):
        return _dot_items(lambda r0, r1: hn_scr[r0:r1, :], wf_ref, p_scr, col_lo, col_hi, C_PROJ_PIECE)

    @pl.when(flat == 0)
    def _():
        hn_scr[...] = hbf0_ref[...].reshape(T, D)
        _emit(proj_items(0, KF_COLS))

    def step():
        xc_flat, hb_flat = _Flat(xc_ref, AW), _Flat(hb_ref, AW)
        q_flat, v_flat, ob_flat = _Flat(q_ref, BW), _Flat(v_ref, BW), _Flat(ob_ref, BW)
        scan = _AScan(a_st, u_st, h_st, p_st, hc_ref, nb, reverse=False)
        gla = _Gla(False, nb, k_scr, lf_st, g_st, s_scr, gla_stage)

        hn_scr[...] = hbfn_ref[...].reshape(T, D)
        for s in range(NSEG):
            rows = pl.ds(s * SEG, SEG)
            xcb_scr[rows, :] = xc_flat[rows, slice(None)].astype(BF16)

        def silu_item(s, c0):
            def run():
                rows = pl.ds(s * SEG, SEG)
                dst = c0 - o_ag if c0 < o_ff else AW + c0 - o_bg
                sg_scr[rows, dst:dst + LANES] = _silu_of_twice(p_scr[rows, c0:c0 + LANES])
            return C_SILU_TILE, run
        gate_cols = list(range(o_ag, o_ag + AW, LANES)) + list(range(o_bg, o_bg + BW, LANES))

        def mlp_items(m):
            lane = lax.broadcasted_iota(jnp.int32, (MLP_TILE, CW), 1)

            def norm_item(r0):
                def run():
                    v = _gelu(p_scr[pl.ds(m * CCH + r0, MLP_TILE), o_cv:o_cv + CW])
                    mu = jnp.mean(v, axis=-1, keepdims=True)
                    dv = v - mu
                    var = jnp.mean(dv * dv, axis=-1, keepdims=True)
                    vn_scr[r0:r0 + MLP_TILE, :] = (dv * lax.rsqrt(var + EPS) * cnw_ref[...]).astype(BF16)
                return C_MLP_TILE, run

            def mix():
                mix_scr[...] = jnp.dot(cws_ref[...], vn_scr[...], preferred_element_type=F32)

            def out_item(r0):
                def run():
                    rows = pl.ds(m * CCH + r0, MLP_TILE)
                    mixed = mix_scr[r0:r0 + MLP_TILE, :]
                    for h in range(1, CH):
                        mixed = jnp.where(lane >= h * CHD, mix_scr[h * CCH + r0:h * CCH + r0 + MLP_TILE, :], mixed)
                    u = _gelu(p_scr[rows, o_cu:o_cu + CW])
                    yc = (u * (mixed + cbias_ref[r0:r0 + MLP_TILE, :])
                          * _silu_of_twice(p_scr[rows, o_cg:o_cg + CW]))
                    y_scr[rows, BW + AW:YW] = yc.astype(BF16)
                return C_MLP_TILE, run
            tiles = range(0, CCH, MLP_TILE)
            return [norm_item(r0) for r0 in tiles] + [(0, mix)] + [out_item(r0) for r0 in tiles]

        def load_q(n, k):
            return q_flat[pl.ds(n * SEG, SEG), slice(k * HP, (k + 1) * HP)]

        def load_v(n, k):
            return v_flat[pl.ds(n * SEG, SEG), slice(k * HP, (k + 1) * HP)]

        def emit_out(n, k, o):
            rows, lanes = pl.ds(n * SEG, SEG), slice(k * HP, (k + 1) * HP)
            o = o + ob_flat[rows, lanes]
            ms = jnp.sum(o * o, axis=-1, keepdims=True) * (1.0 / BDK)
            yb = (o * lax.rsqrt(ms + EPS) * bnw_ref[:, lanes]) * sg_scr[rows, AW + k * HP:AW + (k + 1) * HP]
            y_scr[rows, lanes] = yb.astype(BF16)

        g_scr[...] = jnp.dot(xcb_scr[...], wg_ref[...], preferred_element_type=F32) + bg_ref[...]
        _emit([it for m in range(T // CCH) for it in mlp_items(m)])

        pre = [silu_item(s, c0) for s in range(NSEG) for c0 in gate_cols] + gla.gate_items(p_scr, o_ff, lb_ref)
        coeff, steps, cums = scan.coeff_items(g_scr, xc_flat, cl_ref), scan.step_items(), gla.cumsum_items()
        n_proj = -(-KF_COLS // FILL_COLS) * (T // FILL_ROWS)
        n_out = (D // FILL_COLS) * (T // FILL_ROWS)
        credit = (_cost(pre) + _cost(coeff) + _cost(steps) + _cost(cums)
                  + BH * NSEG * (C_GLA_A_TILE + C_GLA_C_TILE + C_GLA_D_TILE) + NA * NSEG * C_YA_TILE)
        mx = _Fill(proj_items(o_cu, KF_COLS), PACE * (n_proj * C_PROJ_PIECE + n_out * C_OUT_PIECE) / credit)
        _emit(pre, mx)
        mx.extend(proj_items(0, o_cu))

        _emit(coeff, mx)
        _emit(steps, mx)
        cin = scan.chain()
        for q in range(nb):
            hfin_ref[q] = hc_ref[q:q + 1, :]

        def ya_item(s, k):
            def run():
                rows, lanes = pl.ds(s * SEG, SEG), slice(k * LANES, (k + 1) * LANES)
                ya = (scan.full(cin, s, k) + hb_flat[rows, lanes]) * sg_scr[rows, lanes]
                y_scr[rows, BW + k * LANES:BW + (k + 1) * LANES] = ya.astype(BF16)
            return C_YA_TILE, run
        _emit([ya_item(s, k) for s in range(NSEG) for k in range(NA)], mx)
        mx.extend(_dot_items(lambda r0, r1: y_scr[r0:r1, BW:YW], wo_ref.at[BW:YW, :], part_scr, 0, D, C_OUT_PIECE))

        _emit(cums, mx)
        _emit(gla.a_items(load_q, load_v), mx)
        _emit(gla.b_items(load_v))
        _emit(gla.c_items(), mx)
        _emit(gla.d_items(emit_out), mx)
        sfin_ref[...] = s_scr[...]
        mx.flush()

        gm = mod_ref[0, :, 2 * D:3 * D]
        x_flat, xo_flat = _Flat(x_ref, D), _Flat(xo_ref, D)
        def out_item(r0, c0):
            def run():
                rows, cols = pl.ds(r0, FILL_ROWS), slice(c0, min(c0 + FILL_COLS, D))
                part_scr[rows, cols] = part_scr[rows, cols] + jnp.dot(y_scr[rows, 0:BW], wo_ref[0:BW, cols],
                                                                      preferred_element_type=F32)
            return C_OUT_PIECE, run

        def residual_item(r0):
            def run():
                rows = pl.ds(r0, ROW_TILE)
                xin = x_flat[rows, slice(None)]
                if add_pos:
                    xin = xin + _pos_rows(rt_ref, ct_ref, r0)
                xn = xin + gm * part_scr[rows, :]
                if final:
                    ms = jnp.mean(xn * xn, axis=-1, keepdims=True)
                    xn = xn * lax.rsqrt(ms + EPS) * fnw_ref[...]
                xo_flat[rows, slice(None)] = xn
            return C_OUT_PIECE * (D // FILL_COLS) * ROW_TILE // FILL_ROWS, run

        pending = []
        for r0 in range(0, T, FILL_ROWS):
            _paired([out_item(r0, c0) for c0 in range(0, D, FILL_COLS)], pending)
            pending = [residual_item(r) for r in range(r0, r0 + FILL_ROWS, ROW_TILE)]
        _emit(pending)

    step()


def _kf_states_body(hbf0_ref, hbfn_ref, xc_ref, v_ref, wf_ref, wg_ref, bg_ref, cl_ref, lb_ref, h0_ref, s0_ref,
                    hfin_ref, sfin_ref,
                    p_scr, hn_scr, g_scr, k_scr, a_st, u_st, h_st, p_st, lf_st, g_st, hc_ref, s_scr, xcb_scr,
                    *gla_stage, nb, nc):
    ic = pl.program_id(1)
    flat = pl.program_id(0) * nc + ic

    @pl.when(ic == 0)
    def _():
        for q in range(nb):
            hc_ref[q:q + 1, :] = h0_ref[q]
        s_scr[...] = s0_ref[...]

    def proj_items():
        return _dot_items(lambda r0, r1: hn_scr[r0:r1, :], wf_ref.at[:, AW:AW + BW], p_scr, 0, BW, C_PROJ_PIECE)

    @pl.when(flat == 0)
    def _():
        hn_scr[...] = hbf0_ref[...].reshape(T, D)
        _emit(proj_items())

    xc_flat, v_flat = _Flat(xc_ref, AW), _Flat(v_ref, BW)
    scan = _AScan(a_st, u_st, h_st, p_st, hc_ref, nb, reverse=False)
    gla = _Gla(False, nb, k_scr, lf_st, g_st, s_scr, gla_stage, states_only=True)

    hn_scr[...] = hbfn_ref[...].reshape(T, D)
    for s in range(NSEG):
        rows = pl.ds(s * SEG, SEG)
        xcb_scr[rows, :] = xc_flat[rows, slice(None)].astype(BF16)
    g_scr[...] = jnp.dot(xcb_scr[...], wg_ref[...], preferred_element_type=F32) + bg_ref[...]
    _emit(gla.gate_items(p_scr, 0, lb_ref))

    coeff, steps, cums = scan.coeff_items(g_scr, xc_flat, cl_ref), scan.step_items(), gla.cumsum_items()
    proj = proj_items()
    mx = _Fill(proj, _cost(proj) / (_cost(coeff) + _cost(steps) + _cost(cums)))
    _emit(coeff, mx)
    _emit(steps, mx)
    scan.chain()
    for q in range(nb):
        hfin_ref[q] = hc_ref[q:q + 1, :]
    _emit(cums, mx)

    def load_v(n, k):
        return v_flat[pl.ds(n * SEG, SEG), slice(k * HP, (k + 1) * HP)]
    _emit(gla.a_items(None, load_v))
    _emit(gla.b_items(load_v))
    _emit(gla.c_items())
    sfin_ref[...] = s_scr[...]
    mx.flush()


def _mod_body(cs_ref, w_ref, b_ref, o_ref):
    s = _silu(cs_ref[...])
    o_ref[0] = jnp.dot(s, w_ref[0], preferred_element_type=F32, precision=lax.Precision.HIGHEST) + b_ref[0]


def _const_spec(shape):
    nd = len(shape)
    return pl.BlockSpec(shape, lambda ib, ic: (0,) * nd, pipeline_mode=pl.Buffered(1))


def _seq_spec(nb, rs, width, bc):
    return pl.BlockSpec((nb, rs, width), lambda ib, ic: bc(ib, ic) + (0,))


def _step_maps(ngroups, nc, reverse):
    last = ngroups * nc - 1

    def at(f):
        return f // nc, (nc - 1 - f % nc) if reverse else f % nc

    def cur(ib, ic):
        return at(ib * nc + ic)

    def ahead(d):
        return lambda ib, ic: at(jnp.minimum(ib * nc + ic + d, last))

    def fixed(f):
        return lambda ib, ic: at(min(f, last))
    return cur, ahead, fixed


def _scan_scratch():
    return ([pltpu.VMEM((NA, NSEG * PITCH, LANES), F32)] * 4
            + [pltpu.VMEM((BH, NSEG * PITCH, LANES), F32)] * 2)


def _params():
    return pltpu.CompilerParams(dimension_semantics=("arbitrary", "arbitrary"), vmem_limit_bytes=VMEM_LIMIT)


def _state_specs(nb):
    return [pl.BlockSpec((nb, 1, AW), lambda ib, ic: (ib, 0, 0)),
            pl.BlockSpec((nb, BH, HP, HP), lambda ib, ic: (ib, 0, 0, 0))]


def _run_kb(x, mod, mod_row, weights, layer, h0, s0, pos, nb, want_out=True):
    bsz, seq, _ = x.shape
    rs = T // nb
    nc = seq // rs
    add_pos = pos is not None
    cur, ahead, fixed = _step_maps(bsz // nb, nc, reverse=True)
    steps2 = (fixed(0), ahead(1))
    mrow = (lambda g: g) if mod_row is None else (lambda g: mod_row)
    mod_spec = lambda bc: pl.BlockSpec((1, 1, 3 * D), lambda ib, ic: (mrow(bc(ib, ic)[0]), 0, 0))
    ins, specs = [x] * 2, [_seq_spec(nb, rs, D, bc) for bc in steps2]
    if nc > 1:
        ins.append(x)
        specs.append(pl.BlockSpec(
            (1, SUBLANES, D), lambda ib, ic: (ib, jnp.maximum(cur(ib, ic)[1] * (rs // SUBLANES) - 1, 0), 0)))
    if add_pos:
        rowtab, coltab, hpos = pos
        row_spec = lambda bc: pl.BlockSpec((NSEG, D // 2), lambda ib, ic: (bc(ib, ic)[1], 0))
        ins += [rowtab] * 2 + [coltab, hpos]
        specs += [row_spec(bc) for bc in steps2] + [
            _const_spec(coltab.shape), pl.BlockSpec((1, SUBLANES, D), lambda ib, ic: (cur(ib, ic)[1], 0, 0))]
    w_ins, w_specs = _weight_operands(weights, ("nw", "wb", "cw", "cb", "wg", "bg", "cl", "lb"), layer, BWD)
    ins += [mod] * 2 + w_ins + [h0, s0]
    specs += [mod_spec(cur), mod_spec(ahead(1))] + w_specs + _state_specs(nb)
    seq_outs = [(D, BF16), (AW, F32)] + ([(AW, F32), (BW, F32), (BW, F32)] if want_out else []) + [(BW, BF16)]
    out_shape = tuple([jax.ShapeDtypeStruct((bsz, seq, w), dt) for w, dt in seq_outs]
                      + [jax.ShapeDtypeStruct((bsz, 1, AW), F32), jax.ShapeDtypeStruct((bsz, BH, HP, HP), F32)])
    out_specs = tuple([_seq_spec(nb, rs, w, cur) for w, _ in seq_outs] + _state_specs(nb))
    scratch = ([pltpu.VMEM((T, KB_COLS), F32), pltpu.VMEM((T, D), BF16)]
               + [pltpu.VMEM((T + 2 * SUBLANES, AW), F32), pltpu.VMEM((T, 2 * AW), F32), pltpu.VMEM((T, BW), F32)]
               + _scan_scratch()
               + [pltpu.VMEM((SUBLANES, AW), F32), pltpu.VMEM((nb, BH, HP, HP), F32),
                  pltpu.VMEM((SUBLANES, AW), F32)]
               + _gla_scratch())
    return pl.pallas_call(
        functools.partial(_kb_body, nb=nb, nc=nc, add_pos=add_pos, want_out=want_out),
        grid=(bsz // nb, nc), in_specs=specs, out_specs=out_specs, out_shape=out_shape,
        scratch_shapes=scratch, compiler_params=_params(), name="bwd_sweep")(*ins)


def _run_kf(x, kb_out, mod, mod_row, weights, layer, fnw, h0, s0, pos, nb, final):
    bsz, seq, _ = x.shape
    rs = T // nb
    nc = seq // rs
    add_pos = pos is not None
    cur, ahead, fixed = _step_maps(bsz // nb, nc, reverse=False)
    mrow = (lambda g: g) if mod_row is None else (lambda g: mod_row)
    hbf, xc, hb, ob, q, v = kb_out
    ins, specs = [x], [_seq_spec(nb, rs, D, cur)]
    if add_pos:
        rowtab, coltab, _ = pos
        ins += [rowtab, coltab]
        specs += [pl.BlockSpec((NSEG, D // 2), lambda ib, ic: (cur(ib, ic)[1], 0)), _const_spec(coltab.shape)]
    w_ins, w_specs = _weight_operands(
        weights, ("wf", "wg", "bg", "cl", "lb", "bnw", "cnw", "cws", "cbias", "wo"), layer, FWD)
    ins += [hbf, hbf, xc, hb, ob, q, v, mod] + w_ins + [fnw, h0, s0]
    specs += [_seq_spec(nb, rs, D, fixed(0)), _seq_spec(nb, rs, D, ahead(1))]
    specs += [_seq_spec(nb, rs, w, cur) for w in (AW, AW, BW, BW, BW)]
    specs += [pl.BlockSpec((1, 1, 3 * D), lambda ib, ic: (mrow(ib), 0, 0))]
    specs += w_specs + [_const_spec(fnw.shape)] + _state_specs(nb)
    out_shape = (jax.ShapeDtypeStruct((bsz, seq, D), F32),
                 jax.ShapeDtypeStruct((bsz, 1, AW), F32), jax.ShapeDtypeStruct((bsz, BH, HP, HP), F32))
    out_specs = tuple([_seq_spec(nb, rs, D, cur)] + _state_specs(nb))
    scratch = ([pltpu.VMEM((T, KF_COLS), F32), pltpu.VMEM((T, AW + BW), F32), pltpu.VMEM((T, D), BF16)]
               + [pltpu.VMEM((T, 2 * AW), F32), pltpu.VMEM((T, BW), F32), pltpu.VMEM((T, YW), BF16),
                  pltpu.VMEM((T, D), F32)]
               + _scan_scratch()
               + [pltpu.VMEM((SUBLANES, AW), F32), pltpu.VMEM((nb, BH, HP, HP), F32),
                  pltpu.VMEM((T, AW), BF16), pltpu.VMEM((CCH, CW), BF16), pltpu.VMEM((CH * CCH, CW), F32)]
               + _gla_scratch())
    return pl.pallas_call(
        functools.partial(_kf_body, nb=nb, nc=nc, add_pos=add_pos, final=final),
        grid=(bsz // nb, nc), in_specs=specs, out_specs=out_specs, out_shape=out_shape,
        scratch_shapes=scratch, compiler_params=_params(), name="fwd_sweep")(*ins)


def _run_kf_states(hbf, xc, v, weights, layer, h0, s0, nb):
    bsz, seq, _ = hbf.shape
    rs = T // nb
    nc = seq // rs
    cur, ahead, fixed = _step_maps(bsz // nb, nc, reverse=False)
    w_ins, w_specs = _weight_operands(weights, ("wf", "wg", "bg", "cl", "lb"), layer, FWD)
    ins = [hbf, hbf, xc, v] + w_ins + [h0, s0]
    specs = ([_seq_spec(nb, rs, D, fixed(0)), _seq_spec(nb, rs, D, ahead(1)),
              _seq_spec(nb, rs, AW, cur), _seq_spec(nb, rs, BW, cur)] + w_specs + _state_specs(nb))
    out_shape = (jax.ShapeDtypeStruct((bsz, 1, AW), F32), jax.ShapeDtypeStruct((bsz, BH, HP, HP), F32))
    scratch = ([pltpu.VMEM((T, BW), F32), pltpu.VMEM((T, D), BF16), pltpu.VMEM((T, 2 * AW), F32),
                pltpu.VMEM((T, BW), F32)]
               + _scan_scratch()
               + [pltpu.VMEM((SUBLANES, AW), F32), pltpu.VMEM((nb, BH, HP, HP), F32), pltpu.VMEM((T, AW), BF16)]
               + _gla_scratch())
    return pl.pallas_call(
        functools.partial(_kf_states_body, nb=nb, nc=nc),
        grid=(bsz // nb, nc), in_specs=specs, out_specs=tuple(_state_specs(nb)), out_shape=out_shape,
        scratch_shapes=scratch, compiler_params=_params(), name="fwd_states")(*ins)


def _pad_heads(w, fill=0.0):
    lead = w.shape[:-1]
    w = w.reshape(lead + (BH, BDK))
    w = jnp.pad(w, [(0, 0)] * len(lead) + [(0, 0), (0, HP - BDK)], constant_values=fill)
    return w.reshape(lead + (BW,))


IN_B0 = 2 * AW
IN_C0 = IN_B0 + 5 * NBW
REPACK_ROWS = 256


def _in_proj_plan():
    plan = [("b", 0, AW, 0, 1.0), ("f", AW, AW, 0, 0.5),
            ("f", IN_C0, 2 * CW, AW + 2 * BW, 1.0), ("f", IN_C0 + 2 * CW, CW, AW + 2 * BW + 2 * CW, 0.5)]
    for g, (dst, slot, scale) in enumerate((("b", 0, 0.5), ("f", 0, 0.5), ("b", 1, 0.5), ("b", 2, 1.0), ("f", 1, 0.5))):
        for h in range(BH):
            plan.append((dst, IN_B0 + g * NBW + h * BDK, BDK, AW + slot * BW + h * HP, scale))
    return plan


def _repack_in_body(w_ref, wb_ref, wf_ref):
    out = {"b": wb_ref, "f": wf_ref}
    for dst, src0, width, dst0, scale in _in_proj_plan():
        chunk = w_ref[0, :, src0:src0 + width]
        out[dst][0, :, dst0:dst0 + width] = (chunk if scale == 1.0 else chunk * scale).astype(BF16)
        if width == BDK:
            out[dst][0, :, dst0 + BDK:dst0 + HP] = jnp.zeros((REPACK_ROWS, HP - BDK), BF16)


def _repack_out_body(w_ref, wo_ref):
    for h in range(BH):
        wo_ref[0, h * HP:h * HP + BDK, :] = w_ref[0, AW + h * BDK:AW + (h + 1) * BDK, :].astype(BF16)
        wo_ref[0, h * HP + BDK:(h + 1) * HP, :] = jnp.zeros((HP - BDK, D), BF16)
    wo_ref[0, BW:BW + AW, :] = w_ref[0, 0:AW, :].astype(BF16)
    wo_ref[0, BW + AW:YW, :] = w_ref[0, AW + NBW:, :].astype(BF16)


def _repack_projections(w_in, w_out):
    depth = w_in.shape[0]
    wb, wf = pl.pallas_call(
        _repack_in_body, grid=(depth, D // REPACK_ROWS),
        in_specs=[pl.BlockSpec((1, REPACK_ROWS, w_in.shape[2]), lambda l, i: (l, i, 0))],
        out_specs=(pl.BlockSpec((1, REPACK_ROWS, KB_COLS), lambda l, i: (l, i, 0)),
                   pl.BlockSpec((1, REPACK_ROWS, KF_COLS), lambda l, i: (l, i, 0))),
        out_shape=(jax.ShapeDtypeStruct((depth, D, KB_COLS), BF16), jax.ShapeDtypeStruct((depth, D, KF_COLS), BF16)),
        name="repack_in_proj")(w_in)
    wo = pl.pallas_call(
        _repack_out_body, grid=(depth,),
        in_specs=[pl.BlockSpec((1,) + w_out.shape[1:], lambda l: (l, 0, 0))],
        out_specs=pl.BlockSpec((1, YW, D), lambda l: (l, 0, 0)),
        out_shape=jax.ShapeDtypeStruct((depth, YW, D), BF16),
        name="repack_out_proj")(w_out)
    return wb, wf, wo


def _gate_weights_body(wr_ref, wi_ref, wg_ref):
    wg_ref[0, 0] = jnp.zeros((AW, 2 * AW), BF16)
    for h in range(A_HEADS):
        r0 = h * A_HD
        wg_ref[0, 0, r0:r0 + A_HD, r0:r0 + A_HD] = (0.5 * wr_ref[0, 0, h]).astype(BF16)
        wg_ref[0, 0, r0:r0 + A_HD, AW + r0:AW + r0 + A_HD] = (0.5 * wi_ref[0, 0, h]).astype(BF16)


def _gate_weights(a_wr, a_wi):
    depth = a_wr.shape[0]
    head_spec = pl.BlockSpec((1, 1, A_HEADS, A_HD, A_HD), lambda l, d: (l, d, 0, 0, 0))
    return pl.pallas_call(
        _gate_weights_body, grid=(depth, 2), in_specs=[head_spec, head_spec],
        out_specs=pl.BlockSpec((1, 1, AW, 2 * AW), lambda l, d: (l, d, 0, 0)),
        out_shape=jax.ShapeDtypeStruct((depth, 2, AW, 2 * AW), BF16),
        name="gate_weights")(a_wr, a_wi)


def _prepare_weights(norm_w, w_in, a_conv_w, a_conv_b, a_wr, a_br, a_wi, a_bi, a_lambda, lbs, b_norm_w,
                     c_norm_w, c_ws, c_bs, w_out):
    depth = w_in.shape[0]
    wb, wf, wo = _repack_projections(w_in, w_out)
    return {
        "nw": norm_w[:, None, :],
        "wb": wb, "wf": wf, "wo": wo,
        "cw": a_conv_w, "cb": a_conv_b[:, None, :],
        "lb": _pad_heads(lbs[:depth], fill=1.0)[:, None, :],
        "bnw": _pad_heads(b_norm_w)[:, None, :],
        "cnw": c_norm_w[:, None, :],
        "cws": c_ws.reshape(depth, CH * CCH, CCH).astype(BF16),
        "cbias": jnp.repeat(jnp.swapaxes(c_bs, 1, 2), CHD, axis=2),
        "wg": _gate_weights(a_wr, a_wi),
        "bg": 0.5 * jnp.concatenate([a_br, a_bi], axis=-1)[:, :, None, :],
        "cl": (-0.5 * RG_C * jax.nn.softplus(-a_lambda))[:, :, None, :],
    }


FWD, BWD = 0, 1
PER_DIRECTION = ("wg", "bg", "cl")


def _weight_operands(weights, names, layer, direction):
    arrays, specs = [], []
    for n in names:
        w = weights[n]
        lead = (layer, direction) if n in PER_DIRECTION else (layer,)
        rest = w.shape[len(lead):]
        arrays.append(w)
        specs.append(pl.BlockSpec((None,) * len(lead) + rest,
                                  lambda ib, ic, lead=lead, rest=rest: lead + (0,) * len(rest),
                                  pipeline_mode=pl.Buffered(1)))
    return arrays, specs


def _pos_tables(rows):
    qd = D // 4
    omega = 1.0 / (10000.0 ** (jnp.arange(qd, dtype=F32) / qd))
    ar = jnp.arange(rows, dtype=F32)[:, None] * omega
    ac = jnp.arange(GRID_W, dtype=F32)[:, None] * omega
    rowtab = jnp.concatenate([jnp.sin(ar), jnp.cos(ar)], axis=-1)
    coltab = jnp.concatenate([jnp.sin(ac), jnp.cos(ac)], axis=-1)
    nc = rows * GRID_W // T
    hrow = jnp.broadcast_to(rowtab[jnp.maximum(jnp.arange(nc) * NSEG - 1, 0)][:, None, :], (nc, SUBLANES, D // 2))
    hcol = jnp.broadcast_to(coltab[GRID_W - SUBLANES:][None], (nc, SUBLANES, D // 2))
    return rowtab, coltab, jnp.concatenate([hrow, hcol], axis=-1)


def kernel(x, c, ctx, c_ctx, norm_w, w_mod, b_mod, w_in, a_conv_w, a_conv_b, a_wr, a_br, a_wi, a_bi, a_lambda,
           b_lb_logits, b_norm_w, c_norm_w, c_ws, c_bs, w_out, final_norm_w):
    bsz, n_lat, _ = x.shape
    depth = w_in.shape[0]
    assert n_lat % T == 0 and T % ctx.shape[1] == 0 and bsz % (T // ctx.shape[1]) == 0
    nb_ctx = T // ctx.shape[1]
    pos = _pos_tables(n_lat // GRID_W)

    mrows = 2 * SUBLANES
    cs = jnp.zeros((mrows, D), F32).at[:bsz].set(c).at[bsz].set(c_ctx)
    nblk = 3
    mod = pl.pallas_call(
        _mod_body, grid=(depth, nblk),
        in_specs=[pl.BlockSpec((mrows, D), lambda l, j: (0, 0)),
                  pl.BlockSpec((1, D, D), lambda l, j: (l, 0, j)),
                  pl.BlockSpec((1, 1, D), lambda l, j: (l, 0, j))],
        out_specs=pl.BlockSpec((1, mrows, D), lambda l, j: (l, 0, j)),
        out_shape=jax.ShapeDtypeStruct((depth, mrows, 3 * D), F32),
        name="modulation")(cs, w_mod, b_mod[:, None, :])
    mod = mod.reshape(depth, mrows, 1, 3 * D)

    lbs = jnp.cumsum(jax.nn.softmax(b_lb_logits, axis=0), axis=0)
    fnw = final_norm_w[None, :]
    h_zero = jnp.zeros((bsz, 1, AW), F32)
    s_zero = jnp.zeros((bsz, BH, HP, HP), F32)
    weights = _prepare_weights(norm_w, w_in, a_conv_w, a_conv_b, a_wr, a_br, a_wi, a_bi, a_lambda, lbs, b_norm_w,
                               c_norm_w, c_ws, c_bs, w_out)
    for l in range(depth):
        lpos = pos if l == 0 else None
        last = l == depth - 1
        kb = _run_kb(ctx, mod[l], bsz, weights, l, h_zero, s_zero, None, nb_ctx, want_out=not last)
        hb_c, sb_c = kb[-2], kb[-1]
        if last:
            hbf_c, xc_c, v_c = kb[:3]
            hf_c, sf_c = _run_kf_states(hbf_c, xc_c, v_c, weights, l, h_zero, s_zero, nb_ctx)
        else:
            ctx, hf_c, sf_c = _run_kf(ctx, kb[:6], mod[l], bsz, weights, l, fnw, h_zero, s_zero, None, nb_ctx, False)
        kb = _run_kb(x, mod[l], None, weights, l, hb_c, sb_c, lpos, 1)
        x, _, _ = _run_kf(x, kb[:6], mod[l], None, weights, l, fnw, hf_c, sf_c, lpos, 1, last)
    return x
```
